```python
import math
import jax, jax.numpy as jnp
from jax import lax
import numpy as np

D_MODEL = 1024
BATCH = 4
SEQ = 4096
DEPTH = 1
DEC_BATCH = 32
DEC_SEQ = 4
PAST_LEN = 8192
PAGE_SIZE = 128

D_CONV = D_MODEL // 2
CONV_GROUPS = 8
CONV_W = 3
N_HEADS = 4
HEAD_DK = 64
HEAD_DV = 2 * HEAD_DK
D_ATTN = N_HEADS * HEAD_DV
D_MIX = D_CONV + D_ATTN
D_IN = 3 * D_CONV + 3 * D_ATTN
N_MEM = 256
MEM_HEADS = 4
MEM_DH = D_MODEL // MEM_HEADS
D_FF = 2816
FFN_W = 3
REL_BUCKETS = 32
REL_MAX_EXACT = 16
REL_MAX_DIST = 128
Q_BLOCK = 128
EPS = 1e-6
NEG = -1e30

kernel_name = "hymba_conv_diffattn_decoder_step"


def rmsnorm(x, w):
    xf = x.astype(jnp.float32)
    y = xf * lax.rsqrt(jnp.mean(xf * xf, axis=-1, keepdims=True) + EPS)
    return (y * w.astype(jnp.float32)).astype(x.dtype)


def causal_dwconv(x, prefix, w, b):
    width = w.shape[0]
    t = x.shape[1]
    xp = jnp.concatenate([prefix.astype(x.dtype), x], axis=1)
    y = b
    for j in range(width):
        y = y + w[j] * xp[:, j:j + t]
    return y, xp[:, -(width - 1):]


def t5_bucket(rel):
    n = jnp.maximum(rel, 0)
    nf = jnp.maximum(n, 1).astype(jnp.float32)
    large = REL_MAX_EXACT + (jnp.log(nf / REL_MAX_EXACT) / math.log(REL_MAX_DIST / REL_MAX_EXACT)
                             * (REL_BUCKETS - REL_MAX_EXACT)).astype(jnp.int32)
    large = jnp.minimum(large, REL_BUCKETS - 1)
    return jnp.where(n < REL_MAX_EXACT, n, large)


def diff_attn_core(q, k, v, q_pos, k_pos, rel_bias, lam):
    rel = q_pos[:, None] - k_pos[None, :]
    bias = rel_bias[t5_bucket(rel)].astype(jnp.float32)
    bias = jnp.where((rel >= 0)[..., None], bias, NEG)
    bias = jnp.transpose(bias, (2, 0, 1))[None, :, None]
    s = jnp.einsum('bqhcd,bkhcd->bhcqk', q, k).astype(jnp.float32) * (HEAD_DK ** -0.5) + bias
    p = jax.nn.softmax(s, axis=-1)
    a = p[:, :, 0] - lam * p[:, :, 1]
    return jnp.einsum('bhqk,bkhd->bqhd', a.astype(v.dtype), v)


def diff_attn_prompt(q, k, v, rel_bias, lam):
    b, s = q.shape[0], q.shape[1]
    nb = s // Q_BLOCK
    qb = jnp.transpose(q.reshape(b, nb, Q_BLOCK, N_HEADS, 2, HEAD_DK), (1, 0, 2, 3, 4, 5))
    k_pos = jnp.arange(s)

    def block(args):
        i, q_i = args
        q_pos = i * Q_BLOCK + jnp.arange(Q_BLOCK)
        return diff_attn_core(q_i, k, v, q_pos, k_pos, rel_bias, lam)

    out = lax.map(block, (jnp.arange(nb), qb))
    return jnp.transpose(out, (1, 0, 2, 3, 4)).reshape(b, s, N_HEADS, HEAD_DV)


def split_mix(z):
    o = [0, D_CONV, 2 * D_CONV, 3 * D_CONV, 3 * D_CONV + D_ATTN, 3 * D_CONV + 2 * D_ATTN, D_IN]
    return [z[..., o[i]:o[i + 1]] for i in range(6)]


def head_norm(o, w, lam_init):
    o = rmsnorm(o, w) * (1.0 - lam_init)
    return o.reshape(o.shape[0], o.shape[1], D_ATTN)


def memory_kv(mem, norm_mem_w, w_ckv):
    kv = rmsnorm(mem, norm_mem_w) @ w_ckv
    b, m = mem.shape[0], mem.shape[1]
    return (kv[..., :D_MODEL].reshape(b, m, MEM_HEADS, MEM_DH),
            kv[..., D_MODEL:].reshape(b, m, MEM_HEADS, MEM_DH))


def cross_attn(h, mk, mv, w_cq, w_co):
    b, t = h.shape[0], h.shape[1]
    q = (h @ w_cq).reshape(b, t, MEM_HEADS, MEM_DH)
    s = jnp.einsum('bqhd,bkhd->bhqk', q, mk).astype(jnp.float32) * (MEM_DH ** -0.5)
    p = jax.nn.softmax(s, axis=-1)
    o = jnp.einsum('bhqk,bkhd->bqhd', p.astype(mv.dtype), mv).reshape(b, t, D_MODEL)
    return o @ w_co


def conv_ffn(h, prefix, w_up, conv_w, conv_b, w_down):
    up = h @ w_up
    c, new_state = causal_dwconv(up, prefix, conv_w, conv_b)
    g, u = c[..., :D_FF], c[..., D_FF:]
    return (jax.nn.silu(g) * u) @ w_down, new_state


def setup_inputs(seed: int = 0) -> dict:
    key = jax.random.key(seed)
    ks = jax.random.split(key, 40)
    n_pages = PAST_LEN // PAGE_SIZE
    n_used = DEC_BATCH * n_pages
    n_phys = n_used + n_used // 4
    f32 = jnp.float32

    def nrm(k, shape, scale=1.0):
        return jax.random.normal(k, shape, f32) * scale

    page_table = jax.random.permutation(ks[0], n_phys)[:n_used].reshape(DEC_BATCH, n_pages).astype(jnp.int32)
    return {
        "x_prompt": nrm(ks[1], (BATCH, SEQ, D_MODEL)),
        "x_sample": nrm(ks[2], (DEC_BATCH, DEC_SEQ, D_MODEL)),
        "mem_prompt": nrm(ks[3], (BATCH, N_MEM, D_MODEL)),
        "cache_k": nrm(ks[4], (DEPTH, n_phys, PAGE_SIZE, N_HEADS, 2 * HEAD_DK)),
        "cache_v": nrm(ks[5], (DEPTH, n_phys, PAGE_SIZE, N_HEADS, HEAD_DV)),
        "page_table": page_table,
        "state_conv_mix": nrm(ks[6], (DEPTH, DEC_BATCH, CONV_W - 1, D_CONV)),
        "state_conv_ffn": nrm(ks[7], (DEPTH, DEC_BATCH, FFN_W - 1, 2 * D_FF)),
        "cache_mem_k": nrm(ks[8], (DEPTH, DEC_BATCH, N_MEM, MEM_HEADS, MEM_DH)),
        "cache_mem_v": nrm(ks[9], (DEPTH, DEC_BATCH, N_MEM, MEM_HEADS, MEM_DH)),
        "rel_bias": nrm(ks[10], (REL_BUCKETS, N_HEADS), 0.5),
        "norm_mix_w": 1.0 + nrm(ks[11], (DEPTH, D_MODEL), 0.02),
        "w_in": nrm(ks[12], (DEPTH, D_MODEL, D_IN), D_MODEL ** -0.5),
        "conv_mix_w": nrm(ks[13], (DEPTH, CONV_W, D_CONV), CONV_W ** -0.5),
        "conv_mix_b": nrm(ks[14], (DEPTH, D_CONV), 0.02),
        "lambda_q1": nrm(ks[15], (DEPTH, HEAD_DK), 0.1),
        "lambda_k1": nrm(ks[16], (DEPTH, HEAD_DK), 0.1),
        "lambda_q2": nrm(ks[17], (DEPTH, HEAD_DK), 0.1),
        "lambda_k2": nrm(ks[18], (DEPTH, HEAD_DK), 0.1),
        "subln_w": 1.0 + nrm(ks[19], (DEPTH, HEAD_DV), 0.02),
        "w_out": nrm(ks[20], (DEPTH, D_MIX, D_MODEL), D_MIX ** -0.5),
        "norm_cross_w": 1.0 + nrm(ks[21], (DEPTH, D_MODEL), 0.02),
        "norm_mem_w": 1.0 + nrm(ks[22], (DEPTH, D_MODEL), 0.02),
        "w_cq": nrm(ks[23], (DEPTH, D_MODEL, D_MODEL), D_MODEL ** -0.5),
        "w_ckv": nrm(ks[24], (DEPTH, D_MODEL, 2 * D_MODEL), D_MODEL ** -0.5),
        "w_co": nrm(ks[25], (DEPTH, D_MODEL, D_MODEL), D_MODEL ** -0.5),
        "norm_ffn_w": 1.0 + nrm(ks[26], (DEPTH, D_MODEL), 0.02),
        "w_up": nrm(ks[27], (DEPTH, D_MODEL, 2 * D_FF), D_MODEL ** -0.5),
        "conv_ffn_w": nrm(ks[28], (DEPTH, FFN_W, 2 * D_FF), FFN_W ** -0.5),
        "conv_ffn_b": nrm(ks[29], (DEPTH, 2 * D_FF), 0.02),
        "w_down": nrm(ks[30], (DEPTH, D_FF, D_MODEL), D_FF ** -0.5),
        "norm_final_w": 1.0 + nrm(ks[31], (D_MODEL,), 0.02),
    }


def reference(x_prompt, x_sample, mem_prompt, cache_k, cache_v, page_table, state_conv_mix,
              state_conv_ffn, cache_mem_k, cache_mem_v, rel_bias, norm_mix_w, w_in, conv_mix_w,
              conv_mix_b, lambda_q1, lambda_k1, lambda_q2, lambda_k2, subln_w, w_out,
              norm_cross_w, norm_mem_w, w_cq, w_ckv, w_co, norm_ffn_w, w_up, conv_ffn_w,
              conv_ffn_b, w_down, norm_final_w):
    n_pages = PAST_LEN // PAGE_SIZE
    bp, sp = x_prompt.shape[0], x_prompt.shape[1]
    bs, ts = x_sample.shape[0], x_sample.shape[1]
    xp, xs = x_prompt, x_sample
    kp_l, vp_l, ks_l, vs_l = [], [], [], []
    cmp_l, cms_l, cfp_l, cfs_l, mkp_l, mvp_l = [], [], [], [], [], []
    for l in range(DEPTH):
        lam_init = 0.8 - 0.6 * math.exp(-0.3 * l)
        lam = (jnp.exp(jnp.sum(lambda_q1[l].astype(jnp.float32) * lambda_k1[l].astype(jnp.float32)))
               - jnp.exp(jnp.sum(lambda_q2[l].astype(jnp.float32) * lambda_k2[l].astype(jnp.float32)))
               + lam_init)

        gbp, gcp, up_, qp, kp, vp = split_mix(rmsnorm(xp, norm_mix_w[l]) @ w_in[l])
        gbs, gcs, us, qs, kn, vn = split_mix(rmsnorm(xs, norm_mix_w[l]) @ w_in[l])

        pre_p = gcp * up_
        cp, conv_mix_p = causal_dwconv(pre_p, jnp.zeros((bp, CONV_W - 1, D_CONV), pre_p.dtype),
                                       conv_mix_w[l], conv_mix_b[l])
        yconv_p = gbp * cp
        cs, conv_mix_s = causal_dwconv(gcs * us, state_conv_mix[l], conv_mix_w[l], conv_mix_b[l])
        yconv_s = gbs * cs

        qp5 = qp.reshape(bp, sp, N_HEADS, 2, HEAD_DK)
        kp4 = kp.reshape(bp, sp, N_HEADS, 2 * HEAD_DK)
        vp4 = vp.reshape(bp, sp, N_HEADS, HEAD_DV)
        op = diff_attn_prompt(qp5, kp4.reshape(bp, sp, N_HEADS, 2, HEAD_DK), vp4, rel_bias, lam)
        yattn_p = head_norm(op, subln_w[l], lam_init)

        qs5 = qs.reshape(bs, ts, N_HEADS, 2, HEAD_DK)
        kn4 = kn.reshape(bs, ts, N_HEADS, 2 * HEAD_DK)
        vn4 = vn.reshape(bs, ts, N_HEADS, HEAD_DV)
        k_past = cache_k[l][page_table].reshape(bs, n_pages * PAGE_SIZE, N_HEADS, 2 * HEAD_DK)
        v_past = cache_v[l][page_table].reshape(bs, n_pages * PAGE_SIZE, N_HEADS, HEAD_DV)
        k_all = jnp.concatenate([k_past.astype(kn4.dtype), kn4], axis=1)
        v_all = jnp.concatenate([v_past.astype(vn4.dtype), vn4], axis=1)
        q_pos = PAST_LEN + jnp.arange(ts)
        k_pos = jnp.arange(PAST_LEN + ts)
        os_ = diff_attn_core(qs5, k_all.reshape(bs, PAST_LEN + ts, N_HEADS, 2, HEAD_DK), v_all,
                             q_pos, k_pos, rel_bias, lam)
        yattn_s = head_norm(os_, subln_w[l], lam_init)

        xp = xp + jnp.concatenate([yconv_p, yattn_p], axis=-1) @ w_out[l]
        xs = xs + jnp.concatenate([yconv_s, yattn_s], axis=-1) @ w_out[l]

        mkp, mvp = memory_kv(mem_prompt, norm_mem_w[l], w_ckv[l])
        xp = xp + cross_attn(rmsnorm(xp, norm_cross_w[l]), mkp, mvp, w_cq[l], w_co[l])
        xs = xs + cross_attn(rmsnorm(xs, norm_cross_w[l]), cache_mem_k[l].astype(xs.dtype),
                             cache_mem_v[l].astype(xs.dtype), w_cq[l], w_co[l])

        fp, conv_ffn_p = conv_ffn(rmsnorm(xp, norm_ffn_w[l]),
                                  jnp.zeros((bp, FFN_W - 1, 2 * D_FF), xp.dtype),
                                  w_up[l], conv_ffn_w[l], conv_ffn_b[l], w_down[l])
        fs, conv_ffn_s = conv_ffn(rmsnorm(xs, norm_ffn_w[l]), state_conv_ffn[l],
                                  w_up[l], conv_ffn_w[l], conv_ffn_b[l], w_down[l])
        xp = xp + fp
        xs = xs + fs

        kp_l.append(kp4); vp_l.append(vp4); ks_l.append(kn4); vs_l.append(vn4)
        cmp_l.append(conv_mix_p); cms_l.append(conv_mix_s)
        cfp_l.append(conv_ffn_p); cfs_l.append(conv_ffn_s)
        mkp_l.append(mkp); mvp_l.append(mvp)

    y_prompt = rmsnorm(xp, norm_final_w)
    y_sample = rmsnorm(xs, norm_final_w)
    return (y_prompt, y_sample, jnp.stack(kp_l), jnp.stack(vp_l), jnp.stack(ks_l), jnp.stack(vs_l),
            jnp.stack(cmp_l), jnp.stack(cms_l), jnp.stack(cfp_l), jnp.stack(cfs_l),
            jnp.stack(mkp_l), jnp.stack(mvp_l))
```

```python
import functools
import math

import numpy as np
import jax
import jax.numpy as jnp
from jax import lax
from jax.experimental import pallas as pl
from jax.experimental.pallas import tpu as pltpu

F32 = jnp.float32
BF16 = jnp.bfloat16

EPS = 1e-6
NEG = -1e30
LANES = 128
SUBLANES = 8
N_HEADS = 4
HEAD_DK = 64
HEAD_DV = 2 * HEAD_DK
MEM_HEADS = 4
CONV_W = 3
REL_BUCKETS = 32
REL_MAX_EXACT = 16
REL_MAX_DIST = 128
VMEM_LIMIT = 56 * 1024 * 1024

ATTN_TILE = 512
ROW_TILE = 512
FFN_ROW_TILE = 256
PAGES_PER_STEP = 8


def _params(*sem):
    return pltpu.CompilerParams(dimension_semantics=sem, vmem_limit_bytes=VMEM_LIMIT)


def _rms(x, w):
    return x * lax.rsqrt(jnp.mean(x * x, axis=-1, keepdims=True) + EPS) * w


def _dot(a, b):
    return jnp.dot(a, b, preferred_element_type=F32)


def _dot_nt(a, b):
    return lax.dot_general(a, b, (((1,), (1,)), ((), ())), preferred_element_type=F32)


def _lam(lq1, lk1, lq2, lk2, lam_init):
    return (jnp.exp(jnp.sum(lq1[...] * lk1[...], axis=-1, keepdims=True))
            - jnp.exp(jnp.sum(lq2[...] * lk2[...], axis=-1, keepdims=True)) + lam_init)


def _bucket_np(rel):
    n = np.maximum(rel, 0)
    nf = np.maximum(n, 1).astype(np.float32)
    large = REL_MAX_EXACT + (np.log(nf / np.float32(REL_MAX_EXACT))
                             / np.float32(math.log(REL_MAX_DIST / REL_MAX_EXACT))
                             * np.float32(REL_BUCKETS - REL_MAX_EXACT)).astype(np.int32)
    large = np.minimum(large, REL_BUCKETS - 1)
    return np.where(n < REL_MAX_EXACT, n, large).astype(np.int32)


def _softmax_update(chunks, m_prev, l_prev):
    mx = chunks[0]
    for c in chunks[1:]:
        mx = jnp.maximum(mx, c)
    m_next = jnp.maximum(m_prev, jnp.max(mx, axis=1, keepdims=True))
    ps = [jnp.exp(c - m_next) for c in chunks]
    sm = ps[0]
    for p in ps[1:]:
        sm = sm + p
    alpha = jnp.exp(m_prev - m_next)
    l_next = alpha * l_prev + jnp.sum(sm, axis=1, keepdims=True)
    return ps, m_next, l_next, alpha


def _head_out(acc1, l1, acc2, l2, lam, sw, lam_init):
    o = acc1 / l1 - lam * (acc2 / l2)
    return _rms(o, sw) * (1.0 - lam_init)


def _mix_in_prompt_kernel(x_ref, nw_ref, w_ref, cw_ref, cb_ref,
                          yconv_ref, q_ref, kf_ref, vf_ref, kb_ref, vb_ref, st_ref,
                          pre_scr, *, dc, da):
    j = pl.program_id(1)
    tm = x_ref.shape[0]
    h = _rms(x_ref[...], nw_ref[...]).astype(BF16)

    def proj(lo, width):
        return _dot(h, w_ref[:, lo:lo + width])

    @pl.when(j == 0)
    def _zero_prefix():
        pre_scr[0:SUBLANES, :] = jnp.zeros((SUBLANES, dc), F32)

    pre = proj(dc, dc) * proj(2 * dc, dc)
    pre_scr[SUBLANES:SUBLANES + tm, :] = pre
    cw = cw_ref[...]
    conv = (cb_ref[...] + cw[0:1, :] * pre_scr[SUBLANES - 2:SUBLANES - 2 + tm, :]
            + cw[1:2, :] * pre_scr[SUBLANES - 1:SUBLANES - 1 + tm, :] + cw[2:3, :] * pre)
    yconv_ref[...] = (proj(0, dc) * conv).astype(BF16)
    st_ref[...] = pre[tm - 2:tm, :]
    pre_scr[0:SUBLANES, :] = pre[tm - SUBLANES:tm, :]

    q_ref[...] = (proj(3 * dc, da) * (HEAD_DK ** -0.5)).astype(BF16)
    k = proj(3 * dc + da, da)
    kf_ref[...] = k
    kb_ref[...] = k.astype(BF16)
    v = proj(3 * dc + 2 * da, da)
    vf_ref[...] = v
    vb_ref[...] = v.astype(BF16)


def _mix_in_prompt(x2, nw, w_in, cw, cb, *, batch, seq, dc, da):
    d = x2.shape[1]
    tm = min(ROW_TILE, seq)
    nj = seq // tm
    rows = lambda b, j: (b * nj + j, 0)
    const = lambda b, j: (0, 0)
    n = batch * seq
    return pl.pallas_call(
        functools.partial(_mix_in_prompt_kernel, dc=dc, da=da),
        grid=(batch, nj),
        in_specs=[pl.BlockSpec((tm, d), rows),
                  pl.BlockSpec((1, d), const),
                  pl.BlockSpec(w_in.shape, const),
                  pl.BlockSpec(cw.shape, const),
                  pl.BlockSpec((1, dc), const)],
        out_specs=[pl.BlockSpec((tm, dc), rows),
                   pl.BlockSpec((tm, da), rows),
                   pl.BlockSpec((tm, da), rows),
                   pl.BlockSpec((tm, da), rows),
                   pl.BlockSpec((tm, da), rows),
                   pl.BlockSpec((tm, da), rows),
                   pl.BlockSpec((None, CONV_W - 1, dc), lambda b, j: (b, 0, 0))],
        out_shape=[jax.ShapeDtypeStruct((n, dc), BF16),
                   jax.ShapeDtypeStruct((n, da), BF16),
                   jax.ShapeDtypeStruct((n, da), F32),
                   jax.ShapeDtypeStruct((n, da), F32),
                   jax.ShapeDtypeStruct((n, da), BF16),
                   jax.ShapeDtypeStruct((n, da), BF16),
                   jax.ShapeDtypeStruct((batch, CONV_W - 1, dc), F32)],
        scratch_shapes=[pltpu.VMEM((tm + SUBLANES, dc), F32)],
        compiler_params=_params("arbitrary", "arbitrary"),
    )(x2, nw, w_in, cw, cb)


def _conv_time_major(slabs, cw, cb):
    return [cb + cw[0:1, :] * slabs[t] + cw[1:2, :] * slabs[t + 1] + cw[2:3, :] * slabs[t + 2]
            for t in range(len(slabs) - 2)]


def _mix_in_sample_kernel(x_ref, nw_ref, w_ref, cw_ref, cb_ref, s0_ref, s1_ref,
                          yconv_ref, qm_ref, kf_ref, vf_ref, st_ref, *, dc, da, ts, bs):
    h = _rms(x_ref[...], nw_ref[...]).astype(BF16)

    def proj(lo, width):
        return _dot(h, w_ref[:, lo:lo + width])

    pre = proj(dc, dc) * proj(2 * dc, dc)
    gate = proj(0, dc)
    slabs = [s0_ref[...], s1_ref[...]] + [pre[t * bs:(t + 1) * bs, :] for t in range(ts)]
    conv = _conv_time_major(slabs, cw_ref[...], cb_ref[...])
    for t in range(ts):
        yconv_ref[t * bs:(t + 1) * bs, :] = (gate[t * bs:(t + 1) * bs, :] * conv[t]).astype(BF16)
    st_ref[0] = slabs[-2]
    st_ref[1] = slabs[-1]

    q = proj(3 * dc, da) * (HEAD_DK ** -0.5)
    lane = lax.broadcasted_iota(jnp.int32, q.shape, 1) % HEAD_DV
    qm_ref[0] = jnp.where(lane < HEAD_DK, q, 0.0).astype(BF16)
    qm_ref[1] = jnp.where(lane >= HEAD_DK, q, 0.0).astype(BF16)
    kf_ref[...] = proj(3 * dc + da, da)
    vf_ref[...] = proj(3 * dc + 2 * da, da)


def _mix_in_sample(xt, nw, w_in, cw, cb, s0, s1, *, dc, da, ts, bs):
    n = ts * bs
    return pl.pallas_call(
        functools.partial(_mix_in_sample_kernel, dc=dc, da=da, ts=ts, bs=bs),
        out_shape=[jax.ShapeDtypeStruct((n, dc), BF16),
                   jax.ShapeDtypeStruct((2, n, da), BF16),
                   jax.ShapeDtypeStruct((n, da), F32),
                   jax.ShapeDtypeStruct((n, da), F32),
                   jax.ShapeDtypeStruct((CONV_W - 1, bs, dc), F32)],
        compiler_params=pltpu.CompilerParams(vmem_limit_bytes=VMEM_LIMIT),
    )(xt, nw, w_in, cw, cb, s0, s1)


def _mem_kv_kernel(m_ref, nw_ref, w_ref, kf_ref, vf_ref, kb_ref, vb_ref, *, d):
    h = _rms(m_ref[...], nw_ref[...]).astype(BF16)
    k = _dot(h, w_ref[:, 0:d])
    kf_ref[...] = k
    kb_ref[...] = k.astype(BF16)
    v = _dot(h, w_ref[:, d:2 * d])
    vf_ref[...] = v
    vb_ref[...] = v.astype(BF16)


def _mem_kv(mem2, nw, w_ckv, *, batch, n_mem):
    d = mem2.shape[1]
    rows = lambda b: (b, 0)
    const = lambda b: (0, 0)
    n = batch * n_mem
    return pl.pallas_call(
        functools.partial(_mem_kv_kernel, d=d),
        grid=(batch,),
        in_specs=[pl.BlockSpec((n_mem, d), rows), pl.BlockSpec((1, d), const),
                  pl.BlockSpec(w_ckv.shape, const)],
        out_specs=[pl.BlockSpec((n_mem, d), rows)] * 4,
        out_shape=[jax.ShapeDtypeStruct((n, d), F32), jax.ShapeDtypeStruct((n, d), F32),
                   jax.ShapeDtypeStruct((n, d), BF16), jax.ShapeDtypeStruct((n, d), BF16)],
        compiler_params=_params("arbitrary"),
    )(mem2, nw, w_ckv)


def _attn_prompt_kernel(q_ref, k_ref, v_ref, d_ref, lq1, lk1, lq2, lk2, sw_ref, o_ref,
                        qp_scr, m_scr, l_scr, acc_scr, bd_scr, bs_scr, *, T, lam_init):
    qi = pl.program_id(2)
    nb = T // LANES

    @pl.when(qi == 0)
    def _assemble_bias_tiles():
        d0 = d_ref[0]
        d1 = d_ref[1]
        zero = jnp.zeros((LANES, LANES), F32)
        neg = jnp.full((LANES, LANES), NEG, F32)
        for bi in range(nb):
            for bj in range(nb):
                rs = slice(bi * LANES, (bi + 1) * LANES)
                cs = slice(bj * LANES, (bj + 1) * LANES)
                bd_scr[rs, cs] = d0 if bi == bj else d1 if bi == bj + 1 else zero if bi > bj else neg
                bs_scr[rs, cs] = d1 if (bi == 0 and bj == nb - 1) else zero

    qh = q_ref[...].astype(F32)
    lane = lax.broadcasted_iota(jnp.int32, qh.shape, 1)
    qp_scr[0:T, :] = jnp.where(lane < HEAD_DK, qh, 0.0).astype(BF16)
    qp_scr[T:2 * T, :] = jnp.where(lane >= HEAD_DK, qh, 0.0).astype(BF16)
    m_scr[...] = jnp.full(m_scr.shape, -jnp.inf, F32)
    l_scr[...] = jnp.zeros(l_scr.shape, F32)
    acc_scr[...] = jnp.zeros(acc_scr.shape, F32)

    def step(start, bias_ref):
        kh = k_ref[pl.ds(start, T), :]
        vh = v_ref[pl.ds(start, T), :]
        s = _dot_nt(qp_scr[...], kh)
        if bias_ref is not None:
            b = bias_ref[...]
            s = s + jnp.concatenate([b, b], axis=0)
        chunks = [s[:, c * LANES:(c + 1) * LANES] for c in range(nb)]
        ps, m_next, l_next, alpha = _softmax_update(chunks, m_scr[...], l_scr[...])
        p = jnp.concatenate([x.astype(BF16) for x in ps], axis=1)
        acc_scr[...] = alpha * acc_scr[...] + _dot(p, vh)
        m_scr[...] = m_next
        l_scr[...] = l_next

    def far_body(j, carry):
        step(pl.multiple_of(j * T, T), None)
        return carry

    lax.fori_loop(0, jnp.maximum(qi - 1, 0), far_body, 0)

    @pl.when(qi >= 1)
    def _sub_diagonal():
        step(pl.multiple_of((qi - 1) * T, T), bs_scr)

    step(pl.multiple_of(qi * T, T), bd_scr)

    lam = _lam(lq1, lk1, lq2, lk2, lam_init)
    l = l_scr[...]
    acc = acc_scr[...]
    o_ref[...] = _head_out(acc[0:T], l[0:T], acc[T:2 * T], l[T:2 * T], lam, sw_ref[...],
                           lam_init).astype(o_ref.dtype)


def _attn_prompt(q, kb, vb, dtab, lq1, lk1, lq2, lk2, sw, *, batch, seq, lam_init):
    T = min(ATTN_TILE, seq)
    nq = seq // T
    da = q.shape[1]
    k3 = kb.reshape(batch, seq, da)
    v3 = vb.reshape(batch, seq, da)
    vec = lambda b, h, i: (0, 0)
    return pl.pallas_call(
        functools.partial(_attn_prompt_kernel, T=T, lam_init=lam_init),
        grid=(batch, N_HEADS, nq),
        in_specs=[pl.BlockSpec((T, HEAD_DV), lambda b, h, i: (b * nq + i, h)),
                  pl.BlockSpec((None, seq, HEAD_DV), lambda b, h, i: (b, 0, h)),
                  pl.BlockSpec((None, seq, HEAD_DV), lambda b, h, i: (b, 0, h)),
                  pl.BlockSpec((None, 2, LANES, LANES), lambda b, h, i: (h, 0, 0, 0)),
                  pl.BlockSpec((1, HEAD_DK), vec), pl.BlockSpec((1, HEAD_DK), vec),
                  pl.BlockSpec((1, HEAD_DK), vec), pl.BlockSpec((1, HEAD_DK), vec),
                  pl.BlockSpec((1, HEAD_DV), vec)],
        out_specs=pl.BlockSpec((T, HEAD_DV), lambda b, h, i: (b * nq + i, h)),
        out_shape=jax.ShapeDtypeStruct((batch * seq, da), BF16),
        scratch_shapes=[pltpu.VMEM((2 * T, HEAD_DV), BF16),
                        pltpu.VMEM((2 * T, LANES), F32),
                        pltpu.VMEM((2 * T, LANES), F32),
                        pltpu.VMEM((2 * T, HEAD_DV), F32),
                        pltpu.VMEM((T, T), F32),
                        pltpu.VMEM((T, T), F32)],
        compiler_params=_params("arbitrary", "arbitrary", "arbitrary"),
    )(q, k3, v3, dtab, lq1, lk1, lq2, lk2, sw)


def _attn_sample_kernel(pt_ref, qp_ref, kn_ref, vn_ref, bl_ref, bn_ref, lq1, lk1, lq2, lk2, sw_ref,
                        *rest, pages, ts, lam_init):
    del pt_ref
    k_refs = rest[:pages]
    v_refs = rest[pages:2 * pages]
    o_ref = rest[2 * pages]
    m_scr, l_scr, acc_scr = rest[2 * pages + 1:]
    g = pl.program_id(1)
    last = g == pl.num_programs(1) - 1
    rm = qp_ref.shape[1] // 2

    @pl.when(g == 0)
    def _init():
        m_scr[...] = jnp.full(m_scr.shape, -jnp.inf, F32)
        l_scr[...] = jnp.zeros(l_scr.shape, F32)
        acc_scr[...] = jnp.zeros(acc_scr.shape, F32)

    def update(h, chunks, values):
        ps, m_next, l_next, alpha = _softmax_update(chunks, m_scr[h], l_scr[h])
        pv = _dot(ps[0].astype(BF16), values[0])
        for p, v in zip(ps[1:], values[1:]):
            pv = pv + _dot(p.astype(BF16), v)
        acc_scr[h] = alpha * acc_scr[h] + pv
        m_scr[h] = m_next
        l_scr[h] = l_next

    last_scale = jnp.where(last, 1.0, 0.0)
    for h in range(N_HEADS):
        cols = slice(h * HEAD_DV, (h + 1) * HEAD_DV)
        qp = qp_ref[h]
        chunks = [_dot_nt(qp, k_refs[i][:, cols].astype(BF16)) for i in range(pages)]
        chunks[-1] = chunks[-1] + last_scale * bl_ref[h]
        update(h, chunks, [v_refs[i][:, cols].astype(BF16) for i in range(pages)])

    @pl.when(last)
    def _new_rows_and_output():
        lam = _lam(lq1, lk1, lq2, lk2, lam_init)
        for h in range(N_HEADS):
            cols = slice(h * HEAD_DV, (h + 1) * HEAD_DV)
            update(h, [_dot_nt(qp_ref[h], kn_ref[h]) + bn_ref[h]], [vn_ref[h]])
            acc = acc_scr[h]
            l = l_scr[h]
            out = _head_out(acc[0:rm], l[0:rm], acc[rm:2 * rm], l[rm:2 * rm], lam, sw_ref[...], lam_init)
            o_ref[:, cols] = out[0:ts, :]


def _attn_sample(page_table, qpad, knew, vnew, bias_last, bias_new, lq1, lk1, lq2, lk2, sw,
                 cache_k, cache_v, *, ts, lam_init):
    bs, n_pages = page_table.shape
    page = cache_k.shape[1]
    da = cache_k.shape[2]
    pages = math.gcd(PAGES_PER_STEP, n_pages)
    rows = qpad.shape[2]
    vec = lambda b, g, pt: (0, 0)
    per_b = lambda b, g, pt: (b, 0, 0, 0)
    tab = lambda b, g, pt: (0, 0, 0)

    def page_spec(i):
        return pl.BlockSpec((None, page, da), lambda b, g, pt: (pt[b, g * pages + i], 0, 0))

    grid_spec = pltpu.PrefetchScalarGridSpec(
        num_scalar_prefetch=1,
        grid=(bs, n_pages // pages),
        in_specs=[pl.BlockSpec((None, N_HEADS, rows, HEAD_DV), per_b),
                  pl.BlockSpec((None, N_HEADS, page, HEAD_DV), per_b),
                  pl.BlockSpec((None, N_HEADS, page, HEAD_DV), per_b),
                  pl.BlockSpec(bias_last.shape, tab),
                  pl.BlockSpec(bias_new.shape, tab),
                  pl.BlockSpec((1, HEAD_DK), vec), pl.BlockSpec((1, HEAD_DK), vec),
                  pl.BlockSpec((1, HEAD_DK), vec), pl.BlockSpec((1, HEAD_DK), vec),
                  pl.BlockSpec((1, HEAD_DV), vec)]
                 + [page_spec(i) for i in range(pages)] * 2,
        out_specs=pl.BlockSpec((None, ts, da), lambda b, g, pt: (b, 0, 0)),
        scratch_shapes=[pltpu.VMEM((N_HEADS, rows, LANES), F32),
                        pltpu.VMEM((N_HEADS, rows, LANES), F32),
                        pltpu.VMEM((N_HEADS, rows, HEAD_DV), F32)],
    )
    return pl.pallas_call(
        functools.partial(_attn_sample_kernel, pages=pages, ts=ts, lam_init=lam_init),
        grid_spec=grid_spec,
        out_shape=jax.ShapeDtypeStruct((bs, ts, da), F32),
        compiler_params=_params("arbitrary", "arbitrary"),
    )(page_table, qpad, knew, vnew, bias_last, bias_new, lq1, lk1, lq2, lk2, sw,
      *([cache_k] * pages), *([cache_v] * pages))


def _cross_heads(qc, mk_ref, mv_ref, o_scr, dh):
    for hd in range(MEM_HEADS):
        cols = slice(hd * dh, (hd + 1) * dh)
        s = _dot_nt(qc[:, cols], mk_ref[:, cols].astype(BF16))
        p = jnp.exp(s - jnp.max(s, axis=1, keepdims=True))
        o = _dot(p.astype(BF16), mv_ref[:, cols].astype(BF16)) / jnp.sum(p, axis=1, keepdims=True)
        o_scr[:, cols] = o.astype(BF16)


def _cross_prompt_kernel(x_ref, yc_ref, ya_ref, wo_ref, nw_ref, wq_ref, mk_ref, mv_ref, wc_ref,
                         o_ref, o_scr, *, dc, dh):
    x1 = x_ref[...] + _dot(yc_ref[...], wo_ref[0:dc, :]) + _dot(ya_ref[...], wo_ref[dc:, :])
    h = _rms(x1, nw_ref[...]).astype(BF16)
    qc = (_dot(h, wq_ref[...]) * (dh ** -0.5)).astype(BF16)
    _cross_heads(qc, mk_ref, mv_ref, o_scr, dh)
    o_ref[...] = x1 + _dot(o_scr[...], wc_ref[...])


def _cross_prompt(x2, yconv, yattn, w_out, nw, w_cq, mk, mv, w_co, *, batch, seq, n_mem):
    d = x2.shape[1]
    dc = yconv.shape[1]
    da = yattn.shape[1]
    tm = min(ROW_TILE, seq)
    nj = seq // tm
    rows = lambda b, j: (b * nj + j, 0)
    const = lambda b, j: (0, 0)
    memb = lambda b, j: (b, 0)
    return pl.pallas_call(
        functools.partial(_cross_prompt_kernel, dc=dc, dh=d // MEM_HEADS),
        grid=(batch, nj),
        in_specs=[pl.BlockSpec((tm, d), rows), pl.BlockSpec((tm, dc), rows), pl.BlockSpec((tm, da), rows),
                  pl.BlockSpec(w_out.shape, const), pl.BlockSpec((1, d), const),
                  pl.BlockSpec(w_cq.shape, const),
                  pl.BlockSpec((n_mem, d), memb), pl.BlockSpec((n_mem, d), memb),
                  pl.BlockSpec(w_co.shape, const)],
        out_specs=pl.BlockSpec((tm, d), rows),
        out_shape=jax.ShapeDtypeStruct(x2.shape, F32),
        scratch_shapes=[pltpu.VMEM((tm, d), BF16)],
        compiler_params=_params("arbitrary", "arbitrary"),
    )(x2, yconv, yattn, w_out, nw, w_cq, mk, mv, w_co)


def _outproj_q_sample_kernel(x_ref, yc_ref, ya_ref, wo_ref, nw_ref, wq_ref, x1_ref, qc_ref, *, dc, dh):
    x1 = x_ref[...] + _dot(yc_ref[...], wo_ref[0:dc, :]) + _dot(ya_ref[...], wo_ref[dc:, :])
    x1_ref[...] = x1
    h = _rms(x1, nw_ref[...]).astype(BF16)
    qc_ref[...] = (_dot(h, wq_ref[...]) * (dh ** -0.5)).astype(BF16)


def _outproj_q_sample(xt, yconv, yattn, w_out, nw, w_cq):
    d = xt.shape[1]
    return pl.pallas_call(
        functools.partial(_outproj_q_sample_kernel, dc=yconv.shape[1], dh=d // MEM_HEADS),
        out_shape=[jax.ShapeDtypeStruct(xt.shape, F32), jax.ShapeDtypeStruct(xt.shape, BF16)],
        compiler_params=pltpu.CompilerParams(vmem_limit_bytes=VMEM_LIMIT),
    )(xt, yconv, yattn, w_out, nw, w_cq)


def _cross_sample_kernel(qc_ref, mk_ref, mv_ref, o_ref, *, dh):
    _cross_heads(qc_ref[...], mk_ref, mv_ref, o_ref, dh)


def _cross_sample(qc, mk, mv):
    bs, rows, d = qc.shape
    n_mem = mk.shape[1]
    per_b = lambda b: (b, 0, 0)
    return pl.pallas_call(
        functools.partial(_cross_sample_kernel, dh=d // MEM_HEADS),
        grid=(bs,),
        in_specs=[pl.BlockSpec((None, rows, d), per_b), pl.BlockSpec((None, n_mem, d), per_b),
                  pl.BlockSpec((None, n_mem, d), per_b)],
        out_specs=pl.BlockSpec((None, rows, d), per_b),
        out_shape=jax.ShapeDtypeStruct(qc.shape, BF16),
        compiler_params=_params("arbitrary"),
    )(qc, mk, mv)


def _silu(g):
    return g * (1.0 / (1.0 + jnp.exp(-g)))


def _ffn_prompt_kernel(x_ref, nw_ref, wu_ref, cw_ref, cb_ref, wd_ref, fw_ref, y_ref, st_ref,
                       up_scr, carry_scr, *, dff, final):
    j = pl.program_id(1)
    tm = x_ref.shape[0]
    x = x_ref[...]
    h = _rms(x, nw_ref[...]).astype(BF16)

    @pl.when(j == 0)
    def _zero_prefix():
        carry_scr[...] = jnp.zeros(carry_scr.shape, F32)

    def conv_half(lo):
        cols = slice(lo, lo + dff)
        up = _dot(h, wu_ref[:, cols])
        up_scr[0:SUBLANES, :] = carry_scr[:, cols]
        up_scr[SUBLANES:SUBLANES + tm, :] = up
        cw = cw_ref[:, cols]
        conv = (cb_ref[:, cols] + cw[0:1, :] * up_scr[SUBLANES - 2:SUBLANES - 2 + tm, :]
                + cw[1:2, :] * up_scr[SUBLANES - 1:SUBLANES - 1 + tm, :] + cw[2:3, :] * up)
        carry_scr[:, cols] = up[tm - SUBLANES:tm, :]
        st_ref[:, cols] = up[tm - 2:tm, :]
        return conv

    g = conv_half(0)
    u = conv_half(dff)
    x3 = x + _dot((_silu(g) * u).astype(BF16), wd_ref[...])
    y_ref[...] = _rms(x3, fw_ref[...]) if final else x3


def _ffn_prompt(x2, nw, w_up, cw, cb, w_down, fw, *, batch, seq, final):
    d = x2.shape[1]
    dff = w_down.shape[0]
    tm = min(FFN_ROW_TILE, seq)
    nj = seq // tm
    rows = lambda b, j: (b * nj + j, 0)
    const = lambda b, j: (0, 0)
    return pl.pallas_call(
        functools.partial(_ffn_prompt_kernel, dff=dff, final=final),
        grid=(batch, nj),
        in_specs=[pl.BlockSpec((tm, d), rows), pl.BlockSpec((1, d), const),
                  pl.BlockSpec(w_up.shape, const), pl.BlockSpec(cw.shape, const),
                  pl.BlockSpec((1, 2 * dff), const), pl.BlockSpec(w_down.shape, const),
                  pl.BlockSpec((1, d), const)],
        out_specs=[pl.BlockSpec((tm, d), rows),
                   pl.BlockSpec((None, CONV_W - 1, 2 * dff), lambda b, j: (b, 0, 0))],
        out_shape=[jax.ShapeDtypeStruct(x2.shape, F32),
                   jax.ShapeDtypeStruct((batch, CONV_W - 1, 2 * dff), F32)],
        scratch_shapes=[pltpu.VMEM((tm + SUBLANES, dff), F32),
                        pltpu.VMEM((SUBLANES, 2 * dff), F32)],
        compiler_params=_params("arbitrary", "arbitrary"),
    )(x2, nw, w_up, cw, cb, w_down, fw)


def _ffn_sample_kernel(x1_ref, o_ref, wc_ref, nw_ref, wu_ref, cw_ref, cb_ref, wd_ref, fw_ref,
                       s0_ref, s1_ref, y_ref, st_ref, hid_scr, *, dff, ts, bs, final):
    x2 = x1_ref[...] + _dot(o_ref[...], wc_ref[...])
    h = _rms(x2, nw_ref[...]).astype(BF16)

    def conv_half(lo):
        cols = slice(lo, lo + dff)
        up = _dot(h, wu_ref[:, cols])
        slabs = [s0_ref[:, cols], s1_ref[:, cols]] + [up[t * bs:(t + 1) * bs, :] for t in range(ts)]
        st_ref[0, :, cols] = slabs[-2]
        st_ref[1, :, cols] = slabs[-1]
        return _conv_time_major(slabs, cw_ref[:, cols], cb_ref[:, cols])

    g = conv_half(0)
    u = conv_half(dff)
    for t in range(ts):
        hid_scr[t * bs:(t + 1) * bs, :] = (_silu(g[t]) * u[t]).astype(BF16)
    x3 = x2 + _dot(hid_scr[...], wd_ref[...])
    y_ref[...] = _rms(x3, fw_ref[...]) if final else x3


def _ffn_sample(x1, o, w_co, nw, w_up, cw, cb, w_down, fw, s0, s1, *, ts, bs, final):
    dff = w_down.shape[0]
    return pl.pallas_call(
        functools.partial(_ffn_sample_kernel, dff=dff, ts=ts, bs=bs, final=final),
        out_shape=[jax.ShapeDtypeStruct(x1.shape, F32),
                   jax.ShapeDtypeStruct((CONV_W - 1, bs, 2 * dff), F32)],
        scratch_shapes=[pltpu.VMEM((ts * bs, dff), BF16)],
        compiler_params=pltpu.CompilerParams(vmem_limit_bytes=VMEM_LIMIT),
    )(x1, o, w_co, nw, w_up, cw, cb, w_down, fw, s0, s1)


def _bias_tables(rel_bias, past_len, page, ts, rows):
    far = REL_BUCKETS - 1
    assert _bucket_np(np.arange(REL_MAX_DIST, past_len + ts + 1)).min() == far
    rb = (rel_bias.astype(F32) - rel_bias[far].astype(F32)[None, :]).T
    r = np.arange(LANES)[:, None]
    c = np.arange(LANES)[None, :]
    d0 = jnp.where(jnp.asarray(r >= c), rb[:, _bucket_np(r - c)], NEG)
    d1 = rb[:, _bucket_np(LANES + r - c)]
    dtab = jnp.stack([d0, d1], axis=1)

    rm = rows // 2
    tok = np.arange(rows) % rm
    kpos_last = past_len - page + np.arange(page)
    rel_last = (past_len + tok)[:, None] - kpos_last[None, :]
    bias_last = rb[:, _bucket_np(rel_last)]
    rel_new = tok[:, None] - np.arange(page)[None, :]
    valid = (rel_new >= 0) & (np.arange(page)[None, :] < ts)
    bias_new = jnp.where(jnp.asarray(valid), rb[:, _bucket_np(rel_new)], NEG)
    return dtab, bias_last, bias_new


def kernel(x_prompt, x_sample, mem_prompt, cache_k, cache_v, page_table, state_conv_mix, state_conv_ffn, cache_mem_k, cache_mem_v, rel_bias, norm_mix_w, w_in, conv_mix_w, conv_mix_b, lambda_q1, lambda_k1, lambda_q2, lambda_k2, subln_w, w_out, norm_cross_w, norm_mem_w, w_cq, w_ckv, w_co, norm_ffn_w, w_up, conv_ffn_w, conv_ffn_b, w_down, norm_final_w):
    depth = w_in.shape[0]
    bp, sp, d = x_prompt.shape
    bs, ts, _ = x_sample.shape
    n_mem = mem_prompt.shape[1]
    n_phys, page = cache_k.shape[1], cache_k.shape[2]
    n_pages = page_table.shape[1]
    past_len = n_pages * page
    dc = conv_mix_w.shape[2]
    da = N_HEADS * HEAD_DV
    dff = w_down.shape[1]
    dh = d // MEM_HEADS
    assert page == LANES and ts <= SUBLANES and ts >= CONV_W - 1
    rows = 2 * SUBLANES

    row = lambda a: a.reshape(1, -1).astype(F32)
    dtab, bias_last, bias_new = _bias_tables(rel_bias, past_len, page, ts, rows)

    xp = x_prompt.reshape(bp * sp, d)
    xs = x_sample.transpose(1, 0, 2).reshape(ts * bs, d)
    mem2 = mem_prompt.reshape(bp * n_mem, d)
    outs = [[] for _ in range(10)]
    for l in range(depth):
        lam_init = 0.8 - 0.6 * math.exp(-0.3 * l)
        final = l == depth - 1
        w_in_b, w_out_b = w_in[l].astype(BF16), w_out[l].astype(BF16)
        w_cq_b, w_ckv_b, w_co_b = w_cq[l].astype(BF16), w_ckv[l].astype(BF16), w_co[l].astype(BF16)
        w_up_b, w_down_b = w_up[l].astype(BF16), w_down[l].astype(BF16)
        lam_args = (row(lambda_q1[l]), row(lambda_k1[l]), row(lambda_q2[l]), row(lambda_k2[l]),
                    row(subln_w[l]))

        yconv_p, q_p, kf_p, vf_p, kb_p, vb_p, cmix_p = _mix_in_prompt(
            xp, row(norm_mix_w[l]), w_in_b, conv_mix_w[l], row(conv_mix_b[l]),
            batch=bp, seq=sp, dc=dc, da=da)
        yattn_p = _attn_prompt(q_p, kb_p, vb_p, dtab, *lam_args, batch=bp, seq=sp, lam_init=lam_init)
        mkf, mvf, mkb, mvb = _mem_kv(mem2, row(norm_mem_w[l]), w_ckv_b, batch=bp, n_mem=n_mem)
        x2_p = _cross_prompt(xp, yconv_p, yattn_p, w_out_b, row(norm_cross_w[l]), w_cq_b, mkb, mvb,
                             w_co_b, batch=bp, seq=sp, n_mem=n_mem)
        xp, cffn_p = _ffn_prompt(x2_p, row(norm_ffn_w[l]), w_up_b, conv_ffn_w[l], row(conv_ffn_b[l]),
                                 w_down_b, row(norm_final_w), batch=bp, seq=sp, final=final)

        yconv_s, qm_s, kf_s, vf_s, cmix_s = _mix_in_sample(
            xs, row(norm_mix_w[l]), w_in_b, conv_mix_w[l], row(conv_mix_b[l]),
            state_conv_mix[l][:, 0], state_conv_mix[l][:, 1], dc=dc, da=da, ts=ts, bs=bs)
        qpad = qm_s.reshape(2, ts, bs, N_HEADS, HEAD_DV).transpose(2, 3, 0, 1, 4)
        qpad = jnp.pad(qpad, ((0, 0), (0, 0), (0, 0), (0, SUBLANES - ts), (0, 0)))
        qpad = qpad.reshape(bs, N_HEADS, rows, HEAD_DV)
        to_heads = lambda a: jnp.pad(
            a.reshape(ts, bs, N_HEADS, HEAD_DV).transpose(1, 2, 0, 3).astype(BF16),
            ((0, 0), (0, 0), (0, page - ts), (0, 0)))
        yattn_s = _attn_sample(page_table, qpad, to_heads(kf_s), to_heads(vf_s), bias_last, bias_new,
                               *lam_args, cache_k[l].reshape(n_phys, page, da),
                               cache_v[l].reshape(n_phys, page, da), ts=ts, lam_init=lam_init)
        yattn_s = yattn_s.transpose(1, 0, 2).reshape(ts * bs, da).astype(BF16)
        x1_s, qc_s = _outproj_q_sample(xs, yconv_s, yattn_s, w_out_b, row(norm_cross_w[l]), w_cq_b)
        qc_b = jnp.pad(qc_s.reshape(ts, bs, d).transpose(1, 0, 2), ((0, 0), (0, rows - ts), (0, 0)))
        o_b = _cross_sample(qc_b, cache_mem_k[l].reshape(bs, n_mem, d),
                            cache_mem_v[l].reshape(bs, n_mem, d))
        o_s = o_b[:, :ts].transpose(1, 0, 2).reshape(ts * bs, d)
        xs, cffn_s = _ffn_sample(x1_s, o_s, w_co_b, row(norm_ffn_w[l]), w_up_b, conv_ffn_w[l],
                                 row(conv_ffn_b[l]), w_down_b, row(norm_final_w),
                                 state_conv_ffn[l][:, 0], state_conv_ffn[l][:, 1],
                                 ts=ts, bs=bs, final=final)

        t2b = lambda a: a.reshape(ts, bs, N_HEADS, HEAD_DV).transpose(1, 0, 2, 3)
        for lst, val in zip(outs, (
                kf_p.reshape(bp, sp, N_HEADS, 2 * HEAD_DK), vf_p.reshape(bp, sp, N_HEADS, HEAD_DV),
                t2b(kf_s), t2b(vf_s), cmix_p, cmix_s.transpose(1, 0, 2), cffn_p,
                cffn_s.transpose(1, 0, 2), mkf.reshape(bp, n_mem, MEM_HEADS, dh),
                mvf.reshape(bp, n_mem, MEM_HEADS, dh))):
            lst.append(val)

    y_prompt = xp.reshape(bp, sp, d)
    y_sample = xs.reshape(ts, bs, d).transpose(1, 0, 2)
    return (y_prompt, y_sample) + tuple(jnp.stack(o) for o in outs)
```

```python
import functools
import math

import numpy as np
import jax
import jax.numpy as jnp
from jax import lax
from jax.experimental import pallas as pl
from jax.experimental.pallas import tpu as pltpu

F32 = jnp.float32
BF16 = jnp.bfloat16

EPS = 1e-6
NEG = -1e30
LANES = 128
SUBLANES = 8
N_HEADS = 4
HEAD_DK = 64
HEAD_DV = 2 * HEAD_DK
MEM_HEADS = 4
CONV_W = 3
REL_BUCKETS = 32
REL_MAX_EXACT = 16
REL_MAX_DIST = 128
VMEM_LIMIT = 56 * 1024 * 1024

ATTN_TILE = 512
ROW_TILE = 512
FFN_ROW_TILE = 256
PAGES_PER_STEP = 8


def _params(*sem):
    return pltpu.CompilerParams(dimension_semantics=sem, vmem_limit_bytes=VMEM_LIMIT)


def _rms(x, w):
    return x * lax.rsqrt(jnp.mean(x * x, axis=-1, keepdims=True) + EPS) * w


def _dot(a, b):
    return jnp.dot(a, b, preferred_element_type=F32)


def _dot_nt(a, b):
    return lax.dot_general(a, b, (((1,), (1,)), ((), ())), preferred_element_type=F32)


def _lam(lq1, lk1, lq2, lk2, lam_init):
    return (jnp.exp(jnp.sum(lq1[...] * lk1[...], axis=-1, keepdims=True))
            - jnp.exp(jnp.sum(lq2[...] * lk2[...], axis=-1, keepdims=True)) + lam_init)


def _bucket_np(rel):
    n = np.maximum(rel, 0)
    nf = np.maximum(n, 1).astype(np.float32)
    large = REL_MAX_EXACT + (np.log(nf / np.float32(REL_MAX_EXACT))
                             / np.float32(math.log(REL_MAX_DIST / REL_MAX_EXACT))
                             * np.float32(REL_BUCKETS - REL_MAX_EXACT)).astype(np.int32)
    large = np.minimum(large, REL_BUCKETS - 1)
    return np.where(n < REL_MAX_EXACT, n, large).astype(np.int32)


def _bucket_starts():
    buckets = _bucket_np(np.arange(REL_MAX_DIST + 1))
    assert (np.diff(buckets) >= 0).all() and buckets[-1] == REL_BUCKETS - 1
    return [int(np.argmax(buckets >= k)) for k in range(REL_BUCKETS)]


def _rel_bias_tile(rel, rb_ref, h):
    far = rb_ref[REL_BUCKETS - 1, h]
    val = jnp.full(rel.shape, rb_ref[0, h] - far, F32)
    for k, start in enumerate(_bucket_starts()):
        if k > 0:
            val = jnp.where(rel >= start, rb_ref[k, h] - far, val)
    return val


def _softmax_update(chunks, m_prev, l_prev):
    mx = chunks[0]
    for c in chunks[1:]:
        mx = jnp.maximum(mx, c)
    m_next = jnp.maximum(m_prev, jnp.max(mx, axis=1, keepdims=True))
    ps = [jnp.exp(c - m_next) for c in chunks]
    sm = ps[0]
    for p in ps[1:]:
        sm = sm + p
    alpha = jnp.exp(m_prev - m_next)
    l_next = alpha * l_prev + jnp.sum(sm, axis=1, keepdims=True)
    return ps, m_next, l_next, alpha


def _head_out(acc1, l1, acc2, l2, lam, sw, lam_init):
    o = acc1 / l1 - lam * (acc2 / l2)
    return _rms(o, sw) * (1.0 - lam_init)


def _mix_in_prompt_kernel(x_ref, nw_ref, w_ref, cw_ref, cb_ref,
                          yconv_ref, q_ref, kf_ref, vf_ref, kb_ref, vb_ref, st_ref,
                          pre_scr, *, dc, da):
    j = pl.program_id(1)
    tm = x_ref.shape[0]
    h = _rms(x_ref[...], nw_ref[...]).astype(BF16)

    def proj(lo, width):
        return _dot(h, w_ref[:, lo:lo + width])

    @pl.when(j == 0)
    def _zero_prefix():
        pre_scr[0:SUBLANES, :] = jnp.zeros((SUBLANES, dc), F32)

    pre = proj(dc, dc) * proj(2 * dc, dc)
    pre_scr[SUBLANES:SUBLANES + tm, :] = pre
    cw = cw_ref[...]
    conv = (cb_ref[...] + cw[0:1, :] * pre_scr[SUBLANES - 2:SUBLANES - 2 + tm, :]
            + cw[1:2, :] * pre_scr[SUBLANES - 1:SUBLANES - 1 + tm, :] + cw[2:3, :] * pre)
    yconv_ref[...] = (proj(0, dc) * conv).astype(BF16)
    st_ref[...] = pre[tm - 2:tm, :]
    pre_scr[0:SUBLANES, :] = pre[tm - SUBLANES:tm, :]

    q_ref[...] = (proj(3 * dc, da) * (HEAD_DK ** -0.5)).astype(BF16)
    k = proj(3 * dc + da, da)
    kf_ref[...] = k
    kb_ref[...] = k.astype(BF16)
    v = proj(3 * dc + 2 * da, da)
    vf_ref[...] = v
    vb_ref[...] = v.astype(BF16)


def _mix_in_prompt(x2, nw, w_in, cw, cb, *, batch, seq, dc, da):
    d = x2.shape[1]
    tm = min(ROW_TILE, seq)
    nj = seq // tm
    rows = lambda b, j: (b * nj + j, 0)
    const = lambda b, j: (0, 0)
    n = batch * seq
    return pl.pallas_call(
        functools.partial(_mix_in_prompt_kernel, dc=dc, da=da),
        grid=(batch, nj),
        in_specs=[pl.BlockSpec((tm, d), rows),
                  pl.BlockSpec((1, d), const),
                  pl.BlockSpec(w_in.shape, const),
                  pl.BlockSpec(cw.shape, const),
                  pl.BlockSpec((1, dc), const)],
        out_specs=[pl.BlockSpec((tm, dc), rows),
                   pl.BlockSpec((tm, da), rows),
                   pl.BlockSpec((tm, da), rows),
                   pl.BlockSpec((tm, da), rows),
                   pl.BlockSpec((tm, da), rows),
                   pl.BlockSpec((tm, da), rows),
                   pl.BlockSpec((None, CONV_W - 1, dc), lambda b, j: (b, 0, 0))],
        out_shape=[jax.ShapeDtypeStruct((n, dc), BF16),
                   jax.ShapeDtypeStruct((n, da), BF16),
                   jax.ShapeDtypeStruct((n, da), F32),
                   jax.ShapeDtypeStruct((n, da), F32),
                   jax.ShapeDtypeStruct((n, da), BF16),
                   jax.ShapeDtypeStruct((n, da), BF16),
                   jax.ShapeDtypeStruct((batch, CONV_W - 1, dc), F32)],
        scratch_shapes=[pltpu.VMEM((tm + SUBLANES, dc), F32)],
        compiler_params=_params("arbitrary", "arbitrary"),
    )(x2, nw, w_in, cw, cb)


def _conv_time_major(slabs, cw, cb):
    return [cb + cw[0:1, :] * slabs[t] + cw[1:2, :] * slabs[t + 1] + cw[2:3, :] * slabs[t + 2]
            for t in range(len(slabs) - 2)]


def _mix_in_sample_kernel(x_ref, nw_ref, w_ref, cw_ref, cb_ref, s0_ref, s1_ref,
                          yconv_ref, qm_ref, kf_ref, vf_ref, st_ref, *, dc, da, ts, bs):
    h = _rms(x_ref[...], nw_ref[...]).astype(BF16)

    def proj(lo, width):
        return _dot(h, w_ref[:, lo:lo + width])

    pre = proj(dc, dc) * proj(2 * dc, dc)
    gate = proj(0, dc)
    slabs = [s0_ref[...], s1_ref[...]] + [pre[t * bs:(t + 1) * bs, :] for t in range(ts)]
    conv = _conv_time_major(slabs, cw_ref[...], cb_ref[...])
    for t in range(ts):
        yconv_ref[t * bs:(t + 1) * bs, :] = (gate[t * bs:(t + 1) * bs, :] * conv[t]).astype(BF16)
    st_ref[0] = slabs[-2]
    st_ref[1] = slabs[-1]

    q = proj(3 * dc, da) * (HEAD_DK ** -0.5)
    lane = lax.broadcasted_iota(jnp.int32, q.shape, 1) % HEAD_DV
    qm_ref[0] = jnp.where(lane < HEAD_DK, q, 0.0).astype(BF16)
    qm_ref[1] = jnp.where(lane >= HEAD_DK, q, 0.0).astype(BF16)
    kf_ref[...] = proj(3 * dc + da, da)
    vf_ref[...] = proj(3 * dc + 2 * da, da)


def _mix_in_sample(xt, nw, w_in, cw, cb, s0, s1, *, dc, da, ts, bs):
    n = ts * bs
    return pl.pallas_call(
        functools.partial(_mix_in_sample_kernel, dc=dc, da=da, ts=ts, bs=bs),
        out_shape=[jax.ShapeDtypeStruct((n, dc), BF16),
                   jax.ShapeDtypeStruct((2, n, da), BF16),
                   jax.ShapeDtypeStruct((n, da), F32),
                   jax.ShapeDtypeStruct((n, da), F32),
                   jax.ShapeDtypeStruct((CONV_W - 1, bs, dc), F32)],
        compiler_params=pltpu.CompilerParams(vmem_limit_bytes=VMEM_LIMIT),
    )(xt, nw, w_in, cw, cb, s0, s1)


def _mem_kv_kernel(m_ref, nw_ref, w_ref, kf_ref, vf_ref, kb_ref, vb_ref, *, d):
    h = _rms(m_ref[...], nw_ref[...]).astype(BF16)
    k = _dot(h, w_ref[:, 0:d])
    kf_ref[...] = k
    kb_ref[...] = k.astype(BF16)
    v = _dot(h, w_ref[:, d:2 * d])
    vf_ref[...] = v
    vb_ref[...] = v.astype(BF16)


def _mem_kv(mem2, nw, w_ckv, *, batch, n_mem):
    d = mem2.shape[1]
    rows = lambda b: (b, 0)
    const = lambda b: (0, 0)
    n = batch * n_mem
    return pl.pallas_call(
        functools.partial(_mem_kv_kernel, d=d),
        grid=(batch,),
        in_specs=[pl.BlockSpec((n_mem, d), rows), pl.BlockSpec((1, d), const),
                  pl.BlockSpec(w_ckv.shape, const)],
        out_specs=[pl.BlockSpec((n_mem, d), rows)] * 4,
        out_shape=[jax.ShapeDtypeStruct((n, d), F32), jax.ShapeDtypeStruct((n, d), F32),
                   jax.ShapeDtypeStruct((n, d), BF16), jax.ShapeDtypeStruct((n, d), BF16)],
        compiler_params=_params("arbitrary"),
    )(mem2, nw, w_ckv)


def _attn_prompt_kernel(rb_ref, q_ref, k_ref, v_ref, lq1, lk1, lq2, lk2, sw_ref, o_ref,
                        qp_scr, m_scr, l_scr, acc_scr, bd_scr, bs_scr, *, T, lam_init):
    qi = pl.program_id(2)
    nb = T // LANES

    @pl.when(qi == 0)
    def _assemble_bias_tiles():
        h = pl.program_id(1)
        rel = (lax.broadcasted_iota(jnp.int32, (LANES, LANES), 0)
               - lax.broadcasted_iota(jnp.int32, (LANES, LANES), 1))
        d0 = jnp.where(rel >= 0, _rel_bias_tile(rel, rb_ref, h), NEG)
        d1 = _rel_bias_tile(rel + LANES, rb_ref, h)
        zero = jnp.zeros((LANES, LANES), F32)
        neg = jnp.full((LANES, LANES), NEG, F32)
        for bi in range(nb):
            for bj in range(nb):
                rs = slice(bi * LANES, (bi + 1) * LANES)
                cs = slice(bj * LANES, (bj + 1) * LANES)
                bd_scr[rs, cs] = d0 if bi == bj else d1 if bi == bj + 1 else zero if bi > bj else neg
                bs_scr[rs, cs] = d1 if (bi == 0 and bj == nb - 1) else zero

    qh = q_ref[...].astype(F32)
    lane = lax.broadcasted_iota(jnp.int32, qh.shape, 1)
    qp_scr[0:T, :] = jnp.where(lane < HEAD_DK, qh, 0.0).astype(BF16)
    qp_scr[T:2 * T, :] = jnp.where(lane >= HEAD_DK, qh, 0.0).astype(BF16)
    m_scr[...] = jnp.full(m_scr.shape, -jnp.inf, F32)
    l_scr[...] = jnp.zeros(l_scr.shape, F32)
    acc_scr[...] = jnp.zeros(acc_scr.shape, F32)

    def step(start, bias_ref):
        kh = k_ref[pl.ds(start, T), :]
        vh = v_ref[pl.ds(start, T), :]
        s = _dot_nt(qp_scr[...], kh)
        if bias_ref is not None:
            b = bias_ref[...]
            s = s + jnp.concatenate([b, b], axis=0)
        chunks = [s[:, c * LANES:(c + 1) * LANES] for c in range(nb)]
        ps, m_next, l_next, alpha = _softmax_update(chunks, m_scr[...], l_scr[...])
        p = jnp.concatenate([x.astype(BF16) for x in ps], axis=1)
        acc_scr[...] = alpha * acc_scr[...] + _dot(p, vh)
        m_scr[...] = m_next
        l_scr[...] = l_next

    def far_body(j, carry):
        step(pl.multiple_of(j * T, T), None)
        return carry

    lax.fori_loop(0, jnp.maximum(qi - 1, 0), far_body, 0)

    @pl.when(qi >= 1)
    def _sub_diagonal():
        step(pl.multiple_of((qi - 1) * T, T), bs_scr)

    step(pl.multiple_of(qi * T, T), bd_scr)

    lam = _lam(lq1, lk1, lq2, lk2, lam_init)
    l = l_scr[...]
    acc = acc_scr[...]
    o_ref[...] = _head_out(acc[0:T], l[0:T], acc[T:2 * T], l[T:2 * T], lam, sw_ref[...],
                           lam_init).astype(o_ref.dtype)


def _attn_prompt(rel_bias, q, kb, vb, lq1, lk1, lq2, lk2, sw, *, batch, seq, lam_init):
    T = min(ATTN_TILE, seq)
    nq = seq // T
    da = q.shape[1]
    k3 = kb.reshape(batch, seq, da)
    v3 = vb.reshape(batch, seq, da)
    vec = lambda b, h, i: (0, 0)
    return pl.pallas_call(
        functools.partial(_attn_prompt_kernel, T=T, lam_init=lam_init),
        grid=(batch, N_HEADS, nq),
        in_specs=[pl.BlockSpec(memory_space=pltpu.SMEM),
                  pl.BlockSpec((T, HEAD_DV), lambda b, h, i: (b * nq + i, h)),
                  pl.BlockSpec((None, seq, HEAD_DV), lambda b, h, i: (b, 0, h)),
                  pl.BlockSpec((None, seq, HEAD_DV), lambda b, h, i: (b, 0, h)),
                  pl.BlockSpec((1, HEAD_DK), vec), pl.BlockSpec((1, HEAD_DK), vec),
                  pl.BlockSpec((1, HEAD_DK), vec), pl.BlockSpec((1, HEAD_DK), vec),
                  pl.BlockSpec((1, HEAD_DV), vec)],
        out_specs=pl.BlockSpec((T, HEAD_DV), lambda b, h, i: (b * nq + i, h)),
        out_shape=jax.ShapeDtypeStruct((batch * seq, da), BF16),
        scratch_shapes=[pltpu.VMEM((2 * T, HEAD_DV), BF16),
                        pltpu.VMEM((2 * T, LANES), F32),
                        pltpu.VMEM((2 * T, LANES), F32),
                        pltpu.VMEM((2 * T, HEAD_DV), F32),
                        pltpu.VMEM((T, T), F32),
                        pltpu.VMEM((T, T), F32)],
        compiler_params=_params("arbitrary", "arbitrary", "arbitrary"),
    )(rel_bias, q, k3, v3, lq1, lk1, lq2, lk2, sw)


def _attn_sample_kernel(pt_ref, rb_ref, qp_ref, kn_ref, vn_ref, lq1, lk1, lq2, lk2, sw_ref,
                        *rest, pages, ts, lam_init):
    del pt_ref
    k_refs = rest[:pages]
    v_refs = rest[pages:2 * pages]
    o_ref = rest[2 * pages]
    m_scr, l_scr, acc_scr, bl_scr, bn_scr = rest[2 * pages + 1:]
    g = pl.program_id(1)
    last = g == pl.num_programs(1) - 1
    rows, page = bl_scr.shape[1], bl_scr.shape[2]
    rm = rows // 2

    @pl.when((pl.program_id(0) == 0) & (g == 0))
    def _bias_tables():
        tok = lax.broadcasted_iota(jnp.int32, (rows, page), 0) & (rm - 1)
        col = lax.broadcasted_iota(jnp.int32, (rows, page), 1)
        rel_new = tok - col
        for h in range(N_HEADS):
            bl_scr[h] = _rel_bias_tile(tok + page - col, rb_ref, h)
            bn_scr[h] = jnp.where((rel_new >= 0) & (col < ts),
                                  _rel_bias_tile(jnp.maximum(rel_new, 0), rb_ref, h), NEG)

    @pl.when(g == 0)
    def _init():
        m_scr[...] = jnp.full(m_scr.shape, -jnp.inf, F32)
        l_scr[...] = jnp.zeros(l_scr.shape, F32)
        acc_scr[...] = jnp.zeros(acc_scr.shape, F32)

    def update(h, chunks, values):
        ps, m_next, l_next, alpha = _softmax_update(chunks, m_scr[h], l_scr[h])
        pv = _dot(ps[0].astype(BF16), values[0])
        for p, v in zip(ps[1:], values[1:]):
            pv = pv + _dot(p.astype(BF16), v)
        acc_scr[h] = alpha * acc_scr[h] + pv
        m_scr[h] = m_next
        l_scr[h] = l_next

    last_scale = jnp.where(last, 1.0, 0.0)
    for h in range(N_HEADS):
        cols = slice(h * HEAD_DV, (h + 1) * HEAD_DV)
        qp = qp_ref[h]
        chunks = [_dot_nt(qp, k_refs[i][:, cols].astype(BF16)) for i in range(pages)]
        chunks[-1] = chunks[-1] + last_scale * bl_scr[h]
        update(h, chunks, [v_refs[i][:, cols].astype(BF16) for i in range(pages)])

    @pl.when(last)
    def _new_rows_and_output():
        lam = _lam(lq1, lk1, lq2, lk2, lam_init)
        for h in range(N_HEADS):
            cols = slice(h * HEAD_DV, (h + 1) * HEAD_DV)
            update(h, [_dot_nt(qp_ref[h], kn_ref[h]) + bn_scr[h]], [vn_ref[h]])
            acc = acc_scr[h]
            l = l_scr[h]
            out = _head_out(acc[0:rm], l[0:rm], acc[rm:2 * rm], l[rm:2 * rm], lam, sw_ref[...], lam_init)
            o_ref[:, cols] = out[0:ts, :]


def _attn_sample(page_table, rel_bias, qpad, knew, vnew, lq1, lk1, lq2, lk2, sw, cache_k, cache_v,
                 *, page_offset, ts, lam_init):
    bs, n_pages = page_table.shape
    page = cache_k.shape[1]
    da = cache_k.shape[2]
    pages = math.gcd(PAGES_PER_STEP, n_pages)
    rows = qpad.shape[2]
    vec = lambda b, g, pt: (0, 0)
    per_b = lambda b, g, pt: (b, 0, 0, 0)

    def page_spec(i):
        return pl.BlockSpec((None, page, da),
                            lambda b, g, pt: (page_offset + pt[b, g * pages + i], 0, 0))

    grid_spec = pltpu.PrefetchScalarGridSpec(
        num_scalar_prefetch=1,
        grid=(bs, n_pages // pages),
        in_specs=[pl.BlockSpec(memory_space=pltpu.SMEM),
                  pl.BlockSpec((None, N_HEADS, rows, HEAD_DV), per_b),
                  pl.BlockSpec((None, N_HEADS, page, HEAD_DV), per_b),
                  pl.BlockSpec((None, N_HEADS, page, HEAD_DV), per_b),
                  pl.BlockSpec((1, HEAD_DK), vec), pl.BlockSpec((1, HEAD_DK), vec),
                  pl.BlockSpec((1, HEAD_DK), vec), pl.BlockSpec((1, HEAD_DK), vec),
                  pl.BlockSpec((1, HEAD_DV), vec)]
                 + [page_spec(i) for i in range(pages)] * 2,
        out_specs=pl.BlockSpec((None, ts, da), lambda b, g, pt: (b, 0, 0)),
        scratch_shapes=[pltpu.VMEM((N_HEADS, rows, LANES), F32),
                        pltpu.VMEM((N_HEADS, rows, LANES), F32),
                        pltpu.VMEM((N_HEADS, rows, HEAD_DV), F32),
                        pltpu.VMEM((N_HEADS, rows, page), F32),
                        pltpu.VMEM((N_HEADS, rows, page), F32)],
    )
    return pl.pallas_call(
        functools.partial(_attn_sample_kernel, pages=pages, ts=ts, lam_init=lam_init),
        grid_spec=grid_spec,
        out_shape=jax.ShapeDtypeStruct((bs, ts, da), F32),
        compiler_params=_params("arbitrary", "arbitrary"),
    )(page_table, rel_bias, qpad, knew, vnew, lq1, lk1, lq2, lk2, sw,
      *([cache_k] * pages), *([cache_v] * pages))


def _cross_heads(qc, mk_ref, mv_ref, o_scr, dh):
    for hd in range(MEM_HEADS):
        cols = slice(hd * dh, (hd + 1) * dh)
        s = _dot_nt(qc[:, cols], mk_ref[:, cols].astype(BF16))
        p = jnp.exp(s - jnp.max(s, axis=1, keepdims=True))
        o = _dot(p.astype(BF16), mv_ref[:, cols].astype(BF16)) / jnp.sum(p, axis=1, keepdims=True)
        o_scr[:, cols] = o.astype(BF16)


def _cross_prompt_kernel(x_ref, yc_ref, ya_ref, wo_ref, nw_ref, wq_ref, mk_ref, mv_ref, wc_ref,
                         o_ref, o_scr, *, dc, dh):
    x1 = x_ref[...] + _dot(yc_ref[...], wo_ref[0:dc, :]) + _dot(ya_ref[...], wo_ref[dc:, :])
    h = _rms(x1, nw_ref[...]).astype(BF16)
    qc = (_dot(h, wq_ref[...]) * (dh ** -0.5)).astype(BF16)
    _cross_heads(qc, mk_ref, mv_ref, o_scr, dh)
    o_ref[...] = x1 + _dot(o_scr[...], wc_ref[...])


def _cross_prompt(x2, yconv, yattn, w_out, nw, w_cq, mk, mv, w_co, *, batch, seq, n_mem):
    d = x2.shape[1]
    dc = yconv.shape[1]
    da = yattn.shape[1]
    tm = min(ROW_TILE, seq)
    nj = seq // tm
    rows = lambda b, j: (b * nj + j, 0)
    const = lambda b, j: (0, 0)
    memb = lambda b, j: (b, 0)
    return pl.pallas_call(
        functools.partial(_cross_prompt_kernel, dc=dc, dh=d // MEM_HEADS),
        grid=(batch, nj),
        in_specs=[pl.BlockSpec((tm, d), rows), pl.BlockSpec((tm, dc), rows), pl.BlockSpec((tm, da), rows),
                  pl.BlockSpec(w_out.shape, const), pl.BlockSpec((1, d), const),
                  pl.BlockSpec(w_cq.shape, const),
                  pl.BlockSpec((n_mem, d), memb), pl.BlockSpec((n_mem, d), memb),
                  pl.BlockSpec(w_co.shape, const)],
        out_specs=pl.BlockSpec((tm, d), rows),
        out_shape=jax.ShapeDtypeStruct(x2.shape, F32),
        scratch_shapes=[pltpu.VMEM((tm, d), BF16)],
        compiler_params=_params("arbitrary", "arbitrary"),
    )(x2, yconv, yattn, w_out, nw, w_cq, mk, mv, w_co)


def _outproj_q_sample_kernel(x_ref, yc_ref, ya_ref, wo_ref, nw_ref, wq_ref, x1_ref, qc_ref, *, dc, dh):
    x1 = x_ref[...] + _dot(yc_ref[...], wo_ref[0:dc, :]) + _dot(ya_ref[...], wo_ref[dc:, :])
    x1_ref[...] = x1
    h = _rms(x1, nw_ref[...]).astype(BF16)
    qc_ref[...] = (_dot(h, wq_ref[...]) * (dh ** -0.5)).astype(BF16)


def _outproj_q_sample(xt, yconv, yattn, w_out, nw, w_cq):
    d = xt.shape[1]
    return pl.pallas_call(
        functools.partial(_outproj_q_sample_kernel, dc=yconv.shape[1], dh=d // MEM_HEADS),
        out_shape=[jax.ShapeDtypeStruct(xt.shape, F32), jax.ShapeDtypeStruct(xt.shape, BF16)],
        compiler_params=pltpu.CompilerParams(vmem_limit_bytes=VMEM_LIMIT),
    )(xt, yconv, yattn, w_out, nw, w_cq)


def _cross_sample_kernel(qc_ref, mk_ref, mv_ref, o_ref, *, dh):
    _cross_heads(qc_ref[...], mk_ref, mv_ref, o_ref, dh)


def _cross_sample(qc, mk, mv, *, row_offset):
    bs, rows, d = qc.shape
    n_mem = mk.shape[1]
    per_b = lambda b: (b, 0, 0)
    mem_b = lambda b: (row_offset + b, 0, 0)
    return pl.pallas_call(
        functools.partial(_cross_sample_kernel, dh=d // MEM_HEADS),
        grid=(bs,),
        in_specs=[pl.BlockSpec((None, rows, d), per_b), pl.BlockSpec((None, n_mem, d), mem_b),
                  pl.BlockSpec((None, n_mem, d), mem_b)],
        out_specs=pl.BlockSpec((None, rows, d), per_b),
        out_shape=jax.ShapeDtypeStruct(qc.shape, BF16),
        compiler_params=_params("arbitrary"),
    )(qc, mk, mv)


def _silu(g):
    return g * (1.0 / (1.0 + jnp.exp(-g)))


def _ffn_prompt_kernel(x_ref, nw_ref, wu_ref, cw_ref, cb_ref, wd_ref, fw_ref, y_ref, st_ref,
                       up_scr, carry_scr, *, dff, final):
    j = pl.program_id(1)
    tm = x_ref.shape[0]
    x = x_ref[...]
    h = _rms(x, nw_ref[...]).astype(BF16)

    @pl.when(j == 0)
    def _zero_prefix():
        carry_scr[...] = jnp.zeros(carry_scr.shape, F32)

    def conv_half(lo):
        cols = slice(lo, lo + dff)
        up = _dot(h, wu_ref[:, cols])
        up_scr[0:SUBLANES, :] = carry_scr[:, cols]
        up_scr[SUBLANES:SUBLANES + tm, :] = up
        cw = cw_ref[:, cols]
        conv = (cb_ref[:, cols] + cw[0:1, :] * up_scr[SUBLANES - 2:SUBLANES - 2 + tm, :]
                + cw[1:2, :] * up_scr[SUBLANES - 1:SUBLANES - 1 + tm, :] + cw[2:3, :] * up)
        carry_scr[:, cols] = up[tm - SUBLANES:tm, :]
        st_ref[:, cols] = up[tm - 2:tm, :]
        return conv

    g = conv_half(0)
    u = conv_half(dff)
    x3 = x + _dot((_silu(g) * u).astype(BF16), wd_ref[...])
    y_ref[...] = _rms(x3, fw_ref[...]) if final else x3


def _ffn_prompt(x2, nw, w_up, cw, cb, w_down, fw, *, batch, seq, final):
    d = x2.shape[1]
    dff = w_down.shape[0]
    tm = min(FFN_ROW_TILE, seq)
    nj = seq // tm
    rows = lambda b, j: (b * nj + j, 0)
    const = lambda b, j: (0, 0)
    return pl.pallas_call(
        functools.partial(_ffn_prompt_kernel, dff=dff, final=final),
        grid=(batch, nj),
        in_specs=[pl.BlockSpec((tm, d), rows), pl.BlockSpec((1, d), const),
                  pl.BlockSpec(w_up.shape, const), pl.BlockSpec(cw.shape, const),
                  pl.BlockSpec((1, 2 * dff), const), pl.BlockSpec(w_down.shape, const),
                  pl.BlockSpec((1, d), const)],
        out_specs=[pl.BlockSpec((tm, d), rows),
                   pl.BlockSpec((None, CONV_W - 1, 2 * dff), lambda b, j: (b, 0, 0))],
        out_shape=[jax.ShapeDtypeStruct(x2.shape, F32),
                   jax.ShapeDtypeStruct((batch, CONV_W - 1, 2 * dff), F32)],
        scratch_shapes=[pltpu.VMEM((tm + SUBLANES, dff), F32),
                        pltpu.VMEM((SUBLANES, 2 * dff), F32)],
        compiler_params=_params("arbitrary", "arbitrary"),
    )(x2, nw, w_up, cw, cb, w_down, fw)


def _ffn_sample_kernel(x1_ref, o_ref, wc_ref, nw_ref, wu_ref, cw_ref, cb_ref, wd_ref, fw_ref,
                       s0_ref, s1_ref, y_ref, st_ref, hid_scr, *, dff, ts, bs, final):
    x2 = x1_ref[...] + _dot(o_ref[...], wc_ref[...])
    h = _rms(x2, nw_ref[...]).astype(BF16)

    def conv_half(lo):
        cols = slice(lo, lo + dff)
        up = _dot(h, wu_ref[:, cols])
        slabs = [s0_ref[:, cols], s1_ref[:, cols]] + [up[t * bs:(t + 1) * bs, :] for t in range(ts)]
        st_ref[0, :, cols] = slabs[-2]
        st_ref[1, :, cols] = slabs[-1]
        return _conv_time_major(slabs, cw_ref[:, cols], cb_ref[:, cols])

    g = conv_half(0)
    u = conv_half(dff)
    for t in range(ts):
        hid_scr[t * bs:(t + 1) * bs, :] = (_silu(g[t]) * u[t]).astype(BF16)
    x3 = x2 + _dot(hid_scr[...], wd_ref[...])
    y_ref[...] = _rms(x3, fw_ref[...]) if final else x3


def _ffn_sample(x1, o, w_co, nw, w_up, cw, cb, w_down, fw, s0, s1, *, ts, bs, final):
    dff = w_down.shape[0]
    return pl.pallas_call(
        functools.partial(_ffn_sample_kernel, dff=dff, ts=ts, bs=bs, final=final),
        out_shape=[jax.ShapeDtypeStruct(x1.shape, F32),
                   jax.ShapeDtypeStruct((CONV_W - 1, bs, 2 * dff), F32)],
        scratch_shapes=[pltpu.VMEM((ts * bs, dff), BF16)],
        compiler_params=pltpu.CompilerParams(vmem_limit_bytes=VMEM_LIMIT),
    )(x1, o, w_co, nw, w_up, cw, cb, w_down, fw, s0, s1)


def kernel(x_prompt, x_sample, mem_prompt, cache_k, cache_v, page_table, state_conv_mix, state_conv_ffn, cache_mem_k, cache_mem_v, rel_bias, norm_mix_w, w_in, conv_mix_w, conv_mix_b, lambda_q1, lambda_k1, lambda_q2, lambda_k2, subln_w, w_out, norm_cross_w, norm_mem_w, w_cq, w_ckv, w_co, norm_ffn_w, w_up, conv_ffn_w, conv_ffn_b, w_down, norm_final_w):
    depth = w_in.shape[0]
    bp, sp, d = x_prompt.shape
    bs, ts, _ = x_sample.shape
    n_mem = mem_prompt.shape[1]
    n_phys, page = cache_k.shape[1], cache_k.shape[2]
    n_pages = page_table.shape[1]
    past_len = n_pages * page
    dc = conv_mix_w.shape[2]
    da = N_HEADS * HEAD_DV
    dff = w_down.shape[1]
    dh = d // MEM_HEADS
    assert page == LANES and ts <= SUBLANES and ts >= CONV_W - 1
    rows = 2 * SUBLANES

    assert _bucket_np(np.arange(REL_MAX_DIST, max(sp, past_len + ts) + 1)).min() == REL_BUCKETS - 1
    row = lambda a: a.reshape(1, -1).astype(F32)
    rel_bias = rel_bias.astype(F32)
    cache_k3 = cache_k.reshape(depth * n_phys, page, da)
    cache_v3 = cache_v.reshape(depth * n_phys, page, da)
    mem_k3 = cache_mem_k.reshape(depth * bs, n_mem, d)
    mem_v3 = cache_mem_v.reshape(depth * bs, n_mem, d)

    xp = x_prompt.reshape(bp * sp, d)
    xs = x_sample.transpose(1, 0, 2).reshape(ts * bs, d)
    mem2 = mem_prompt.reshape(bp * n_mem, d)
    outs = [[] for _ in range(10)]
    for l in range(depth):
        lam_init = 0.8 - 0.6 * math.exp(-0.3 * l)
        final = l == depth - 1
        w_in_b, w_out_b = w_in[l].astype(BF16), w_out[l].astype(BF16)
        w_cq_b, w_ckv_b, w_co_b = w_cq[l].astype(BF16), w_ckv[l].astype(BF16), w_co[l].astype(BF16)
        w_up_b, w_down_b = w_up[l].astype(BF16), w_down[l].astype(BF16)
        lam_args = (row(lambda_q1[l]), row(lambda_k1[l]), row(lambda_q2[l]), row(lambda_k2[l]),
                    row(subln_w[l]))

        yconv_p, q_p, kf_p, vf_p, kb_p, vb_p, cmix_p = _mix_in_prompt(
            xp, row(norm_mix_w[l]), w_in_b, conv_mix_w[l], row(conv_mix_b[l]),
            batch=bp, seq=sp, dc=dc, da=da)
        yattn_p = _attn_prompt(rel_bias, q_p, kb_p, vb_p, *lam_args, batch=bp, seq=sp, lam_init=lam_init)
        mkf, mvf, mkb, mvb = _mem_kv(mem2, row(norm_mem_w[l]), w_ckv_b, batch=bp, n_mem=n_mem)
        x2_p = _cross_prompt(xp, yconv_p, yattn_p, w_out_b, row(norm_cross_w[l]), w_cq_b, mkb, mvb,
                             w_co_b, batch=bp, seq=sp, n_mem=n_mem)
        xp, cffn_p = _ffn_prompt(x2_p, row(norm_ffn_w[l]), w_up_b, conv_ffn_w[l], row(conv_ffn_b[l]),
                                 w_down_b, row(norm_final_w), batch=bp, seq=sp, final=final)

        yconv_s, qm_s, kf_s, vf_s, cmix_s = _mix_in_sample(
            xs, row(norm_mix_w[l]), w_in_b, conv_mix_w[l], row(conv_mix_b[l]),
            state_conv_mix[l][:, 0], state_conv_mix[l][:, 1], dc=dc, da=da, ts=ts, bs=bs)
        qpad = qm_s.reshape(2, ts, bs, N_HEADS, HEAD_DV).transpose(2, 3, 0, 1, 4)
        qpad = jnp.pad(qpad, ((0, 0), (0, 0), (0, 0), (0, SUBLANES - ts), (0, 0)))
        qpad = qpad.reshape(bs, N_HEADS, rows, HEAD_DV)
        to_heads = lambda a: jnp.pad(
            a.reshape(ts, bs, N_HEADS, HEAD_DV).transpose(1, 2, 0, 3).astype(BF16),
            ((0, 0), (0, 0), (0, page - ts), (0, 0)))
        yattn_s = _attn_sample(page_table, rel_bias, qpad, to_heads(kf_s), to_heads(vf_s), *lam_args,
                               cache_k3, cache_v3, page_offset=l * n_phys, ts=ts, lam_init=lam_init)
        yattn_s = yattn_s.transpose(1, 0, 2).reshape(ts * bs, da).astype(BF16)
        x1_s, qc_s = _outproj_q_sample(xs, yconv_s, yattn_s, w_out_b, row(norm_cross_w[l]), w_cq_b)
        qc_b = jnp.pad(qc_s.reshape(ts, bs, d).transpose(1, 0, 2), ((0, 0), (0, rows - ts), (0, 0)))
        o_b = _cross_sample(qc_b, mem_k3, mem_v3, row_offset=l * bs)
        o_s = o_b[:, :ts].transpose(1, 0, 2).reshape(ts * bs, d)
        xs, cffn_s = _ffn_sample(x1_s, o_s, w_co_b, row(norm_ffn_w[l]), w_up_b, conv_ffn_w[l],
                                 row(conv_ffn_b[l]), w_down_b, row(norm_final_w),
                                 state_conv_ffn[l][:, 0], state_conv_ffn[l][:, 1],
                                 ts=ts, bs=bs, final=final)

        t2b = lambda a: a.reshape(ts, bs, N_HEADS, HEAD_DV).transpose(1, 0, 2, 3)
        for lst, val in zip(outs, (
                kf_p.reshape(bp, sp, N_HEADS, 2 * HEAD_DK), vf_p.reshape(bp, sp, N_HEADS, HEAD_DV),
                t2b(kf_s), t2b(vf_s), cmix_p, cmix_s.transpose(1, 0, 2), cffn_p,
                cffn_s.transpose(1, 0, 2), mkf.reshape(bp, n_mem, MEM_HEADS, dh),
                mvf.reshape(bp, n_mem, MEM_HEADS, dh))):
            lst.append(val)

    y_prompt = xp.reshape(bp, sp, d)
    y_sample = xs.reshape(ts, bs, d).transpose(1, 0, 2)
    return (y_prompt, y_sample) + tuple(jnp.stack(o) for o in outs)
```

```python
import functools
import math

import numpy as np
import jax
import jax.numpy as jnp
from jax import lax
from jax.experimental import pallas as pl
from jax.experimental.pallas import tpu as pltpu

F32 = jnp.float32
BF16 = jnp.bfloat16

EPS = 1e-6
NEG = -1e30
LANES = 128
SUBLANES = 8
N_HEADS = 4
HEAD_DK = 64
HEAD_DV = 2 * HEAD_DK
MEM_HEADS = 4
CONV_W = 3
REL_BUCKETS = 32
REL_MAX_EXACT = 16
REL_MAX_DIST = 128
VMEM_LIMIT = 56 * 1024 * 1024

ATTN_TILE = 512
ROW_TILE = 512
FFN_ROW_TILE = 256
PAGES_PER_STEP = 8


def _params(*sem):
    return pltpu.CompilerParams(dimension_semantics=sem, vmem_limit_bytes=VMEM_LIMIT)


def _rms(x, w):
    return x * lax.rsqrt(jnp.mean(x * x, axis=-1, keepdims=True) + EPS) * w


def _dot(a, b):
    return jnp.dot(a, b, preferred_element_type=F32)


def _dot_nt(a, b):
    return lax.dot_general(a, b, (((1,), (1,)), ((), ())), preferred_element_type=F32)


def _lam(lq1, lk1, lq2, lk2, lam_init):
    return (jnp.exp(jnp.sum(lq1[...] * lk1[...], axis=-1, keepdims=True))
            - jnp.exp(jnp.sum(lq2[...] * lk2[...], axis=-1, keepdims=True)) + lam_init)


def _bucket_np(rel):
    n = np.maximum(rel, 0)
    nf = np.maximum(n, 1).astype(np.float32)
    large = REL_MAX_EXACT + (np.log(nf / np.float32(REL_MAX_EXACT))
                             / np.float32(math.log(REL_MAX_DIST / REL_MAX_EXACT))
                             * np.float32(REL_BUCKETS - REL_MAX_EXACT)).astype(np.int32)
    large = np.minimum(large, REL_BUCKETS - 1)
    return np.where(n < REL_MAX_EXACT, n, large).astype(np.int32)


def _bucket_starts():
    buckets = _bucket_np(np.arange(REL_MAX_DIST + 1))
    assert (np.diff(buckets) >= 0).all() and buckets[-1] == REL_BUCKETS - 1
    return [int(np.argmax(buckets >= k)) for k in range(REL_BUCKETS)]


def _rel_bias_tile(rel, rb_ref, h):
    far = rb_ref[REL_BUCKETS - 1, h]
    val = jnp.full(rel.shape, rb_ref[0, h] - far, F32)
    for k, start in enumerate(_bucket_starts()):
        if k > 0:
            val = jnp.where(rel >= start, rb_ref[k, h] - far, val)
    return val


def _softmax_update(chunks, m_prev, l_prev):
    mx = chunks[0]
    for c in chunks[1:]:
        mx = jnp.maximum(mx, c)
    m_next = jnp.maximum(m_prev, jnp.max(mx, axis=1, keepdims=True))
    ps = [jnp.exp(c - m_next) for c in chunks]
    sm = ps[0]
    for p in ps[1:]:
        sm = sm + p
    alpha = jnp.exp(m_prev - m_next)
    l_next = alpha * l_prev + jnp.sum(sm, axis=1, keepdims=True)
    return ps, m_next, l_next, alpha


def _head_out(acc1, l1, acc2, l2, lam, sw, lam_init):
    o = acc1 / l1 - lam * (acc2 / l2)
    return _rms(o, sw) * (1.0 - lam_init)


def _mix_in_prompt_kernel(x_ref, nw_ref, w_ref, cw_ref, cb_ref,
                          yconv_ref, q_ref, kf_ref, vf_ref, kb_ref, vb_ref, st_ref,
                          pre_scr, *, dc, da):
    j = pl.program_id(1)
    tm = x_ref.shape[0]
    h = _rms(x_ref[...], nw_ref[...]).astype(BF16)

    def proj(lo, width):
        return _dot(h, w_ref[:, lo:lo + width])

    @pl.when(j == 0)
    def _zero_prefix():
        pre_scr[0:SUBLANES, :] = jnp.zeros((SUBLANES, dc), F32)

    pre = proj(dc, dc) * proj(2 * dc, dc)
    pre_scr[SUBLANES:SUBLANES + tm, :] = pre
    cw = cw_ref[...]
    conv = (cb_ref[...] + cw[0:1, :] * pre_scr[SUBLANES - 2:SUBLANES - 2 + tm, :]
            + cw[1:2, :] * pre_scr[SUBLANES - 1:SUBLANES - 1 + tm, :] + cw[2:3, :] * pre)
    yconv_ref[...] = (proj(0, dc) * conv).astype(BF16)
    st_ref[...] = pre[tm - 2:tm, :]
    pre_scr[0:SUBLANES, :] = pre[tm - SUBLANES:tm, :]

    q_ref[...] = (proj(3 * dc, da) * (HEAD_DK ** -0.5)).astype(BF16)
    k = proj(3 * dc + da, da)
    kf_ref[...] = k
    kb_ref[...] = k.astype(BF16)
    v = proj(3 * dc + 2 * da, da)
    vf_ref[...] = v
    vb_ref[...] = v.astype(BF16)


def _mix_in_prompt(x2, nw, w_in, cw, cb, *, batch, seq, dc, da):
    d = x2.shape[1]
    tm = min(ROW_TILE, seq)
    nj = seq // tm
    rows = lambda b, j: (b * nj + j, 0)
    const = lambda b, j: (0, 0)
    n = batch * seq
    return pl.pallas_call(
        functools.partial(_mix_in_prompt_kernel, dc=dc, da=da),
        grid=(batch, nj),
        in_specs=[pl.BlockSpec((tm, d), rows),
                  pl.BlockSpec((1, d), const),
                  pl.BlockSpec(w_in.shape, const),
                  pl.BlockSpec(cw.shape, const),
                  pl.BlockSpec((1, dc), const)],
        out_specs=[pl.BlockSpec((tm, dc), rows),
                   pl.BlockSpec((tm, da), rows),
                   pl.BlockSpec((tm, da), rows),
                   pl.BlockSpec((tm, da), rows),
                   pl.BlockSpec((tm, da), rows),
                   pl.BlockSpec((tm, da), rows),
                   pl.BlockSpec((None, CONV_W - 1, dc), lambda b, j: (b, 0, 0))],
        out_shape=[jax.ShapeDtypeStruct((n, dc), BF16),
                   jax.ShapeDtypeStruct((n, da), BF16),
                   jax.ShapeDtypeStruct((n, da), F32),
                   jax.ShapeDtypeStruct((n, da), F32),
                   jax.ShapeDtypeStruct((n, da), BF16),
                   jax.ShapeDtypeStruct((n, da), BF16),
                   jax.ShapeDtypeStruct((batch, CONV_W - 1, dc), F32)],
        scratch_shapes=[pltpu.VMEM((tm + SUBLANES, dc), F32)],
        compiler_params=_params("arbitrary", "arbitrary"),
    )(x2, nw, w_in, cw, cb)


def _conv_time_major(slabs, cw, cb):
    return [cb + cw[0:1, :] * slabs[t] + cw[1:2, :] * slabs[t + 1] + cw[2:3, :] * slabs[t + 2]
            for t in range(len(slabs) - 2)]


def _mix_in_sample_kernel(x_ref, nw_ref, w_ref, cw_ref, cb_ref, s0_ref, s1_ref,
                          yconv_ref, qm_ref, kf_ref, vf_ref, st_ref, *, dc, da, ts, bs):
    h = _rms(x_ref[...], nw_ref[...]).astype(BF16)

    def proj(lo, width):
        return _dot(h, w_ref[:, lo:lo + width])

    pre = proj(dc, dc) * proj(2 * dc, dc)
    gate = proj(0, dc)
    slabs = [s0_ref[...], s1_ref[...]] + [pre[t * bs:(t + 1) * bs, :] for t in range(ts)]
    conv = _conv_time_major(slabs, cw_ref[...], cb_ref[...])
    for t in range(ts):
        yconv_ref[t * bs:(t + 1) * bs, :] = (gate[t * bs:(t + 1) * bs, :] * conv[t]).astype(BF16)
    st_ref[0] = slabs[-2]
    st_ref[1] = slabs[-1]

    q = proj(3 * dc, da) * (HEAD_DK ** -0.5)
    lane = lax.broadcasted_iota(jnp.int32, q.shape, 1) % HEAD_DV
    qm_ref[0] = jnp.where(lane < HEAD_DK, q, 0.0).astype(BF16)
    qm_ref[1] = jnp.where(lane >= HEAD_DK, q, 0.0).astype(BF16)
    kf_ref[...] = proj(3 * dc + da, da)
    vf_ref[...] = proj(3 * dc + 2 * da, da)


def _mix_in_sample(xt, nw, w_in, cw, cb, s0, s1, *, dc, da, ts, bs):
    n = ts * bs
    return pl.pallas_call(
        functools.partial(_mix_in_sample_kernel, dc=dc, da=da, ts=ts, bs=bs),
        out_shape=[jax.ShapeDtypeStruct((n, dc), BF16),
                   jax.ShapeDtypeStruct((2, n, da), BF16),
                   jax.ShapeDtypeStruct((n, da), F32),
                   jax.ShapeDtypeStruct((n, da), F32),
                   jax.ShapeDtypeStruct((CONV_W - 1, bs, dc), F32)],
        compiler_params=pltpu.CompilerParams(vmem_limit_bytes=VMEM_LIMIT),
    )(xt, nw, w_in, cw, cb, s0, s1)


def _mem_kv_kernel(m_ref, nw_ref, w_ref, kf_ref, vf_ref, kb_ref, vb_ref, *, d):
    h = _rms(m_ref[...], nw_ref[...]).astype(BF16)
    k = _dot(h, w_ref[:, 0:d])
    kf_ref[...] = k
    kb_ref[...] = k.astype(BF16)
    v = _dot(h, w_ref[:, d:2 * d])
    vf_ref[...] = v
    vb_ref[...] = v.astype(BF16)


def _mem_kv(mem2, nw, w_ckv, *, batch, n_mem):
    d = mem2.shape[1]
    rows = lambda b: (b, 0)
    const = lambda b: (0, 0)
    n = batch * n_mem
    return pl.pallas_call(
        functools.partial(_mem_kv_kernel, d=d),
        grid=(batch,),
        in_specs=[pl.BlockSpec((n_mem, d), rows), pl.BlockSpec((1, d), const),
                  pl.BlockSpec(w_ckv.shape, const)],
        out_specs=[pl.BlockSpec((n_mem, d), rows)] * 4,
        out_shape=[jax.ShapeDtypeStruct((n, d), F32), jax.ShapeDtypeStruct((n, d), F32),
                   jax.ShapeDtypeStruct((n, d), BF16), jax.ShapeDtypeStruct((n, d), BF16)],
        compiler_params=_params("arbitrary"),
    )(mem2, nw, w_ckv)


def _attn_prompt_kernel(rb_ref, q_ref, k_ref, v_ref, lq1, lk1, lq2, lk2, sw_ref, o_ref,
                        qp_scr, m_scr, l_scr, acc_scr, bd_scr, bs_scr, *, T, lam_init):
    qi = pl.program_id(2)
    nb = T // LANES

    @pl.when(qi == 0)
    def _assemble_bias_tiles():
        h = pl.program_id(1)
        rel = (lax.broadcasted_iota(jnp.int32, (LANES, LANES), 0)
               - lax.broadcasted_iota(jnp.int32, (LANES, LANES), 1))
        d0 = jnp.where(rel >= 0, _rel_bias_tile(rel, rb_ref, h), NEG)
        d1 = _rel_bias_tile(rel + LANES, rb_ref, h)
        zero = jnp.zeros((LANES, LANES), F32)
        neg = jnp.full((LANES, LANES), NEG, F32)
        for bi in range(nb):
            for bj in range(nb):
                rs = slice(bi * LANES, (bi + 1) * LANES)
                cs = slice(bj * LANES, (bj + 1) * LANES)
                bd_scr[rs, cs] = d0 if bi == bj else d1 if bi == bj + 1 else zero if bi > bj else neg
                bs_scr[rs, cs] = d1 if (bi == 0 and bj == nb - 1) else zero

    qh = q_ref[...].astype(F32)
    lane = lax.broadcasted_iota(jnp.int32, qh.shape, 1)
    qp_scr[0:T, :] = jnp.where(lane < HEAD_DK, qh, 0.0).astype(BF16)
    qp_scr[T:2 * T, :] = jnp.where(lane >= HEAD_DK, qh, 0.0).astype(BF16)
    m_scr[...] = jnp.full(m_scr.shape, -jnp.inf, F32)
    l_scr[...] = jnp.zeros(l_scr.shape, F32)
    acc_scr[...] = jnp.zeros(acc_scr.shape, F32)

    def step(start, bias_ref):
        kh = k_ref[pl.ds(start, T), :]
        vh = v_ref[pl.ds(start, T), :]
        s = _dot_nt(qp_scr[...], kh)
        if bias_ref is not None:
            b = bias_ref[...]
            s = s + jnp.concatenate([b, b], axis=0)
        chunks = [s[:, c * LANES:(c + 1) * LANES] for c in range(nb)]
        ps, m_next, l_next, alpha = _softmax_update(chunks, m_scr[...], l_scr[...])
        p = jnp.concatenate([x.astype(BF16) for x in ps], axis=1)
        acc_scr[...] = alpha * acc_scr[...] + _dot(p, vh)
        m_scr[...] = m_next
        l_scr[...] = l_next

    def far_body(j, carry):
        step(pl.multiple_of(j * T, T), None)
        return carry

    lax.fori_loop(0, jnp.maximum(qi - 1, 0), far_body, 0)

    @pl.when(qi >= 1)
    def _sub_diagonal():
        step(pl.multiple_of((qi - 1) * T, T), bs_scr)

    step(pl.multiple_of(qi * T, T), bd_scr)

    lam = _lam(lq1, lk1, lq2, lk2, lam_init)
    l = l_scr[...]
    acc = acc_scr[...]
    o_ref[...] = _head_out(acc[0:T], l[0:T], acc[T:2 * T], l[T:2 * T], lam, sw_ref[...],
                           lam_init).astype(o_ref.dtype)


def _attn_prompt(rel_bias, q, kb, vb, lq1, lk1, lq2, lk2, sw, *, batch, seq, lam_init):
    T = min(ATTN_TILE, seq)
    nq = seq // T
    da = q.shape[1]
    k3 = kb.reshape(batch, seq, da)
    v3 = vb.reshape(batch, seq, da)
    vec = lambda b, h, i: (0, 0)
    return pl.pallas_call(
        functools.partial(_attn_prompt_kernel, T=T, lam_init=lam_init),
        grid=(batch, N_HEADS, nq),
        in_specs=[pl.BlockSpec(memory_space=pltpu.SMEM),
                  pl.BlockSpec((T, HEAD_DV), lambda b, h, i: (b * nq + i, h)),
                  pl.BlockSpec((None, seq, HEAD_DV), lambda b, h, i: (b, 0, h)),
                  pl.BlockSpec((None, seq, HEAD_DV), lambda b, h, i: (b, 0, h)),
                  pl.BlockSpec((1, HEAD_DK), vec), pl.BlockSpec((1, HEAD_DK), vec),
                  pl.BlockSpec((1, HEAD_DK), vec), pl.BlockSpec((1, HEAD_DK), vec),
                  pl.BlockSpec((1, HEAD_DV), vec)],
        out_specs=pl.BlockSpec((T, HEAD_DV), lambda b, h, i: (b * nq + i, h)),
        out_shape=jax.ShapeDtypeStruct((batch * seq, da), BF16),
        scratch_shapes=[pltpu.VMEM((2 * T, HEAD_DV), BF16),
                        pltpu.VMEM((2 * T, LANES), F32),
                        pltpu.VMEM((2 * T, LANES), F32),
                        pltpu.VMEM((2 * T, HEAD_DV), F32),
                        pltpu.VMEM((T, T), F32),
                        pltpu.VMEM((T, T), F32)],
        compiler_params=_params("arbitrary", "arbitrary", "arbitrary"),
    )(rel_bias, q, k3, v3, lq1, lk1, lq2, lk2, sw)


def _attn_sample_kernel(pt_ref, rb_ref, q_ref, kn_ref, vn_ref, lq1, lk1, lq2, lk2, sw_ref,
                        *rest, pages, ts, lam_init):
    del pt_ref
    k_refs = rest[:pages]
    v_refs = rest[pages:2 * pages]
    o_ref = rest[2 * pages]
    m_scr, l_scr, acc_scr, hm_scr, bl_scr, bn_scr = rest[2 * pages + 1:]
    g = pl.program_id(1)
    last = g == pl.num_programs(1) - 1
    rows, pk = hm_scr.shape
    nk = bn_scr.shape[1]
    rm = rows // 2
    page = pk // N_HEADS
    hbits = N_HEADS.bit_length() - 1
    tbits = ts.bit_length() - 1

    @pl.when((pl.program_id(0) == 0) & (g == 0))
    def _mask_and_bias_tables():
        def tables(ncols, rel_of):
            r = lax.broadcasted_iota(jnp.int32, (rows, ncols), 0)
            c = lax.broadcasted_iota(jnp.int32, (rows, ncols), 1)
            rhead = (r >> tbits) & (N_HEADS - 1)
            same = rhead == (c & (N_HEADS - 1))
            rel = rel_of(r & (ts - 1), c >> hbits)
            bias = jnp.zeros((rows, ncols), F32)
            for h in range(N_HEADS):
                bias = jnp.where(rhead == h, _rel_bias_tile(jnp.maximum(rel, 0), rb_ref, h), bias)
            return same, rel, c >> hbits, bias

        same, _, _, bias = tables(pk, lambda tok, key: tok + page - key)
        hm_scr[...] = jnp.where(same, 0.0, NEG)
        bl_scr[...] = jnp.where(same, bias, NEG)
        same, rel, key, bias = tables(nk, lambda tok, key: tok - key)
        bn_scr[...] = jnp.where(same & (rel >= 0) & (key < ts), bias, NEG)

    @pl.when(g == 0)
    def _init():
        m_scr[...] = jnp.full(m_scr.shape, -jnp.inf, F32)
        l_scr[...] = jnp.zeros(l_scr.shape, F32)
        acc_scr[...] = jnp.zeros(acc_scr.shape, F32)

    def update(scores, values):
        chunks, sizes = [], []
        for s in scores:
            n = s.shape[1] // LANES
            sizes.append(n)
            chunks += [s[:, c * LANES:(c + 1) * LANES] for c in range(n)]
        ps, m_next, l_next, alpha = _softmax_update(chunks, m_scr[...], l_scr[...])
        pv, at = None, 0
        for n, v in zip(sizes, values):
            p = jnp.concatenate([x.astype(BF16) for x in ps[at:at + n]], axis=1) if n > 1 \
                else ps[at].astype(BF16)
            at += n
            pv = _dot(p, v) if pv is None else pv + _dot(p, v)
        acc_scr[...] = alpha * acc_scr[...] + pv
        m_scr[...] = m_next
        l_scr[...] = l_next

    q = q_ref[...]
    scores = [_dot_nt(q, k_refs[i][...].astype(BF16)) + hm_scr[...] for i in range(pages - 1)]
    scores.append(_dot_nt(q, k_refs[pages - 1][...].astype(BF16))
                  + jnp.where(last, bl_scr[...], hm_scr[...]))
    update(scores, [v_refs[i][...].astype(BF16) for i in range(pages)])

    @pl.when(last)
    def _new_rows_and_output():
        update([_dot_nt(q, kn_ref[...]) + bn_scr[...]], [vn_ref[...]])
        lam = _lam(lq1, lk1, lq2, lk2, lam_init)
        acc = acc_scr[...]
        l = l_scr[...]
        o_ref[...] = _head_out(acc[0:rm], l[0:rm], acc[rm:rows], l[rm:rows], lam, sw_ref[...], lam_init)


def _attn_sample(page_table, rel_bias, q_all, knew, vnew, lq1, lk1, lq2, lk2, sw, cache_k, cache_v,
                 *, page, page_offset, ts, lam_init):
    bs, n_pages = page_table.shape
    rows = q_all.shape[1]
    nk = knew.shape[1]
    pk = page * N_HEADS
    pages = math.gcd(PAGES_PER_STEP, n_pages)
    vec = lambda b, g, pt: (0, 0)
    per_b = lambda b, g, pt: (b, 0, 0)

    def page_spec(i):
        return pl.BlockSpec((pk, HEAD_DV), lambda b, g, pt: (page_offset + pt[b, g * pages + i], 0))

    grid_spec = pltpu.PrefetchScalarGridSpec(
        num_scalar_prefetch=1,
        grid=(bs, n_pages // pages),
        in_specs=[pl.BlockSpec(memory_space=pltpu.SMEM),
                  pl.BlockSpec((None, rows, HEAD_DV), per_b),
                  pl.BlockSpec((None, nk, HEAD_DV), per_b),
                  pl.BlockSpec((None, nk, HEAD_DV), per_b),
                  pl.BlockSpec((1, HEAD_DK), vec), pl.BlockSpec((1, HEAD_DK), vec),
                  pl.BlockSpec((1, HEAD_DK), vec), pl.BlockSpec((1, HEAD_DK), vec),
                  pl.BlockSpec((1, HEAD_DV), vec)]
                 + [page_spec(i) for i in range(pages)] * 2,
        out_specs=pl.BlockSpec((None, rows // 2, HEAD_DV), per_b),
        scratch_shapes=[pltpu.VMEM((rows, LANES), F32),
                        pltpu.VMEM((rows, LANES), F32),
                        pltpu.VMEM((rows, HEAD_DV), F32),
                        pltpu.VMEM((rows, pk), F32),
                        pltpu.VMEM((rows, pk), F32),
                        pltpu.VMEM((rows, nk), F32)],
    )
    return pl.pallas_call(
        functools.partial(_attn_sample_kernel, pages=pages, ts=ts, lam_init=lam_init),
        grid_spec=grid_spec,
        out_shape=jax.ShapeDtypeStruct((bs, rows // 2, HEAD_DV), F32),
        compiler_params=_params("arbitrary", "arbitrary"),
    )(page_table, rel_bias, q_all, knew, vnew, lq1, lk1, lq2, lk2, sw,
      *([cache_k] * pages), *([cache_v] * pages))


def _cross_heads(qc, mk_ref, mv_ref, o_scr, dh):
    for hd in range(MEM_HEADS):
        cols = slice(hd * dh, (hd + 1) * dh)
        s = _dot_nt(qc[:, cols], mk_ref[:, cols].astype(BF16))
        p = jnp.exp(s - jnp.max(s, axis=1, keepdims=True))
        o = _dot(p.astype(BF16), mv_ref[:, cols].astype(BF16)) / jnp.sum(p, axis=1, keepdims=True)
        o_scr[:, cols] = o.astype(BF16)


def _cross_prompt_kernel(x_ref, yc_ref, ya_ref, wo_ref, nw_ref, wq_ref, mk_ref, mv_ref, wc_ref,
                         o_ref, o_scr, *, dc, dh):
    x1 = x_ref[...] + _dot(yc_ref[...], wo_ref[0:dc, :]) + _dot(ya_ref[...], wo_ref[dc:, :])
    h = _rms(x1, nw_ref[...]).astype(BF16)
    qc = (_dot(h, wq_ref[...]) * (dh ** -0.5)).astype(BF16)
    _cross_heads(qc, mk_ref, mv_ref, o_scr, dh)
    o_ref[...] = x1 + _dot(o_scr[...], wc_ref[...])


def _cross_prompt(x2, yconv, yattn, w_out, nw, w_cq, mk, mv, w_co, *, batch, seq, n_mem):
    d = x2.shape[1]
    dc = yconv.shape[1]
    da = yattn.shape[1]
    tm = min(ROW_TILE, seq)
    nj = seq // tm
    rows = lambda b, j: (b * nj + j, 0)
    const = lambda b, j: (0, 0)
    memb = lambda b, j: (b, 0)
    return pl.pallas_call(
        functools.partial(_cross_prompt_kernel, dc=dc, dh=d // MEM_HEADS),
        grid=(batch, nj),
        in_specs=[pl.BlockSpec((tm, d), rows), pl.BlockSpec((tm, dc), rows), pl.BlockSpec((tm, da), rows),
                  pl.BlockSpec(w_out.shape, const), pl.BlockSpec((1, d), const),
                  pl.BlockSpec(w_cq.shape, const),
                  pl.BlockSpec((n_mem, d), memb), pl.BlockSpec((n_mem, d), memb),
                  pl.BlockSpec(w_co.shape, const)],
        out_specs=pl.BlockSpec((tm, d), rows),
        out_shape=jax.ShapeDtypeStruct(x2.shape, F32),
        scratch_shapes=[pltpu.VMEM((tm, d), BF16)],
        compiler_params=_params("arbitrary", "arbitrary"),
    )(x2, yconv, yattn, w_out, nw, w_cq, mk, mv, w_co)


def _outproj_q_sample_kernel(x_ref, yc_ref, ya_ref, wo_ref, nw_ref, wq_ref, x1_ref, qc_ref, *, dc, dh):
    x1 = x_ref[...] + _dot(yc_ref[...], wo_ref[0:dc, :]) + _dot(ya_ref[...], wo_ref[dc:, :])
    x1_ref[...] = x1
    h = _rms(x1, nw_ref[...]).astype(BF16)
    qc_ref[...] = (_dot(h, wq_ref[...]) * (dh ** -0.5)).astype(BF16)


def _outproj_q_sample(xt, yconv, yattn, w_out, nw, w_cq):
    d = xt.shape[1]
    return pl.pallas_call(
        functools.partial(_outproj_q_sample_kernel, dc=yconv.shape[1], dh=d // MEM_HEADS),
        out_shape=[jax.ShapeDtypeStruct(xt.shape, F32), jax.ShapeDtypeStruct(xt.shape, BF16)],
        compiler_params=pltpu.CompilerParams(vmem_limit_bytes=VMEM_LIMIT),
    )(xt, yconv, yattn, w_out, nw, w_cq)


def _cross_sample_kernel(qc_ref, mk_ref, mv_ref, o_ref, *, dh):
    _cross_heads(qc_ref[...], mk_ref, mv_ref, o_ref, dh)


def _cross_sample(qc, mk, mv, *, row_offset):
    bs, rows, d = qc.shape
    n_mem = mk.shape[1]
    per_b = lambda b: (b, 0, 0)
    mem_b = lambda b: (row_offset + b, 0, 0)
    return pl.pallas_call(
        functools.partial(_cross_sample_kernel, dh=d // MEM_HEADS),
        grid=(bs,),
        in_specs=[pl.BlockSpec((None, rows, d), per_b), pl.BlockSpec((None, n_mem, d), mem_b),
                  pl.BlockSpec((None, n_mem, d), mem_b)],
        out_specs=pl.BlockSpec((None, rows, d), per_b),
        out_shape=jax.ShapeDtypeStruct(qc.shape, BF16),
        compiler_params=_params("arbitrary"),
    )(qc, mk, mv)


def _silu(g):
    return g * (1.0 / (1.0 + jnp.exp(-g)))


def _ffn_prompt_kernel(x_ref, nw_ref, wu_ref, cw_ref, cb_ref, wd_ref, fw_ref, y_ref, st_ref,
                       up_scr, carry_scr, *, dff, final):
    j = pl.program_id(1)
    tm = x_ref.shape[0]
    x = x_ref[...]
    h = _rms(x, nw_ref[...]).astype(BF16)

    @pl.when(j == 0)
    def _zero_prefix():
        carry_scr[...] = jnp.zeros(carry_scr.shape, F32)

    def conv_half(lo):
        cols = slice(lo, lo + dff)
        up = _dot(h, wu_ref[:, cols])
        up_scr[0:SUBLANES, :] = carry_scr[:, cols]
        up_scr[SUBLANES:SUBLANES + tm, :] = up
        cw = cw_ref[:, cols]
        conv = (cb_ref[:, cols] + cw[0:1, :] * up_scr[SUBLANES - 2:SUBLANES - 2 + tm, :]
                + cw[1:2, :] * up_scr[SUBLANES - 1:SUBLANES - 1 + tm, :] + cw[2:3, :] * up)
        carry_scr[:, cols] = up[tm - SUBLANES:tm, :]
        st_ref[:, cols] = up[tm - 2:tm, :]
        return conv

    g = conv_half(0)
    u = conv_half(dff)
    x3 = x + _dot((_silu(g) * u).astype(BF16), wd_ref[...])
    y_ref[...] = _rms(x3, fw_ref[...]) if final else x3


def _ffn_prompt(x2, nw, w_up, cw, cb, w_down, fw, *, batch, seq, final):
    d = x2.shape[1]
    dff = w_down.shape[0]
    tm = min(FFN_ROW_TILE, seq)
    nj = seq // tm
    rows = lambda b, j: (b * nj + j, 0)
    const = lambda b, j: (0, 0)
    return pl.pallas_call(
        functools.partial(_ffn_prompt_kernel, dff=dff, final=final),
        grid=(batch, nj),
        in_specs=[pl.BlockSpec((tm, d), rows), pl.BlockSpec((1, d), const),
                  pl.BlockSpec(w_up.shape, const), pl.BlockSpec(cw.shape, const),
                  pl.BlockSpec((1, 2 * dff), const), pl.BlockSpec(w_down.shape, const),
                  pl.BlockSpec((1, d), const)],
        out_specs=[pl.BlockSpec((tm, d), rows),
                   pl.BlockSpec((None, CONV_W - 1, 2 * dff), lambda b, j: (b, 0, 0))],
        out_shape=[jax.ShapeDtypeStruct(x2.shape, F32),
                   jax.ShapeDtypeStruct((batch, CONV_W - 1, 2 * dff), F32)],
        scratch_shapes=[pltpu.VMEM((tm + SUBLANES, dff), F32),
                        pltpu.VMEM((SUBLANES, 2 * dff), F32)],
        compiler_params=_params("arbitrary", "arbitrary"),
    )(x2, nw, w_up, cw, cb, w_down, fw)


def _ffn_sample_kernel(x1_ref, o_ref, wc_ref, nw_ref, wu_ref, cw_ref, cb_ref, wd_ref, fw_ref,
                       s0_ref, s1_ref, y_ref, st_ref, hid_scr, *, dff, ts, bs, final):
    x2 = x1_ref[...] + _dot(o_ref[...], wc_ref[...])
    h = _rms(x2, nw_ref[...]).astype(BF16)

    def conv_half(lo):
        cols = slice(lo, lo + dff)
        up = _dot(h, wu_ref[:, cols])
        slabs = [s0_ref[:, cols], s1_ref[:, cols]] + [up[t * bs:(t + 1) * bs, :] for t in range(ts)]
        st_ref[0, :, cols] = slabs[-2]
        st_ref[1, :, cols] = slabs[-1]
        return _conv_time_major(slabs, cw_ref[:, cols], cb_ref[:, cols])

    g = conv_half(0)
    u = conv_half(dff)
    for t in range(ts):
        hid_scr[t * bs:(t + 1) * bs, :] = (_silu(g[t]) * u[t]).astype(BF16)
    x3 = x2 + _dot(hid_scr[...], wd_ref[...])
    y_ref[...] = _rms(x3, fw_ref[...]) if final else x3


def _ffn_sample(x1, o, w_co, nw, w_up, cw, cb, w_down, fw, s0, s1, *, ts, bs, final):
    dff = w_down.shape[0]
    return pl.pallas_call(
        functools.partial(_ffn_sample_kernel, dff=dff, ts=ts, bs=bs, final=final),
        out_shape=[jax.ShapeDtypeStruct(x1.shape, F32),
                   jax.ShapeDtypeStruct((CONV_W - 1, bs, 2 * dff), F32)],
        scratch_shapes=[pltpu.VMEM((ts * bs, dff), BF16)],
        compiler_params=pltpu.CompilerParams(vmem_limit_bytes=VMEM_LIMIT),
    )(x1, o, w_co, nw, w_up, cw, cb, w_down, fw, s0, s1)


def kernel(x_prompt, x_sample, mem_prompt, cache_k, cache_v, page_table, state_conv_mix, state_conv_ffn, cache_mem_k, cache_mem_v, rel_bias, norm_mix_w, w_in, conv_mix_w, conv_mix_b, lambda_q1, lambda_k1, lambda_q2, lambda_k2, subln_w, w_out, norm_cross_w, norm_mem_w, w_cq, w_ckv, w_co, norm_ffn_w, w_up, conv_ffn_w, conv_ffn_b, w_down, norm_final_w):
    depth = w_in.shape[0]
    bp, sp, d = x_prompt.shape
    bs, ts, _ = x_sample.shape
    n_mem = mem_prompt.shape[1]
    n_phys, page = cache_k.shape[1], cache_k.shape[2]
    n_pages = page_table.shape[1]
    past_len = n_pages * page
    dc = conv_mix_w.shape[2]
    da = N_HEADS * HEAD_DV
    dff = w_down.shape[1]
    dh = d // MEM_HEADS
    assert page == LANES and CONV_W - 1 <= ts <= SUBLANES and ts & (ts - 1) == 0
    assert (N_HEADS * ts) % SUBLANES == 0 and cache_k.shape[3:] == (N_HEADS, HEAD_DV)
    rows = 2 * SUBLANES

    assert _bucket_np(np.arange(REL_MAX_DIST, max(sp, past_len + ts) + 1)).min() == REL_BUCKETS - 1
    row = lambda a: a.reshape(1, -1).astype(F32)
    rel_bias = rel_bias.astype(F32)
    cache_k2 = cache_k.reshape(depth * n_phys * page * N_HEADS, HEAD_DV)
    cache_v2 = cache_v.reshape(depth * n_phys * page * N_HEADS, HEAD_DV)
    mem_k3 = cache_mem_k.reshape(depth * bs, n_mem, d)
    mem_v3 = cache_mem_v.reshape(depth * bs, n_mem, d)

    xp = x_prompt.reshape(bp * sp, d)
    xs = x_sample.transpose(1, 0, 2).reshape(ts * bs, d)
    mem2 = mem_prompt.reshape(bp * n_mem, d)
    outs = [[] for _ in range(10)]
    for l in range(depth):
        lam_init = 0.8 - 0.6 * math.exp(-0.3 * l)
        final = l == depth - 1
        w_in_b, w_out_b = w_in[l].astype(BF16), w_out[l].astype(BF16)
        w_cq_b, w_ckv_b, w_co_b = w_cq[l].astype(BF16), w_ckv[l].astype(BF16), w_co[l].astype(BF16)
        w_up_b, w_down_b = w_up[l].astype(BF16), w_down[l].astype(BF16)
        lam_args = (row(lambda_q1[l]), row(lambda_k1[l]), row(lambda_q2[l]), row(lambda_k2[l]),
                    row(subln_w[l]))

        yconv_p, q_p, kf_p, vf_p, kb_p, vb_p, cmix_p = _mix_in_prompt(
            xp, row(norm_mix_w[l]), w_in_b, conv_mix_w[l], row(conv_mix_b[l]),
            batch=bp, seq=sp, dc=dc, da=da)
        yattn_p = _attn_prompt(rel_bias, q_p, kb_p, vb_p, *lam_args, batch=bp, seq=sp, lam_init=lam_init)
        mkf, mvf, mkb, mvb = _mem_kv(mem2, row(norm_mem_w[l]), w_ckv_b, batch=bp, n_mem=n_mem)
        x2_p = _cross_prompt(xp, yconv_p, yattn_p, w_out_b, row(norm_cross_w[l]), w_cq_b, mkb, mvb,
                             w_co_b, batch=bp, seq=sp, n_mem=n_mem)
        xp, cffn_p = _ffn_prompt(x2_p, row(norm_ffn_w[l]), w_up_b, conv_ffn_w[l], row(conv_ffn_b[l]),
                                 w_down_b, row(norm_final_w), batch=bp, seq=sp, final=final)

        yconv_s, qm_s, kf_s, vf_s, cmix_s = _mix_in_sample(
            xs, row(norm_mix_w[l]), w_in_b, conv_mix_w[l], row(conv_mix_b[l]),
            state_conv_mix[l][:, 0], state_conv_mix[l][:, 1], dc=dc, da=da, ts=ts, bs=bs)
        q_all = qm_s.reshape(2, ts, bs, N_HEADS, HEAD_DV).transpose(2, 0, 3, 1, 4)
        q_all = q_all.reshape(bs, 2 * N_HEADS * ts, HEAD_DV)
        new_rows = lambda a: jnp.pad(
            a.reshape(ts, bs, N_HEADS * HEAD_DV).transpose(1, 0, 2).reshape(bs, ts * N_HEADS, HEAD_DV),
            ((0, 0), (0, LANES - ts * N_HEADS), (0, 0))).astype(BF16)
        yattn_s = _attn_sample(page_table, rel_bias, q_all, new_rows(kf_s), new_rows(vf_s), *lam_args,
                               cache_k2, cache_v2, page=page, page_offset=l * n_phys, ts=ts,
                               lam_init=lam_init)
        yattn_s = yattn_s.reshape(bs, N_HEADS, ts, HEAD_DV).transpose(2, 0, 1, 3)
        yattn_s = yattn_s.reshape(ts * bs, da).astype(BF16)
        x1_s, qc_s = _outproj_q_sample(xs, yconv_s, yattn_s, w_out_b, row(norm_cross_w[l]), w_cq_b)
        qc_b = jnp.pad(qc_s.reshape(ts, bs, d).transpose(1, 0, 2), ((0, 0), (0, rows - ts), (0, 0)))
        o_b = _cross_sample(qc_b, mem_k3, mem_v3, row_offset=l * bs)
        o_s = o_b[:, :ts].transpose(1, 0, 2).reshape(ts * bs, d)
        xs, cffn_s = _ffn_sample(x1_s, o_s, w_co_b, row(norm_ffn_w[l]), w_up_b, conv_ffn_w[l],
                                 row(conv_ffn_b[l]), w_down_b, row(norm_final_w),
                                 state_conv_ffn[l][:, 0], state_conv_ffn[l][:, 1],
                                 ts=ts, bs=bs, final=final)

        t2b = lambda a: a.reshape(ts, bs, N_HEADS, HEAD_DV).transpose(1, 0, 2, 3)
        for lst, val in zip(outs, (
                kf_p.reshape(bp, sp, N_HEADS, 2 * HEAD_DK), vf_p.reshape(bp, sp, N_HEADS, HEAD_DV),
                t2b(kf_s), t2b(vf_s), cmix_p, cmix_s.transpose(1, 0, 2), cffn_p,
                cffn_s.transpose(1, 0, 2), mkf.reshape(bp, n_mem, MEM_HEADS, dh),
                mvf.reshape(bp, n_mem, MEM_HEADS, dh))):
            lst.append(val)

    y_prompt = xp.reshape(bp, sp, d)
    y_sample = xs.reshape(ts, bs, d).transpose(1, 0, 2)
    return (y_prompt, y_sample) + tuple(jnp.stack(o) for o in outs)
```

```python
import functools
import math

import numpy as np
import jax
import jax.numpy as jnp
from jax import lax
from jax.experimental import pallas as pl
from jax.experimental.pallas import tpu as pltpu

F32 = jnp.float32
BF16 = jnp.bfloat16

EPS = 1e-6
NEG = -1e30
LANES = 128
SUBLANES = 8
N_HEADS = 4
HEAD_DK = 64
HEAD_DV = 2 * HEAD_DK
MEM_HEADS = 4
CONV_W = 3
REL_BUCKETS = 32
REL_MAX_EXACT = 16
REL_MAX_DIST = 128
VMEM_LIMIT = 56 * 1024 * 1024

ATTN_TILE = 512
QUERY_CHUNK = 256
ONES_ROWS = 16
LOG2E = 1.4426950408889634
ROW_TILE = 512
FFN_ROW_TILE = 256
PAGES_PER_STEP = 8


def _params(*sem):
    return pltpu.CompilerParams(dimension_semantics=sem, vmem_limit_bytes=VMEM_LIMIT)


def _rms(x, w):
    return x * lax.rsqrt(jnp.mean(x * x, axis=-1, keepdims=True) + EPS) * w


def _dot(a, b):
    return jnp.dot(a, b, preferred_element_type=F32)


def _dot_nt(a, b):
    return lax.dot_general(a, b, (((1,), (1,)), ((), ())), preferred_element_type=F32)


def _lam(lq1, lk1, lq2, lk2, lam_init):
    return (jnp.exp(jnp.sum(lq1[...] * lk1[...], axis=-1, keepdims=True))
            - jnp.exp(jnp.sum(lq2[...] * lk2[...], axis=-1, keepdims=True)) + lam_init)


def _bucket_np(rel):
    n = np.maximum(rel, 0)
    nf = np.maximum(n, 1).astype(np.float32)
    large = REL_MAX_EXACT + (np.log(nf / np.float32(REL_MAX_EXACT))
                             / np.float32(math.log(REL_MAX_DIST / REL_MAX_EXACT))
                             * np.float32(REL_BUCKETS - REL_MAX_EXACT)).astype(np.int32)
    large = np.minimum(large, REL_BUCKETS - 1)
    return np.where(n < REL_MAX_EXACT, n, large).astype(np.int32)


def _bucket_starts():
    buckets = _bucket_np(np.arange(REL_MAX_DIST + 1))
    assert (np.diff(buckets) >= 0).all() and buckets[-1] == REL_BUCKETS - 1
    return [int(np.argmax(buckets >= k)) for k in range(REL_BUCKETS)]


def _rel_bias_tile(rel, rb_ref, h):
    far = rb_ref[REL_BUCKETS - 1, h]
    val = jnp.full(rel.shape, rb_ref[0, h] - far, F32)
    for k, start in enumerate(_bucket_starts()):
        if k > 0:
            val = jnp.where(rel >= start, rb_ref[k, h] - far, val)
    return val


def _softmax_update(chunks, m_prev, l_prev):
    mx = chunks[0]
    for c in chunks[1:]:
        mx = jnp.maximum(mx, c)
    m_next = jnp.maximum(m_prev, jnp.max(mx, axis=1, keepdims=True))
    ps = [jnp.exp(c - m_next) for c in chunks]
    sm = ps[0]
    for p in ps[1:]:
        sm = sm + p
    alpha = jnp.exp(m_prev - m_next)
    l_next = alpha * l_prev + jnp.sum(sm, axis=1, keepdims=True)
    return ps, m_next, l_next, alpha


def _head_out(acc1, l1, acc2, l2, lam, sw, lam_init):
    o = acc1 / l1 - lam * (acc2 / l2)
    return _rms(o, sw) * (1.0 - lam_init)


def _mix_in_prompt_kernel(x_ref, nw_ref, w_ref, cw_ref, cb_ref,
                          yconv_ref, q_ref, kf_ref, vf_ref, kb_ref, vb_ref, st_ref,
                          pre_scr, *, dc, da):
    j = pl.program_id(1)
    tm = x_ref.shape[0]
    h = _rms(x_ref[...], nw_ref[...]).astype(BF16)

    def proj(lo, width):
        return _dot(h, w_ref[:, lo:lo + width])

    @pl.when(j == 0)
    def _zero_prefix():
        pre_scr[0:SUBLANES, :] = jnp.zeros((SUBLANES, dc), F32)

    pre = proj(dc, dc) * proj(2 * dc, dc)
    pre_scr[SUBLANES:SUBLANES + tm, :] = pre
    cw = cw_ref[...]
    conv = (cb_ref[...] + cw[0:1, :] * pre_scr[SUBLANES - 2:SUBLANES - 2 + tm, :]
            + cw[1:2, :] * pre_scr[SUBLANES - 1:SUBLANES - 1 + tm, :] + cw[2:3, :] * pre)
    yconv_ref[...] = (proj(0, dc) * conv).astype(BF16)
    st_ref[...] = pre[tm - 2:tm, :]
    pre_scr[0:SUBLANES, :] = pre[tm - SUBLANES:tm, :]

    q = proj(3 * dc, da) * (HEAD_DK ** -0.5 * LOG2E)
    k = proj(3 * dc + da, da)
    kf_ref[...] = k
    kb_ref[...] = k.astype(BF16)
    v = proj(3 * dc + 2 * da, da)
    vf_ref[...] = v
    ones_rows = (lax.broadcasted_iota(jnp.int32, (ONES_ROWS, tm), 0) == 0).astype(BF16)
    for hd in range(N_HEADS):
        cols = slice(hd * HEAD_DV, (hd + 1) * HEAD_DV)
        q_ref[hd] = q[:, cols].T.astype(BF16)
        vb_ref[hd, 0:HEAD_DV, :] = v[:, cols].T.astype(BF16)
        vb_ref[hd, HEAD_DV:, :] = ones_rows


def _mix_in_prompt(x2, nw, w_in, cw, cb, *, batch, seq, dc, da):
    d = x2.shape[1]
    tm = min(ATTN_TILE, seq)
    nj = seq // tm
    rows = lambda b, j: (b * nj + j, 0)
    const = lambda b, j: (0, 0)
    tposed = lambda r: pl.BlockSpec((None, N_HEADS, None, r, tm), lambda b, j: (b, 0, j, 0, 0))
    tposed_shape = lambda r: jax.ShapeDtypeStruct((batch, N_HEADS, nj, r, tm), BF16)
    n = batch * seq
    return pl.pallas_call(
        functools.partial(_mix_in_prompt_kernel, dc=dc, da=da),
        grid=(batch, nj),
        in_specs=[pl.BlockSpec((tm, d), rows),
                  pl.BlockSpec((1, d), const),
                  pl.BlockSpec(w_in.shape, const),
                  pl.BlockSpec(cw.shape, const),
                  pl.BlockSpec((1, dc), const)],
        out_specs=[pl.BlockSpec((tm, dc), rows),
                   tposed(HEAD_DV),
                   pl.BlockSpec((tm, da), rows),
                   pl.BlockSpec((tm, da), rows),
                   pl.BlockSpec((tm, da), rows),
                   tposed(HEAD_DV + ONES_ROWS),
                   pl.BlockSpec((None, CONV_W - 1, dc), lambda b, j: (b, 0, 0))],
        out_shape=[jax.ShapeDtypeStruct((n, dc), BF16),
                   tposed_shape(HEAD_DV),
                   jax.ShapeDtypeStruct((n, da), F32),
                   jax.ShapeDtypeStruct((n, da), F32),
                   jax.ShapeDtypeStruct((n, da), BF16),
                   tposed_shape(HEAD_DV + ONES_ROWS),
                   jax.ShapeDtypeStruct((batch, CONV_W - 1, dc), F32)],
        scratch_shapes=[pltpu.VMEM((tm + SUBLANES, dc), F32)],
        compiler_params=_params("arbitrary", "arbitrary"),
    )(x2, nw, w_in, cw, cb)


def _conv_time_major(slabs, cw, cb):
    return [cb + cw[0:1, :] * slabs[t] + cw[1:2, :] * slabs[t + 1] + cw[2:3, :] * slabs[t + 2]
            for t in range(len(slabs) - 2)]


def _mix_in_sample_kernel(x_ref, nw_ref, w_ref, cw_ref, cb_ref, s0_ref, s1_ref,
                          yconv_ref, qm_ref, kf_ref, vf_ref, st_ref, *, dc, da, ts, bs):
    h = _rms(x_ref[...], nw_ref[...]).astype(BF16)

    def proj(lo, width):
        return _dot(h, w_ref[:, lo:lo + width])

    pre = proj(dc, dc) * proj(2 * dc, dc)
    gate = proj(0, dc)
    slabs = [s0_ref[...], s1_ref[...]] + [pre[t * bs:(t + 1) * bs, :] for t in range(ts)]
    conv = _conv_time_major(slabs, cw_ref[...], cb_ref[...])
    for t in range(ts):
        yconv_ref[t * bs:(t + 1) * bs, :] = (gate[t * bs:(t + 1) * bs, :] * conv[t]).astype(BF16)
    st_ref[0] = slabs[-2]
    st_ref[1] = slabs[-1]

    q = proj(3 * dc, da) * (HEAD_DK ** -0.5)
    lane = lax.broadcasted_iota(jnp.int32, q.shape, 1) % HEAD_DV
    qm_ref[0] = jnp.where(lane < HEAD_DK, q, 0.0).astype(BF16)
    qm_ref[1] = jnp.where(lane >= HEAD_DK, q, 0.0).astype(BF16)
    kf_ref[...] = proj(3 * dc + da, da)
    vf_ref[...] = proj(3 * dc + 2 * da, da)


def _mix_in_sample(xt, nw, w_in, cw, cb, s0, s1, *, dc, da, ts, bs):
    n = ts * bs
    return pl.pallas_call(
        functools.partial(_mix_in_sample_kernel, dc=dc, da=da, ts=ts, bs=bs),
        out_shape=[jax.ShapeDtypeStruct((n, dc), BF16),
                   jax.ShapeDtypeStruct((2, n, da), BF16),
                   jax.ShapeDtypeStruct((n, da), F32),
                   jax.ShapeDtypeStruct((n, da), F32),
                   jax.ShapeDtypeStruct((CONV_W - 1, bs, dc), F32)],
        compiler_params=pltpu.CompilerParams(vmem_limit_bytes=VMEM_LIMIT),
    )(xt, nw, w_in, cw, cb, s0, s1)


def _mem_kv_kernel(m_ref, nw_ref, w_ref, kf_ref, vf_ref, kb_ref, vb_ref, *, d):
    h = _rms(m_ref[...], nw_ref[...]).astype(BF16)
    k = _dot(h, w_ref[:, 0:d])
    kf_ref[...] = k
    kb_ref[...] = k.astype(BF16)
    v = _dot(h, w_ref[:, d:2 * d])
    vf_ref[...] = v
    vb_ref[...] = v.astype(BF16)


def _mem_kv(mem2, nw, w_ckv, *, batch, n_mem):
    d = mem2.shape[1]
    rows = lambda b: (b, 0)
    const = lambda b: (0, 0)
    n = batch * n_mem
    return pl.pallas_call(
        functools.partial(_mem_kv_kernel, d=d),
        grid=(batch,),
        in_specs=[pl.BlockSpec((n_mem, d), rows), pl.BlockSpec((1, d), const),
                  pl.BlockSpec(w_ckv.shape, const)],
        out_specs=[pl.BlockSpec((n_mem, d), rows)] * 4,
        out_shape=[jax.ShapeDtypeStruct((n, d), F32), jax.ShapeDtypeStruct((n, d), F32),
                   jax.ShapeDtypeStruct((n, d), BF16), jax.ShapeDtypeStruct((n, d), BF16)],
        compiler_params=_params("arbitrary"),
    )(mem2, nw, w_ckv)


def _attn_prompt_kernel(rb_ref, q_ref, k_ref, vt_ref, lq1, lk1, lq2, lk2, sw_ref, o_ref,
                        qp_scr, m_scr, acc_scr, bd_scr, bs_scr, sa_scr, sb_scr, *, T, lam_init):
    qi = pl.program_id(2)
    nb = T // LANES

    @pl.when(qi == 0)
    def _assemble_bias_tiles():
        h = pl.program_id(1)
        rel = (lax.broadcasted_iota(jnp.int32, (LANES, LANES), 1)
               - lax.broadcasted_iota(jnp.int32, (LANES, LANES), 0))
        d0 = jnp.where(rel >= 0, _rel_bias_tile(jnp.maximum(rel, 0), rb_ref, h) * LOG2E, NEG)
        d1 = _rel_bias_tile(rel + LANES, rb_ref, h) * LOG2E
        zero = jnp.zeros((LANES, LANES), F32)
        neg = jnp.full((LANES, LANES), NEG, F32)
        for bi in range(nb):
            for bj in range(nb):
                rs = slice(bi * LANES, (bi + 1) * LANES)
                cs = slice(bj * LANES, (bj + 1) * LANES)
                bd_scr[rs, cs] = d0 if bi == bj else d1 if bj == bi + 1 else zero if bj > bi else neg
                bs_scr[rs, cs] = d1 if (bj == 0 and bi == nb - 1) else zero

    qt = q_ref[...].astype(F32)
    sub = lax.broadcasted_iota(jnp.int32, qt.shape, 0)
    qp_scr[:, 0:T] = jnp.where(sub < HEAD_DK, qt, 0.0).astype(BF16)
    qp_scr[:, T:2 * T] = jnp.where(sub >= HEAD_DK, qt, 0.0).astype(BF16)
    m_scr[...] = jnp.full(m_scr.shape, -jnp.inf, F32)
    acc_scr[...] = jnp.zeros(acc_scr.shape, F32)

    def scores(j, dst):
        dst[...] = _dot(k_ref[pl.ds(pl.multiple_of(j * T, T), T), :], qp_scr[...])

    def stage(j, src, bias_ref, j_next=None, dst=None):
        if j_next is not None:
            scores(j_next, dst)
        vt = vt_ref[j]
        for c in range(2 * T // QUERY_CHUNK):
            cs = slice(c * QUERY_CHUNK, (c + 1) * QUERY_CHUNK)
            s = src[:, cs]
            if bias_ref is not None:
                lo = (c * QUERY_CHUNK) % T
                s = s + bias_ref[:, lo:lo + QUERY_CHUNK]
            m_prev = m_scr[:, cs]
            m_next = jnp.maximum(m_prev, jnp.max(s, axis=0, keepdims=True))
            p = jnp.exp2(s - m_next).astype(BF16)
            acc_scr[:, cs] = jnp.exp2(m_prev - m_next) * acc_scr[:, cs] + _dot(vt, p)
            m_scr[:, cs] = m_next

    odd = qi % 2 == 0
    first = jnp.where(odd, 1, 0)

    @pl.when(odd)
    def _first_scores_odd():
        scores(0, sb_scr)

    @pl.when(jnp.logical_not(odd))
    def _first_scores_even():
        scores(0, sa_scr)

    @pl.when(odd & (qi >= 2))
    def _single_far_block():
        stage(0, sb_scr, None, 1, sa_scr)

    def far_pair(i, carry):
        j = first + 2 * i
        stage(j, sa_scr, None, j + 1, sb_scr)
        stage(j + 1, sb_scr, None, j + 2, sa_scr)
        return carry

    lax.fori_loop(0, jnp.maximum(qi - 1, 0) // 2, far_pair, 0)

    @pl.when(qi >= 1)
    def _sub_diagonal_and_diagonal():
        stage(qi - 1, sa_scr, bs_scr, qi, sb_scr)
        stage(qi, sb_scr, bd_scr)

    @pl.when(qi == 0)
    def _diagonal_only():
        stage(0, sb_scr, bd_scr)

    lam = _lam(lq1, lk1, lq2, lk2, lam_init)
    acc = acc_scr[0:HEAD_DV, :]
    l = acc_scr[HEAD_DV:HEAD_DV + 1, :]
    o = acc[:, 0:T] / l[:, 0:T] - lam * (acc[:, T:2 * T] / l[:, T:2 * T])
    y = o * lax.rsqrt(jnp.mean(o * o, axis=0, keepdims=True) + EPS) * sw_ref[...] * (1.0 - lam_init)
    o_ref[...] = y.T.astype(o_ref.dtype)


def _attn_prompt(rel_bias, qt, kb, vt, lq1, lk1, lq2, lk2, sw_col, *, batch, seq, lam_init):
    T = qt.shape[-1]
    nq = seq // T
    da = kb.shape[1]
    k3 = kb.reshape(batch, seq, da)
    vec = lambda b, h, i: (0, 0)
    return pl.pallas_call(
        functools.partial(_attn_prompt_kernel, T=T, lam_init=lam_init),
        grid=(batch, N_HEADS, nq),
        in_specs=[pl.BlockSpec(memory_space=pltpu.SMEM),
                  pl.BlockSpec((None, None, None, HEAD_DV, T), lambda b, h, i: (b, h, i, 0, 0)),
                  pl.BlockSpec((None, seq, HEAD_DV), lambda b, h, i: (b, 0, h)),
                  pl.BlockSpec((None, None, nq, vt.shape[3], T), lambda b, h, i: (b, h, 0, 0, 0)),
                  pl.BlockSpec((1, HEAD_DK), vec), pl.BlockSpec((1, HEAD_DK), vec),
                  pl.BlockSpec((1, HEAD_DK), vec), pl.BlockSpec((1, HEAD_DK), vec),
                  pl.BlockSpec((HEAD_DV, 1), vec)],
        out_specs=pl.BlockSpec((T, HEAD_DV), lambda b, h, i: (b * nq + i, h)),
        out_shape=jax.ShapeDtypeStruct((batch * seq, da), BF16),
        scratch_shapes=[pltpu.VMEM((HEAD_DV, 2 * T), BF16),
                        pltpu.VMEM((1, 2 * T), F32),
                        pltpu.VMEM((vt.shape[3], 2 * T), F32),
                        pltpu.VMEM((T, T), F32),
                        pltpu.VMEM((T, T), F32),
                        pltpu.VMEM((T, 2 * T), F32),
                        pltpu.VMEM((T, 2 * T), F32)],
        compiler_params=_params("arbitrary", "arbitrary", "arbitrary"),
    )(rel_bias, qt, k3, vt, lq1, lk1, lq2, lk2, sw_col)


def _attn_sample_kernel(pt_ref, rb_ref, q_ref, kn_ref, vn_ref, lq1, lk1, lq2, lk2, sw_ref,
                        *rest, pages, ts, lam_init):
    del pt_ref
    k_refs = rest[:pages]
    v_refs = rest[pages:2 * pages]
    o_ref = rest[2 * pages]
    m_scr, l_scr, acc_scr, hm_scr, bl_scr, bn_scr = rest[2 * pages + 1:]
    g = pl.program_id(1)
    last = g == pl.num_programs(1) - 1
    rows, pk = hm_scr.shape
    nk = bn_scr.shape[1]
    rm = rows // 2
    page = pk // N_HEADS
    hbits = N_HEADS.bit_length() - 1
    tbits = ts.bit_length() - 1

    @pl.when((pl.program_id(0) == 0) & (g == 0))
    def _mask_and_bias_tables():
        def tables(ncols, rel_of):
            r = lax.broadcasted_iota(jnp.int32, (rows, ncols), 0)
            c = lax.broadcasted_iota(jnp.int32, (rows, ncols), 1)
            rhead = (r >> tbits) & (N_HEADS - 1)
            same = rhead == (c & (N_HEADS - 1))
            rel = rel_of(r & (ts - 1), c >> hbits)
            bias = jnp.zeros((rows, ncols), F32)
            for h in range(N_HEADS):
                bias = jnp.where(rhead == h, _rel_bias_tile(jnp.maximum(rel, 0), rb_ref, h), bias)
            return same, rel, c >> hbits, bias

        same, _, _, bias = tables(pk, lambda tok, key: tok + page - key)
        hm_scr[...] = jnp.where(same, 0.0, NEG)
        bl_scr[...] = jnp.where(same, bias, NEG)
        same, rel, key, bias = tables(nk, lambda tok, key: tok - key)
        bn_scr[...] = jnp.where(same & (rel >= 0) & (key < ts), bias, NEG)

    @pl.when(g == 0)
    def _init():
        m_scr[...] = jnp.full(m_scr.shape, -jnp.inf, F32)
        l_scr[...] = jnp.zeros(l_scr.shape, F32)
        acc_scr[...] = jnp.zeros(acc_scr.shape, F32)

    def update(scores, values):
        chunks, sizes = [], []
        for s in scores:
            n = s.shape[1] // LANES
            sizes.append(n)
            chunks += [s[:, c * LANES:(c + 1) * LANES] for c in range(n)]
        ps, m_next, l_next, alpha = _softmax_update(chunks, m_scr[...], l_scr[...])
        pv, at = None, 0
        for n, v in zip(sizes, values):
            p = jnp.concatenate([x.astype(BF16) for x in ps[at:at + n]], axis=1) if n > 1 \
                else ps[at].astype(BF16)
            at += n
            pv = _dot(p, v) if pv is None else pv + _dot(p, v)
        acc_scr[...] = alpha * acc_scr[...] + pv
        m_scr[...] = m_next
        l_scr[...] = l_next

    q = q_ref[...]
    scores = [_dot_nt(q, k_refs[i][...].astype(BF16)) + hm_scr[...] for i in range(pages - 1)]
    scores.append(_dot_nt(q, k_refs[pages - 1][...].astype(BF16))
                  + jnp.where(last, bl_scr[...], hm_scr[...]))
    update(scores, [v_refs[i][...].astype(BF16) for i in range(pages)])

    @pl.when(last)
    def _new_rows_and_output():
        update([_dot_nt(q, kn_ref[...]) + bn_scr[...]], [vn_ref[...]])
        lam = _lam(lq1, lk1, lq2, lk2, lam_init)
        acc = acc_scr[...]
        l = l_scr[...]
        o_ref[...] = _head_out(acc[0:rm], l[0:rm], acc[rm:rows], l[rm:rows], lam, sw_ref[...], lam_init)


def _attn_sample(page_table, rel_bias, q_all, knew, vnew, lq1, lk1, lq2, lk2, sw, cache_k, cache_v,
                 *, page, page_offset, ts, lam_init):
    bs, n_pages = page_table.shape
    rows = q_all.shape[1]
    nk = knew.shape[1]
    pk = page * N_HEADS
    pages = math.gcd(PAGES_PER_STEP, n_pages)
    vec = lambda b, g, pt: (0, 0)
    per_b = lambda b, g, pt: (b, 0, 0)

    def page_spec(i):
        return pl.BlockSpec((pk, HEAD_DV), lambda b, g, pt: (page_offset + pt[b, g * pages + i], 0))

    grid_spec = pltpu.PrefetchScalarGridSpec(
        num_scalar_prefetch=1,
        grid=(bs, n_pages // pages),
        in_specs=[pl.BlockSpec(memory_space=pltpu.SMEM),
                  pl.BlockSpec((None, rows, HEAD_DV), per_b),
                  pl.BlockSpec((None, nk, HEAD_DV), per_b),
                  pl.BlockSpec((None, nk, HEAD_DV), per_b),
                  pl.BlockSpec((1, HEAD_DK), vec), pl.BlockSpec((1, HEAD_DK), vec),
                  pl.BlockSpec((1, HEAD_DK), vec), pl.BlockSpec((1, HEAD_DK), vec),
                  pl.BlockSpec((1, HEAD_DV), vec)]
                 + [page_spec(i) for i in range(pages)] * 2,
        out_specs=pl.BlockSpec((None, rows // 2, HEAD_DV), per_b),
        scratch_shapes=[pltpu.VMEM((rows, LANES), F32),
                        pltpu.VMEM((rows, LANES), F32),
                        pltpu.VMEM((rows, HEAD_DV), F32),
                        pltpu.VMEM((rows, pk), F32),
                        pltpu.VMEM((rows, pk), F32),
                        pltpu.VMEM((rows, nk), F32)],
    )
    return pl.pallas_call(
        functools.partial(_attn_sample_kernel, pages=pages, ts=ts, lam_init=lam_init),
        grid_spec=grid_spec,
        out_shape=jax.ShapeDtypeStruct((bs, rows // 2, HEAD_DV), F32),
        compiler_params=_params("arbitrary", "arbitrary"),
    )(page_table, rel_bias, q_all, knew, vnew, lq1, lk1, lq2, lk2, sw,
      *([cache_k] * pages), *([cache_v] * pages))


def _cross_heads(qc, mk_ref, mv_ref, o_scr, dh):
    for hd in range(MEM_HEADS):
        cols = slice(hd * dh, (hd + 1) * dh)
        s = _dot_nt(qc[:, cols], mk_ref[:, cols].astype(BF16))
        p = jnp.exp(s - jnp.max(s, axis=1, keepdims=True))
        o = _dot(p.astype(BF16), mv_ref[:, cols].astype(BF16)) / jnp.sum(p, axis=1, keepdims=True)
        o_scr[:, cols] = o.astype(BF16)


def _cross_prompt_kernel(x_ref, yc_ref, ya_ref, wo_ref, nw_ref, wq_ref, mk_ref, mv_ref, wc_ref,
                         o_ref, o_scr, *, dc, dh):
    x1 = x_ref[...] + _dot(yc_ref[...], wo_ref[0:dc, :]) + _dot(ya_ref[...], wo_ref[dc:, :])
    h = _rms(x1, nw_ref[...]).astype(BF16)
    qc = (_dot(h, wq_ref[...]) * (dh ** -0.5)).astype(BF16)
    _cross_heads(qc, mk_ref, mv_ref, o_scr, dh)
    o_ref[...] = x1 + _dot(o_scr[...], wc_ref[...])


def _cross_prompt(x2, yconv, yattn, w_out, nw, w_cq, mk, mv, w_co, *, batch, seq, n_mem):
    d = x2.shape[1]
    dc = yconv.shape[1]
    da = yattn.shape[1]
    tm = min(ROW_TILE, seq)
    nj = seq // tm
    rows = lambda b, j: (b * nj + j, 0)
    const = lambda b, j: (0, 0)
    memb = lambda b, j: (b, 0)
    return pl.pallas_call(
        functools.partial(_cross_prompt_kernel, dc=dc, dh=d // MEM_HEADS),
        grid=(batch, nj),
        in_specs=[pl.BlockSpec((tm, d), rows), pl.BlockSpec((tm, dc), rows), pl.BlockSpec((tm, da), rows),
                  pl.BlockSpec(w_out.shape, const), pl.BlockSpec((1, d), const),
                  pl.BlockSpec(w_cq.shape, const),
                  pl.BlockSpec((n_mem, d), memb), pl.BlockSpec((n_mem, d), memb),
                  pl.BlockSpec(w_co.shape, const)],
        out_specs=pl.BlockSpec((tm, d), rows),
        out_shape=jax.ShapeDtypeStruct(x2.shape, F32),
        scratch_shapes=[pltpu.VMEM((tm, d), BF16)],
        compiler_params=_params("arbitrary", "arbitrary"),
    )(x2, yconv, yattn, w_out, nw, w_cq, mk, mv, w_co)


def _outproj_q_sample_kernel(x_ref, yc_ref, ya_ref, wo_ref, nw_ref, wq_ref, x1_ref, qc_ref, *, dc, dh):
    x1 = x_ref[...] + _dot(yc_ref[...], wo_ref[0:dc, :]) + _dot(ya_ref[...], wo_ref[dc:, :])
    x1_ref[...] = x1
    h = _rms(x1, nw_ref[...]).astype(BF16)
    qc_ref[...] = (_dot(h, wq_ref[...]) * (dh ** -0.5)).astype(BF16)


def _outproj_q_sample(xt, yconv, yattn, w_out, nw, w_cq):
    d = xt.shape[1]
    return pl.pallas_call(
        functools.partial(_outproj_q_sample_kernel, dc=yconv.shape[1], dh=d // MEM_HEADS),
        out_shape=[jax.ShapeDtypeStruct(xt.shape, F32), jax.ShapeDtypeStruct(xt.shape, BF16)],
        compiler_params=pltpu.CompilerParams(vmem_limit_bytes=VMEM_LIMIT),
    )(xt, yconv, yattn, w_out, nw, w_cq)


def _cross_sample_kernel(qc_ref, mk_ref, mv_ref, o_ref, *, dh):
    _cross_heads(qc_ref[...], mk_ref, mv_ref, o_ref, dh)


def _cross_sample(qc, mk, mv, *, row_offset):
    bs, rows, d = qc.shape
    n_mem = mk.shape[1]
    per_b = lambda b: (b, 0, 0)
    mem_b = lambda b: (row_offset + b, 0, 0)
    return pl.pallas_call(
        functools.partial(_cross_sample_kernel, dh=d // MEM_HEADS),
        grid=(bs,),
        in_specs=[pl.BlockSpec((None, rows, d), per_b), pl.BlockSpec((None, n_mem, d), mem_b),
                  pl.BlockSpec((None, n_mem, d), mem_b)],
        out_specs=pl.BlockSpec((None, rows, d), per_b),
        out_shape=jax.ShapeDtypeStruct(qc.shape, BF16),
        compiler_params=_params("arbitrary"),
    )(qc, mk, mv)


def _silu(g):
    return g * (1.0 / (1.0 + jnp.exp(-g)))


def _ffn_prompt_kernel(x_ref, nw_ref, wu_ref, cw_ref, cb_ref, wd_ref, fw_ref, y_ref, st_ref,
                       up_scr, carry_scr, *, dff, final):
    j = pl.program_id(1)
    tm = x_ref.shape[0]
    x = x_ref[...]
    h = _rms(x, nw_ref[...]).astype(BF16)

    @pl.when(j == 0)
    def _zero_prefix():
        carry_scr[...] = jnp.zeros(carry_scr.shape, F32)

    def conv_half(lo):
        cols = slice(lo, lo + dff)
        up = _dot(h, wu_ref[:, cols])
        up_scr[0:SUBLANES, :] = carry_scr[:, cols]
        up_scr[SUBLANES:SUBLANES + tm, :] = up
        cw = cw_ref[:, cols]
        conv = (cb_ref[:, cols] + cw[0:1, :] * up_scr[SUBLANES - 2:SUBLANES - 2 + tm, :]
                + cw[1:2, :] * up_scr[SUBLANES - 1:SUBLANES - 1 + tm, :] + cw[2:3, :] * up)
        carry_scr[:, cols] = up[tm - SUBLANES:tm, :]
        st_ref[:, cols] = up[tm - 2:tm, :]
        return conv

    g = conv_half(0)
    u = conv_half(dff)
    x3 = x + _dot((_silu(g) * u).astype(BF16), wd_ref[...])
    y_ref[...] = _rms(x3, fw_ref[...]) if final else x3


def _ffn_prompt(x2, nw, w_up, cw, cb, w_down, fw, *, batch, seq, final):
    d = x2.shape[1]
    dff = w_down.shape[0]
    tm = min(FFN_ROW_TILE, seq)
    nj = seq // tm
    rows = lambda b, j: (b * nj + j, 0)
    const = lambda b, j: (0, 0)
    return pl.pallas_call(
        functools.partial(_ffn_prompt_kernel, dff=dff, final=final),
        grid=(batch, nj),
        in_specs=[pl.BlockSpec((tm, d), rows), pl.BlockSpec((1, d), const),
                  pl.BlockSpec(w_up.shape, const), pl.BlockSpec(cw.shape, const),
                  pl.BlockSpec((1, 2 * dff), const), pl.BlockSpec(w_down.shape, const),
                  pl.BlockSpec((1, d), const)],
        out_specs=[pl.BlockSpec((tm, d), rows),
                   pl.BlockSpec((None, CONV_W - 1, 2 * dff), lambda b, j: (b, 0, 0))],
        out_shape=[jax.ShapeDtypeStruct(x2.shape, F32),
                   jax.ShapeDtypeStruct((batch, CONV_W - 1, 2 * dff), F32)],
        scratch_shapes=[pltpu.VMEM((tm + SUBLANES, dff), F32),
                        pltpu.VMEM((SUBLANES, 2 * dff), F32)],
        compiler_params=_params("arbitrary", "arbitrary"),
    )(x2, nw, w_up, cw, cb, w_down, fw)


def _ffn_sample_kernel(x1_ref, o_ref, wc_ref, nw_ref, wu_ref, cw_ref, cb_ref, wd_ref, fw_ref,
                       s0_ref, s1_ref, y_ref, st_ref, hid_scr, *, dff, ts, bs, final):
    x2 = x1_ref[...] + _dot(o_ref[...], wc_ref[...])
    h = _rms(x2, nw_ref[...]).astype(BF16)

    def conv_half(lo):
        cols = slice(lo, lo + dff)
        up = _dot(h, wu_ref[:, cols])
        slabs = [s0_ref[:, cols], s1_ref[:, cols]] + [up[t * bs:(t + 1) * bs, :] for t in range(ts)]
        st_ref[0, :, cols] = slabs[-2]
        st_ref[1, :, cols] = slabs[-1]
        return _conv_time_major(slabs, cw_ref[:, cols], cb_ref[:, cols])

    g = conv_half(0)
    u = conv_half(dff)
    for t in range(ts):
        hid_scr[t * bs:(t + 1) * bs, :] = (_silu(g[t]) * u[t]).astype(BF16)
    x3 = x2 + _dot(hid_scr[...], wd_ref[...])
    y_ref[...] = _rms(x3, fw_ref[...]) if final else x3


def _ffn_sample(x1, o, w_co, nw, w_up, cw, cb, w_down, fw, s0, s1, *, ts, bs, final):
    dff = w_down.shape[0]
    return pl.pallas_call(
        functools.partial(_ffn_sample_kernel, dff=dff, ts=ts, bs=bs, final=final),
        out_shape=[jax.ShapeDtypeStruct(x1.shape, F32),
                   jax.ShapeDtypeStruct((CONV_W - 1, bs, 2 * dff), F32)],
        scratch_shapes=[pltpu.VMEM((ts * bs, dff), BF16)],
        compiler_params=pltpu.CompilerParams(vmem_limit_bytes=VMEM_LIMIT),
    )(x1, o, w_co, nw, w_up, cw, cb, w_down, fw, s0, s1)


def kernel(x_prompt, x_sample, mem_prompt, cache_k, cache_v, page_table, state_conv_mix, state_conv_ffn, cache_mem_k, cache_mem_v, rel_bias, norm_mix_w, w_in, conv_mix_w, conv_mix_b, lambda_q1, lambda_k1, lambda_q2, lambda_k2, subln_w, w_out, norm_cross_w, norm_mem_w, w_cq, w_ckv, w_co, norm_ffn_w, w_up, conv_ffn_w, conv_ffn_b, w_down, norm_final_w):
    depth = w_in.shape[0]
    bp, sp, d = x_prompt.shape
    bs, ts, _ = x_sample.shape
    n_mem = mem_prompt.shape[1]
    n_phys, page = cache_k.shape[1], cache_k.shape[2]
    n_pages = page_table.shape[1]
    past_len = n_pages * page
    dc = conv_mix_w.shape[2]
    da = N_HEADS * HEAD_DV
    dff = w_down.shape[1]
    dh = d // MEM_HEADS
    assert page == LANES and CONV_W - 1 <= ts <= SUBLANES and ts & (ts - 1) == 0
    assert (N_HEADS * ts) % SUBLANES == 0 and cache_k.shape[3:] == (N_HEADS, HEAD_DV)
    rows = 2 * SUBLANES

    assert _bucket_np(np.arange(REL_MAX_DIST, max(sp, past_len + ts) + 1)).min() == REL_BUCKETS - 1
    row = lambda a: a.reshape(1, -1).astype(F32)
    rel_bias = rel_bias.astype(F32)
    cache_k2 = cache_k.reshape(depth * n_phys * page * N_HEADS, HEAD_DV)
    cache_v2 = cache_v.reshape(depth * n_phys * page * N_HEADS, HEAD_DV)
    mem_k3 = cache_mem_k.reshape(depth * bs, n_mem, d)
    mem_v3 = cache_mem_v.reshape(depth * bs, n_mem, d)

    xp = x_prompt.reshape(bp * sp, d)
    xs = x_sample.transpose(1, 0, 2).reshape(ts * bs, d)
    mem2 = mem_prompt.reshape(bp * n_mem, d)
    outs = [[] for _ in range(10)]
    for l in range(depth):
        lam_init = 0.8 - 0.6 * math.exp(-0.3 * l)
        final = l == depth - 1
        w_in_b, w_out_b = w_in[l].astype(BF16), w_out[l].astype(BF16)
        w_cq_b, w_ckv_b, w_co_b = w_cq[l].astype(BF16), w_ckv[l].astype(BF16), w_co[l].astype(BF16)
        w_up_b, w_down_b = w_up[l].astype(BF16), w_down[l].astype(BF16)
        lam_args = (row(lambda_q1[l]), row(lambda_k1[l]), row(lambda_q2[l]), row(lambda_k2[l]),
                    row(subln_w[l]))

        yconv_p, q_p, kf_p, vf_p, kb_p, vb_p, cmix_p = _mix_in_prompt(
            xp, row(norm_mix_w[l]), w_in_b, conv_mix_w[l], row(conv_mix_b[l]),
            batch=bp, seq=sp, dc=dc, da=da)
        yattn_p = _attn_prompt(rel_bias, q_p, kb_p, vb_p, *lam_args[:4], subln_w[l].reshape(-1, 1).astype(F32),
                               batch=bp, seq=sp, lam_init=lam_init)
        mkf, mvf, mkb, mvb = _mem_kv(mem2, row(norm_mem_w[l]), w_ckv_b, batch=bp, n_mem=n_mem)
        x2_p = _cross_prompt(xp, yconv_p, yattn_p, w_out_b, row(norm_cross_w[l]), w_cq_b, mkb, mvb,
                             w_co_b, batch=bp, seq=sp, n_mem=n_mem)
        xp, cffn_p = _ffn_prompt(x2_p, row(norm_ffn_w[l]), w_up_b, conv_ffn_w[l], row(conv_ffn_b[l]),
                                 w_down_b, row(norm_final_w), batch=bp, seq=sp, final=final)

        yconv_s, qm_s, kf_s, vf_s, cmix_s = _mix_in_sample(
            xs, row(norm_mix_w[l]), w_in_b, conv_mix_w[l], row(conv_mix_b[l]),
            state_conv_mix[l][:, 0], state_conv_mix[l][:, 1], dc=dc, da=da, ts=ts, bs=bs)
        q_all = qm_s.reshape(2, ts, bs, N_HEADS, HEAD_DV).transpose(2, 0, 3, 1, 4)
        q_all = q_all.reshape(bs, 2 * N_HEADS * ts, HEAD_DV)
        new_rows = lambda a: jnp.pad(
            a.reshape(ts, bs, N_HEADS * HEAD_DV).transpose(1, 0, 2).reshape(bs, ts * N_HEADS, HEAD_DV),
            ((0, 0), (0, LANES - ts * N_HEADS), (0, 0))).astype(BF16)
        yattn_s = _attn_sample(page_table, rel_bias, q_all, new_rows(kf_s), new_rows(vf_s), *lam_args,
                               cache_k2, cache_v2, page=page, page_offset=l * n_phys, ts=ts,
                               lam_init=lam_init)
        yattn_s = yattn_s.reshape(bs, N_HEADS, ts, HEAD_DV).transpose(2, 0, 1, 3)
        yattn_s = yattn_s.reshape(ts * bs, da).astype(BF16)
        x1_s, qc_s = _outproj_q_sample(xs, yconv_s, yattn_s, w_out_b, row(norm_cross_w[l]), w_cq_b)
        qc_b = jnp.pad(qc_s.reshape(ts, bs, d).transpose(1, 0, 2), ((0, 0), (0, rows - ts), (0, 0)))
        o_b = _cross_sample(qc_b, mem_k3, mem_v3, row_offset=l * bs)
        o_s = o_b[:, :ts].transpose(1, 0, 2).reshape(ts * bs, d)
        xs, cffn_s = _ffn_sample(x1_s, o_s, w_co_b, row(norm_ffn_w[l]), w_up_b, conv_ffn_w[l],
                                 row(conv_ffn_b[l]), w_down_b, row(norm_final_w),
                                 state_conv_ffn[l][:, 0], state_conv_ffn[l][:, 1],
                                 ts=ts, bs=bs, final=final)

        t2b = lambda a: a.reshape(ts, bs, N_HEADS, HEAD_DV).transpose(1, 0, 2, 3)
        for lst, val in zip(outs, (
                kf_p.reshape(bp, sp, N_HEADS, 2 * HEAD_DK), vf_p.reshape(bp, sp, N_HEADS, HEAD_DV),
                t2b(kf_s), t2b(vf_s), cmix_p, cmix_s.transpose(1, 0, 2), cffn_p,
                cffn_s.transpose(1, 0, 2), mkf.reshape(bp, n_mem, MEM_HEADS, dh),
                mvf.reshape(bp, n_mem, MEM_HEADS, dh))):
            lst.append(val)

    y_prompt = xp.reshape(bp, sp, d)
    y_sample = xs.reshape(ts, bs, d).transpose(1, 0, 2)
    return (y_prompt, y_sample) + tuple(jnp.stack(o) for o in outs)
```

```python
import functools
import math

import numpy as np
import jax
import jax.numpy as jnp
from jax import lax
from jax.experimental import pallas as pl
from jax.experimental.pallas import tpu as pltpu

F32 = jnp.float32
BF16 = jnp.bfloat16

EPS = 1e-6
NEG = -1e30
LANES = 128
SUBLANES = 8
N_HEADS = 4
HEAD_DK = 64
HEAD_DV = 2 * HEAD_DK
MEM_HEADS = 4
CONV_W = 3
REL_BUCKETS = 32
REL_MAX_EXACT = 16
REL_MAX_DIST = 128
VMEM_LIMIT = 56 * 1024 * 1024

ATTN_TILE = 512
QUERY_CHUNK = 256
ONES_ROWS = 16
LOG2E = 1.4426950408889634
ROW_TILE = 512
FFN_ROW_TILE = 256
PAGES_PER_STEP = 16


def _params(*sem):
    return pltpu.CompilerParams(dimension_semantics=sem, vmem_limit_bytes=VMEM_LIMIT)


def _rms(x, w):
    return x * lax.rsqrt(jnp.mean(x * x, axis=-1, keepdims=True) + EPS) * w


def _dot(a, b):
    return jnp.dot(a, b, preferred_element_type=F32)


def _dot_nt(a, b):
    return lax.dot_general(a, b, (((1,), (1,)), ((), ())), preferred_element_type=F32)


def _lam(lq1, lk1, lq2, lk2, lam_init):
    return (jnp.exp(jnp.sum(lq1[...] * lk1[...], axis=-1, keepdims=True))
            - jnp.exp(jnp.sum(lq2[...] * lk2[...], axis=-1, keepdims=True)) + lam_init)


def _bucket_np(rel):
    n = np.maximum(rel, 0)
    nf = np.maximum(n, 1).astype(np.float32)
    large = REL_MAX_EXACT + (np.log(nf / np.float32(REL_MAX_EXACT))
                             / np.float32(math.log(REL_MAX_DIST / REL_MAX_EXACT))
                             * np.float32(REL_BUCKETS - REL_MAX_EXACT)).astype(np.int32)
    large = np.minimum(large, REL_BUCKETS - 1)
    return np.where(n < REL_MAX_EXACT, n, large).astype(np.int32)


def _bucket_starts():
    buckets = _bucket_np(np.arange(REL_MAX_DIST + 1))
    assert (np.diff(buckets) >= 0).all() and buckets[-1] == REL_BUCKETS - 1
    return [int(np.argmax(buckets >= k)) for k in range(REL_BUCKETS)]


def _rel_bias_tile(rel, rb_ref, h):
    far = rb_ref[REL_BUCKETS - 1, h]
    val = jnp.full(rel.shape, rb_ref[0, h] - far, F32)
    for k, start in enumerate(_bucket_starts()):
        if k > 0:
            val = jnp.where(rel >= start, rb_ref[k, h] - far, val)
    return val


def _softmax_update(chunks, m_prev, l_prev):
    mx = chunks[0]
    for c in chunks[1:]:
        mx = jnp.maximum(mx, c)
    m_next = jnp.maximum(m_prev, jnp.max(mx, axis=1, keepdims=True))
    ps = [jnp.exp(c - m_next) for c in chunks]
    sm = ps[0]
    for p in ps[1:]:
        sm = sm + p
    alpha = jnp.exp(m_prev - m_next)
    l_next = alpha * l_prev + jnp.sum(sm, axis=1, keepdims=True)
    return ps, m_next, l_next, alpha


def _head_out(acc1, l1, acc2, l2, lam, sw, lam_init):
    o = acc1 / l1 - lam * (acc2 / l2)
    return _rms(o, sw) * (1.0 - lam_init)


def _mix_in_prompt_kernel(x_ref, nw_ref, w_ref, cw_ref, cb_ref,
                          yconv_ref, q_ref, kf_ref, vf_ref, kb_ref, vb_ref, st_ref,
                          pre_scr, *, dc, da):
    j = pl.program_id(1)
    tm = x_ref.shape[0]
    h = _rms(x_ref[...], nw_ref[...]).astype(BF16)

    def proj(lo, width):
        return _dot(h, w_ref[:, lo:lo + width])

    @pl.when(j == 0)
    def _zero_prefix():
        pre_scr[0:SUBLANES, :] = jnp.zeros((SUBLANES, dc), F32)

    pre = proj(dc, dc) * proj(2 * dc, dc)
    pre_scr[SUBLANES:SUBLANES + tm, :] = pre
    cw = cw_ref[...]
    conv = (cb_ref[...] + cw[0:1, :] * pre_scr[SUBLANES - 2:SUBLANES - 2 + tm, :]
            + cw[1:2, :] * pre_scr[SUBLANES - 1:SUBLANES - 1 + tm, :] + cw[2:3, :] * pre)
    yconv_ref[...] = (proj(0, dc) * conv).astype(BF16)
    st_ref[...] = pre[tm - 2:tm, :]
    pre_scr[0:SUBLANES, :] = pre[tm - SUBLANES:tm, :]

    q = proj(3 * dc, da) * (HEAD_DK ** -0.5 * LOG2E)
    k = proj(3 * dc + da, da)
    kb_ref[...] = k.astype(BF16)
    v = proj(3 * dc + 2 * da, da)
    ones_rows = (lax.broadcasted_iota(jnp.int32, (ONES_ROWS, tm), 0) == 0).astype(BF16)
    for hd in range(N_HEADS):
        cols = slice(hd * HEAD_DV, (hd + 1) * HEAD_DV)
        kf_ref[pl.ds(hd, tm, stride=N_HEADS), :] = k[:, cols]
        vf_ref[pl.ds(hd, tm, stride=N_HEADS), :] = v[:, cols]
        q_ref[hd] = q[:, cols].T.astype(BF16)
        vb_ref[hd, 0:HEAD_DV, :] = v[:, cols].T.astype(BF16)
        vb_ref[hd, HEAD_DV:, :] = ones_rows


def _mix_in_prompt(x2, nw, w_in, cw, cb, *, batch, seq, dc, da):
    d = x2.shape[1]
    tm = min(ATTN_TILE, seq)
    nj = seq // tm
    rows = lambda b, j: (b * nj + j, 0)
    const = lambda b, j: (0, 0)
    tposed = lambda r: pl.BlockSpec((None, N_HEADS, None, r, tm), lambda b, j: (b, 0, j, 0, 0))
    tposed_shape = lambda r: jax.ShapeDtypeStruct((batch, N_HEADS, nj, r, tm), BF16)
    n = batch * seq
    return pl.pallas_call(
        functools.partial(_mix_in_prompt_kernel, dc=dc, da=da),
        grid=(batch, nj),
        in_specs=[pl.BlockSpec((tm, d), rows),
                  pl.BlockSpec((1, d), const),
                  pl.BlockSpec(w_in.shape, const),
                  pl.BlockSpec(cw.shape, const),
                  pl.BlockSpec((1, dc), const)],
        out_specs=[pl.BlockSpec((tm, dc), rows),
                   tposed(HEAD_DV),
                   pl.BlockSpec((tm * N_HEADS, HEAD_DV), rows),
                   pl.BlockSpec((tm * N_HEADS, HEAD_DV), rows),
                   pl.BlockSpec((tm, da), rows),
                   tposed(HEAD_DV + ONES_ROWS),
                   pl.BlockSpec((None, CONV_W - 1, dc), lambda b, j: (b, 0, 0))],
        out_shape=[jax.ShapeDtypeStruct((n, dc), BF16),
                   tposed_shape(HEAD_DV),
                   jax.ShapeDtypeStruct((n * N_HEADS, HEAD_DV), F32),
                   jax.ShapeDtypeStruct((n * N_HEADS, HEAD_DV), F32),
                   jax.ShapeDtypeStruct((n, da), BF16),
                   tposed_shape(HEAD_DV + ONES_ROWS),
                   jax.ShapeDtypeStruct((batch, CONV_W - 1, dc), F32)],
        scratch_shapes=[pltpu.VMEM((tm + SUBLANES, dc), F32)],
        compiler_params=_params("arbitrary", "arbitrary"),
    )(x2, nw, w_in, cw, cb)


def _conv_time_major(slabs, cw, cb):
    return [cb + cw[0:1, :] * slabs[t] + cw[1:2, :] * slabs[t + 1] + cw[2:3, :] * slabs[t + 2]
            for t in range(len(slabs) - 2)]


def _mix_in_sample_kernel(x_ref, nw_ref, w_ref, cw_ref, cb_ref, s0_ref, s1_ref,
                          yconv_ref, qm_ref, kf_ref, vf_ref, st_ref, *, dc, da, ts, bs):
    h = _rms(x_ref[...], nw_ref[...]).astype(BF16)

    def proj(lo, width):
        return _dot(h, w_ref[:, lo:lo + width])

    pre = proj(dc, dc) * proj(2 * dc, dc)
    gate = proj(0, dc)
    slabs = [s0_ref[...], s1_ref[...]] + [pre[t * bs:(t + 1) * bs, :] for t in range(ts)]
    conv = _conv_time_major(slabs, cw_ref[...], cb_ref[...])
    for t in range(ts):
        yconv_ref[t * bs:(t + 1) * bs, :] = (gate[t * bs:(t + 1) * bs, :] * conv[t]).astype(BF16)
    st_ref[0] = slabs[-2]
    st_ref[1] = slabs[-1]

    q = proj(3 * dc, da) * (HEAD_DK ** -0.5)
    lane = lax.broadcasted_iota(jnp.int32, q.shape, 1) % HEAD_DV
    qm_ref[0] = jnp.where(lane < HEAD_DK, q, 0.0).astype(BF16)
    qm_ref[1] = jnp.where(lane >= HEAD_DK, q, 0.0).astype(BF16)
    kf_ref[...] = proj(3 * dc + da, da)
    vf_ref[...] = proj(3 * dc + 2 * da, da)


def _mix_in_sample(xt, nw, w_in, cw, cb, s0, s1, *, dc, da, ts, bs):
    n = ts * bs
    return pl.pallas_call(
        functools.partial(_mix_in_sample_kernel, dc=dc, da=da, ts=ts, bs=bs),
        out_shape=[jax.ShapeDtypeStruct((n, dc), BF16),
                   jax.ShapeDtypeStruct((2, n, da), BF16),
                   jax.ShapeDtypeStruct((n, da), F32),
                   jax.ShapeDtypeStruct((n, da), F32),
                   jax.ShapeDtypeStruct((CONV_W - 1, bs, dc), F32)],
        compiler_params=pltpu.CompilerParams(vmem_limit_bytes=VMEM_LIMIT),
    )(xt, nw, w_in, cw, cb, s0, s1)


def _mem_kv_kernel(m_ref, nw_ref, w_ref, kf_ref, vf_ref, kb_ref, vb_ref, *, d):
    h = _rms(m_ref[...], nw_ref[...]).astype(BF16)
    k = _dot(h, w_ref[:, 0:d])
    kf_ref[...] = k
    kb_ref[...] = k.astype(BF16)
    v = _dot(h, w_ref[:, d:2 * d])
    vf_ref[...] = v
    vb_ref[...] = v.astype(BF16)


def _mem_kv(mem2, nw, w_ckv, *, batch, n_mem):
    d = mem2.shape[1]
    rows = lambda b: (b, 0)
    const = lambda b: (0, 0)
    n = batch * n_mem
    return pl.pallas_call(
        functools.partial(_mem_kv_kernel, d=d),
        grid=(batch,),
        in_specs=[pl.BlockSpec((n_mem, d), rows), pl.BlockSpec((1, d), const),
                  pl.BlockSpec(w_ckv.shape, const)],
        out_specs=[pl.BlockSpec((n_mem, d), rows)] * 4,
        out_shape=[jax.ShapeDtypeStruct((n, d), F32), jax.ShapeDtypeStruct((n, d), F32),
                   jax.ShapeDtypeStruct((n, d), BF16), jax.ShapeDtypeStruct((n, d), BF16)],
        compiler_params=_params("arbitrary"),
    )(mem2, nw, w_ckv)


def _attn_prompt_kernel(rb_ref, q_ref, k_ref, vt_ref, lq1, lk1, lq2, lk2, sw_ref, o_ref,
                        qp_scr, m_scr, acc_scr, bd_scr, bs_scr, sa_scr, sb_scr, *, T, lam_init):
    qi = pl.program_id(2)
    nb = T // LANES

    @pl.when(qi == 0)
    def _assemble_bias_tiles():
        h = pl.program_id(1)
        rel = (lax.broadcasted_iota(jnp.int32, (LANES, LANES), 1)
               - lax.broadcasted_iota(jnp.int32, (LANES, LANES), 0))
        d0 = jnp.where(rel >= 0, _rel_bias_tile(jnp.maximum(rel, 0), rb_ref, h) * LOG2E, NEG)
        d1 = _rel_bias_tile(rel + LANES, rb_ref, h) * LOG2E
        zero = jnp.zeros((LANES, LANES), F32)
        neg = jnp.full((LANES, LANES), NEG, F32)
        for bi in range(nb):
            for bj in range(nb):
                rs = slice(bi * LANES, (bi + 1) * LANES)
                cs = slice(bj * LANES, (bj + 1) * LANES)
                bd_scr[rs, cs] = d0 if bi == bj else d1 if bj == bi + 1 else zero if bj > bi else neg
                bs_scr[rs, cs] = d1 if (bj == 0 and bi == nb - 1) else zero

    qt = q_ref[...].astype(F32)
    sub = lax.broadcasted_iota(jnp.int32, qt.shape, 0)
    qp_scr[:, 0:T] = jnp.where(sub < HEAD_DK, qt, 0.0).astype(BF16)
    qp_scr[:, T:2 * T] = jnp.where(sub >= HEAD_DK, qt, 0.0).astype(BF16)
    m_scr[...] = jnp.full(m_scr.shape, -jnp.inf, F32)
    acc_scr[...] = jnp.zeros(acc_scr.shape, F32)

    def scores(j, dst):
        dst[...] = _dot(k_ref[pl.ds(pl.multiple_of(j * T, T), T), :], qp_scr[...])

    def stage(j, src, bias_ref, j_next=None, dst=None):
        if j_next is not None:
            scores(j_next, dst)
        vt = vt_ref[j]
        for c in range(2 * T // QUERY_CHUNK):
            cs = slice(c * QUERY_CHUNK, (c + 1) * QUERY_CHUNK)
            s = src[:, cs]
            if bias_ref is not None:
                lo = (c * QUERY_CHUNK) % T
                s = s + bias_ref[:, lo:lo + QUERY_CHUNK]
            m_prev = m_scr[:, cs]
            m_next = jnp.maximum(m_prev, jnp.max(s, axis=0, keepdims=True))
            p = jnp.exp2(s - m_next).astype(BF16)
            acc_scr[:, cs] = jnp.exp2(m_prev - m_next) * acc_scr[:, cs] + _dot(vt, p)
            m_scr[:, cs] = m_next

    odd = qi % 2 == 0
    first = jnp.where(odd, 1, 0)

    @pl.when(odd)
    def _first_scores_odd():
        scores(0, sb_scr)

    @pl.when(jnp.logical_not(odd))
    def _first_scores_even():
        scores(0, sa_scr)

    @pl.when(odd & (qi >= 2))
    def _single_far_block():
        stage(0, sb_scr, None, 1, sa_scr)

    def far_pair(i, carry):
        j = first + 2 * i
        stage(j, sa_scr, None, j + 1, sb_scr)
        stage(j + 1, sb_scr, None, j + 2, sa_scr)
        return carry

    lax.fori_loop(0, jnp.maximum(qi - 1, 0) // 2, far_pair, 0)

    @pl.when(qi >= 1)
    def _sub_diagonal_and_diagonal():
        stage(qi - 1, sa_scr, bs_scr, qi, sb_scr)
        stage(qi, sb_scr, bd_scr)

    @pl.when(qi == 0)
    def _diagonal_only():
        stage(0, sb_scr, bd_scr)

    lam = _lam(lq1, lk1, lq2, lk2, lam_init)
    acc = acc_scr[0:HEAD_DV, :]
    l = acc_scr[HEAD_DV:HEAD_DV + 1, :]
    o = acc[:, 0:T] / l[:, 0:T] - lam * (acc[:, T:2 * T] / l[:, T:2 * T])
    y = o * lax.rsqrt(jnp.mean(o * o, axis=0, keepdims=True) + EPS) * sw_ref[...] * (1.0 - lam_init)
    o_ref[...] = y.T.astype(o_ref.dtype)


def _attn_prompt(rel_bias, qt, kb, vt, lq1, lk1, lq2, lk2, sw_col, *, batch, seq, lam_init):
    T = qt.shape[-1]
    nq = seq // T
    da = kb.shape[1]
    k3 = kb.reshape(batch, seq, da)
    vec = lambda b, h, i: (0, 0)
    return pl.pallas_call(
        functools.partial(_attn_prompt_kernel, T=T, lam_init=lam_init),
        grid=(batch, N_HEADS, nq),
        in_specs=[pl.BlockSpec(memory_space=pltpu.SMEM),
                  pl.BlockSpec((None, None, None, HEAD_DV, T), lambda b, h, i: (b, h, i, 0, 0)),
                  pl.BlockSpec((None, seq, HEAD_DV), lambda b, h, i: (b, 0, h)),
                  pl.BlockSpec((None, None, nq, vt.shape[3], T), lambda b, h, i: (b, h, 0, 0, 0)),
                  pl.BlockSpec((1, HEAD_DK), vec), pl.BlockSpec((1, HEAD_DK), vec),
                  pl.BlockSpec((1, HEAD_DK), vec), pl.BlockSpec((1, HEAD_DK), vec),
                  pl.BlockSpec((HEAD_DV, 1), vec)],
        out_specs=pl.BlockSpec((T, HEAD_DV), lambda b, h, i: (b * nq + i, h)),
        out_shape=jax.ShapeDtypeStruct((batch * seq, da), BF16),
        scratch_shapes=[pltpu.VMEM((HEAD_DV, 2 * T), BF16),
                        pltpu.VMEM((1, 2 * T), F32),
                        pltpu.VMEM((vt.shape[3], 2 * T), F32),
                        pltpu.VMEM((T, T), F32),
                        pltpu.VMEM((T, T), F32),
                        pltpu.VMEM((T, 2 * T), F32),
                        pltpu.VMEM((T, 2 * T), F32)],
        compiler_params=_params("arbitrary", "arbitrary", "arbitrary"),
    )(rel_bias, qt, k3, vt, lq1, lk1, lq2, lk2, sw_col)


def _attn_sample_kernel(pt_ref, rb_ref, q_ref, kn_ref, vn_ref, lq1, lk1, lq2, lk2, sw_ref,
                        *rest, pages, ts, lam_init):
    del pt_ref
    k_refs = rest[:pages]
    v_refs = rest[pages:2 * pages]
    o_ref = rest[2 * pages]
    m_scr, l_scr, acc_scr, hm_scr, bl_scr, bn_scr = rest[2 * pages + 1:]
    g = pl.program_id(1)
    last = g == pl.num_programs(1) - 1
    rows, pk = hm_scr.shape
    nk = bn_scr.shape[1]
    rm = rows // 2
    page = pk // N_HEADS
    hbits = N_HEADS.bit_length() - 1
    tbits = ts.bit_length() - 1

    @pl.when((pl.program_id(0) == 0) & (g == 0))
    def _mask_and_bias_tables():
        def tables(ncols, rel_of):
            r = lax.broadcasted_iota(jnp.int32, (rows, ncols), 0)
            c = lax.broadcasted_iota(jnp.int32, (rows, ncols), 1)
            rhead = (r >> tbits) & (N_HEADS - 1)
            same = rhead == (c & (N_HEADS - 1))
            rel = rel_of(r & (ts - 1), c >> hbits)
            bias = jnp.zeros((rows, ncols), F32)
            for h in range(N_HEADS):
                bias = jnp.where(rhead == h, _rel_bias_tile(jnp.maximum(rel, 0), rb_ref, h), bias)
            return same, rel, c >> hbits, bias

        same, _, _, bias = tables(pk, lambda tok, key: tok + page - key)
        hm_scr[...] = jnp.where(same, 0.0, NEG)
        bl_scr[...] = jnp.where(same, bias, NEG)
        same, rel, key, bias = tables(nk, lambda tok, key: tok - key)
        bn_scr[...] = jnp.where(same & (rel >= 0) & (key < ts), bias, NEG)

    @pl.when(g == 0)
    def _init():
        m_scr[...] = jnp.full(m_scr.shape, -jnp.inf, F32)
        l_scr[...] = jnp.zeros(l_scr.shape, F32)
        acc_scr[...] = jnp.zeros(acc_scr.shape, F32)

    def update(scores, values):
        chunks, sizes = [], []
        for s in scores:
            n = s.shape[1] // LANES
            sizes.append(n)
            chunks += [s[:, c * LANES:(c + 1) * LANES] for c in range(n)]
        ps, m_next, l_next, alpha = _softmax_update(chunks, m_scr[...], l_scr[...])
        pv, at = None, 0
        for n, v in zip(sizes, values):
            p = jnp.concatenate([x.astype(BF16) for x in ps[at:at + n]], axis=1) if n > 1 \
                else ps[at].astype(BF16)
            at += n
            pv = _dot(p, v) if pv is None else pv + _dot(p, v)
        acc_scr[...] = alpha * acc_scr[...] + pv
        m_scr[...] = m_next
        l_scr[...] = l_next

    q = q_ref[...]
    scores = [_dot_nt(q, k_refs[i][...].astype(BF16)) + hm_scr[...] for i in range(pages - 1)]
    scores.append(_dot_nt(q, k_refs[pages - 1][...].astype(BF16))
                  + jnp.where(last, bl_scr[...], hm_scr[...]))
    update(scores, [v_refs[i][...].astype(BF16) for i in range(pages)])

    @pl.when(last)
    def _new_rows_and_output():
        update([_dot_nt(q, kn_ref[...]) + bn_scr[...]], [vn_ref[...]])
        lam = _lam(lq1, lk1, lq2, lk2, lam_init)
        acc = acc_scr[...]
        l = l_scr[...]
        o_ref[...] = _head_out(acc[0:rm], l[0:rm], acc[rm:rows], l[rm:rows], lam, sw_ref[...], lam_init)


def _attn_sample(page_table, rel_bias, q_all, knew, vnew, lq1, lk1, lq2, lk2, sw, cache_k, cache_v,
                 *, page, page_offset, ts, lam_init):
    bs, n_pages = page_table.shape
    rows = q_all.shape[1]
    nk = knew.shape[1]
    pk = page * N_HEADS
    pages = math.gcd(PAGES_PER_STEP, n_pages)
    vec = lambda b, g, pt: (0, 0)
    per_b = lambda b, g, pt: (b, 0, 0)

    def page_spec(i):
        return pl.BlockSpec((pk, HEAD_DV), lambda b, g, pt: (page_offset + pt[b, g * pages + i], 0))

    grid_spec = pltpu.PrefetchScalarGridSpec(
        num_scalar_prefetch=1,
        grid=(bs, n_pages // pages),
        in_specs=[pl.BlockSpec(memory_space=pltpu.SMEM),
                  pl.BlockSpec((None, rows, HEAD_DV), per_b),
                  pl.BlockSpec((None, nk, HEAD_DV), per_b),
                  pl.BlockSpec((None, nk, HEAD_DV), per_b),
                  pl.BlockSpec((1, HEAD_DK), vec), pl.BlockSpec((1, HEAD_DK), vec),
                  pl.BlockSpec((1, HEAD_DK), vec), pl.BlockSpec((1, HEAD_DK), vec),
                  pl.BlockSpec((1, HEAD_DV), vec)]
                 + [page_spec(i) for i in range(pages)] * 2,
        out_specs=pl.BlockSpec((None, rows // 2, HEAD_DV), per_b),
        scratch_shapes=[pltpu.VMEM((rows, LANES), F32),
                        pltpu.VMEM((rows, LANES), F32),
                        pltpu.VMEM((rows, HEAD_DV), F32),
                        pltpu.VMEM((rows, pk), F32),
                        pltpu.VMEM((rows, pk), F32),
                        pltpu.VMEM((rows, nk), F32)],
    )
    return pl.pallas_call(
        functools.partial(_attn_sample_kernel, pages=pages, ts=ts, lam_init=lam_init),
        grid_spec=grid_spec,
        out_shape=jax.ShapeDtypeStruct((bs, rows // 2, HEAD_DV), F32),
        compiler_params=_params("arbitrary", "arbitrary"),
    )(page_table, rel_bias, q_all, knew, vnew, lq1, lk1, lq2, lk2, sw,
      *([cache_k] * pages), *([cache_v] * pages))


def _cross_heads(qc, mk_ref, mv_ref, o_scr, dh):
    for hd in range(MEM_HEADS):
        cols = slice(hd * dh, (hd + 1) * dh)
        s = _dot_nt(qc[:, cols], mk_ref[:, cols].astype(BF16))
        p = jnp.exp(s - jnp.max(s, axis=1, keepdims=True))
        o = _dot(p.astype(BF16), mv_ref[:, cols].astype(BF16)) / jnp.sum(p, axis=1, keepdims=True)
        o_scr[:, cols] = o.astype(BF16)


def _cross_prompt_kernel(x_ref, yc_ref, ya_ref, wo_ref, nw_ref, wq_ref, mk_ref, mv_ref, wc_ref,
                         o_ref, o_scr, *, dc, dh):
    x1 = x_ref[...] + _dot(yc_ref[...], wo_ref[0:dc, :]) + _dot(ya_ref[...], wo_ref[dc:, :])
    h = _rms(x1, nw_ref[...]).astype(BF16)
    qc = (_dot(h, wq_ref[...]) * (dh ** -0.5)).astype(BF16)
    _cross_heads(qc, mk_ref, mv_ref, o_scr, dh)
    o_ref[...] = x1 + _dot(o_scr[...], wc_ref[...])


def _cross_prompt(x2, yconv, yattn, w_out, nw, w_cq, mk, mv, w_co, *, batch, seq, n_mem):
    d = x2.shape[1]
    dc = yconv.shape[1]
    da = yattn.shape[1]
    tm = min(ROW_TILE, seq)
    nj = seq // tm
    rows = lambda b, j: (b * nj + j, 0)
    const = lambda b, j: (0, 0)
    memb = lambda b, j: (b, 0)
    return pl.pallas_call(
        functools.partial(_cross_prompt_kernel, dc=dc, dh=d // MEM_HEADS),
        grid=(batch, nj),
        in_specs=[pl.BlockSpec((tm, d), rows), pl.BlockSpec((tm, dc), rows), pl.BlockSpec((tm, da), rows),
                  pl.BlockSpec(w_out.shape, const), pl.BlockSpec((1, d), const),
                  pl.BlockSpec(w_cq.shape, const),
                  pl.BlockSpec((n_mem, d), memb), pl.BlockSpec((n_mem, d), memb),
                  pl.BlockSpec(w_co.shape, const)],
        out_specs=pl.BlockSpec((tm, d), rows),
        out_shape=jax.ShapeDtypeStruct(x2.shape, F32),
        scratch_shapes=[pltpu.VMEM((tm, d), BF16)],
        compiler_params=_params("arbitrary", "arbitrary"),
    )(x2, yconv, yattn, w_out, nw, w_cq, mk, mv, w_co)


def _outproj_q_sample_kernel(x_ref, yc_ref, ya_ref, wo_ref, nw_ref, wq_ref, x1_ref, qc_ref, *, dc, dh):
    x1 = x_ref[...] + _dot(yc_ref[...], wo_ref[0:dc, :]) + _dot(ya_ref[...], wo_ref[dc:, :])
    x1_ref[...] = x1
    h = _rms(x1, nw_ref[...]).astype(BF16)
    qc_ref[...] = (_dot(h, wq_ref[...]) * (dh ** -0.5)).astype(BF16)


def _outproj_q_sample(xt, yconv, yattn, w_out, nw, w_cq):
    d = xt.shape[1]
    return pl.pallas_call(
        functools.partial(_outproj_q_sample_kernel, dc=yconv.shape[1], dh=d // MEM_HEADS),
        out_shape=[jax.ShapeDtypeStruct(xt.shape, F32), jax.ShapeDtypeStruct(xt.shape, BF16)],
        compiler_params=pltpu.CompilerParams(vmem_limit_bytes=VMEM_LIMIT),
    )(xt, yconv, yattn, w_out, nw, w_cq)


def _cross_sample_kernel(qc_ref, mk_ref, mv_ref, o_ref, *, dh):
    qc = qc_ref[...]
    for hd in range(MEM_HEADS):
        cols = slice(hd * dh, (hd + 1) * dh)
        s = _dot_nt(qc[:, cols], mk_ref[:, hd, :].astype(BF16))
        p = jnp.exp(s - jnp.max(s, axis=1, keepdims=True))
        o = _dot(p.astype(BF16), mv_ref[:, hd, :].astype(BF16)) / jnp.sum(p, axis=1, keepdims=True)
        o_ref[:, cols] = o.astype(BF16)


def _cross_sample(qc, mk, mv, *, row_offset):
    bs, rows, d = qc.shape
    n_mem = mk.shape[1]
    per_b = lambda b: (b, 0, 0)
    mem_b = lambda b: (row_offset + b, 0, 0, 0)
    mem_spec = pl.BlockSpec((None, n_mem, MEM_HEADS, d // MEM_HEADS), mem_b)
    return pl.pallas_call(
        functools.partial(_cross_sample_kernel, dh=d // MEM_HEADS),
        grid=(bs,),
        in_specs=[pl.BlockSpec((None, rows, d), per_b), mem_spec, mem_spec],
        out_specs=pl.BlockSpec((None, rows, d), per_b),
        out_shape=jax.ShapeDtypeStruct(qc.shape, BF16),
        compiler_params=_params("arbitrary"),
    )(qc, mk, mv)


def _silu(g):
    return g * (1.0 / (1.0 + jnp.exp(-g)))


def _ffn_prompt_kernel(x_ref, nw_ref, wu_ref, cw_ref, cb_ref, wd_ref, fw_ref, y_ref, st_ref,
                       up_scr, carry_scr, *, dff, final):
    j = pl.program_id(1)
    tm = x_ref.shape[0]
    x = x_ref[...]
    h = _rms(x, nw_ref[...]).astype(BF16)

    @pl.when(j == 0)
    def _zero_prefix():
        carry_scr[...] = jnp.zeros(carry_scr.shape, F32)

    def conv_half(lo):
        cols = slice(lo, lo + dff)
        up = _dot(h, wu_ref[:, cols])
        up_scr[0:SUBLANES, :] = carry_scr[:, cols]
        up_scr[SUBLANES:SUBLANES + tm, :] = up
        cw = cw_ref[:, cols]
        conv = (cb_ref[:, cols] + cw[0:1, :] * up_scr[SUBLANES - 2:SUBLANES - 2 + tm, :]
                + cw[1:2, :] * up_scr[SUBLANES - 1:SUBLANES - 1 + tm, :] + cw[2:3, :] * up)
        carry_scr[:, cols] = up[tm - SUBLANES:tm, :]
        st_ref[:, cols] = up[tm - 2:tm, :]
        return conv

    g = conv_half(0)
    u = conv_half(dff)
    x3 = x + _dot((_silu(g) * u).astype(BF16), wd_ref[...])
    y_ref[...] = _rms(x3, fw_ref[...]) if final else x3


def _ffn_prompt(x2, nw, w_up, cw, cb, w_down, fw, *, batch, seq, final):
    d = x2.shape[1]
    dff = w_down.shape[0]
    tm = min(FFN_ROW_TILE, seq)
    nj = seq // tm
    rows = lambda b, j: (b * nj + j, 0)
    const = lambda b, j: (0, 0)
    return pl.pallas_call(
        functools.partial(_ffn_prompt_kernel, dff=dff, final=final),
        grid=(batch, nj),
        in_specs=[pl.BlockSpec((tm, d), rows), pl.BlockSpec((1, d), const),
                  pl.BlockSpec(w_up.shape, const), pl.BlockSpec(cw.shape, const),
                  pl.BlockSpec((1, 2 * dff), const), pl.BlockSpec(w_down.shape, const),
                  pl.BlockSpec((1, d), const)],
        out_specs=[pl.BlockSpec((tm, d), rows),
                   pl.BlockSpec((None, CONV_W - 1, 2 * dff), lambda b, j: (b, 0, 0))],
        out_shape=[jax.ShapeDtypeStruct(x2.shape, F32),
                   jax.ShapeDtypeStruct((batch, CONV_W - 1, 2 * dff), F32)],
        scratch_shapes=[pltpu.VMEM((tm + SUBLANES, dff), F32),
                        pltpu.VMEM((SUBLANES, 2 * dff), F32)],
        compiler_params=_params("arbitrary", "arbitrary"),
    )(x2, nw, w_up, cw, cb, w_down, fw)


def _ffn_sample_kernel(x1_ref, o_ref, wc_ref, nw_ref, wu_ref, cw_ref, cb_ref, wd_ref, fw_ref,
                       s0_ref, s1_ref, y_ref, st_ref, hid_scr, *, dff, ts, bs, final):
    x2 = x1_ref[...] + _dot(o_ref[...], wc_ref[...])
    h = _rms(x2, nw_ref[...]).astype(BF16)

    def conv_half(lo):
        cols = slice(lo, lo + dff)
        up = _dot(h, wu_ref[:, cols])
        slabs = [s0_ref[:, cols], s1_ref[:, cols]] + [up[t * bs:(t + 1) * bs, :] for t in range(ts)]
        st_ref[0, :, cols] = slabs[-2]
        st_ref[1, :, cols] = slabs[-1]
        return _conv_time_major(slabs, cw_ref[:, cols], cb_ref[:, cols])

    g = conv_half(0)
    u = conv_half(dff)
    for t in range(ts):
        hid_scr[t * bs:(t + 1) * bs, :] = (_silu(g[t]) * u[t]).astype(BF16)
    x3 = x2 + _dot(hid_scr[...], wd_ref[...])
    y_ref[...] = _rms(x3, fw_ref[...]) if final else x3


def _ffn_sample(x1, o, w_co, nw, w_up, cw, cb, w_down, fw, s0, s1, *, ts, bs, final):
    dff = w_down.shape[0]
    return pl.pallas_call(
        functools.partial(_ffn_sample_kernel, dff=dff, ts=ts, bs=bs, final=final),
        out_shape=[jax.ShapeDtypeStruct(x1.shape, F32),
                   jax.ShapeDtypeStruct((CONV_W - 1, bs, 2 * dff), F32)],
        scratch_shapes=[pltpu.VMEM((ts * bs, dff), BF16)],
        compiler_params=pltpu.CompilerParams(vmem_limit_bytes=VMEM_LIMIT),
    )(x1, o, w_co, nw, w_up, cw, cb, w_down, fw, s0, s1)


def kernel(x_prompt, x_sample, mem_prompt, cache_k, cache_v, page_table, state_conv_mix, state_conv_ffn, cache_mem_k, cache_mem_v, rel_bias, norm_mix_w, w_in, conv_mix_w, conv_mix_b, lambda_q1, lambda_k1, lambda_q2, lambda_k2, subln_w, w_out, norm_cross_w, norm_mem_w, w_cq, w_ckv, w_co, norm_ffn_w, w_up, conv_ffn_w, conv_ffn_b, w_down, norm_final_w):
    depth = w_in.shape[0]
    bp, sp, d = x_prompt.shape
    bs, ts, _ = x_sample.shape
    n_mem = mem_prompt.shape[1]
    n_phys, page = cache_k.shape[1], cache_k.shape[2]
    n_pages = page_table.shape[1]
    past_len = n_pages * page
    dc = conv_mix_w.shape[2]
    da = N_HEADS * HEAD_DV
    dff = w_down.shape[1]
    dh = d // MEM_HEADS
    assert page == LANES and CONV_W - 1 <= ts <= SUBLANES and ts & (ts - 1) == 0
    assert (N_HEADS * ts) % SUBLANES == 0 and cache_k.shape[3:] == (N_HEADS, HEAD_DV)
    rows = 2 * SUBLANES

    assert _bucket_np(np.arange(REL_MAX_DIST, max(sp, past_len + ts) + 1)).min() == REL_BUCKETS - 1
    row = lambda a: a.reshape(1, -1).astype(F32)
    rel_bias = rel_bias.astype(F32)
    cache_k2 = cache_k.reshape(depth * n_phys * page * N_HEADS, HEAD_DV)
    cache_v2 = cache_v.reshape(depth * n_phys * page * N_HEADS, HEAD_DV)
    mem_k3 = cache_mem_k.reshape(depth * bs, n_mem, MEM_HEADS, dh)
    mem_v3 = cache_mem_v.reshape(depth * bs, n_mem, MEM_HEADS, dh)

    xp = x_prompt.reshape(bp * sp, d)
    xs = x_sample.transpose(1, 0, 2).reshape(ts * bs, d)
    mem2 = mem_prompt.reshape(bp * n_mem, d)
    outs = [[] for _ in range(10)]
    for l in range(depth):
        lam_init = 0.8 - 0.6 * math.exp(-0.3 * l)
        final = l == depth - 1
        w_in_b, w_out_b = w_in[l].astype(BF16), w_out[l].astype(BF16)
        w_cq_b, w_ckv_b, w_co_b = w_cq[l].astype(BF16), w_ckv[l].astype(BF16), w_co[l].astype(BF16)
        w_up_b, w_down_b = w_up[l].astype(BF16), w_down[l].astype(BF16)
        lam_args = (row(lambda_q1[l]), row(lambda_k1[l]), row(lambda_q2[l]), row(lambda_k2[l]),
                    row(subln_w[l]))

        yconv_p, q_p, kf_p, vf_p, kb_p, vb_p, cmix_p = _mix_in_prompt(
            xp, row(norm_mix_w[l]), w_in_b, conv_mix_w[l], row(conv_mix_b[l]),
            batch=bp, seq=sp, dc=dc, da=da)
        yattn_p = _attn_prompt(rel_bias, q_p, kb_p, vb_p, *lam_args[:4], subln_w[l].reshape(-1, 1).astype(F32),
                               batch=bp, seq=sp, lam_init=lam_init)
        mkf, mvf, mkb, mvb = _mem_kv(mem2, row(norm_mem_w[l]), w_ckv_b, batch=bp, n_mem=n_mem)
        x2_p = _cross_prompt(xp, yconv_p, yattn_p, w_out_b, row(norm_cross_w[l]), w_cq_b, mkb, mvb,
                             w_co_b, batch=bp, seq=sp, n_mem=n_mem)
        xp, cffn_p = _ffn_prompt(x2_p, row(norm_ffn_w[l]), w_up_b, conv_ffn_w[l], row(conv_ffn_b[l]),
                                 w_down_b, row(norm_final_w), batch=bp, seq=sp, final=final)

        yconv_s, qm_s, kf_s, vf_s, cmix_s = _mix_in_sample(
            xs, row(norm_mix_w[l]), w_in_b, conv_mix_w[l], row(conv_mix_b[l]),
            state_conv_mix[l][:, 0], state_conv_mix[l][:, 1], dc=dc, da=da, ts=ts, bs=bs)
        q_all = qm_s.reshape(2, ts, bs, N_HEADS, HEAD_DV).transpose(2, 0, 3, 1, 4)
        q_all = q_all.reshape(bs, 2 * N_HEADS * ts, HEAD_DV)
        new_rows = lambda a: jnp.pad(
            a.reshape(ts, bs, N_HEADS * HEAD_DV).transpose(1, 0, 2).reshape(bs, ts * N_HEADS, HEAD_DV),
            ((0, 0), (0, LANES - ts * N_HEADS), (0, 0))).astype(BF16)
        yattn_s = _attn_sample(page_table, rel_bias, q_all, new_rows(kf_s), new_rows(vf_s), *lam_args,
                               cache_k2, cache_v2, page=page, page_offset=l * n_phys, ts=ts,
                               lam_init=lam_init)
        yattn_s = yattn_s.reshape(bs, N_HEADS, ts, HEAD_DV).transpose(2, 0, 1, 3)
        yattn_s = yattn_s.reshape(ts * bs, da).astype(BF16)
        x1_s, qc_s = _outproj_q_sample(xs, yconv_s, yattn_s, w_out_b, row(norm_cross_w[l]), w_cq_b)
        qc_b = jnp.pad(qc_s.reshape(ts, bs, d).transpose(1, 0, 2), ((0, 0), (0, rows - ts), (0, 0)))
        o_b = _cross_sample(qc_b, mem_k3, mem_v3, row_offset=l * bs)
        o_s = o_b[:, :ts].transpose(1, 0, 2).reshape(ts * bs, d)
        xs, cffn_s = _ffn_sample(x1_s, o_s, w_co_b, row(norm_ffn_w[l]), w_up_b, conv_ffn_w[l],
                                 row(conv_ffn_b[l]), w_down_b, row(norm_final_w),
                                 state_conv_ffn[l][:, 0], state_conv_ffn[l][:, 1],
                                 ts=ts, bs=bs, final=final)

        t2b = lambda a: a.reshape(ts, bs, N_HEADS, HEAD_DV).transpose(1, 0, 2, 3)
        for lst, val in zip(outs, (
                kf_p.reshape(bp, sp, N_HEADS, 2 * HEAD_DK), vf_p.reshape(bp, sp, N_HEADS, HEAD_DV),
                t2b(kf_s), t2b(vf_s), cmix_p, cmix_s.transpose(1, 0, 2), cffn_p,
                cffn_s.transpose(1, 0, 2), mkf.reshape(bp, n_mem, MEM_HEADS, dh),
                mvf.reshape(bp, n_mem, MEM_HEADS, dh))):
            lst.append(val)

    y_prompt = xp.reshape(bp, sp, d)
    y_sample = xs.reshape(ts, bs, d).transpose(1, 0, 2)
    return (y_prompt, y_sample) + tuple(jnp.stack(o) for o in outs)
```

```python
import functools
import math

import numpy as np
import jax
import jax.numpy as jnp
from jax import lax
from jax.experimental import pallas as pl
from jax.experimental.pallas import tpu as pltpu

F32 = jnp.float32
BF16 = jnp.bfloat16

EPS = 1e-6
NEG = -1e30
LANES = 128
SUBLANES = 8
N_HEADS = 4
HEAD_DK = 64
HEAD_DV = 2 * HEAD_DK
MEM_HEADS = 4
CONV_W = 3
REL_BUCKETS = 32
REL_MAX_EXACT = 16
REL_MAX_DIST = 128
VMEM_LIMIT = 56 * 1024 * 1024

ATTN_TILE = 512
QUERY_CHUNK = 256
ONES_ROWS = 16
LOG2E = 1.4426950408889634
ROW_TILE = 512
FFN_ROW_TILE = 512
PAGES_PER_GROUP = 8
PAGE_SLOTS = 3


def _params(*sem):
    return pltpu.CompilerParams(dimension_semantics=sem, vmem_limit_bytes=VMEM_LIMIT)


def _rms(x, w):
    return x * lax.rsqrt(jnp.mean(x * x, axis=-1, keepdims=True) + EPS) * w


def _dot(a, b):
    return jnp.dot(a, b, preferred_element_type=F32)


def _dot_nt(a, b):
    return lax.dot_general(a, b, (((1,), (1,)), ((), ())), preferred_element_type=F32)


def _lam(lq1, lk1, lq2, lk2, lam_init):
    return (jnp.exp(jnp.sum(lq1[...] * lk1[...], axis=-1, keepdims=True))
            - jnp.exp(jnp.sum(lq2[...] * lk2[...], axis=-1, keepdims=True)) + lam_init)


def _bucket_np(rel):
    n = np.maximum(rel, 0)
    nf = np.maximum(n, 1).astype(np.float32)
    large = REL_MAX_EXACT + (np.log(nf / np.float32(REL_MAX_EXACT))
                             / np.float32(math.log(REL_MAX_DIST / REL_MAX_EXACT))
                             * np.float32(REL_BUCKETS - REL_MAX_EXACT)).astype(np.int32)
    large = np.minimum(large, REL_BUCKETS - 1)
    return np.where(n < REL_MAX_EXACT, n, large).astype(np.int32)


def _bucket_starts():
    buckets = _bucket_np(np.arange(REL_MAX_DIST + 1))
    assert (np.diff(buckets) >= 0).all() and buckets[-1] == REL_BUCKETS - 1
    return [int(np.argmax(buckets >= k)) for k in range(REL_BUCKETS)]


def _rel_bias_tile(rel, rb_ref, h):
    far = rb_ref[REL_BUCKETS - 1, h]
    val = jnp.full(rel.shape, rb_ref[0, h] - far, F32)
    for k, start in enumerate(_bucket_starts()):
        if k > 0:
            val = jnp.where(rel >= start, rb_ref[k, h] - far, val)
    return val


def _softmax_update(chunks, m_prev, l_prev):
    mx = chunks[0]
    for c in chunks[1:]:
        mx = jnp.maximum(mx, c)
    m_next = jnp.maximum(m_prev, jnp.max(mx, axis=1, keepdims=True))
    ps = [jnp.exp(c - m_next) for c in chunks]
    sm = ps[0]
    for p in ps[1:]:
        sm = sm + p
    alpha = jnp.exp(m_prev - m_next)
    l_next = alpha * l_prev + jnp.sum(sm, axis=1, keepdims=True)
    return ps, m_next, l_next, alpha


def _head_out(acc1, l1, acc2, l2, lam, sw, lam_init):
    o = acc1 / l1 - lam * (acc2 / l2)
    return _rms(o, sw) * (1.0 - lam_init)


def _mix_in_prompt_kernel(x_ref, nw_ref, w_ref, cw_ref, cb_ref,
                          yconv_ref, q_ref, kf_ref, vf_ref, kb_ref, vb_ref, st_ref,
                          pre_scr, *, dc, da):
    j = pl.program_id(1)
    tm = x_ref.shape[0]
    h = _rms(x_ref[...], nw_ref[...]).astype(BF16)

    def proj(lo, width):
        return _dot(h, w_ref[:, lo:lo + width])

    @pl.when(j == 0)
    def _zero_prefix():
        pre_scr[0:SUBLANES, :] = jnp.zeros((SUBLANES, dc), F32)

    pre = proj(dc, dc) * proj(2 * dc, dc)
    pre_scr[SUBLANES:SUBLANES + tm, :] = pre
    cw = cw_ref[...]
    conv = (cb_ref[...] + cw[0:1, :] * pre_scr[SUBLANES - 2:SUBLANES - 2 + tm, :]
            + cw[1:2, :] * pre_scr[SUBLANES - 1:SUBLANES - 1 + tm, :] + cw[2:3, :] * pre)
    yconv_ref[...] = (proj(0, dc) * conv).astype(BF16)
    st_ref[...] = pre[tm - 2:tm, :]
    pre_scr[0:SUBLANES, :] = pre[tm - SUBLANES:tm, :]

    q = proj(3 * dc, da) * (HEAD_DK ** -0.5 * LOG2E)
    k = proj(3 * dc + da, da)
    kb_ref[...] = k.astype(BF16)
    v = proj(3 * dc + 2 * da, da)
    ones_rows = (lax.broadcasted_iota(jnp.int32, (ONES_ROWS, tm), 0) == 0).astype(BF16)
    for hd in range(N_HEADS):
        cols = slice(hd * HEAD_DV, (hd + 1) * HEAD_DV)
        kf_ref[pl.ds(hd, tm, stride=N_HEADS), :] = k[:, cols]
        vf_ref[pl.ds(hd, tm, stride=N_HEADS), :] = v[:, cols]
        q_ref[hd] = q[:, cols].T.astype(BF16)
        vb_ref[hd, 0:HEAD_DV, :] = v[:, cols].T.astype(BF16)
        vb_ref[hd, HEAD_DV:, :] = ones_rows


def _mix_in_prompt(x2, nw, w_in, cw, cb, *, batch, seq, dc, da):
    d = x2.shape[1]
    tm = min(ATTN_TILE, seq)
    nj = seq // tm
    rows = lambda b, j: (b * nj + j, 0)
    const = lambda b, j: (0, 0)
    tposed = lambda r: pl.BlockSpec((None, N_HEADS, None, r, tm), lambda b, j: (b, 0, j, 0, 0))
    tposed_shape = lambda r: jax.ShapeDtypeStruct((batch, N_HEADS, nj, r, tm), BF16)
    n = batch * seq
    return pl.pallas_call(
        functools.partial(_mix_in_prompt_kernel, dc=dc, da=da),
        grid=(batch, nj),
        in_specs=[pl.BlockSpec((tm, d), rows),
                  pl.BlockSpec((1, d), const),
                  pl.BlockSpec(w_in.shape, const),
                  pl.BlockSpec(cw.shape, const),
                  pl.BlockSpec((1, dc), const)],
        out_specs=[pl.BlockSpec((tm, dc), rows),
                   tposed(HEAD_DV),
                   pl.BlockSpec((tm * N_HEADS, HEAD_DV), rows),
                   pl.BlockSpec((tm * N_HEADS, HEAD_DV), rows),
                   pl.BlockSpec((tm, da), rows),
                   tposed(HEAD_DV + ONES_ROWS),
                   pl.BlockSpec((None, CONV_W - 1, dc), lambda b, j: (b, 0, 0))],
        out_shape=[jax.ShapeDtypeStruct((n, dc), BF16),
                   tposed_shape(HEAD_DV),
                   jax.ShapeDtypeStruct((n * N_HEADS, HEAD_DV), F32),
                   jax.ShapeDtypeStruct((n * N_HEADS, HEAD_DV), F32),
                   jax.ShapeDtypeStruct((n, da), BF16),
                   tposed_shape(HEAD_DV + ONES_ROWS),
                   jax.ShapeDtypeStruct((batch, CONV_W - 1, dc), F32)],
        scratch_shapes=[pltpu.VMEM((tm + SUBLANES, dc), F32)],
        compiler_params=_params("arbitrary", "arbitrary"),
    )(x2, nw, w_in, cw, cb)


def _conv_time_major(slabs, cw, cb):
    return [cb + cw[0:1, :] * slabs[t] + cw[1:2, :] * slabs[t + 1] + cw[2:3, :] * slabs[t + 2]
            for t in range(len(slabs) - 2)]


def _mix_in_sample_kernel(x_ref, nw_ref, w_ref, cw_ref, cb_ref, s0_ref, s1_ref,
                          yconv_ref, qm_ref, kf_ref, vf_ref, st_ref, *, dc, da, ts, bs):
    h = _rms(x_ref[...], nw_ref[...]).astype(BF16)

    def proj(lo, width):
        return _dot(h, w_ref[:, lo:lo + width])

    pre = proj(dc, dc) * proj(2 * dc, dc)
    gate = proj(0, dc)
    slabs = [s0_ref[...], s1_ref[...]] + [pre[t * bs:(t + 1) * bs, :] for t in range(ts)]
    conv = _conv_time_major(slabs, cw_ref[...], cb_ref[...])
    for t in range(ts):
        yconv_ref[t * bs:(t + 1) * bs, :] = (gate[t * bs:(t + 1) * bs, :] * conv[t]).astype(BF16)
    st_ref[0] = slabs[-2]
    st_ref[1] = slabs[-1]

    q = proj(3 * dc, da) * (HEAD_DK ** -0.5)
    lane = lax.broadcasted_iota(jnp.int32, q.shape, 1) % HEAD_DV
    qm_ref[0] = jnp.where(lane < HEAD_DK, q, 0.0).astype(BF16)
    qm_ref[1] = jnp.where(lane >= HEAD_DK, q, 0.0).astype(BF16)
    kf_ref[...] = proj(3 * dc + da, da)
    vf_ref[...] = proj(3 * dc + 2 * da, da)


def _mix_in_sample(xt, nw, w_in, cw, cb, s0, s1, *, dc, da, ts, bs):
    n = ts * bs
    return pl.pallas_call(
        functools.partial(_mix_in_sample_kernel, dc=dc, da=da, ts=ts, bs=bs),
        out_shape=[jax.ShapeDtypeStruct((n, dc), BF16),
                   jax.ShapeDtypeStruct((2, n, da), BF16),
                   jax.ShapeDtypeStruct((n, da), F32),
                   jax.ShapeDtypeStruct((n, da), F32),
                   jax.ShapeDtypeStruct((CONV_W - 1, bs, dc), F32)],
        compiler_params=pltpu.CompilerParams(vmem_limit_bytes=VMEM_LIMIT),
    )(xt, nw, w_in, cw, cb, s0, s1)


def _mem_kv_kernel(m_ref, nw_ref, w_ref, kf_ref, vf_ref, kb_ref, vb_ref, *, d):
    h = _rms(m_ref[...], nw_ref[...]).astype(BF16)
    k = _dot(h, w_ref[:, 0:d])
    kf_ref[...] = k
    kb_ref[...] = k.astype(BF16)
    v = _dot(h, w_ref[:, d:2 * d])
    vf_ref[...] = v
    vb_ref[...] = v.astype(BF16)


def _mem_kv(mem2, nw, w_ckv, *, batch, n_mem):
    d = mem2.shape[1]
    rows = lambda b: (b, 0)
    const = lambda b: (0, 0)
    n = batch * n_mem
    return pl.pallas_call(
        functools.partial(_mem_kv_kernel, d=d),
        grid=(batch,),
        in_specs=[pl.BlockSpec((n_mem, d), rows), pl.BlockSpec((1, d), const),
                  pl.BlockSpec(w_ckv.shape, const)],
        out_specs=[pl.BlockSpec((n_mem, d), rows)] * 4,
        out_shape=[jax.ShapeDtypeStruct((n, d), F32), jax.ShapeDtypeStruct((n, d), F32),
                   jax.ShapeDtypeStruct((n, d), BF16), jax.ShapeDtypeStruct((n, d), BF16)],
        compiler_params=_params("arbitrary"),
    )(mem2, nw, w_ckv)


def _attn_prompt_kernel(rb_ref, q_ref, k_ref, vt_ref, lq1, lk1, lq2, lk2, sw_ref, o_ref,
                        qp_scr, m_scr, acc_scr, bd_scr, bs_scr, sa_scr, sb_scr, *, T, lam_init):
    qi = pl.program_id(2)
    nb = T // LANES

    @pl.when(qi == 0)
    def _assemble_bias_tiles():
        h = pl.program_id(1)
        rel = (lax.broadcasted_iota(jnp.int32, (LANES, LANES), 1)
               - lax.broadcasted_iota(jnp.int32, (LANES, LANES), 0))
        d0 = jnp.where(rel >= 0, _rel_bias_tile(jnp.maximum(rel, 0), rb_ref, h) * LOG2E, NEG)
        d1 = _rel_bias_tile(rel + LANES, rb_ref, h) * LOG2E
        zero = jnp.zeros((LANES, LANES), F32)
        neg = jnp.full((LANES, LANES), NEG, F32)
        for bi in range(nb):
            for bj in range(nb):
                rs = slice(bi * LANES, (bi + 1) * LANES)
                cs = slice(bj * LANES, (bj + 1) * LANES)
                bd_scr[rs, cs] = d0 if bi == bj else d1 if bj == bi + 1 else zero if bj > bi else neg
                bs_scr[rs, cs] = d1 if (bj == 0 and bi == nb - 1) else zero

    qt = q_ref[...].astype(F32)
    sub = lax.broadcasted_iota(jnp.int32, qt.shape, 0)
    qp_scr[:, 0:T] = jnp.where(sub < HEAD_DK, qt, 0.0).astype(BF16)
    qp_scr[:, T:2 * T] = jnp.where(sub >= HEAD_DK, qt, 0.0).astype(BF16)
    m_scr[...] = jnp.full(m_scr.shape, -jnp.inf, F32)
    acc_scr[...] = jnp.zeros(acc_scr.shape, F32)

    def scores(j, dst):
        dst[...] = _dot(k_ref[pl.ds(pl.multiple_of(j * T, T), T), :], qp_scr[...])

    def stage(j, src, bias_ref, j_next=None, dst=None):
        if j_next is not None:
            scores(j_next, dst)
        vt = vt_ref[j]
        for c in range(2 * T // QUERY_CHUNK):
            cs = slice(c * QUERY_CHUNK, (c + 1) * QUERY_CHUNK)
            s = src[:, cs]
            if bias_ref is not None:
                lo = (c * QUERY_CHUNK) % T
                s = s + bias_ref[:, lo:lo + QUERY_CHUNK]
            m_prev = m_scr[:, cs]
            m_next = jnp.maximum(m_prev, jnp.max(s, axis=0, keepdims=True))
            p = jnp.exp2(s - m_next).astype(BF16)
            acc_scr[:, cs] = jnp.exp2(m_prev - m_next) * acc_scr[:, cs] + _dot(vt, p)
            m_scr[:, cs] = m_next

    odd = qi % 2 == 0
    first = jnp.where(odd, 1, 0)

    @pl.when(odd)
    def _first_scores_odd():
        scores(0, sb_scr)

    @pl.when(jnp.logical_not(odd))
    def _first_scores_even():
        scores(0, sa_scr)

    @pl.when(odd & (qi >= 2))
    def _single_far_block():
        stage(0, sb_scr, None, 1, sa_scr)

    def far_pair(i, carry):
        j = first + 2 * i
        stage(j, sa_scr, None, j + 1, sb_scr)
        stage(j + 1, sb_scr, None, j + 2, sa_scr)
        return carry

    lax.fori_loop(0, jnp.maximum(qi - 1, 0) // 2, far_pair, 0)

    @pl.when(qi >= 1)
    def _sub_diagonal_and_diagonal():
        stage(qi - 1, sa_scr, bs_scr, qi, sb_scr)
        stage(qi, sb_scr, bd_scr)

    @pl.when(qi == 0)
    def _diagonal_only():
        stage(0, sb_scr, bd_scr)

    lam = _lam(lq1, lk1, lq2, lk2, lam_init)
    acc = acc_scr[0:HEAD_DV, :]
    l = acc_scr[HEAD_DV:HEAD_DV + 1, :]
    o = acc[:, 0:T] / l[:, 0:T] - lam * (acc[:, T:2 * T] / l[:, T:2 * T])
    y = o * lax.rsqrt(jnp.mean(o * o, axis=0, keepdims=True) + EPS) * sw_ref[...] * (1.0 - lam_init)
    o_ref[...] = y.T.astype(o_ref.dtype)


def _attn_prompt(rel_bias, qt, kb, vt, lq1, lk1, lq2, lk2, sw_col, *, batch, seq, lam_init):
    T = qt.shape[-1]
    nq = seq // T
    da = kb.shape[1]
    k3 = kb.reshape(batch, seq, da)
    vec = lambda b, h, i: (0, 0)
    return pl.pallas_call(
        functools.partial(_attn_prompt_kernel, T=T, lam_init=lam_init),
        grid=(batch, N_HEADS, nq),
        in_specs=[pl.BlockSpec(memory_space=pltpu.SMEM),
                  pl.BlockSpec((None, None, None, HEAD_DV, T), lambda b, h, i: (b, h, i, 0, 0)),
                  pl.BlockSpec((None, seq, HEAD_DV), lambda b, h, i: (b, 0, h)),
                  pl.BlockSpec((None, None, nq, vt.shape[3], T), lambda b, h, i: (b, h, 0, 0, 0)),
                  pl.BlockSpec((1, HEAD_DK), vec), pl.BlockSpec((1, HEAD_DK), vec),
                  pl.BlockSpec((1, HEAD_DK), vec), pl.BlockSpec((1, HEAD_DK), vec),
                  pl.BlockSpec((HEAD_DV, 1), vec)],
        out_specs=pl.BlockSpec((T, HEAD_DV), lambda b, h, i: (b * nq + i, h)),
        out_shape=jax.ShapeDtypeStruct((batch * seq, da), BF16),
        scratch_shapes=[pltpu.VMEM((HEAD_DV, 2 * T), BF16),
                        pltpu.VMEM((1, 2 * T), F32),
                        pltpu.VMEM((vt.shape[3], 2 * T), F32),
                        pltpu.VMEM((T, T), F32),
                        pltpu.VMEM((T, T), F32),
                        pltpu.VMEM((T, 2 * T), F32),
                        pltpu.VMEM((T, 2 * T), F32)],
        compiler_params=_params("arbitrary", "arbitrary", "arbitrary"),
    )(rel_bias, qt, k3, vt, lq1, lk1, lq2, lk2, sw_col)


def _attn_sample_kernel(pt_ref, rb_ref, q_ref, kn_ref, vn_ref, lq1, lk1, lq2, lk2, sw_ref, k_hbm, v_hbm,
                        o_ref, m_scr, l_scr, acc_scr, hm_scr, bl_scr, bn_scr, kbuf, vbuf, sems,
                        *, pages, page_offset, ts, lam_init):
    b = pl.program_id(0)
    nb = pl.num_programs(0)
    rows, pk = hm_scr.shape
    nk = bn_scr.shape[1]
    rm = rows // 2
    page = pk // N_HEADS
    groups = pt_ref.shape[1] // pages
    hbits = N_HEADS.bit_length() - 1
    tbits = ts.bit_length() - 1

    def group_copies(t, slot):
        bt = t // groups
        g0 = (t % groups) * pages
        out = []
        for i in range(pages):
            row0 = pl.multiple_of((page_offset + pt_ref[bt, g0 + i]) * pk, pk)
            dst = pl.ds(i * pk, pk)
            out.append(pltpu.make_async_copy(k_hbm.at[pl.ds(row0, pk)], kbuf.at[slot, dst], sems.at[0, slot]))
            out.append(pltpu.make_async_copy(v_hbm.at[pl.ds(row0, pk)], vbuf.at[slot, dst], sems.at[1, slot]))
        return out

    def start_group(t):
        @pl.when(t < nb * groups)
        def _():
            for cp in group_copies(t, t % PAGE_SLOTS):
                cp.start()

    @pl.when(b == 0)
    def _first_step():
        for t in range(PAGE_SLOTS - 1):
            start_group(jnp.int32(t))

        def tables(ncols, rel_of):
            r = lax.broadcasted_iota(jnp.int32, (rows, ncols), 0)
            c = lax.broadcasted_iota(jnp.int32, (rows, ncols), 1)
            rhead = (r >> tbits) & (N_HEADS - 1)
            same = rhead == (c & (N_HEADS - 1))
            rel = rel_of(r & (ts - 1), c >> hbits)
            bias = jnp.zeros((rows, ncols), F32)
            for h in range(N_HEADS):
                bias = jnp.where(rhead == h, _rel_bias_tile(jnp.maximum(rel, 0), rb_ref, h), bias)
            return same, rel, c >> hbits, bias

        same, _, _, bias = tables(pk, lambda tok, key: tok + page - key)
        hm_scr[...] = jnp.where(same, 0.0, NEG)
        bl_scr[...] = jnp.where(same, bias, NEG)
        same, rel, key, bias = tables(nk, lambda tok, key: tok - key)
        bn_scr[...] = jnp.where(same & (rel >= 0) & (key < ts), bias, NEG)

    m_scr[...] = jnp.full(m_scr.shape, -jnp.inf, F32)
    l_scr[...] = jnp.zeros(l_scr.shape, F32)
    acc_scr[...] = jnp.zeros(acc_scr.shape, F32)

    def update(scores, values):
        chunks, sizes = [], []
        for s in scores:
            n = s.shape[1] // LANES
            sizes.append(n)
            chunks += [s[:, c * LANES:(c + 1) * LANES] for c in range(n)]
        ps, m_next, l_next, alpha = _softmax_update(chunks, m_scr[...], l_scr[...])
        pv, at = None, 0
        for n, v in zip(sizes, values):
            p = jnp.concatenate([x.astype(BF16) for x in ps[at:at + n]], axis=1) if n > 1 \
                else ps[at].astype(BF16)
            at += n
            pv = _dot(p, v) if pv is None else pv + _dot(p, v)
        acc_scr[...] = alpha * acc_scr[...] + pv
        m_scr[...] = m_next
        l_scr[...] = l_next

    q = q_ref[...]

    def group_body(g, carry):
        t = b * groups + g
        slot = t % PAGE_SLOTS
        for cp in group_copies(t, slot):
            cp.wait()
        start_group(t + PAGE_SLOTS - 1)
        kg = kbuf.at[slot]
        vg = vbuf.at[slot]
        last = g == groups - 1
        scores = [_dot_nt(q, kg[i * pk:(i + 1) * pk, :].astype(BF16)) + hm_scr[...]
                  for i in range(pages - 1)]
        scores.append(_dot_nt(q, kg[(pages - 1) * pk:pages * pk, :].astype(BF16))
                      + jnp.where(last, bl_scr[...], hm_scr[...]))
        update(scores, [vg[i * pk:(i + 1) * pk, :].astype(BF16) for i in range(pages)])
        return carry

    lax.fori_loop(0, groups, group_body, 0)

    update([_dot_nt(q, kn_ref[...]) + bn_scr[...]], [vn_ref[...]])
    lam = _lam(lq1, lk1, lq2, lk2, lam_init)
    acc = acc_scr[...]
    l = l_scr[...]
    o_ref[...] = _head_out(acc[0:rm], l[0:rm], acc[rm:rows], l[rm:rows], lam, sw_ref[...], lam_init)


def _attn_sample(page_table, rel_bias, q_all, knew, vnew, lq1, lk1, lq2, lk2, sw, cache_k, cache_v,
                 *, page, page_offset, ts, lam_init):
    bs, n_pages = page_table.shape
    rows = q_all.shape[1]
    nk = knew.shape[1]
    pk = page * N_HEADS
    pages = math.gcd(PAGES_PER_GROUP, n_pages)
    vec = lambda b, pt: (0, 0)
    per_b = lambda b, pt: (b, 0, 0)
    grid_spec = pltpu.PrefetchScalarGridSpec(
        num_scalar_prefetch=1,
        grid=(bs,),
        in_specs=[pl.BlockSpec(memory_space=pltpu.SMEM),
                  pl.BlockSpec((None, rows, HEAD_DV), per_b),
                  pl.BlockSpec((None, nk, HEAD_DV), per_b),
                  pl.BlockSpec((None, nk, HEAD_DV), per_b),
                  pl.BlockSpec((1, HEAD_DK), vec), pl.BlockSpec((1, HEAD_DK), vec),
                  pl.BlockSpec((1, HEAD_DK), vec), pl.BlockSpec((1, HEAD_DK), vec),
                  pl.BlockSpec((1, HEAD_DV), vec),
                  pl.BlockSpec(memory_space=pl.ANY), pl.BlockSpec(memory_space=pl.ANY)],
        out_specs=pl.BlockSpec((None, rows // 2, HEAD_DV), per_b),
        scratch_shapes=[pltpu.VMEM((rows, LANES), F32),
                        pltpu.VMEM((rows, LANES), F32),
                        pltpu.VMEM((rows, HEAD_DV), F32),
                        pltpu.VMEM((rows, pk), F32),
                        pltpu.VMEM((rows, pk), F32),
                        pltpu.VMEM((rows, nk), F32),
                        pltpu.VMEM((PAGE_SLOTS, pages * pk, HEAD_DV), F32),
                        pltpu.VMEM((PAGE_SLOTS, pages * pk, HEAD_DV), F32),
                        pltpu.SemaphoreType.DMA((2, PAGE_SLOTS))],
    )
    return pl.pallas_call(
        functools.partial(_attn_sample_kernel, pages=pages, page_offset=page_offset, ts=ts,
                          lam_init=lam_init),
        grid_spec=grid_spec,
        out_shape=jax.ShapeDtypeStruct((bs, rows // 2, HEAD_DV), F32),
        compiler_params=_params("arbitrary"),
    )(page_table, rel_bias, q_all, knew, vnew, lq1, lk1, lq2, lk2, sw, cache_k, cache_v)


def _cross_heads(qc, mk_ref, mv_ref, o_scr, dh):
    for hd in range(MEM_HEADS):
        cols = slice(hd * dh, (hd + 1) * dh)
        s = _dot_nt(qc[:, cols], mk_ref[:, cols].astype(BF16))
        p = jnp.exp(s - jnp.max(s, axis=1, keepdims=True))
        o = _dot(p.astype(BF16), mv_ref[:, cols].astype(BF16)) / jnp.sum(p, axis=1, keepdims=True)
        o_scr[:, cols] = o.astype(BF16)


def _cross_prompt_kernel(x_ref, yc_ref, ya_ref, wo_ref, nw_ref, wq_ref, mk_ref, mv_ref, wc_ref,
                         o_ref, o_scr, *, dc, dh):
    x1 = x_ref[...] + _dot(yc_ref[...], wo_ref[0:dc, :]) + _dot(ya_ref[...], wo_ref[dc:, :])
    h = _rms(x1, nw_ref[...]).astype(BF16)
    qc = (_dot(h, wq_ref[...]) * (dh ** -0.5)).astype(BF16)
    _cross_heads(qc, mk_ref, mv_ref, o_scr, dh)
    o_ref[...] = x1 + _dot(o_scr[...], wc_ref[...])


def _cross_prompt(x2, yconv, yattn, w_out, nw, w_cq, mk, mv, w_co, *, batch, seq, n_mem):
    d = x2.shape[1]
    dc = yconv.shape[1]
    da = yattn.shape[1]
    tm = min(ROW_TILE, seq)
    nj = seq // tm
    rows = lambda b, j: (b * nj + j, 0)
    const = lambda b, j: (0, 0)
    memb = lambda b, j: (b, 0)
    return pl.pallas_call(
        functools.partial(_cross_prompt_kernel, dc=dc, dh=d // MEM_HEADS),
        grid=(batch, nj),
        in_specs=[pl.BlockSpec((tm, d), rows), pl.BlockSpec((tm, dc), rows), pl.BlockSpec((tm, da), rows),
                  pl.BlockSpec(w_out.shape, const), pl.BlockSpec((1, d), const),
                  pl.BlockSpec(w_cq.shape, const),
                  pl.BlockSpec((n_mem, d), memb), pl.BlockSpec((n_mem, d), memb),
                  pl.BlockSpec(w_co.shape, const)],
        out_specs=pl.BlockSpec((tm, d), rows),
        out_shape=jax.ShapeDtypeStruct(x2.shape, F32),
        scratch_shapes=[pltpu.VMEM((tm, d), BF16)],
        compiler_params=_params("arbitrary", "arbitrary"),
    )(x2, yconv, yattn, w_out, nw, w_cq, mk, mv, w_co)


def _outproj_q_sample_kernel(x_ref, yc_ref, ya_ref, wo_ref, nw_ref, wq_ref, x1_ref, qc_ref, *, dc, dh):
    x1 = x_ref[...] + _dot(yc_ref[...], wo_ref[0:dc, :]) + _dot(ya_ref[...], wo_ref[dc:, :])
    x1_ref[...] = x1
    h = _rms(x1, nw_ref[...]).astype(BF16)
    qc_ref[...] = (_dot(h, wq_ref[...]) * (dh ** -0.5)).astype(BF16)


def _outproj_q_sample(xt, yconv, yattn, w_out, nw, w_cq):
    d = xt.shape[1]
    return pl.pallas_call(
        functools.partial(_outproj_q_sample_kernel, dc=yconv.shape[1], dh=d // MEM_HEADS),
        out_shape=[jax.ShapeDtypeStruct(xt.shape, F32), jax.ShapeDtypeStruct(xt.shape, BF16)],
        compiler_params=pltpu.CompilerParams(vmem_limit_bytes=VMEM_LIMIT),
    )(xt, yconv, yattn, w_out, nw, w_cq)


def _cross_sample_kernel(qc_ref, mk_ref, mv_ref, o_ref, *, dh):
    _cross_heads(qc_ref[...], mk_ref, mv_ref, o_ref, dh)


def _cross_sample(qc, mk, mv, *, row_offset):
    bs, rows, d = qc.shape
    n_mem = mk.shape[1]
    per_b = lambda b: (b, 0, 0)
    mem_b = lambda b: (row_offset + b, 0, 0)
    return pl.pallas_call(
        functools.partial(_cross_sample_kernel, dh=d // MEM_HEADS),
        grid=(bs,),
        in_specs=[pl.BlockSpec((None, rows, d), per_b), pl.BlockSpec((None, n_mem, d), mem_b),
                  pl.BlockSpec((None, n_mem, d), mem_b)],
        out_specs=pl.BlockSpec((None, rows, d), per_b),
        out_shape=jax.ShapeDtypeStruct(qc.shape, BF16),
        compiler_params=_params("arbitrary"),
    )(qc, mk, mv)


def _silu(g):
    return g * (1.0 / (1.0 + jnp.exp(-g)))


def _ffn_prompt_kernel(x_ref, nw_ref, wu_ref, cw_ref, cb_ref, wd_ref, fw_ref, y_ref, st_ref,
                       up_scr, carry_scr, *, dff, final):
    j = pl.program_id(1)
    tm = x_ref.shape[0]
    x = x_ref[...]
    h = _rms(x, nw_ref[...]).astype(BF16)

    @pl.when(j == 0)
    def _zero_prefix():
        carry_scr[...] = jnp.zeros(carry_scr.shape, F32)

    def conv_half(lo):
        cols = slice(lo, lo + dff)
        up = _dot(h, wu_ref[:, cols])
        up_scr[0:SUBLANES, :] = carry_scr[:, cols]
        up_scr[SUBLANES:SUBLANES + tm, :] = up
        cw = cw_ref[:, cols]
        conv = (cb_ref[:, cols] + cw[0:1, :] * up_scr[SUBLANES - 2:SUBLANES - 2 + tm, :]
                + cw[1:2, :] * up_scr[SUBLANES - 1:SUBLANES - 1 + tm, :] + cw[2:3, :] * up)
        carry_scr[:, cols] = up[tm - SUBLANES:tm, :]
        st_ref[:, cols] = up[tm - 2:tm, :]
        return conv

    g = conv_half(0)
    u = conv_half(dff)
    x3 = x + _dot((_silu(g) * u).astype(BF16), wd_ref[...])
    y_ref[...] = _rms(x3, fw_ref[...]) if final else x3


def _ffn_prompt(x2, nw, w_up, cw, cb, w_down, fw, *, batch, seq, final):
    d = x2.shape[1]
    dff = w_down.shape[0]
    tm = min(FFN_ROW_TILE, seq)
    nj = seq // tm
    rows = lambda b, j: (b * nj + j, 0)
    const = lambda b, j: (0, 0)
    resident = lambda shape: pl.BlockSpec(shape, const, pipeline_mode=pl.Buffered(1))
    return pl.pallas_call(
        functools.partial(_ffn_prompt_kernel, dff=dff, final=final),
        grid=(batch, nj),
        in_specs=[pl.BlockSpec((tm, d), rows), pl.BlockSpec((1, d), const),
                  resident(w_up.shape), pl.BlockSpec(cw.shape, const),
                  pl.BlockSpec((1, 2 * dff), const), resident(w_down.shape),
                  pl.BlockSpec((1, d), const)],
        out_specs=[pl.BlockSpec((tm, d), rows),
                   pl.BlockSpec((None, CONV_W - 1, 2 * dff), lambda b, j: (b, 0, 0))],
        out_shape=[jax.ShapeDtypeStruct(x2.shape, F32),
                   jax.ShapeDtypeStruct((batch, CONV_W - 1, 2 * dff), F32)],
        scratch_shapes=[pltpu.VMEM((tm + SUBLANES, dff), F32),
                        pltpu.VMEM((SUBLANES, 2 * dff), F32)],
        compiler_params=_params("arbitrary", "arbitrary"),
    )(x2, nw, w_up, cw, cb, w_down, fw)


def _ffn_sample_kernel(x1_ref, o_ref, wc_ref, nw_ref, wu_ref, cw_ref, cb_ref, wd_ref, fw_ref,
                       s0_ref, s1_ref, y_ref, st_ref, hid_scr, *, dff, ts, bs, final):
    x2 = x1_ref[...] + _dot(o_ref[...], wc_ref[...])
    h = _rms(x2, nw_ref[...]).astype(BF16)

    def conv_half(lo):
        cols = slice(lo, lo + dff)
        up = _dot(h, wu_ref[:, cols])
        slabs = [s0_ref[:, cols], s1_ref[:, cols]] + [up[t * bs:(t + 1) * bs, :] for t in range(ts)]
        st_ref[0, :, cols] = slabs[-2]
        st_ref[1, :, cols] = slabs[-1]
        return _conv_time_major(slabs, cw_ref[:, cols], cb_ref[:, cols])

    g = conv_half(0)
    u = conv_half(dff)
    for t in range(ts):
        hid_scr[t * bs:(t + 1) * bs, :] = (_silu(g[t]) * u[t]).astype(BF16)
    x3 = x2 + _dot(hid_scr[...], wd_ref[...])
    y_ref[...] = _rms(x3, fw_ref[...]) if final else x3


def _ffn_sample(x1, o, w_co, nw, w_up, cw, cb, w_down, fw, s0, s1, *, ts, bs, final):
    dff = w_down.shape[0]
    return pl.pallas_call(
        functools.partial(_ffn_sample_kernel, dff=dff, ts=ts, bs=bs, final=final),
        out_shape=[jax.ShapeDtypeStruct(x1.shape, F32),
                   jax.ShapeDtypeStruct((CONV_W - 1, bs, 2 * dff), F32)],
        scratch_shapes=[pltpu.VMEM((ts * bs, dff), BF16)],
        compiler_params=pltpu.CompilerParams(vmem_limit_bytes=VMEM_LIMIT),
    )(x1, o, w_co, nw, w_up, cw, cb, w_down, fw, s0, s1)


def kernel(x_prompt, x_sample, mem_prompt, cache_k, cache_v, page_table, state_conv_mix, state_conv_ffn, cache_mem_k, cache_mem_v, rel_bias, norm_mix_w, w_in, conv_mix_w, conv_mix_b, lambda_q1, lambda_k1, lambda_q2, lambda_k2, subln_w, w_out, norm_cross_w, norm_mem_w, w_cq, w_ckv, w_co, norm_ffn_w, w_up, conv_ffn_w, conv_ffn_b, w_down, norm_final_w):
    depth = w_in.shape[0]
    bp, sp, d = x_prompt.shape
    bs, ts, _ = x_sample.shape
    n_mem = mem_prompt.shape[1]
    n_phys, page = cache_k.shape[1], cache_k.shape[2]
    n_pages = page_table.shape[1]
    past_len = n_pages * page
    dc = conv_mix_w.shape[2]
    da = N_HEADS * HEAD_DV
    dff = w_down.shape[1]
    dh = d // MEM_HEADS
    assert page == LANES and CONV_W - 1 <= ts <= SUBLANES and ts & (ts - 1) == 0
    assert (N_HEADS * ts) % SUBLANES == 0 and cache_k.shape[3:] == (N_HEADS, HEAD_DV)
    rows = 2 * SUBLANES

    assert _bucket_np(np.arange(REL_MAX_DIST, max(sp, past_len + ts) + 1)).min() == REL_BUCKETS - 1
    row = lambda a: a.reshape(1, -1).astype(F32)
    rel_bias = rel_bias.astype(F32)
    cache_k2 = cache_k.reshape(depth * n_phys * page * N_HEADS, HEAD_DV)
    cache_v2 = cache_v.reshape(depth * n_phys * page * N_HEADS, HEAD_DV)
    mem_k3 = cache_mem_k.reshape(depth * bs, n_mem, d)
    mem_v3 = cache_mem_v.reshape(depth * bs, n_mem, d)

    xp = x_prompt.reshape(bp * sp, d)
    xs = x_sample.transpose(1, 0, 2).reshape(ts * bs, d)
    mem2 = mem_prompt.reshape(bp * n_mem, d)
    outs = [[] for _ in range(10)]
    for l in range(depth):
        lam_init = 0.8 - 0.6 * math.exp(-0.3 * l)
        final = l == depth - 1
        w_in_b, w_out_b = w_in[l].astype(BF16), w_out[l].astype(BF16)
        w_cq_b, w_ckv_b, w_co_b = w_cq[l].astype(BF16), w_ckv[l].astype(BF16), w_co[l].astype(BF16)
        w_up_b, w_down_b = w_up[l].astype(BF16), w_down[l].astype(BF16)
        lam_args = (row(lambda_q1[l]), row(lambda_k1[l]), row(lambda_q2[l]), row(lambda_k2[l]),
                    row(subln_w[l]))

        yconv_p, q_p, kf_p, vf_p, kb_p, vb_p, cmix_p = _mix_in_prompt(
            xp, row(norm_mix_w[l]), w_in_b, conv_mix_w[l], row(conv_mix_b[l]),
            batch=bp, seq=sp, dc=dc, da=da)
        yattn_p = _attn_prompt(rel_bias, q_p, kb_p, vb_p, *lam_args[:4], subln_w[l].reshape(-1, 1).astype(F32),
                               batch=bp, seq=sp, lam_init=lam_init)
        mkf, mvf, mkb, mvb = _mem_kv(mem2, row(norm_mem_w[l]), w_ckv_b, batch=bp, n_mem=n_mem)
        x2_p = _cross_prompt(xp, yconv_p, yattn_p, w_out_b, row(norm_cross_w[l]), w_cq_b, mkb, mvb,
                             w_co_b, batch=bp, seq=sp, n_mem=n_mem)
        xp, cffn_p = _ffn_prompt(x2_p, row(norm_ffn_w[l]), w_up_b, conv_ffn_w[l], row(conv_ffn_b[l]),
                                 w_down_b, row(norm_final_w), batch=bp, seq=sp, final=final)

        yconv_s, qm_s, kf_s, vf_s, cmix_s = _mix_in_sample(
            xs, row(norm_mix_w[l]), w_in_b, conv_mix_w[l], row(conv_mix_b[l]),
            state_conv_mix[l][:, 0], state_conv_mix[l][:, 1], dc=dc, da=da, ts=ts, bs=bs)
        q_all = qm_s.reshape(2, ts, bs, N_HEADS, HEAD_DV).transpose(2, 0, 3, 1, 4)
        q_all = q_all.reshape(bs, 2 * N_HEADS * ts, HEAD_DV)
        new_rows = lambda a: jnp.pad(
            a.reshape(ts, bs, N_HEADS * HEAD_DV).transpose(1, 0, 2).reshape(bs, ts * N_HEADS, HEAD_DV),
            ((0, 0), (0, LANES - ts * N_HEADS), (0, 0))).astype(BF16)
        yattn_s = _attn_sample(page_table, rel_bias, q_all, new_rows(kf_s), new_rows(vf_s), *lam_args,
                               cache_k2, cache_v2, page=page, page_offset=l * n_phys, ts=ts,
                               lam_init=lam_init)
        yattn_s = yattn_s.reshape(bs, N_HEADS, ts, HEAD_DV).transpose(2, 0, 1, 3)
        yattn_s = yattn_s.reshape(ts * bs, da).astype(BF16)
        x1_s, qc_s = _outproj_q_sample(xs, yconv_s, yattn_s, w_out_b, row(norm_cross_w[l]), w_cq_b)
        qc_b = jnp.pad(qc_s.reshape(ts, bs, d).transpose(1, 0, 2), ((0, 0), (0, rows - ts), (0, 0)))
        o_b = _cross_sample(qc_b, mem_k3, mem_v3, row_offset=l * bs)
        o_s = o_b[:, :ts].transpose(1, 0, 2).reshape(ts * bs, d)
        xs, cffn_s = _ffn_sample(x1_s, o_s, w_co_b, row(norm_ffn_w[l]), w_up_b, conv_ffn_w[l],
                                 row(conv_ffn_b[l]), w_down_b, row(norm_final_w),
                                 state_conv_ffn[l][:, 0], state_conv_ffn[l][:, 1],
                                 ts=ts, bs=bs, final=final)

        t2b = lambda a: a.reshape(ts, bs, N_HEADS, HEAD_DV).transpose(1, 0, 2, 3)
        for lst, val in zip(outs, (
                kf_p.reshape(bp, sp, N_HEADS, 2 * HEAD_DK), vf_p.reshape(bp, sp, N_HEADS, HEAD_DV),
                t2b(kf_s), t2b(vf_s), cmix_p, cmix_s.transpose(1, 0, 2), cffn_p,
                cffn_s.transpose(1, 0, 2), mkf.reshape(bp, n_mem, MEM_HEADS, dh),
                mvf.reshape(bp, n_mem, MEM_HEADS, dh))):
            lst.append(val)

    y_prompt = xp.reshape(bp, sp, d)
    y_sample = xs.reshape(ts, bs, d).transpose(1, 0, 2)
    return (y_prompt, y_sample) + tuple(jnp.stack(o) for o in outs)
```

```python
import functools
import math

import numpy as np
import jax
import jax.numpy as jnp
from jax import lax
from jax.experimental import pallas as pl
from jax.experimental.pallas import tpu as pltpu

F32 = jnp.float32
BF16 = jnp.bfloat16

EPS = 1e-6
NEG = -1e30
LANES = 128
SUBLANES = 8
N_HEADS = 4
HEAD_DK = 64
HEAD_DV = 2 * HEAD_DK
MEM_HEADS = 4
CONV_W = 3
REL_BUCKETS = 32
REL_MAX_EXACT = 16
REL_MAX_DIST = 128
VMEM_LIMIT = 56 * 1024 * 1024

ATTN_TILE = 512
QUERY_CHUNK = 256
ONES_ROWS = 16
LOG2E = 1.4426950408889634
ROW_TILE = 512
FFN_ROW_TILE = 512
PAGES_PER_GROUP = 8
PAGE_SLOTS = 3


def _params(*sem):
    return pltpu.CompilerParams(dimension_semantics=sem, vmem_limit_bytes=VMEM_LIMIT)


def _rms(x, w):
    return x * lax.rsqrt(jnp.mean(x * x, axis=-1, keepdims=True) + EPS) * w


def _dot(a, b):
    return jnp.dot(a, b, preferred_element_type=F32)


def _dot_nt(a, b):
    return lax.dot_general(a, b, (((1,), (1,)), ((), ())), preferred_element_type=F32)


def _lam(lq1, lk1, lq2, lk2, lam_init):
    return (jnp.exp(jnp.sum(lq1[...] * lk1[...], axis=-1, keepdims=True))
            - jnp.exp(jnp.sum(lq2[...] * lk2[...], axis=-1, keepdims=True)) + lam_init)


def _bucket_np(rel):
    n = np.maximum(rel, 0)
    nf = np.maximum(n, 1).astype(np.float32)
    large = REL_MAX_EXACT + (np.log(nf / np.float32(REL_MAX_EXACT))
                             / np.float32(math.log(REL_MAX_DIST / REL_MAX_EXACT))
                             * np.float32(REL_BUCKETS - REL_MAX_EXACT)).astype(np.int32)
    large = np.minimum(large, REL_BUCKETS - 1)
    return np.where(n < REL_MAX_EXACT, n, large).astype(np.int32)


def _bucket_starts():
    buckets = _bucket_np(np.arange(REL_MAX_DIST + 1))
    assert (np.diff(buckets) >= 0).all() and buckets[-1] == REL_BUCKETS - 1
    return [int(np.argmax(buckets >= k)) for k in range(REL_BUCKETS)]


def _rel_bias_tile(rel, rb_ref, h):
    far = rb_ref[REL_BUCKETS - 1, h]
    val = jnp.full(rel.shape, rb_ref[0, h] - far, F32)
    for k, start in enumerate(_bucket_starts()):
        if k > 0:
            val = jnp.where(rel >= start, rb_ref[k, h] - far, val)
    return val


def _softmax_update(chunks, m_prev, l_prev):
    mx = chunks[0]
    for c in chunks[1:]:
        mx = jnp.maximum(mx, c)
    m_next = jnp.maximum(m_prev, jnp.max(mx, axis=1, keepdims=True))
    ps = [jnp.exp(c - m_next) for c in chunks]
    sm = ps[0]
    for p in ps[1:]:
        sm = sm + p
    alpha = jnp.exp(m_prev - m_next)
    l_next = alpha * l_prev + jnp.sum(sm, axis=1, keepdims=True)
    return ps, m_next, l_next, alpha


def _head_out(acc1, l1, acc2, l2, lam, sw, lam_init):
    o = acc1 / l1 - lam * (acc2 / l2)
    return _rms(o, sw) * (1.0 - lam_init)


def _mix_in_prompt_kernel(x_ref, nw_ref, w_ref, cw_ref, cb_ref,
                          yconv_ref, q_ref, kf_ref, vf_ref, kb_ref, vb_ref, st_ref,
                          pre_scr, *, dc, da):
    j = pl.program_id(1)
    tm = x_ref.shape[0]
    h = _rms(x_ref[...], nw_ref[...]).astype(BF16)

    def proj(lo, width):
        return _dot(h, w_ref[:, lo:lo + width])

    @pl.when(j == 0)
    def _zero_prefix():
        pre_scr[0:SUBLANES, :] = jnp.zeros((SUBLANES, dc), F32)

    pre = proj(dc, dc) * proj(2 * dc, dc)
    pre_scr[SUBLANES:SUBLANES + tm, :] = pre
    cw = cw_ref[...]
    conv = (cb_ref[...] + cw[0:1, :] * pre_scr[SUBLANES - 2:SUBLANES - 2 + tm, :]
            + cw[1:2, :] * pre_scr[SUBLANES - 1:SUBLANES - 1 + tm, :] + cw[2:3, :] * pre)
    yconv_ref[...] = (proj(0, dc) * conv).astype(BF16)
    st_ref[...] = pre[tm - 2:tm, :]
    pre_scr[0:SUBLANES, :] = pre[tm - SUBLANES:tm, :]

    q = proj(3 * dc, da) * (HEAD_DK ** -0.5 * LOG2E)
    k = proj(3 * dc + da, da)
    v = proj(3 * dc + 2 * da, da)
    ones_rows = (lax.broadcasted_iota(jnp.int32, (ONES_ROWS, tm), 0) == 0).astype(BF16)
    for hd in range(N_HEADS):
        cols = slice(hd * HEAD_DV, (hd + 1) * HEAD_DV)
        kb_ref[hd] = k[:, cols].astype(BF16)
        kf_ref[pl.ds(hd, tm, stride=N_HEADS), :] = k[:, cols]
        vf_ref[pl.ds(hd, tm, stride=N_HEADS), :] = v[:, cols]
        q_ref[hd] = q[:, cols].T.astype(BF16)
        vb_ref[hd, 0:HEAD_DV, :] = v[:, cols].T.astype(BF16)
        vb_ref[hd, HEAD_DV:, :] = ones_rows


def _mix_in_prompt(x2, nw, w_in, cw, cb, *, batch, seq, dc, da):
    d = x2.shape[1]
    tm = min(ATTN_TILE, seq)
    nj = seq // tm
    rows = lambda b, j: (b * nj + j, 0)
    const = lambda b, j: (0, 0)
    tposed = lambda r: pl.BlockSpec((None, N_HEADS, None, r, tm), lambda b, j: (b, 0, j, 0, 0))
    tposed_shape = lambda r: jax.ShapeDtypeStruct((batch, N_HEADS, nj, r, tm), BF16)
    n = batch * seq
    return pl.pallas_call(
        functools.partial(_mix_in_prompt_kernel, dc=dc, da=da),
        grid=(batch, nj),
        in_specs=[pl.BlockSpec((tm, d), rows),
                  pl.BlockSpec((1, d), const),
                  pl.BlockSpec(w_in.shape, const),
                  pl.BlockSpec(cw.shape, const),
                  pl.BlockSpec((1, dc), const)],
        out_specs=[pl.BlockSpec((tm, dc), rows),
                   tposed(HEAD_DV),
                   pl.BlockSpec((tm * N_HEADS, HEAD_DV), rows),
                   pl.BlockSpec((tm * N_HEADS, HEAD_DV), rows),
                   pl.BlockSpec((None, N_HEADS, tm, HEAD_DV), lambda b, j: (b, 0, j, 0)),
                   tposed(HEAD_DV + ONES_ROWS),
                   pl.BlockSpec((None, CONV_W - 1, dc), lambda b, j: (b, 0, 0))],
        out_shape=[jax.ShapeDtypeStruct((n, dc), BF16),
                   tposed_shape(HEAD_DV),
                   jax.ShapeDtypeStruct((n * N_HEADS, HEAD_DV), F32),
                   jax.ShapeDtypeStruct((n * N_HEADS, HEAD_DV), F32),
                   jax.ShapeDtypeStruct((batch, N_HEADS, seq, HEAD_DV), BF16),
                   tposed_shape(HEAD_DV + ONES_ROWS),
                   jax.ShapeDtypeStruct((batch, CONV_W - 1, dc), F32)],
        scratch_shapes=[pltpu.VMEM((tm + SUBLANES, dc), F32)],
        compiler_params=_params("arbitrary", "arbitrary"),
    )(x2, nw, w_in, cw, cb)


def _conv_time_major(slabs, cw, cb):
    return [cb + cw[0:1, :] * slabs[t] + cw[1:2, :] * slabs[t + 1] + cw[2:3, :] * slabs[t + 2]
            for t in range(len(slabs) - 2)]


def _mix_in_sample_kernel(x_ref, nw_ref, w_ref, cw_ref, cb_ref, s0_ref, s1_ref,
                          yconv_ref, qm_ref, kf_ref, vf_ref, st_ref, *, dc, da, ts, bs):
    h = _rms(x_ref[...], nw_ref[...]).astype(BF16)

    def proj(lo, width):
        return _dot(h, w_ref[:, lo:lo + width])

    pre = proj(dc, dc) * proj(2 * dc, dc)
    gate = proj(0, dc)
    slabs = [s0_ref[...], s1_ref[...]] + [pre[t * bs:(t + 1) * bs, :] for t in range(ts)]
    conv = _conv_time_major(slabs, cw_ref[...], cb_ref[...])
    for t in range(ts):
        yconv_ref[t * bs:(t + 1) * bs, :] = (gate[t * bs:(t + 1) * bs, :] * conv[t]).astype(BF16)
    st_ref[0] = slabs[-2]
    st_ref[1] = slabs[-1]

    q = proj(3 * dc, da) * (HEAD_DK ** -0.5)
    lane = lax.broadcasted_iota(jnp.int32, q.shape, 1) % HEAD_DV
    qm_ref[0] = jnp.where(lane < HEAD_DK, q, 0.0).astype(BF16)
    qm_ref[1] = jnp.where(lane >= HEAD_DK, q, 0.0).astype(BF16)
    kf_ref[...] = proj(3 * dc + da, da)
    vf_ref[...] = proj(3 * dc + 2 * da, da)


def _mix_in_sample(xt, nw, w_in, cw, cb, s0, s1, *, dc, da, ts, bs):
    n = ts * bs
    return pl.pallas_call(
        functools.partial(_mix_in_sample_kernel, dc=dc, da=da, ts=ts, bs=bs),
        out_shape=[jax.ShapeDtypeStruct((n, dc), BF16),
                   jax.ShapeDtypeStruct((2, n, da), BF16),
                   jax.ShapeDtypeStruct((n, da), F32),
                   jax.ShapeDtypeStruct((n, da), F32),
                   jax.ShapeDtypeStruct((CONV_W - 1, bs, dc), F32)],
        compiler_params=pltpu.CompilerParams(vmem_limit_bytes=VMEM_LIMIT),
    )(xt, nw, w_in, cw, cb, s0, s1)


def _mem_kv_kernel(m_ref, nw_ref, w_ref, kf_ref, vf_ref, kb_ref, vb_ref, *, d):
    h = _rms(m_ref[...], nw_ref[...]).astype(BF16)
    k = _dot(h, w_ref[:, 0:d])
    kf_ref[...] = k
    kb_ref[...] = k.astype(BF16)
    v = _dot(h, w_ref[:, d:2 * d])
    vf_ref[...] = v
    vb_ref[...] = v.astype(BF16)


def _mem_kv(mem2, nw, w_ckv, *, batch, n_mem):
    d = mem2.shape[1]
    rows = lambda b: (b, 0)
    const = lambda b: (0, 0)
    n = batch * n_mem
    return pl.pallas_call(
        functools.partial(_mem_kv_kernel, d=d),
        grid=(batch,),
        in_specs=[pl.BlockSpec((n_mem, d), rows), pl.BlockSpec((1, d), const),
                  pl.BlockSpec(w_ckv.shape, const)],
        out_specs=[pl.BlockSpec((n_mem, d), rows)] * 4,
        out_shape=[jax.ShapeDtypeStruct((n, d), F32), jax.ShapeDtypeStruct((n, d), F32),
                   jax.ShapeDtypeStruct((n, d), BF16), jax.ShapeDtypeStruct((n, d), BF16)],
        compiler_params=_params("arbitrary"),
    )(mem2, nw, w_ckv)


def _attn_prompt_kernel(rb_ref, q_ref, k_ref, vt_ref, lq1, lk1, lq2, lk2, sw_ref, o_ref,
                        qp_scr, m_scr, acc_scr, bd_scr, bs_scr, sa_scr, sb_scr, *, T, lam_init):
    nb = T // LANES
    n_tiles = q_ref.shape[0]

    def _assemble_bias_tiles():
        h = pl.program_id(1)
        rel = (lax.broadcasted_iota(jnp.int32, (LANES, LANES), 1)
               - lax.broadcasted_iota(jnp.int32, (LANES, LANES), 0))
        d0 = jnp.where(rel >= 0, _rel_bias_tile(jnp.maximum(rel, 0), rb_ref, h) * LOG2E, NEG)
        d1 = _rel_bias_tile(rel + LANES, rb_ref, h) * LOG2E
        zero = jnp.zeros((LANES, LANES), F32)
        neg = jnp.full((LANES, LANES), NEG, F32)
        for bi in range(nb):
            for bj in range(nb):
                rs = slice(bi * LANES, (bi + 1) * LANES)
                cs = slice(bj * LANES, (bj + 1) * LANES)
                bd_scr[rs, cs] = d0 if bi == bj else d1 if bj == bi + 1 else zero if bj > bi else neg
                bs_scr[rs, cs] = d1 if (bj == 0 and bi == nb - 1) else zero

    _assemble_bias_tiles()

    def tile(qi, carry):
        qt = q_ref[qi].astype(F32)
        sub = lax.broadcasted_iota(jnp.int32, qt.shape, 0)
        qp_scr[:, 0:T] = jnp.where(sub < HEAD_DK, qt, 0.0).astype(BF16)
        qp_scr[:, T:2 * T] = jnp.where(sub >= HEAD_DK, qt, 0.0).astype(BF16)
        m_scr[...] = jnp.full(m_scr.shape, -jnp.inf, F32)
        acc_scr[...] = jnp.zeros(acc_scr.shape, F32)

        def scores(j, dst):
            dst[...] = _dot(k_ref[pl.ds(pl.multiple_of(j * T, T), T), :], qp_scr[...])

        def stage(j, src, bias_ref, j_next=None, dst=None):
            if j_next is not None:
                scores(j_next, dst)
            vt = vt_ref[j]
            for c in range(2 * T // QUERY_CHUNK):
                cs = slice(c * QUERY_CHUNK, (c + 1) * QUERY_CHUNK)
                s = src[:, cs]
                if bias_ref is not None:
                    lo = (c * QUERY_CHUNK) % T
                    s = s + bias_ref[:, lo:lo + QUERY_CHUNK]
                m_prev = m_scr[:, cs]
                m_next = jnp.maximum(m_prev, jnp.max(s, axis=0, keepdims=True))
                p = jnp.exp2(s - m_next).astype(BF16)
                acc_scr[:, cs] = jnp.exp2(m_prev - m_next) * acc_scr[:, cs] + _dot(vt, p)
                m_scr[:, cs] = m_next

        odd = qi % 2 == 0
        first = jnp.where(odd, 1, 0)

        @pl.when(odd)
        def _first_scores_odd():
            scores(0, sb_scr)

        @pl.when(jnp.logical_not(odd))
        def _first_scores_even():
            scores(0, sa_scr)

        @pl.when(odd & (qi >= 2))
        def _single_far_block():
            stage(0, sb_scr, None, 1, sa_scr)

        def far_pair(i, c):
            j = first + 2 * i
            stage(j, sa_scr, None, j + 1, sb_scr)
            stage(j + 1, sb_scr, None, j + 2, sa_scr)
            return c

        lax.fori_loop(0, jnp.maximum(qi - 1, 0) // 2, far_pair, 0)

        @pl.when(qi >= 1)
        def _sub_diagonal_and_diagonal():
            stage(qi - 1, sa_scr, bs_scr, qi, sb_scr)
            stage(qi, sb_scr, bd_scr)

        @pl.when(qi == 0)
        def _diagonal_only():
            stage(0, sb_scr, bd_scr)

        lam = _lam(lq1, lk1, lq2, lk2, lam_init)
        acc = acc_scr[0:HEAD_DV, :]
        l = acc_scr[HEAD_DV:HEAD_DV + 1, :]
        o = acc[:, 0:T] / l[:, 0:T] - lam * (acc[:, T:2 * T] / l[:, T:2 * T])
        y = o * lax.rsqrt(jnp.mean(o * o, axis=0, keepdims=True) + EPS) * sw_ref[...] * (1.0 - lam_init)
        o_ref[pl.ds(pl.multiple_of(qi * T, T), T), :] = y.T.astype(o_ref.dtype)
        return carry

    lax.fori_loop(0, n_tiles, tile, 0)


def _attn_prompt(rel_bias, qt, kb, vt, lq1, lk1, lq2, lk2, sw_col, *, batch, seq, lam_init):
    T = qt.shape[-1]
    nq = seq // T
    da = N_HEADS * HEAD_DV
    vec = lambda b, h: (0, 0)
    return pl.pallas_call(
        functools.partial(_attn_prompt_kernel, T=T, lam_init=lam_init),
        grid=(batch, N_HEADS),
        in_specs=[pl.BlockSpec(memory_space=pltpu.SMEM),
                  pl.BlockSpec((None, None, nq, HEAD_DV, T), lambda b, h: (b, h, 0, 0, 0)),
                  pl.BlockSpec((None, None, seq, HEAD_DV), lambda b, h: (b, h, 0, 0)),
                  pl.BlockSpec((None, None, nq, vt.shape[3], T), lambda b, h: (b, h, 0, 0, 0)),
                  pl.BlockSpec((1, HEAD_DK), vec), pl.BlockSpec((1, HEAD_DK), vec),
                  pl.BlockSpec((1, HEAD_DK), vec), pl.BlockSpec((1, HEAD_DK), vec),
                  pl.BlockSpec((HEAD_DV, 1), vec)],
        out_specs=pl.BlockSpec((seq, HEAD_DV), lambda b, h: (b, h)),
        out_shape=jax.ShapeDtypeStruct((batch * seq, da), BF16),
        scratch_shapes=[pltpu.VMEM((HEAD_DV, 2 * T), BF16),
                        pltpu.VMEM((1, 2 * T), F32),
                        pltpu.VMEM((vt.shape[3], 2 * T), F32),
                        pltpu.VMEM((T, T), F32),
                        pltpu.VMEM((T, T), F32),
                        pltpu.VMEM((T, 2 * T), F32),
                        pltpu.VMEM((T, 2 * T), F32)],
        compiler_params=_params("arbitrary", "arbitrary"),
    )(rel_bias, qt, kb, vt, lq1, lk1, lq2, lk2, sw_col)


def _attn_sample_kernel(pt_ref, rb_ref, q_ref, kn_ref, vn_ref, lq1, lk1, lq2, lk2, sw_ref, k_hbm, v_hbm,
                        o_ref, m_scr, l_scr, acc_scr, hm_scr, bl_scr, bn_scr, kbuf, vbuf, sems,
                        *, pages, page_offset, ts, lam_init):
    b = pl.program_id(0)
    nb = pl.num_programs(0)
    rows, pk = hm_scr.shape
    nk = bn_scr.shape[1]
    rm = rows // 2
    page = pk // N_HEADS
    groups = pt_ref.shape[1] // pages
    hbits = N_HEADS.bit_length() - 1
    tbits = ts.bit_length() - 1

    def group_copies(t, slot):
        bt = t // groups
        g0 = (t % groups) * pages
        out = []
        for i in range(pages):
            row0 = pl.multiple_of((page_offset + pt_ref[bt, g0 + i]) * pk, pk)
            dst = pl.ds(i * pk, pk)
            out.append(pltpu.make_async_copy(k_hbm.at[pl.ds(row0, pk)], kbuf.at[slot, dst], sems.at[0, slot]))
            out.append(pltpu.make_async_copy(v_hbm.at[pl.ds(row0, pk)], vbuf.at[slot, dst], sems.at[1, slot]))
        return out

    def start_group(t):
        @pl.when(t < nb * groups)
        def _():
            for cp in group_copies(t, t % PAGE_SLOTS):
                cp.start()

    @pl.when(b == 0)
    def _first_step():
        for t in range(PAGE_SLOTS - 1):
            start_group(jnp.int32(t))

        def tables(ncols, rel_of):
            r = lax.broadcasted_iota(jnp.int32, (rows, ncols), 0)
            c = lax.broadcasted_iota(jnp.int32, (rows, ncols), 1)
            rhead = (r >> tbits) & (N_HEADS - 1)
            same = rhead == (c & (N_HEADS - 1))
            rel = rel_of(r & (ts - 1), c >> hbits)
            bias = jnp.zeros((rows, ncols), F32)
            for h in range(N_HEADS):
                bias = jnp.where(rhead == h, _rel_bias_tile(jnp.maximum(rel, 0), rb_ref, h), bias)
            return same, rel, c >> hbits, bias

        same, _, _, bias = tables(pk, lambda tok, key: tok + page - key)
        hm_scr[...] = jnp.where(same, 0.0, NEG)
        bl_scr[...] = jnp.where(same, bias, NEG)
        same, rel, key, bias = tables(nk, lambda tok, key: tok - key)
        bn_scr[...] = jnp.where(same & (rel >= 0) & (key < ts), bias, NEG)

    m_scr[...] = jnp.full(m_scr.shape, -jnp.inf, F32)
    l_scr[...] = jnp.zeros(l_scr.shape, F32)
    acc_scr[...] = jnp.zeros(acc_scr.shape, F32)

    def update(scores, values):
        chunks, sizes = [], []
        for s in scores:
            n = s.shape[1] // LANES
            sizes.append(n)
            chunks += [s[:, c * LANES:(c + 1) * LANES] for c in range(n)]
        ps, m_next, l_next, alpha = _softmax_update(chunks, m_scr[...], l_scr[...])
        pv, at = None, 0
        for n, v in zip(sizes, values):
            p = jnp.concatenate([x.astype(BF16) for x in ps[at:at + n]], axis=1) if n > 1 \
                else ps[at].astype(BF16)
            at += n
            pv = _dot(p, v) if pv is None else pv + _dot(p, v)
        acc_scr[...] = alpha * acc_scr[...] + pv
        m_scr[...] = m_next
        l_scr[...] = l_next

    q = q_ref[...]

    def group_body(g, carry):
        t = b * groups + g
        slot = t % PAGE_SLOTS
        for cp in group_copies(t, slot):
            cp.wait()
        start_group(t + PAGE_SLOTS - 1)
        kg = kbuf.at[slot]
        vg = vbuf.at[slot]
        last = g == groups - 1
        scores = [_dot_nt(q, kg[i * pk:(i + 1) * pk, :].astype(BF16)) + hm_scr[...]
                  for i in range(pages - 1)]
        scores.append(_dot_nt(q, kg[(pages - 1) * pk:pages * pk, :].astype(BF16))
                      + jnp.where(last, bl_scr[...], hm_scr[...]))
        update(scores, [vg[i * pk:(i + 1) * pk, :].astype(BF16) for i in range(pages)])
        return carry

    lax.fori_loop(0, groups, group_body, 0)

    update([_dot_nt(q, kn_ref[...]) + bn_scr[...]], [vn_ref[...]])
    lam = _lam(lq1, lk1, lq2, lk2, lam_init)
    acc = acc_scr[...]
    l = l_scr[...]
    o_ref[...] = _head_out(acc[0:rm], l[0:rm], acc[rm:rows], l[rm:rows], lam, sw_ref[...], lam_init)


def _attn_sample(page_table, rel_bias, q_all, knew, vnew, lq1, lk1, lq2, lk2, sw, cache_k, cache_v,
                 *, page, page_offset, ts, lam_init):
    bs, n_pages = page_table.shape
    rows = q_all.shape[1]
    nk = knew.shape[1]
    pk = page * N_HEADS
    pages = math.gcd(PAGES_PER_GROUP, n_pages)
    vec = lambda b, pt: (0, 0)
    per_b = lambda b, pt: (b, 0, 0)
    grid_spec = pltpu.PrefetchScalarGridSpec(
        num_scalar_prefetch=1,
        grid=(bs,),
        in_specs=[pl.BlockSpec(memory_space=pltpu.SMEM),
                  pl.BlockSpec((None, rows, HEAD_DV), per_b),
                  pl.BlockSpec((None, nk, HEAD_DV), per_b),
                  pl.BlockSpec((None, nk, HEAD_DV), per_b),
                  pl.BlockSpec((1, HEAD_DK), vec), pl.BlockSpec((1, HEAD_DK), vec),
                  pl.BlockSpec((1, HEAD_DK), vec), pl.BlockSpec((1, HEAD_DK), vec),
                  pl.BlockSpec((1, HEAD_DV), vec),
                  pl.BlockSpec(memory_space=pl.ANY), pl.BlockSpec(memory_space=pl.ANY)],
        out_specs=pl.BlockSpec((None, rows // 2, HEAD_DV), per_b),
        scratch_shapes=[pltpu.VMEM((rows, LANES), F32),
                        pltpu.VMEM((rows, LANES), F32),
                        pltpu.VMEM((rows, HEAD_DV), F32),
                        pltpu.VMEM((rows, pk), F32),
                        pltpu.VMEM((rows, pk), F32),
                        pltpu.VMEM((rows, nk), F32),
                        pltpu.VMEM((PAGE_SLOTS, pages * pk, HEAD_DV), F32),
                        pltpu.VMEM((PAGE_SLOTS, pages * pk, HEAD_DV), F32),
                        pltpu.SemaphoreType.DMA((2, PAGE_SLOTS))],
    )
    return pl.pallas_call(
        functools.partial(_attn_sample_kernel, pages=pages, page_offset=page_offset, ts=ts,
                          lam_init=lam_init),
        grid_spec=grid_spec,
        out_shape=jax.ShapeDtypeStruct((bs, rows // 2, HEAD_DV), F32),
        compiler_params=_params("arbitrary"),
    )(page_table, rel_bias, q_all, knew, vnew, lq1, lk1, lq2, lk2, sw, cache_k, cache_v)


def _cross_heads(qc, mk_ref, mv_ref, o_scr, dh):
    for hd in range(MEM_HEADS):
        cols = slice(hd * dh, (hd + 1) * dh)
        s = _dot_nt(qc[:, cols], mk_ref[:, cols].astype(BF16))
        p = jnp.exp(s - jnp.max(s, axis=1, keepdims=True))
        o = _dot(p.astype(BF16), mv_ref[:, cols].astype(BF16)) / jnp.sum(p, axis=1, keepdims=True)
        o_scr[:, cols] = o.astype(BF16)


def _cross_prompt_kernel(x_ref, yc_ref, ya_ref, wo_ref, nw_ref, wq_ref, mk_ref, mv_ref, wc_ref,
                         o_ref, o_scr, *, dc, dh):
    x1 = x_ref[...] + _dot(yc_ref[...], wo_ref[0:dc, :]) + _dot(ya_ref[...], wo_ref[dc:, :])
    h = _rms(x1, nw_ref[...]).astype(BF16)
    qc = (_dot(h, wq_ref[...]) * (dh ** -0.5)).astype(BF16)
    _cross_heads(qc, mk_ref, mv_ref, o_scr, dh)
    o_ref[...] = x1 + _dot(o_scr[...], wc_ref[...])


def _cross_prompt(x2, yconv, yattn, w_out, nw, w_cq, mk, mv, w_co, *, batch, seq, n_mem):
    d = x2.shape[1]
    dc = yconv.shape[1]
    da = yattn.shape[1]
    tm = min(ROW_TILE, seq)
    nj = seq // tm
    rows = lambda b, j: (b * nj + j, 0)
    const = lambda b, j: (0, 0)
    memb = lambda b, j: (b, 0)
    return pl.pallas_call(
        functools.partial(_cross_prompt_kernel, dc=dc, dh=d // MEM_HEADS),
        grid=(batch, nj),
        in_specs=[pl.BlockSpec((tm, d), rows), pl.BlockSpec((tm, dc), rows), pl.BlockSpec((tm, da), rows),
                  pl.BlockSpec(w_out.shape, const), pl.BlockSpec((1, d), const),
                  pl.BlockSpec(w_cq.shape, const),
                  pl.BlockSpec((n_mem, d), memb), pl.BlockSpec((n_mem, d), memb),
                  pl.BlockSpec(w_co.shape, const)],
        out_specs=pl.BlockSpec((tm, d), rows),
        out_shape=jax.ShapeDtypeStruct(x2.shape, F32),
        scratch_shapes=[pltpu.VMEM((tm, d), BF16)],
        compiler_params=_params("arbitrary", "arbitrary"),
    )(x2, yconv, yattn, w_out, nw, w_cq, mk, mv, w_co)


def _outproj_q_sample_kernel(x_ref, yc_ref, ya_ref, wo_ref, nw_ref, wq_ref, x1_ref, qc_ref, *, dc, dh):
    x1 = x_ref[...] + _dot(yc_ref[...], wo_ref[0:dc, :]) + _dot(ya_ref[...], wo_ref[dc:, :])
    x1_ref[...] = x1
    h = _rms(x1, nw_ref[...]).astype(BF16)
    qc_ref[...] = (_dot(h, wq_ref[...]) * (dh ** -0.5)).astype(BF16)


def _outproj_q_sample(xt, yconv, yattn, w_out, nw, w_cq):
    d = xt.shape[1]
    return pl.pallas_call(
        functools.partial(_outproj_q_sample_kernel, dc=yconv.shape[1], dh=d // MEM_HEADS),
        out_shape=[jax.ShapeDtypeStruct(xt.shape, F32), jax.ShapeDtypeStruct(xt.shape, BF16)],
        compiler_params=pltpu.CompilerParams(vmem_limit_bytes=VMEM_LIMIT),
    )(xt, yconv, yattn, w_out, nw, w_cq)


def _cross_sample_kernel(qc_ref, mk_ref, mv_ref, o_ref, *, dh):
    _cross_heads(qc_ref[...], mk_ref, mv_ref, o_ref, dh)


def _cross_sample(qc, mk, mv, *, row_offset):
    bs, rows, d = qc.shape
    n_mem = mk.shape[1]
    per_b = lambda b: (b, 0, 0)
    mem_b = lambda b: (row_offset + b, 0, 0)
    return pl.pallas_call(
        functools.partial(_cross_sample_kernel, dh=d // MEM_HEADS),
        grid=(bs,),
        in_specs=[pl.BlockSpec((None, rows, d), per_b), pl.BlockSpec((None, n_mem, d), mem_b),
                  pl.BlockSpec((None, n_mem, d), mem_b)],
        out_specs=pl.BlockSpec((None, rows, d), per_b),
        out_shape=jax.ShapeDtypeStruct(qc.shape, BF16),
        compiler_params=_params("arbitrary"),
    )(qc, mk, mv)


def _silu(g):
    return g * (1.0 / (1.0 + jnp.exp(-g)))


def _ffn_prompt_kernel(x_ref, nw_ref, wu_ref, cw_ref, cb_ref, wd_ref, fw_ref, y_ref, st_ref,
                       up_scr, carry_scr, *, dff, final):
    j = pl.program_id(1)
    tm = x_ref.shape[0]
    x = x_ref[...]
    h = _rms(x, nw_ref[...]).astype(BF16)

    @pl.when(j == 0)
    def _zero_prefix():
        carry_scr[...] = jnp.zeros(carry_scr.shape, F32)

    def conv_half(lo):
        cols = slice(lo, lo + dff)
        up = _dot(h, wu_ref[:, cols])
        up_scr[0:SUBLANES, :] = carry_scr[:, cols]
        up_scr[SUBLANES:SUBLANES + tm, :] = up
        cw = cw_ref[:, cols]
        conv = (cb_ref[:, cols] + cw[0:1, :] * up_scr[SUBLANES - 2:SUBLANES - 2 + tm, :]
                + cw[1:2, :] * up_scr[SUBLANES - 1:SUBLANES - 1 + tm, :] + cw[2:3, :] * up)
        carry_scr[:, cols] = up[tm - SUBLANES:tm, :]
        st_ref[:, cols] = up[tm - 2:tm, :]
        return conv

    g = conv_half(0)
    u = conv_half(dff)
    x3 = x + _dot((_silu(g) * u).astype(BF16), wd_ref[...])
    y_ref[...] = _rms(x3, fw_ref[...]) if final else x3


def _ffn_prompt(x2, nw, w_up, cw, cb, w_down, fw, *, batch, seq, final):
    d = x2.shape[1]
    dff = w_down.shape[0]
    tm = min(FFN_ROW_TILE, seq)
    nj = seq // tm
    rows = lambda b, j: (b * nj + j, 0)
    const = lambda b, j: (0, 0)
    resident = lambda shape: pl.BlockSpec(shape, const, pipeline_mode=pl.Buffered(1))
    return pl.pallas_call(
        functools.partial(_ffn_prompt_kernel, dff=dff, final=final),
        grid=(batch, nj),
        in_specs=[pl.BlockSpec((tm, d), rows), pl.BlockSpec((1, d), const),
                  resident(w_up.shape), pl.BlockSpec(cw.shape, const),
                  pl.BlockSpec((1, 2 * dff), const), resident(w_down.shape),
                  pl.BlockSpec((1, d), const)],
        out_specs=[pl.BlockSpec((tm, d), rows),
                   pl.BlockSpec((None, CONV_W - 1, 2 * dff), lambda b, j: (b, 0, 0))],
        out_shape=[jax.ShapeDtypeStruct(x2.shape, F32),
                   jax.ShapeDtypeStruct((batch, CONV_W - 1, 2 * dff), F32)],
        scratch_shapes=[pltpu.VMEM((tm + SUBLANES, dff), F32),
                        pltpu.VMEM((SUBLANES, 2 * dff), F32)],
        compiler_params=_params("arbitrary", "arbitrary"),
    )(x2, nw, w_up, cw, cb, w_down, fw)


def _ffn_sample_kernel(x1_ref, o_ref, wc_ref, nw_ref, wu_ref, cw_ref, cb_ref, wd_ref, fw_ref,
                       s0_ref, s1_ref, y_ref, st_ref, hid_scr, *, dff, ts, bs, final):
    x2 = x1_ref[...] + _dot(o_ref[...], wc_ref[...])
    h = _rms(x2, nw_ref[...]).astype(BF16)

    def conv_half(lo):
        cols = slice(lo, lo + dff)
        up = _dot(h, wu_ref[:, cols])
        slabs = [s0_ref[:, cols], s1_ref[:, cols]] + [up[t * bs:(t + 1) * bs, :] for t in range(ts)]
        st_ref[0, :, cols] = slabs[-2]
        st_ref[1, :, cols] = slabs[-1]
        return _conv_time_major(slabs, cw_ref[:, cols], cb_ref[:, cols])

    g = conv_half(0)
    u = conv_half(dff)
    for t in range(ts):
        hid_scr[t * bs:(t + 1) * bs, :] = (_silu(g[t]) * u[t]).astype(BF16)
    x3 = x2 + _dot(hid_scr[...], wd_ref[...])
    y_ref[...] = _rms(x3, fw_ref[...]) if final else x3


def _ffn_sample(x1, o, w_co, nw, w_up, cw, cb, w_down, fw, s0, s1, *, ts, bs, final):
    dff = w_down.shape[0]
    return pl.pallas_call(
        functools.partial(_ffn_sample_kernel, dff=dff, ts=ts, bs=bs, final=final),
        out_shape=[jax.ShapeDtypeStruct(x1.shape, F32),
                   jax.ShapeDtypeStruct((CONV_W - 1, bs, 2 * dff), F32)],
        scratch_shapes=[pltpu.VMEM((ts * bs, dff), BF16)],
        compiler_params=pltpu.CompilerParams(vmem_limit_bytes=VMEM_LIMIT),
    )(x1, o, w_co, nw, w_up, cw, cb, w_down, fw, s0, s1)


def kernel(x_prompt, x_sample, mem_prompt, cache_k, cache_v, page_table, state_conv_mix, state_conv_ffn, cache_mem_k, cache_mem_v, rel_bias, norm_mix_w, w_in, conv_mix_w, conv_mix_b, lambda_q1, lambda_k1, lambda_q2, lambda_k2, subln_w, w_out, norm_cross_w, norm_mem_w, w_cq, w_ckv, w_co, norm_ffn_w, w_up, conv_ffn_w, conv_ffn_b, w_down, norm_final_w):
    depth = w_in.shape[0]
    bp, sp, d = x_prompt.shape
    bs, ts, _ = x_sample.shape
    n_mem = mem_prompt.shape[1]
    n_phys, page = cache_k.shape[1], cache_k.shape[2]
    n_pages = page_table.shape[1]
    past_len = n_pages * page
    dc = conv_mix_w.shape[2]
    da = N_HEADS * HEAD_DV
    dff = w_down.shape[1]
    dh = d // MEM_HEADS
    assert page == LANES and CONV_W - 1 <= ts <= SUBLANES and ts & (ts - 1) == 0
    assert (N_HEADS * ts) % SUBLANES == 0 and cache_k.shape[3:] == (N_HEADS, HEAD_DV)
    rows = 2 * SUBLANES

    assert _bucket_np(np.arange(REL_MAX_DIST, max(sp, past_len + ts) + 1)).min() == REL_BUCKETS - 1
    row = lambda a: a.reshape(1, -1).astype(F32)
    rel_bias = rel_bias.astype(F32)
    cache_k2 = cache_k.reshape(depth * n_phys * page * N_HEADS, HEAD_DV)
    cache_v2 = cache_v.reshape(depth * n_phys * page * N_HEADS, HEAD_DV)
    mem_k3 = cache_mem_k.reshape(depth * bs, n_mem, d)
    mem_v3 = cache_mem_v.reshape(depth * bs, n_mem, d)

    xp = x_prompt.reshape(bp * sp, d)
    xs = x_sample.transpose(1, 0, 2).reshape(ts * bs, d)
    mem2 = mem_prompt.reshape(bp * n_mem, d)
    outs = [[] for _ in range(10)]
    for l in range(depth):
        lam_init = 0.8 - 0.6 * math.exp(-0.3 * l)
        final = l == depth - 1
        w_in_b, w_out_b = w_in[l].astype(BF16), w_out[l].astype(BF16)
        w_cq_b, w_ckv_b, w_co_b = w_cq[l].astype(BF16), w_ckv[l].astype(BF16), w_co[l].astype(BF16)
        w_up_b, w_down_b = w_up[l].astype(BF16), w_down[l].astype(BF16)
        lam_args = (row(lambda_q1[l]), row(lambda_k1[l]), row(lambda_q2[l]), row(lambda_k2[l]),
                    row(subln_w[l]))

        yconv_p, q_p, kf_p, vf_p, kb_p, vb_p, cmix_p = _mix_in_prompt(
            xp, row(norm_mix_w[l]), w_in_b, conv_mix_w[l], row(conv_mix_b[l]),
            batch=bp, seq=sp, dc=dc, da=da)
        yattn_p = _attn_prompt(rel_bias, q_p, kb_p, vb_p, *lam_args[:4], subln_w[l].reshape(-1, 1).astype(F32),
                               batch=bp, seq=sp, lam_init=lam_init)
        mkf, mvf, mkb, mvb = _mem_kv(mem2, row(norm_mem_w[l]), w_ckv_b, batch=bp, n_mem=n_mem)
        x2_p = _cross_prompt(xp, yconv_p, yattn_p, w_out_b, row(norm_cross_w[l]), w_cq_b, mkb, mvb,
                             w_co_b, batch=bp, seq=sp, n_mem=n_mem)
        xp, cffn_p = _ffn_prompt(x2_p, row(norm_ffn_w[l]), w_up_b, conv_ffn_w[l], row(conv_ffn_b[l]),
                                 w_down_b, row(norm_final_w), batch=bp, seq=sp, final=final)

        yconv_s, qm_s, kf_s, vf_s, cmix_s = _mix_in_sample(
            xs, row(norm_mix_w[l]), w_in_b, conv_mix_w[l], row(conv_mix_b[l]),
            state_conv_mix[l][:, 0], state_conv_mix[l][:, 1], dc=dc, da=da, ts=ts, bs=bs)
        q_all = qm_s.reshape(2, ts, bs, N_HEADS, HEAD_DV).transpose(2, 0, 3, 1, 4)
        q_all = q_all.reshape(bs, 2 * N_HEADS * ts, HEAD_DV)
        new_rows = lambda a: jnp.pad(
            a.reshape(ts, bs, N_HEADS * HEAD_DV).transpose(1, 0, 2).reshape(bs, ts * N_HEADS, HEAD_DV),
            ((0, 0), (0, LANES - ts * N_HEADS), (0, 0))).astype(BF16)
        yattn_s = _attn_sample(page_table, rel_bias, q_all, new_rows(kf_s), new_rows(vf_s), *lam_args,
                               cache_k2, cache_v2, page=page, page_offset=l * n_phys, ts=ts,
                               lam_init=lam_init)
        yattn_s = yattn_s.reshape(bs, N_HEADS, ts, HEAD_DV).transpose(2, 0, 1, 3)
        yattn_s = yattn_s.reshape(ts * bs, da).astype(BF16)
        x1_s, qc_s = _outproj_q_sample(xs, yconv_s, yattn_s, w_out_b, row(norm_cross_w[l]), w_cq_b)
        qc_b = jnp.pad(qc_s.reshape(ts, bs, d).transpose(1, 0, 2), ((0, 0), (0, rows - ts), (0, 0)))
        o_b = _cross_sample(qc_b, mem_k3, mem_v3, row_offset=l * bs)
        o_s = o_b[:, :ts].transpose(1, 0, 2).reshape(ts * bs, d)
        xs, cffn_s = _ffn_sample(x1_s, o_s, w_co_b, row(norm_ffn_w[l]), w_up_b, conv_ffn_w[l],
                                 row(conv_ffn_b[l]), w_down_b, row(norm_final_w),
                                 state_conv_ffn[l][:, 0], state_conv_ffn[l][:, 1],
                                 ts=ts, bs=bs, final=final)

        t2b = lambda a: a.reshape(ts, bs, N_HEADS, HEAD_DV).transpose(1, 0, 2, 3)
        for lst, val in zip(outs, (
                kf_p.reshape(bp, sp, N_HEADS, 2 * HEAD_DK), vf_p.reshape(bp, sp, N_HEADS, HEAD_DV),
                t2b(kf_s), t2b(vf_s), cmix_p, cmix_s.transpose(1, 0, 2), cffn_p,
                cffn_s.transpose(1, 0, 2), mkf.reshape(bp, n_mem, MEM_HEADS, dh),
                mvf.reshape(bp, n_mem, MEM_HEADS, dh))):
            lst.append(val)

    y_prompt = xp.reshape(bp, sp, d)
    y_sample = xs.reshape(ts, bs, d).transpose(1, 0, 2)
    return (y_prompt, y_sample) + tuple(jnp.stack(o) for o in outs)
```

```python
import functools
import math

import numpy as np
import jax
import jax.numpy as jnp
from jax import lax
from jax.experimental import pallas as pl
from jax.experimental.pallas import tpu as pltpu

F32 = jnp.float32
BF16 = jnp.bfloat16

EPS = 1e-6
NEG = -1e30
LANES = 128
SUBLANES = 8
N_HEADS = 4
HEAD_DK = 64
HEAD_DV = 2 * HEAD_DK
MEM_HEADS = 4
CONV_W = 3
REL_BUCKETS = 32
REL_MAX_EXACT = 16
REL_MAX_DIST = 128
VMEM_LIMIT = 56 * 1024 * 1024

ATTN_TILE = 512
QUERY_CHUNK = 256
ONES_ROWS = 16
LOG2E = 1.4426950408889634
ROW_TILE = 512
FFN_ROW_TILE = 512
PAGES_PER_GROUP = 8
PAGE_SLOTS = 3


def _params(*sem):
    return pltpu.CompilerParams(dimension_semantics=sem, vmem_limit_bytes=VMEM_LIMIT)


def _rms(x, w):
    return x * lax.rsqrt(jnp.mean(x * x, axis=-1, keepdims=True) + EPS) * w


def _dot(a, b):
    return jnp.dot(a, b, preferred_element_type=F32)


def _dot_nt(a, b):
    return lax.dot_general(a, b, (((1,), (1,)), ((), ())), preferred_element_type=F32)


def _lam(lq1, lk1, lq2, lk2, lam_init):
    return (jnp.exp(jnp.sum(lq1[...] * lk1[...], axis=-1, keepdims=True))
            - jnp.exp(jnp.sum(lq2[...] * lk2[...], axis=-1, keepdims=True)) + lam_init)


def _bucket_np(rel):
    n = np.maximum(rel, 0)
    nf = np.maximum(n, 1).astype(np.float32)
    large = REL_MAX_EXACT + (np.log(nf / np.float32(REL_MAX_EXACT))
                             / np.float32(math.log(REL_MAX_DIST / REL_MAX_EXACT))
                             * np.float32(REL_BUCKETS - REL_MAX_EXACT)).astype(np.int32)
    large = np.minimum(large, REL_BUCKETS - 1)
    return np.where(n < REL_MAX_EXACT, n, large).astype(np.int32)


def _bucket_starts():
    buckets = _bucket_np(np.arange(REL_MAX_DIST + 1))
    assert (np.diff(buckets) >= 0).all() and buckets[-1] == REL_BUCKETS - 1
    return [int(np.argmax(buckets >= k)) for k in range(REL_BUCKETS)]


def _rel_bias_tile(rel, rb_ref, h):
    far = rb_ref[REL_BUCKETS - 1, h]
    val = jnp.full(rel.shape, rb_ref[0, h] - far, F32)
    for k, start in enumerate(_bucket_starts()):
        if k > 0:
            val = jnp.where(rel >= start, rb_ref[k, h] - far, val)
    return val


def _softmax_update(chunks, m_prev, l_prev):
    mx = chunks[0]
    for c in chunks[1:]:
        mx = jnp.maximum(mx, c)
    m_next = jnp.maximum(m_prev, jnp.max(mx, axis=1, keepdims=True))
    ps = [jnp.exp(c - m_next) for c in chunks]
    sm = ps[0]
    for p in ps[1:]:
        sm = sm + p
    alpha = jnp.exp(m_prev - m_next)
    l_next = alpha * l_prev + jnp.sum(sm, axis=1, keepdims=True)
    return ps, m_next, l_next, alpha


def _head_out(acc1, l1, acc2, l2, lam, sw, lam_init):
    o = acc1 / l1 - lam * (acc2 / l2)
    return _rms(o, sw) * (1.0 - lam_init)


def _mix_in_prompt_kernel(x_ref, nw_ref, w_ref, cw_ref, cb_ref,
                          yconv_ref, q_ref, kf_ref, vf_ref, kb_ref, vb_ref, st_ref,
                          pre_scr, *, dc, da):
    j = pl.program_id(1)
    tm = x_ref.shape[0]
    h = _rms(x_ref[...], nw_ref[...]).astype(BF16)

    def proj(lo, width):
        return _dot(h, w_ref[:, lo:lo + width])

    @pl.when(j == 0)
    def _zero_prefix():
        pre_scr[0:SUBLANES, :] = jnp.zeros((SUBLANES, dc), F32)

    pre = proj(dc, dc) * proj(2 * dc, dc)
    pre_scr[SUBLANES:SUBLANES + tm, :] = pre
    cw = cw_ref[...]
    conv = (cb_ref[...] + cw[0:1, :] * pre_scr[SUBLANES - 2:SUBLANES - 2 + tm, :]
            + cw[1:2, :] * pre_scr[SUBLANES - 1:SUBLANES - 1 + tm, :] + cw[2:3, :] * pre)
    yconv_ref[...] = (proj(0, dc) * conv).astype(BF16)
    st_ref[...] = pre[tm - 2:tm, :]
    pre_scr[0:SUBLANES, :] = pre[tm - SUBLANES:tm, :]

    q = proj(3 * dc, da) * (HEAD_DK ** -0.5 * LOG2E)
    k = proj(3 * dc + da, da)
    v = proj(3 * dc + 2 * da, da)
    ones_rows = (lax.broadcasted_iota(jnp.int32, (ONES_ROWS, tm), 0) == 0).astype(BF16)
    for hd in range(N_HEADS):
        cols = slice(hd * HEAD_DV, (hd + 1) * HEAD_DV)
        kb_ref[hd] = k[:, cols].astype(BF16)
        kf_ref[pl.ds(hd, tm, stride=N_HEADS), :] = k[:, cols]
        vf_ref[pl.ds(hd, tm, stride=N_HEADS), :] = v[:, cols]
        q_ref[hd] = q[:, cols].T.astype(BF16)
        vb_ref[hd, 0:HEAD_DV, :] = v[:, cols].T.astype(BF16)
        vb_ref[hd, HEAD_DV:, :] = ones_rows


def _mix_in_prompt(x2, nw, w_in, cw, cb, *, batch, seq, dc, da):
    d = x2.shape[1]
    tm = min(ATTN_TILE, seq)
    nj = seq // tm
    rows = lambda b, j: (b * nj + j, 0)
    const = lambda b, j: (0, 0)
    tposed = lambda r: pl.BlockSpec((None, N_HEADS, None, r, tm), lambda b, j: (b, 0, j, 0, 0))
    tposed_shape = lambda r: jax.ShapeDtypeStruct((batch, N_HEADS, nj, r, tm), BF16)
    n = batch * seq
    return pl.pallas_call(
        functools.partial(_mix_in_prompt_kernel, dc=dc, da=da),
        grid=(batch, nj),
        in_specs=[pl.BlockSpec((tm, d), rows),
                  pl.BlockSpec((1, d), const),
                  pl.BlockSpec(w_in.shape, const),
                  pl.BlockSpec(cw.shape, const),
                  pl.BlockSpec((1, dc), const)],
        out_specs=[pl.BlockSpec((tm, dc), rows),
                   tposed(HEAD_DV),
                   pl.BlockSpec((tm * N_HEADS, HEAD_DV), rows),
                   pl.BlockSpec((tm * N_HEADS, HEAD_DV), rows),
                   pl.BlockSpec((None, N_HEADS, tm, HEAD_DV), lambda b, j: (b, 0, j, 0)),
                   tposed(HEAD_DV + ONES_ROWS),
                   pl.BlockSpec((None, CONV_W - 1, dc), lambda b, j: (b, 0, 0))],
        out_shape=[jax.ShapeDtypeStruct((n, dc), BF16),
                   tposed_shape(HEAD_DV),
                   jax.ShapeDtypeStruct((n * N_HEADS, HEAD_DV), F32),
                   jax.ShapeDtypeStruct((n * N_HEADS, HEAD_DV), F32),
                   jax.ShapeDtypeStruct((batch, N_HEADS, seq, HEAD_DV), BF16),
                   tposed_shape(HEAD_DV + ONES_ROWS),
                   jax.ShapeDtypeStruct((batch, CONV_W - 1, dc), F32)],
        scratch_shapes=[pltpu.VMEM((tm + SUBLANES, dc), F32)],
        compiler_params=_params("arbitrary", "arbitrary"),
    )(x2, nw, w_in, cw, cb)


def _conv_time_major(slabs, cw, cb):
    return [cb + cw[0:1, :] * slabs[t] + cw[1:2, :] * slabs[t + 1] + cw[2:3, :] * slabs[t + 2]
            for t in range(len(slabs) - 2)]


def _mix_in_sample_kernel(x_ref, nw_ref, w_ref, cw_ref, cb_ref, s0_ref, s1_ref,
                          yconv_ref, qm_ref, kf_ref, vf_ref, st_ref, *, dc, da, ts, bs):
    h = _rms(x_ref[...], nw_ref[...]).astype(BF16)

    def proj(lo, width):
        return _dot(h, w_ref[:, lo:lo + width])

    pre = proj(dc, dc) * proj(2 * dc, dc)
    gate = proj(0, dc)
    slabs = [s0_ref[...], s1_ref[...]] + [pre[t * bs:(t + 1) * bs, :] for t in range(ts)]
    conv = _conv_time_major(slabs, cw_ref[...], cb_ref[...])
    for t in range(ts):
        yconv_ref[t * bs:(t + 1) * bs, :] = (gate[t * bs:(t + 1) * bs, :] * conv[t]).astype(BF16)
    st_ref[0] = slabs[-2]
    st_ref[1] = slabs[-1]

    q = proj(3 * dc, da) * (HEAD_DK ** -0.5)
    lane = lax.broadcasted_iota(jnp.int32, q.shape, 1) % HEAD_DV
    qm_ref[0] = jnp.where(lane < HEAD_DK, q, 0.0).astype(BF16)
    qm_ref[1] = jnp.where(lane >= HEAD_DK, q, 0.0).astype(BF16)
    kf_ref[...] = proj(3 * dc + da, da)
    vf_ref[...] = proj(3 * dc + 2 * da, da)


def _mix_in_sample(xt, nw, w_in, cw, cb, s0, s1, *, dc, da, ts, bs):
    n = ts * bs
    return pl.pallas_call(
        functools.partial(_mix_in_sample_kernel, dc=dc, da=da, ts=ts, bs=bs),
        out_shape=[jax.ShapeDtypeStruct((n, dc), BF16),
                   jax.ShapeDtypeStruct((2, n, da), BF16),
                   jax.ShapeDtypeStruct((n, da), F32),
                   jax.ShapeDtypeStruct((n, da), F32),
                   jax.ShapeDtypeStruct((CONV_W - 1, bs, dc), F32)],
        compiler_params=pltpu.CompilerParams(vmem_limit_bytes=VMEM_LIMIT),
    )(xt, nw, w_in, cw, cb, s0, s1)


def _mem_kv_kernel(m_ref, nw_ref, w_ref, kf_ref, vf_ref, kb_ref, vb_ref, *, d):
    h = _rms(m_ref[...], nw_ref[...]).astype(BF16)
    k = _dot(h, w_ref[:, 0:d])
    kf_ref[...] = k
    kb_ref[...] = k.astype(BF16)
    v = _dot(h, w_ref[:, d:2 * d])
    vf_ref[...] = v
    vb_ref[...] = v.astype(BF16)


def _mem_kv(mem2, nw, w_ckv, *, batch, n_mem):
    d = mem2.shape[1]
    rows = lambda b: (b, 0)
    const = lambda b: (0, 0)
    n = batch * n_mem
    return pl.pallas_call(
        functools.partial(_mem_kv_kernel, d=d),
        grid=(batch,),
        in_specs=[pl.BlockSpec((n_mem, d), rows), pl.BlockSpec((1, d), const),
                  pl.BlockSpec(w_ckv.shape, const)],
        out_specs=[pl.BlockSpec((n_mem, d), rows)] * 4,
        out_shape=[jax.ShapeDtypeStruct((n, d), F32), jax.ShapeDtypeStruct((n, d), F32),
                   jax.ShapeDtypeStruct((n, d), BF16), jax.ShapeDtypeStruct((n, d), BF16)],
        compiler_params=_params("arbitrary"),
    )(mem2, nw, w_ckv)


def _attn_prompt_kernel(rb_ref, q_ref, k_ref, vt_ref, lq1, lk1, lq2, lk2, sw_ref, o_ref,
                        qp_scr, m_scr, acc_scr, bd_scr, bs_scr, sa_scr, sb_scr, *, T, lam_init):
    nb = T // LANES
    n_tiles = q_ref.shape[0]

    def _assemble_bias_tiles():
        h = pl.program_id(1)
        rel = (lax.broadcasted_iota(jnp.int32, (LANES, LANES), 1)
               - lax.broadcasted_iota(jnp.int32, (LANES, LANES), 0))
        d0 = jnp.where(rel >= 0, _rel_bias_tile(jnp.maximum(rel, 0), rb_ref, h) * LOG2E, NEG)
        d1 = _rel_bias_tile(rel + LANES, rb_ref, h) * LOG2E
        zero = jnp.zeros((LANES, LANES), F32)
        neg = jnp.full((LANES, LANES), NEG, F32)
        for bi in range(nb):
            for bj in range(nb):
                rs = slice(bi * LANES, (bi + 1) * LANES)
                cs = slice(bj * LANES, (bj + 1) * LANES)
                bd_scr[rs, cs] = d0 if bi == bj else d1 if bj == bi + 1 else zero if bj > bi else neg
                bs_scr[rs, cs] = d1 if (bj == 0 and bi == nb - 1) else zero

    _assemble_bias_tiles()

    def tile(qi, carry):
        qt = q_ref[qi].astype(F32)
        sub = lax.broadcasted_iota(jnp.int32, qt.shape, 0)
        qp_scr[:, 0:T] = jnp.where(sub < HEAD_DK, qt, 0.0).astype(BF16)
        qp_scr[:, T:2 * T] = jnp.where(sub >= HEAD_DK, qt, 0.0).astype(BF16)
        m_scr[...] = jnp.full(m_scr.shape, -jnp.inf, F32)
        acc_scr[...] = jnp.zeros(acc_scr.shape, F32)

        def scores(j, dst):
            dst[...] = _dot(k_ref[pl.ds(pl.multiple_of(j * T, T), T), :], qp_scr[...])

        def stage(j, src, bias_ref, j_next=None, dst=None):
            if j_next is not None:
                scores(j_next, dst)
            vt = vt_ref[j]
            for c in range(2 * T // QUERY_CHUNK):
                cs = slice(c * QUERY_CHUNK, (c + 1) * QUERY_CHUNK)
                s = src[:, cs]
                if bias_ref is not None:
                    lo = (c * QUERY_CHUNK) % T
                    s = s + bias_ref[:, lo:lo + QUERY_CHUNK]
                m_prev = m_scr[:, cs]
                m_next = jnp.maximum(m_prev, jnp.max(s, axis=0, keepdims=True))
                p = jnp.exp2(s - m_next).astype(BF16)
                acc_scr[:, cs] = jnp.exp2(m_prev - m_next) * acc_scr[:, cs] + _dot(vt, p)
                m_scr[:, cs] = m_next

        odd = qi % 2 == 0
        first = jnp.where(odd, 1, 0)

        @pl.when(odd)
        def _first_scores_odd():
            scores(0, sb_scr)

        @pl.when(jnp.logical_not(odd))
        def _first_scores_even():
            scores(0, sa_scr)

        @pl.when(odd & (qi >= 2))
        def _single_far_block():
            stage(0, sb_scr, None, 1, sa_scr)

        def far_pair(i, c):
            j = first + 2 * i
            stage(j, sa_scr, None, j + 1, sb_scr)
            stage(j + 1, sb_scr, None, j + 2, sa_scr)
            return c

        lax.fori_loop(0, jnp.maximum(qi - 1, 0) // 2, far_pair, 0)

        @pl.when(qi >= 1)
        def _sub_diagonal_and_diagonal():
            stage(qi - 1, sa_scr, bs_scr, qi, sb_scr)
            stage(qi, sb_scr, bd_scr)

        @pl.when(qi == 0)
        def _diagonal_only():
            stage(0, sb_scr, bd_scr)

        lam = _lam(lq1, lk1, lq2, lk2, lam_init)
        acc = acc_scr[0:HEAD_DV, :]
        l = acc_scr[HEAD_DV:HEAD_DV + 1, :]
        o = acc[:, 0:T] / l[:, 0:T] - lam * (acc[:, T:2 * T] / l[:, T:2 * T])
        y = o * lax.rsqrt(jnp.mean(o * o, axis=0, keepdims=True) + EPS) * sw_ref[...] * (1.0 - lam_init)
        o_ref[pl.ds(pl.multiple_of(qi * T, T), T), :] = y.T.astype(o_ref.dtype)
        return carry

    lax.fori_loop(0, n_tiles, tile, 0)


def _attn_prompt(rel_bias, qt, kb, vt, lq1, lk1, lq2, lk2, sw_col, *, batch, seq, lam_init):
    T = qt.shape[-1]
    nq = seq // T
    da = N_HEADS * HEAD_DV
    vec = lambda b, h: (0, 0)
    return pl.pallas_call(
        functools.partial(_attn_prompt_kernel, T=T, lam_init=lam_init),
        grid=(batch, N_HEADS),
        in_specs=[pl.BlockSpec(memory_space=pltpu.SMEM),
                  pl.BlockSpec((None, None, nq, HEAD_DV, T), lambda b, h: (b, h, 0, 0, 0)),
                  pl.BlockSpec((None, None, seq, HEAD_DV), lambda b, h: (b, h, 0, 0)),
                  pl.BlockSpec((None, None, nq, vt.shape[3], T), lambda b, h: (b, h, 0, 0, 0)),
                  pl.BlockSpec((1, HEAD_DK), vec), pl.BlockSpec((1, HEAD_DK), vec),
                  pl.BlockSpec((1, HEAD_DK), vec), pl.BlockSpec((1, HEAD_DK), vec),
                  pl.BlockSpec((HEAD_DV, 1), vec)],
        out_specs=pl.BlockSpec((seq, HEAD_DV), lambda b, h: (b, h)),
        out_shape=jax.ShapeDtypeStruct((batch * seq, da), BF16),
        scratch_shapes=[pltpu.VMEM((HEAD_DV, 2 * T), BF16),
                        pltpu.VMEM((1, 2 * T), F32),
                        pltpu.VMEM((vt.shape[3], 2 * T), F32),
                        pltpu.VMEM((T, T), F32),
                        pltpu.VMEM((T, T), F32),
                        pltpu.VMEM((T, 2 * T), F32),
                        pltpu.VMEM((T, 2 * T), F32)],
        compiler_params=_params("arbitrary", "arbitrary"),
    )(rel_bias, qt, kb, vt, lq1, lk1, lq2, lk2, sw_col)


def _attn_sample_kernel(pt_ref, rb_ref, q_ref, kn_ref, vn_ref, lq1, lk1, lq2, lk2, sw_ref, k_hbm, v_hbm,
                        o_ref, m_scr, l_scr, acc_scr, hm_scr, bl_scr, bn_scr, kbuf, vbuf, sems,
                        *, pages, page_offset, ts, lam_init):
    b = pl.program_id(0)
    nb = pl.num_programs(0)
    rows, pk = hm_scr.shape
    nk = bn_scr.shape[1]
    rm = rows // 2
    page = pk // N_HEADS
    groups = pt_ref.shape[1] // pages
    hbits = N_HEADS.bit_length() - 1
    tbits = ts.bit_length() - 1

    def group_copies(t, slot):
        bt = t // groups
        g0 = (t % groups) * pages
        out = []
        for i in range(pages):
            row0 = pl.multiple_of((page_offset + pt_ref[bt, g0 + i]) * pk, pk)
            dst = pl.ds(i * pk, pk)
            out.append(pltpu.make_async_copy(k_hbm.at[pl.ds(row0, pk)], kbuf.at[slot, dst], sems.at[0, slot]))
            out.append(pltpu.make_async_copy(v_hbm.at[pl.ds(row0, pk)], vbuf.at[slot, dst], sems.at[1, slot]))
        return out

    def start_group(t):
        @pl.when(t < nb * groups)
        def _():
            for cp in group_copies(t, t % PAGE_SLOTS):
                cp.start()

    @pl.when(b == 0)
    def _first_step():
        for t in range(PAGE_SLOTS - 1):
            start_group(jnp.int32(t))

        def tables(ncols, rel_of):
            r = lax.broadcasted_iota(jnp.int32, (rows, ncols), 0)
            c = lax.broadcasted_iota(jnp.int32, (rows, ncols), 1)
            rhead = (r >> tbits) & (N_HEADS - 1)
            same = rhead == (c & (N_HEADS - 1))
            rel = rel_of(r & (ts - 1), c >> hbits)
            bias = jnp.zeros((rows, ncols), F32)
            for h in range(N_HEADS):
                bias = jnp.where(rhead == h, _rel_bias_tile(jnp.maximum(rel, 0), rb_ref, h), bias)
            return same, rel, c >> hbits, bias

        same, _, _, bias = tables(pk, lambda tok, key: tok + page - key)
        hm_scr[...] = jnp.where(same, 0.0, NEG)
        bl_scr[...] = jnp.where(same, bias, NEG)
        same, rel, key, bias = tables(nk, lambda tok, key: tok - key)
        bn_scr[...] = jnp.where(same & (rel >= 0) & (key < ts), bias, NEG)

    m_scr[...] = jnp.full(m_scr.shape, -jnp.inf, F32)
    l_scr[...] = jnp.zeros(l_scr.shape, F32)
    acc_scr[...] = jnp.zeros(acc_scr.shape, F32)

    def update(scores, values):
        chunks, sizes = [], []
        for s in scores:
            n = s.shape[1] // LANES
            sizes.append(n)
            chunks += [s[:, c * LANES:(c + 1) * LANES] for c in range(n)]
        ps, m_next, l_next, alpha = _softmax_update(chunks, m_scr[...], l_scr[...])
        pv, at = None, 0
        for n, v in zip(sizes, values):
            p = jnp.concatenate([x.astype(BF16) for x in ps[at:at + n]], axis=1) if n > 1 \
                else ps[at].astype(BF16)
            at += n
            pv = _dot(p, v) if pv is None else pv + _dot(p, v)
        acc_scr[...] = alpha * acc_scr[...] + pv
        m_scr[...] = m_next
        l_scr[...] = l_next

    q = q_ref[...]

    def group_body(g, carry):
        t = b * groups + g
        slot = t % PAGE_SLOTS
        for cp in group_copies(t, slot):
            cp.wait()
        start_group(t + PAGE_SLOTS - 1)
        kg = kbuf.at[slot]
        vg = vbuf.at[slot]
        last = g == groups - 1
        scores = [_dot_nt(q, kg[i * pk:(i + 1) * pk, :].astype(BF16)) + hm_scr[...]
                  for i in range(pages - 1)]
        scores.append(_dot_nt(q, kg[(pages - 1) * pk:pages * pk, :].astype(BF16))
                      + jnp.where(last, bl_scr[...], hm_scr[...]))
        update(scores, [vg[i * pk:(i + 1) * pk, :].astype(BF16) for i in range(pages)])
        return carry

    lax.fori_loop(0, groups, group_body, 0)

    update([_dot_nt(q, kn_ref[...]) + bn_scr[...]], [vn_ref[...]])
    lam = _lam(lq1, lk1, lq2, lk2, lam_init)
    acc = acc_scr[...]
    l = l_scr[...]
    o_ref[...] = _head_out(acc[0:rm], l[0:rm], acc[rm:rows], l[rm:rows], lam, sw_ref[...], lam_init)


def _attn_sample(page_table, rel_bias, q_all, knew, vnew, lq1, lk1, lq2, lk2, sw, cache_k, cache_v,
                 *, page, page_offset, ts, lam_init):
    bs, n_pages = page_table.shape
    rows = q_all.shape[1]
    nk = knew.shape[1]
    pk = page * N_HEADS
    pages = math.gcd(PAGES_PER_GROUP, n_pages)
    vec = lambda b, pt: (0, 0)
    per_b = lambda b, pt: (b, 0, 0)
    grid_spec = pltpu.PrefetchScalarGridSpec(
        num_scalar_prefetch=1,
        grid=(bs,),
        in_specs=[pl.BlockSpec(memory_space=pltpu.SMEM),
                  pl.BlockSpec((None, rows, HEAD_DV), per_b),
                  pl.BlockSpec((None, nk, HEAD_DV), per_b),
                  pl.BlockSpec((None, nk, HEAD_DV), per_b),
                  pl.BlockSpec((1, HEAD_DK), vec), pl.BlockSpec((1, HEAD_DK), vec),
                  pl.BlockSpec((1, HEAD_DK), vec), pl.BlockSpec((1, HEAD_DK), vec),
                  pl.BlockSpec((1, HEAD_DV), vec),
                  pl.BlockSpec(memory_space=pl.ANY), pl.BlockSpec(memory_space=pl.ANY)],
        out_specs=pl.BlockSpec((None, rows // 2, HEAD_DV), per_b),
        scratch_shapes=[pltpu.VMEM((rows, LANES), F32),
                        pltpu.VMEM((rows, LANES), F32),
                        pltpu.VMEM((rows, HEAD_DV), F32),
                        pltpu.VMEM((rows, pk), F32),
                        pltpu.VMEM((rows, pk), F32),
                        pltpu.VMEM((rows, nk), F32),
                        pltpu.VMEM((PAGE_SLOTS, pages * pk, HEAD_DV), F32),
                        pltpu.VMEM((PAGE_SLOTS, pages * pk, HEAD_DV), F32),
                        pltpu.SemaphoreType.DMA((2, PAGE_SLOTS))],
    )
    return pl.pallas_call(
        functools.partial(_attn_sample_kernel, pages=pages, page_offset=page_offset, ts=ts,
                          lam_init=lam_init),
        grid_spec=grid_spec,
        out_shape=jax.ShapeDtypeStruct((bs, rows // 2, HEAD_DV), F32),
        compiler_params=_params("arbitrary"),
    )(page_table, rel_bias, q_all, knew, vnew, lq1, lk1, lq2, lk2, sw, cache_k, cache_v)


def _cross_heads(qc, mk_ref, mv_ref, o_scr, dh):
    for hd in range(MEM_HEADS):
        cols = slice(hd * dh, (hd + 1) * dh)
        s = _dot_nt(qc[:, cols], mk_ref[:, cols].astype(BF16))
        p = jnp.exp(s - jnp.max(s, axis=1, keepdims=True))
        o = _dot(p.astype(BF16), mv_ref[:, cols].astype(BF16)) / jnp.sum(p, axis=1, keepdims=True)
        o_scr[:, cols] = o.astype(BF16)


def _cross_prompt_kernel(x_ref, yc_ref, ya_ref, wo_ref, nw_ref, wq_ref, mk_ref, mv_ref, wc_ref,
                         o_ref, o_scr, *, dc, dh):
    x1 = x_ref[...] + _dot(yc_ref[...], wo_ref[0:dc, :]) + _dot(ya_ref[...], wo_ref[dc:, :])
    h = _rms(x1, nw_ref[...]).astype(BF16)
    qc = (_dot(h, wq_ref[...]) * (dh ** -0.5)).astype(BF16)
    _cross_heads(qc, mk_ref, mv_ref, o_scr, dh)
    o_ref[...] = x1 + _dot(o_scr[...], wc_ref[...])


def _cross_prompt(x2, yconv, yattn, w_out, nw, w_cq, mk, mv, w_co, *, batch, seq, n_mem):
    d = x2.shape[1]
    dc = yconv.shape[1]
    da = yattn.shape[1]
    tm = min(ROW_TILE, seq)
    nj = seq // tm
    rows = lambda b, j: (b * nj + j, 0)
    const = lambda b, j: (0, 0)
    memb = lambda b, j: (b, 0)
    return pl.pallas_call(
        functools.partial(_cross_prompt_kernel, dc=dc, dh=d // MEM_HEADS),
        grid=(batch, nj),
        in_specs=[pl.BlockSpec((tm, d), rows), pl.BlockSpec((tm, dc), rows), pl.BlockSpec((tm, da), rows),
                  pl.BlockSpec(w_out.shape, const), pl.BlockSpec((1, d), const),
                  pl.BlockSpec(w_cq.shape, const),
                  pl.BlockSpec((n_mem, d), memb), pl.BlockSpec((n_mem, d), memb),
                  pl.BlockSpec(w_co.shape, const)],
        out_specs=pl.BlockSpec((tm, d), rows),
        out_shape=jax.ShapeDtypeStruct(x2.shape, F32),
        scratch_shapes=[pltpu.VMEM((tm, d), BF16)],
        compiler_params=_params("arbitrary", "arbitrary"),
    )(x2, yconv, yattn, w_out, nw, w_cq, mk, mv, w_co)


def _outproj_q_sample_kernel(x_ref, yc_ref, ya_ref, wo_ref, nw_ref, wq_ref, x1_ref, qc_ref, *, dc, dh):
    x1 = x_ref[...] + _dot(yc_ref[...], wo_ref[0:dc, :]) + _dot(ya_ref[...], wo_ref[dc:, :])
    x1_ref[...] = x1
    h = _rms(x1, nw_ref[...]).astype(BF16)
    qc_ref[...] = (_dot(h, wq_ref[...]) * (dh ** -0.5)).astype(BF16)


def _outproj_q_sample(xt, yconv, yattn, w_out, nw, w_cq):
    d = xt.shape[1]
    return pl.pallas_call(
        functools.partial(_outproj_q_sample_kernel, dc=yconv.shape[1], dh=d // MEM_HEADS),
        out_shape=[jax.ShapeDtypeStruct(xt.shape, F32), jax.ShapeDtypeStruct(xt.shape, BF16)],
        compiler_params=pltpu.CompilerParams(vmem_limit_bytes=VMEM_LIMIT),
    )(xt, yconv, yattn, w_out, nw, w_cq)


def _cross_sample_kernel(q_ref, mk_ref, mv_ref, o_ref, *, ts):
    rq = q_ref.shape[0]
    ncols = mk_ref.shape[0]
    per_key = (rq // (MEM_HEADS * ts)) * MEM_HEADS
    nh = per_key // MEM_HEADS
    rows = MEM_HEADS * ts
    s2 = _dot_nt(q_ref[...], mk_ref[...].astype(BF16))
    s = s2[0:rows]
    for half in range(1, nh):
        s = s + pltpu.roll(s2[half * rows:(half + 1) * rows], ncols - half * MEM_HEADS, axis=1)
    col = lax.broadcasted_iota(jnp.int32, (rows, ncols), 1) & (per_key - 1)
    head = lax.broadcasted_iota(jnp.int32, (rows, ncols), 0) >> (ts.bit_length() - 1)
    s = jnp.where(col == head, s, NEG)
    p = jnp.exp(s - jnp.max(s, axis=1, keepdims=True))
    l = jnp.sum(p, axis=1, keepdims=True)
    lhs = jnp.concatenate([p] + [pltpu.roll(p, half * MEM_HEADS, axis=1) for half in range(1, nh)], axis=0)
    o2 = _dot(lhs.astype(BF16), mv_ref[...].astype(BF16))
    o = jnp.concatenate([o2[half * rows:(half + 1) * rows] for half in range(nh)], axis=1) / l
    o_ref[...] = o.astype(o_ref.dtype)


def _cross_sample(q2, mk, mv, *, row_offset, ts):
    bs, rq, _ = q2.shape
    ncols = mk.shape[1]
    rows = MEM_HEADS * ts
    nh = rq // rows
    per_b = lambda b: (b, 0, 0)
    mem_b = lambda b: (row_offset + b, 0, 0)
    return pl.pallas_call(
        functools.partial(_cross_sample_kernel, ts=ts),
        grid=(bs,),
        in_specs=[pl.BlockSpec((None, rq, LANES), per_b), pl.BlockSpec((None, ncols, LANES), mem_b),
                  pl.BlockSpec((None, ncols, LANES), mem_b)],
        out_specs=pl.BlockSpec((None, rows, nh * LANES), per_b),
        out_shape=jax.ShapeDtypeStruct((bs, rows, nh * LANES), BF16),
        compiler_params=_params("arbitrary"),
    )(q2, mk, mv)


def _silu(g):
    return g * (1.0 / (1.0 + jnp.exp(-g)))


def _ffn_prompt_kernel(x_ref, nw_ref, wu_ref, cw_ref, cb_ref, wd_ref, fw_ref, y_ref, st_ref,
                       up_scr, carry_scr, *, dff, final):
    j = pl.program_id(1)
    tm = x_ref.shape[0]
    x = x_ref[...]
    h = _rms(x, nw_ref[...]).astype(BF16)

    @pl.when(j == 0)
    def _zero_prefix():
        carry_scr[...] = jnp.zeros(carry_scr.shape, F32)

    def conv_half(lo):
        cols = slice(lo, lo + dff)
        up = _dot(h, wu_ref[:, cols])
        up_scr[0:SUBLANES, :] = carry_scr[:, cols]
        up_scr[SUBLANES:SUBLANES + tm, :] = up
        cw = cw_ref[:, cols]
        conv = (cb_ref[:, cols] + cw[0:1, :] * up_scr[SUBLANES - 2:SUBLANES - 2 + tm, :]
                + cw[1:2, :] * up_scr[SUBLANES - 1:SUBLANES - 1 + tm, :] + cw[2:3, :] * up)
        carry_scr[:, cols] = up[tm - SUBLANES:tm, :]
        st_ref[:, cols] = up[tm - 2:tm, :]
        return conv

    g = conv_half(0)
    u = conv_half(dff)
    x3 = x + _dot((_silu(g) * u).astype(BF16), wd_ref[...])
    y_ref[...] = _rms(x3, fw_ref[...]) if final else x3


def _ffn_prompt(x2, nw, w_up, cw, cb, w_down, fw, *, batch, seq, final):
    d = x2.shape[1]
    dff = w_down.shape[0]
    tm = min(FFN_ROW_TILE, seq)
    nj = seq // tm
    rows = lambda b, j: (b * nj + j, 0)
    const = lambda b, j: (0, 0)
    resident = lambda shape: pl.BlockSpec(shape, const, pipeline_mode=pl.Buffered(1))
    return pl.pallas_call(
        functools.partial(_ffn_prompt_kernel, dff=dff, final=final),
        grid=(batch, nj),
        in_specs=[pl.BlockSpec((tm, d), rows), pl.BlockSpec((1, d), const),
                  resident(w_up.shape), pl.BlockSpec(cw.shape, const),
                  pl.BlockSpec((1, 2 * dff), const), resident(w_down.shape),
                  pl.BlockSpec((1, d), const)],
        out_specs=[pl.BlockSpec((tm, d), rows),
                   pl.BlockSpec((None, CONV_W - 1, 2 * dff), lambda b, j: (b, 0, 0))],
        out_shape=[jax.ShapeDtypeStruct(x2.shape, F32),
                   jax.ShapeDtypeStruct((batch, CONV_W - 1, 2 * dff), F32)],
        scratch_shapes=[pltpu.VMEM((tm + SUBLANES, dff), F32),
                        pltpu.VMEM((SUBLANES, 2 * dff), F32)],
        compiler_params=_params("arbitrary", "arbitrary"),
    )(x2, nw, w_up, cw, cb, w_down, fw)


def _ffn_sample_kernel(x1_ref, o_ref, wc_ref, nw_ref, wu_ref, cw_ref, cb_ref, wd_ref, fw_ref,
                       s0_ref, s1_ref, y_ref, st_ref, hid_scr, *, dff, ts, bs, final):
    x2 = x1_ref[...] + _dot(o_ref[...], wc_ref[...])
    h = _rms(x2, nw_ref[...]).astype(BF16)

    def conv_half(lo):
        cols = slice(lo, lo + dff)
        up = _dot(h, wu_ref[:, cols])
        slabs = [s0_ref[:, cols], s1_ref[:, cols]] + [up[t * bs:(t + 1) * bs, :] for t in range(ts)]
        st_ref[0, :, cols] = slabs[-2]
        st_ref[1, :, cols] = slabs[-1]
        return _conv_time_major(slabs, cw_ref[:, cols], cb_ref[:, cols])

    g = conv_half(0)
    u = conv_half(dff)
    for t in range(ts):
        hid_scr[t * bs:(t + 1) * bs, :] = (_silu(g[t]) * u[t]).astype(BF16)
    x3 = x2 + _dot(hid_scr[...], wd_ref[...])
    y_ref[...] = _rms(x3, fw_ref[...]) if final else x3


def _ffn_sample(x1, o, w_co, nw, w_up, cw, cb, w_down, fw, s0, s1, *, ts, bs, final):
    dff = w_down.shape[0]
    return pl.pallas_call(
        functools.partial(_ffn_sample_kernel, dff=dff, ts=ts, bs=bs, final=final),
        out_shape=[jax.ShapeDtypeStruct(x1.shape, F32),
                   jax.ShapeDtypeStruct((CONV_W - 1, bs, 2 * dff), F32)],
        scratch_shapes=[pltpu.VMEM((ts * bs, dff), BF16)],
        compiler_params=pltpu.CompilerParams(vmem_limit_bytes=VMEM_LIMIT),
    )(x1, o, w_co, nw, w_up, cw, cb, w_down, fw, s0, s1)


def kernel(x_prompt, x_sample, mem_prompt, cache_k, cache_v, page_table, state_conv_mix, state_conv_ffn, cache_mem_k, cache_mem_v, rel_bias, norm_mix_w, w_in, conv_mix_w, conv_mix_b, lambda_q1, lambda_k1, lambda_q2, lambda_k2, subln_w, w_out, norm_cross_w, norm_mem_w, w_cq, w_ckv, w_co, norm_ffn_w, w_up, conv_ffn_w, conv_ffn_b, w_down, norm_final_w):
    depth = w_in.shape[0]
    bp, sp, d = x_prompt.shape
    bs, ts, _ = x_sample.shape
    n_mem = mem_prompt.shape[1]
    n_phys, page = cache_k.shape[1], cache_k.shape[2]
    n_pages = page_table.shape[1]
    past_len = n_pages * page
    dc = conv_mix_w.shape[2]
    da = N_HEADS * HEAD_DV
    dff = w_down.shape[1]
    dh = d // MEM_HEADS
    assert page == LANES and CONV_W - 1 <= ts <= SUBLANES and ts & (ts - 1) == 0
    assert (N_HEADS * ts) % SUBLANES == 0 and cache_k.shape[3:] == (N_HEADS, HEAD_DV)
    rows = 2 * SUBLANES

    assert _bucket_np(np.arange(REL_MAX_DIST, max(sp, past_len + ts) + 1)).min() == REL_BUCKETS - 1
    row = lambda a: a.reshape(1, -1).astype(F32)
    rel_bias = rel_bias.astype(F32)
    cache_k2 = cache_k.reshape(depth * n_phys * page * N_HEADS, HEAD_DV)
    cache_v2 = cache_v.reshape(depth * n_phys * page * N_HEADS, HEAD_DV)
    mem_rows = lambda a: a.reshape(depth * bs, n_mem, MEM_HEADS, dh // LANES, LANES).transpose(
        0, 1, 3, 2, 4).reshape(depth * bs, n_mem * (dh // LANES) * MEM_HEADS, LANES)
    mem_k3 = mem_rows(cache_mem_k)
    mem_v3 = mem_rows(cache_mem_v)

    xp = x_prompt.reshape(bp * sp, d)
    xs = x_sample.transpose(1, 0, 2).reshape(ts * bs, d)
    mem2 = mem_prompt.reshape(bp * n_mem, d)
    outs = [[] for _ in range(10)]
    for l in range(depth):
        lam_init = 0.8 - 0.6 * math.exp(-0.3 * l)
        final = l == depth - 1
        w_in_b, w_out_b = w_in[l].astype(BF16), w_out[l].astype(BF16)
        w_cq_b, w_ckv_b, w_co_b = w_cq[l].astype(BF16), w_ckv[l].astype(BF16), w_co[l].astype(BF16)
        w_up_b, w_down_b = w_up[l].astype(BF16), w_down[l].astype(BF16)
        lam_args = (row(lambda_q1[l]), row(lambda_k1[l]), row(lambda_q2[l]), row(lambda_k2[l]),
                    row(subln_w[l]))

        yconv_p, q_p, kf_p, vf_p, kb_p, vb_p, cmix_p = _mix_in_prompt(
            xp, row(norm_mix_w[l]), w_in_b, conv_mix_w[l], row(conv_mix_b[l]),
            batch=bp, seq=sp, dc=dc, da=da)
        yattn_p = _attn_prompt(rel_bias, q_p, kb_p, vb_p, *lam_args[:4], subln_w[l].reshape(-1, 1).astype(F32),
                               batch=bp, seq=sp, lam_init=lam_init)
        mkf, mvf, mkb, mvb = _mem_kv(mem2, row(norm_mem_w[l]), w_ckv_b, batch=bp, n_mem=n_mem)
        x2_p = _cross_prompt(xp, yconv_p, yattn_p, w_out_b, row(norm_cross_w[l]), w_cq_b, mkb, mvb,
                             w_co_b, batch=bp, seq=sp, n_mem=n_mem)
        xp, cffn_p = _ffn_prompt(x2_p, row(norm_ffn_w[l]), w_up_b, conv_ffn_w[l], row(conv_ffn_b[l]),
                                 w_down_b, row(norm_final_w), batch=bp, seq=sp, final=final)

        yconv_s, qm_s, kf_s, vf_s, cmix_s = _mix_in_sample(
            xs, row(norm_mix_w[l]), w_in_b, conv_mix_w[l], row(conv_mix_b[l]),
            state_conv_mix[l][:, 0], state_conv_mix[l][:, 1], dc=dc, da=da, ts=ts, bs=bs)
        q_all = qm_s.reshape(2, ts, bs, N_HEADS, HEAD_DV).transpose(2, 0, 3, 1, 4)
        q_all = q_all.reshape(bs, 2 * N_HEADS * ts, HEAD_DV)
        new_rows = lambda a: jnp.pad(
            a.reshape(ts, bs, N_HEADS * HEAD_DV).transpose(1, 0, 2).reshape(bs, ts * N_HEADS, HEAD_DV),
            ((0, 0), (0, LANES - ts * N_HEADS), (0, 0))).astype(BF16)
        yattn_s = _attn_sample(page_table, rel_bias, q_all, new_rows(kf_s), new_rows(vf_s), *lam_args,
                               cache_k2, cache_v2, page=page, page_offset=l * n_phys, ts=ts,
                               lam_init=lam_init)
        yattn_s = yattn_s.reshape(bs, N_HEADS, ts, HEAD_DV).transpose(2, 0, 1, 3)
        yattn_s = yattn_s.reshape(ts * bs, da).astype(BF16)
        x1_s, qc_s = _outproj_q_sample(xs, yconv_s, yattn_s, w_out_b, row(norm_cross_w[l]), w_cq_b)
        q2 = qc_s.reshape(ts, bs, MEM_HEADS, dh // LANES, LANES).transpose(1, 3, 2, 0, 4)
        q2 = q2.reshape(bs, (dh // LANES) * MEM_HEADS * ts, LANES)
        o_b = _cross_sample(q2, mem_k3, mem_v3, row_offset=l * bs, ts=ts)
        o_s = o_b.reshape(bs, MEM_HEADS, ts, dh).transpose(2, 0, 1, 3).reshape(ts * bs, d)
        xs, cffn_s = _ffn_sample(x1_s, o_s, w_co_b, row(norm_ffn_w[l]), w_up_b, conv_ffn_w[l],
                                 row(conv_ffn_b[l]), w_down_b, row(norm_final_w),
                                 state_conv_ffn[l][:, 0], state_conv_ffn[l][:, 1],
                                 ts=ts, bs=bs, final=final)

        t2b = lambda a: a.reshape(ts, bs, N_HEADS, HEAD_DV).transpose(1, 0, 2, 3)
        for lst, val in zip(outs, (
                kf_p.reshape(bp, sp, N_HEADS, 2 * HEAD_DK), vf_p.reshape(bp, sp, N_HEADS, HEAD_DV),
                t2b(kf_s), t2b(vf_s), cmix_p, cmix_s.transpose(1, 0, 2), cffn_p,
                cffn_s.transpose(1, 0, 2), mkf.reshape(bp, n_mem, MEM_HEADS, dh),
                mvf.reshape(bp, n_mem, MEM_HEADS, dh))):
            lst.append(val)

    y_prompt = xp.reshape(bp, sp, d)
    y_sample = xs.reshape(ts, bs, d).transpose(1, 0, 2)
    return (y_prompt, y_sample) + tuple(jnp.stack(o) for o in outs)
```

```python
import functools
import math

import numpy as np
import jax
import jax.numpy as jnp
from jax import lax
from jax.experimental import pallas as pl
from jax.experimental.pallas import tpu as pltpu

F32 = jnp.float32
BF16 = jnp.bfloat16

EPS = 1e-6
NEG = -1e30
LANES = 128
SUBLANES = 8
N_HEADS = 4
HEAD_DK = 64
HEAD_DV = 2 * HEAD_DK
MEM_HEADS = 4
CONV_W = 3
REL_BUCKETS = 32
REL_MAX_EXACT = 16
REL_MAX_DIST = 128
VMEM_LIMIT = 56 * 1024 * 1024

ATTN_TILE = 512
QUERY_CHUNK = 256
ONES_ROWS = 16
LOG2E = 1.4426950408889634
ROW_TILE = 512
FFN_ROW_TILE = 512
PAGES_PER_GROUP = 8
PAGE_SLOTS = 3


def _params(*sem):
    return pltpu.CompilerParams(dimension_semantics=sem, vmem_limit_bytes=VMEM_LIMIT)


def _rms(x, w):
    return x * lax.rsqrt(jnp.mean(x * x, axis=-1, keepdims=True) + EPS) * w


def _dot(a, b):
    return jnp.dot(a, b, preferred_element_type=F32)


def _dot_nt(a, b):
    return lax.dot_general(a, b, (((1,), (1,)), ((), ())), preferred_element_type=F32)


def _lam(lq1, lk1, lq2, lk2, lam_init):
    return (jnp.exp(jnp.sum(lq1[...] * lk1[...], axis=-1, keepdims=True))
            - jnp.exp(jnp.sum(lq2[...] * lk2[...], axis=-1, keepdims=True)) + lam_init)


def _bucket_np(rel):
    n = np.maximum(rel, 0)
    nf = np.maximum(n, 1).astype(np.float32)
    large = REL_MAX_EXACT + (np.log(nf / np.float32(REL_MAX_EXACT))
                             / np.float32(math.log(REL_MAX_DIST / REL_MAX_EXACT))
                             * np.float32(REL_BUCKETS - REL_MAX_EXACT)).astype(np.int32)
    large = np.minimum(large, REL_BUCKETS - 1)
    return np.where(n < REL_MAX_EXACT, n, large).astype(np.int32)


def _bucket_starts():
    buckets = _bucket_np(np.arange(REL_MAX_DIST + 1))
    assert (np.diff(buckets) >= 0).all() and buckets[-1] == REL_BUCKETS - 1
    return [int(np.argmax(buckets >= k)) for k in range(REL_BUCKETS)]


def _rel_bias_tile(rel, rb_ref, h):
    far = rb_ref[REL_BUCKETS - 1, h]
    val = jnp.full(rel.shape, rb_ref[0, h] - far, F32)
    for k, start in enumerate(_bucket_starts()):
        if k > 0:
            val = jnp.where(rel >= start, rb_ref[k, h] - far, val)
    return val


def _softmax_update(chunks, m_prev, l_prev):
    mx = chunks[0]
    for c in chunks[1:]:
        mx = jnp.maximum(mx, c)
    m_next = jnp.maximum(m_prev, jnp.max(mx, axis=1, keepdims=True))
    ps = [jnp.exp(c - m_next) for c in chunks]
    sm = ps[0]
    for p in ps[1:]:
        sm = sm + p
    alpha = jnp.exp(m_prev - m_next)
    l_next = alpha * l_prev + jnp.sum(sm, axis=1, keepdims=True)
    return ps, m_next, l_next, alpha


def _head_out(acc1, l1, acc2, l2, lam, sw, lam_init):
    o = acc1 / l1 - lam * (acc2 / l2)
    return _rms(o, sw) * (1.0 - lam_init)


def _mix_in_prompt_kernel(x_ref, nw_ref, w_ref, cw_ref, cb_ref,
                          yconv_ref, q_ref, kf_ref, vf_ref, kb_ref, vb_ref, st_ref,
                          pre_scr, *, dc, da):
    j = pl.program_id(1)
    tm = x_ref.shape[0]
    h = _rms(x_ref[...], nw_ref[...]).astype(BF16)

    def proj(lo, width):
        return _dot(h, w_ref[:, lo:lo + width])

    @pl.when(j == 0)
    def _zero_prefix():
        pre_scr[0:SUBLANES, :] = jnp.zeros((SUBLANES, dc), F32)

    pre = proj(dc, dc) * proj(2 * dc, dc)
    pre_scr[SUBLANES:SUBLANES + tm, :] = pre
    cw = cw_ref[...]
    conv = (cb_ref[...] + cw[0:1, :] * pre_scr[SUBLANES - 2:SUBLANES - 2 + tm, :]
            + cw[1:2, :] * pre_scr[SUBLANES - 1:SUBLANES - 1 + tm, :] + cw[2:3, :] * pre)
    yconv_ref[...] = (proj(0, dc) * conv).astype(BF16)
    st_ref[...] = pre[tm - 2:tm, :]
    pre_scr[0:SUBLANES, :] = pre[tm - SUBLANES:tm, :]

    q = proj(3 * dc, da) * (HEAD_DK ** -0.5 * LOG2E)
    k = proj(3 * dc + da, da)
    v = proj(3 * dc + 2 * da, da)
    ones_rows = (lax.broadcasted_iota(jnp.int32, (ONES_ROWS, tm), 0) == 0).astype(BF16)
    for hd in range(N_HEADS):
        cols = slice(hd * HEAD_DV, (hd + 1) * HEAD_DV)
        kb_ref[hd] = k[:, cols].astype(BF16)
        kf_ref[pl.ds(hd, tm, stride=N_HEADS), :] = k[:, cols]
        vf_ref[pl.ds(hd, tm, stride=N_HEADS), :] = v[:, cols]
        q_ref[hd] = q[:, cols].T.astype(BF16)
        vb_ref[hd, 0:HEAD_DV, :] = v[:, cols].T.astype(BF16)
        vb_ref[hd, HEAD_DV:, :] = ones_rows


def _mix_in_prompt(x2, nw, w_in, cw, cb, *, batch, seq, dc, da):
    d = x2.shape[1]
    tm = min(ATTN_TILE, seq)
    nj = seq // tm
    rows = lambda b, j: (b * nj + j, 0)
    const = lambda b, j: (0, 0)
    tposed = lambda r: pl.BlockSpec((None, N_HEADS, None, r, tm), lambda b, j: (b, 0, j, 0, 0))
    tposed_shape = lambda r: jax.ShapeDtypeStruct((batch, N_HEADS, nj, r, tm), BF16)
    n = batch * seq
    return pl.pallas_call(
        functools.partial(_mix_in_prompt_kernel, dc=dc, da=da),
        grid=(batch, nj),
        in_specs=[pl.BlockSpec((tm, d), rows),
                  pl.BlockSpec((1, d), const),
                  pl.BlockSpec(w_in.shape, const),
                  pl.BlockSpec(cw.shape, const),
                  pl.BlockSpec((1, dc), const)],
        out_specs=[pl.BlockSpec((tm, dc), rows),
                   tposed(HEAD_DV),
                   pl.BlockSpec((tm * N_HEADS, HEAD_DV), rows),
                   pl.BlockSpec((tm * N_HEADS, HEAD_DV), rows),
                   pl.BlockSpec((None, N_HEADS, tm, HEAD_DV), lambda b, j: (b, 0, j, 0)),
                   tposed(HEAD_DV + ONES_ROWS),
                   pl.BlockSpec((None, CONV_W - 1, dc), lambda b, j: (b, 0, 0))],
        out_shape=[jax.ShapeDtypeStruct((n, dc), BF16),
                   tposed_shape(HEAD_DV),
                   jax.ShapeDtypeStruct((n * N_HEADS, HEAD_DV), F32),
                   jax.ShapeDtypeStruct((n * N_HEADS, HEAD_DV), F32),
                   jax.ShapeDtypeStruct((batch, N_HEADS, seq, HEAD_DV), BF16),
                   tposed_shape(HEAD_DV + ONES_ROWS),
                   jax.ShapeDtypeStruct((batch, CONV_W - 1, dc), F32)],
        scratch_shapes=[pltpu.VMEM((tm + SUBLANES, dc), F32)],
        compiler_params=_params("arbitrary", "arbitrary"),
    )(x2, nw, w_in, cw, cb)


def _conv_time_major(slabs, cw, cb):
    return [cb + cw[0:1, :] * slabs[t] + cw[1:2, :] * slabs[t + 1] + cw[2:3, :] * slabs[t + 2]
            for t in range(len(slabs) - 2)]


def _mix_in_sample_kernel(x_ref, nw_ref, w_ref, cw_ref, cb_ref, s0_ref, s1_ref,
                          yconv_ref, qm_ref, kf_ref, vf_ref, st_ref, *, dc, da, ts, bs):
    h = _rms(x_ref[...], nw_ref[...]).astype(BF16)

    def proj(lo, width):
        return _dot(h, w_ref[:, lo:lo + width])

    pre = proj(dc, dc) * proj(2 * dc, dc)
    gate = proj(0, dc)
    slabs = [s0_ref[...], s1_ref[...]] + [pre[t * bs:(t + 1) * bs, :] for t in range(ts)]
    conv = _conv_time_major(slabs, cw_ref[...], cb_ref[...])
    for t in range(ts):
        yconv_ref[t * bs:(t + 1) * bs, :] = (gate[t * bs:(t + 1) * bs, :] * conv[t]).astype(BF16)
    st_ref[0] = slabs[-2]
    st_ref[1] = slabs[-1]

    q = proj(3 * dc, da) * (HEAD_DK ** -0.5)
    lane = lax.broadcasted_iota(jnp.int32, q.shape, 1) % HEAD_DV
    qm_ref[0] = jnp.where(lane < HEAD_DK, q, 0.0).astype(BF16)
    qm_ref[1] = jnp.where(lane >= HEAD_DK, q, 0.0).astype(BF16)
    kf_ref[...] = proj(3 * dc + da, da)
    vf_ref[...] = proj(3 * dc + 2 * da, da)


def _mix_in_sample(xt, nw, w_in, cw, cb, s0, s1, *, dc, da, ts, bs):
    n = ts * bs
    return pl.pallas_call(
        functools.partial(_mix_in_sample_kernel, dc=dc, da=da, ts=ts, bs=bs),
        out_shape=[jax.ShapeDtypeStruct((n, dc), BF16),
                   jax.ShapeDtypeStruct((2, n, da), BF16),
                   jax.ShapeDtypeStruct((n, da), F32),
                   jax.ShapeDtypeStruct((n, da), F32),
                   jax.ShapeDtypeStruct((CONV_W - 1, bs, dc), F32)],
        compiler_params=pltpu.CompilerParams(vmem_limit_bytes=VMEM_LIMIT),
    )(xt, nw, w_in, cw, cb, s0, s1)


def _mem_kv_kernel(m_ref, nw_ref, w_ref, kf_ref, vf_ref, kb_ref, vb_ref, *, d):
    h = _rms(m_ref[...], nw_ref[...]).astype(BF16)
    k = _dot(h, w_ref[:, 0:d])
    kf_ref[...] = k
    kb_ref[...] = k.astype(BF16)
    v = _dot(h, w_ref[:, d:2 * d])
    vf_ref[...] = v
    vb_ref[...] = v.astype(BF16)


def _mem_kv(mem2, nw, w_ckv, *, batch, n_mem):
    d = mem2.shape[1]
    rows = lambda b: (b, 0)
    const = lambda b: (0, 0)
    n = batch * n_mem
    return pl.pallas_call(
        functools.partial(_mem_kv_kernel, d=d),
        grid=(batch,),
        in_specs=[pl.BlockSpec((n_mem, d), rows), pl.BlockSpec((1, d), const),
                  pl.BlockSpec(w_ckv.shape, const)],
        out_specs=[pl.BlockSpec((n_mem, d), rows)] * 4,
        out_shape=[jax.ShapeDtypeStruct((n, d), F32), jax.ShapeDtypeStruct((n, d), F32),
                   jax.ShapeDtypeStruct((n, d), BF16), jax.ShapeDtypeStruct((n, d), BF16)],
        compiler_params=_params("arbitrary"),
    )(mem2, nw, w_ckv)


def _attn_prompt_kernel(rb_ref, q_ref, k_ref, vt_ref, lq1, lk1, lq2, lk2, sw_ref, o_ref,
                        qp_scr, m_scr, acc_scr, bd_scr, bs_scr, sa_scr, sb_scr, *, T, lam_init):
    nb = T // LANES
    n_tiles = q_ref.shape[0]

    def _assemble_bias_tiles():
        h = pl.program_id(1)
        rel = (lax.broadcasted_iota(jnp.int32, (LANES, LANES), 1)
               - lax.broadcasted_iota(jnp.int32, (LANES, LANES), 0))
        d0 = jnp.where(rel >= 0, _rel_bias_tile(jnp.maximum(rel, 0), rb_ref, h) * LOG2E, NEG)
        d1 = _rel_bias_tile(rel + LANES, rb_ref, h) * LOG2E
        zero = jnp.zeros((LANES, LANES), F32)
        neg = jnp.full((LANES, LANES), NEG, F32)
        for bi in range(nb):
            for bj in range(nb):
                rs = slice(bi * LANES, (bi + 1) * LANES)
                cs = slice(bj * LANES, (bj + 1) * LANES)
                bd_scr[rs, cs] = d0 if bi == bj else d1 if bj == bi + 1 else zero if bj > bi else neg
                bs_scr[rs, cs] = d1 if (bj == 0 and bi == nb - 1) else zero

    _assemble_bias_tiles()

    def tile(qi, carry):
        qt = q_ref[qi].astype(F32)
        sub = lax.broadcasted_iota(jnp.int32, qt.shape, 0)
        qp_scr[:, 0:T] = jnp.where(sub < HEAD_DK, qt, 0.0).astype(BF16)
        qp_scr[:, T:2 * T] = jnp.where(sub >= HEAD_DK, qt, 0.0).astype(BF16)
        m_scr[...] = jnp.full(m_scr.shape, -jnp.inf, F32)
        acc_scr[...] = jnp.zeros(acc_scr.shape, F32)

        n_chunks = 2 * T // QUERY_CHUNK

        def chunk_scores(kh, dst, c):
            dst[c] = _dot(kh, qp_scr[:, c * QUERY_CHUNK:(c + 1) * QUERY_CHUNK])

        def scores(j, dst):
            kh = k_ref[pl.ds(pl.multiple_of(j * T, T), T), :]
            for c in range(n_chunks):
                chunk_scores(kh, dst, c)

        def stage(j, src, bias_ref, j_next=None, dst=None):
            if j_next is not None:
                kh_next = k_ref[pl.ds(pl.multiple_of(j_next * T, T), T), :]
            vt = vt_ref[j]
            for c in range(n_chunks):
                if j_next is not None:
                    chunk_scores(kh_next, dst, c)
                cs = slice(c * QUERY_CHUNK, (c + 1) * QUERY_CHUNK)
                s = src[c]
                if bias_ref is not None:
                    lo = (c * QUERY_CHUNK) % T
                    s = s + bias_ref[:, lo:lo + QUERY_CHUNK]
                m_prev = m_scr[:, cs]
                m_next = jnp.maximum(m_prev, jnp.max(s, axis=0, keepdims=True))
                p = jnp.exp2(s - m_next).astype(BF16)
                acc_scr[:, cs] = jnp.exp2(m_prev - m_next) * acc_scr[:, cs] + _dot(vt, p)
                m_scr[:, cs] = m_next

        odd = qi % 2 == 0
        first = jnp.where(odd, 1, 0)

        @pl.when(odd)
        def _first_scores_odd():
            scores(0, sb_scr)

        @pl.when(jnp.logical_not(odd))
        def _first_scores_even():
            scores(0, sa_scr)

        @pl.when(odd & (qi >= 2))
        def _single_far_block():
            stage(0, sb_scr, None, 1, sa_scr)

        def far_pair(i, c):
            j = first + 2 * i
            stage(j, sa_scr, None, j + 1, sb_scr)
            stage(j + 1, sb_scr, None, j + 2, sa_scr)
            return c

        lax.fori_loop(0, jnp.maximum(qi - 1, 0) // 2, far_pair, 0)

        @pl.when(qi >= 1)
        def _sub_diagonal_and_diagonal():
            stage(qi - 1, sa_scr, bs_scr, qi, sb_scr)
            stage(qi, sb_scr, bd_scr)

        @pl.when(qi == 0)
        def _diagonal_only():
            stage(0, sb_scr, bd_scr)

        lam = _lam(lq1, lk1, lq2, lk2, lam_init)
        acc = acc_scr[0:HEAD_DV, :]
        l = acc_scr[HEAD_DV:HEAD_DV + 1, :]
        o = acc[:, 0:T] / l[:, 0:T] - lam * (acc[:, T:2 * T] / l[:, T:2 * T])
        y = o * lax.rsqrt(jnp.mean(o * o, axis=0, keepdims=True) + EPS) * sw_ref[...] * (1.0 - lam_init)
        o_ref[pl.ds(pl.multiple_of(qi * T, T), T), :] = y.T.astype(o_ref.dtype)
        return carry

    lax.fori_loop(0, n_tiles, tile, 0)


def _attn_prompt(rel_bias, qt, kb, vt, lq1, lk1, lq2, lk2, sw_col, *, batch, seq, lam_init):
    T = qt.shape[-1]
    nq = seq // T
    da = N_HEADS * HEAD_DV
    vec = lambda b, h: (0, 0)
    return pl.pallas_call(
        functools.partial(_attn_prompt_kernel, T=T, lam_init=lam_init),
        grid=(batch, N_HEADS),
        in_specs=[pl.BlockSpec(memory_space=pltpu.SMEM),
                  pl.BlockSpec((None, None, nq, HEAD_DV, T), lambda b, h: (b, h, 0, 0, 0)),
                  pl.BlockSpec((None, None, seq, HEAD_DV), lambda b, h: (b, h, 0, 0)),
                  pl.BlockSpec((None, None, nq, vt.shape[3], T), lambda b, h: (b, h, 0, 0, 0)),
                  pl.BlockSpec((1, HEAD_DK), vec), pl.BlockSpec((1, HEAD_DK), vec),
                  pl.BlockSpec((1, HEAD_DK), vec), pl.BlockSpec((1, HEAD_DK), vec),
                  pl.BlockSpec((HEAD_DV, 1), vec)],
        out_specs=pl.BlockSpec((seq, HEAD_DV), lambda b, h: (b, h)),
        out_shape=jax.ShapeDtypeStruct((batch * seq, da), BF16),
        scratch_shapes=[pltpu.VMEM((HEAD_DV, 2 * T), BF16),
                        pltpu.VMEM((1, 2 * T), F32),
                        pltpu.VMEM((vt.shape[3], 2 * T), F32),
                        pltpu.VMEM((T, T), F32),
                        pltpu.VMEM((T, T), F32),
                        pltpu.VMEM((2 * T // QUERY_CHUNK, T, QUERY_CHUNK), F32),
                        pltpu.VMEM((2 * T // QUERY_CHUNK, T, QUERY_CHUNK), F32)],
        compiler_params=_params("arbitrary", "arbitrary"),
    )(rel_bias, qt, kb, vt, lq1, lk1, lq2, lk2, sw_col)


def _attn_sample_kernel(pt_ref, rb_ref, q_ref, kn_ref, vn_ref, lq1, lk1, lq2, lk2, sw_ref, k_hbm, v_hbm,
                        o_ref, m_scr, l_scr, acc_scr, hm_scr, bl_scr, bn_scr, kbuf, vbuf, sems,
                        *, pages, page_offset, ts, lam_init):
    b = pl.program_id(0)
    nb = pl.num_programs(0)
    rows, pk = hm_scr.shape
    nk = bn_scr.shape[1]
    rm = rows // 2
    page = pk // N_HEADS
    groups = pt_ref.shape[1] // pages
    hbits = N_HEADS.bit_length() - 1
    tbits = ts.bit_length() - 1

    def group_copies(t, slot):
        bt = t // groups
        g0 = (t % groups) * pages
        out = []
        for i in range(pages):
            row0 = pl.multiple_of((page_offset + pt_ref[bt, g0 + i]) * pk, pk)
            dst = pl.ds(i * pk, pk)
            out.append(pltpu.make_async_copy(k_hbm.at[pl.ds(row0, pk)], kbuf.at[slot, dst], sems.at[0, slot]))
            out.append(pltpu.make_async_copy(v_hbm.at[pl.ds(row0, pk)], vbuf.at[slot, dst], sems.at[1, slot]))
        return out

    def start_group(t):
        @pl.when(t < nb * groups)
        def _():
            for cp in group_copies(t, t % PAGE_SLOTS):
                cp.start()

    @pl.when(b == 0)
    def _first_step():
        for t in range(PAGE_SLOTS - 1):
            start_group(jnp.int32(t))

        def tables(ncols, rel_of):
            r = lax.broadcasted_iota(jnp.int32, (rows, ncols), 0)
            c = lax.broadcasted_iota(jnp.int32, (rows, ncols), 1)
            rhead = (r >> tbits) & (N_HEADS - 1)
            same = rhead == (c & (N_HEADS - 1))
            rel = rel_of(r & (ts - 1), c >> hbits)
            bias = jnp.zeros((rows, ncols), F32)
            for h in range(N_HEADS):
                bias = jnp.where(rhead == h, _rel_bias_tile(jnp.maximum(rel, 0), rb_ref, h), bias)
            return same, rel, c >> hbits, bias

        same, _, _, bias = tables(pk, lambda tok, key: tok + page - key)
        hm_scr[...] = jnp.where(same, 0.0, NEG)
        bl_scr[...] = jnp.where(same, bias, NEG)
        same, rel, key, bias = tables(nk, lambda tok, key: tok - key)
        bn_scr[...] = jnp.where(same & (rel >= 0) & (key < ts), bias, NEG)

    m_scr[...] = jnp.full(m_scr.shape, -jnp.inf, F32)
    l_scr[...] = jnp.zeros(l_scr.shape, F32)
    acc_scr[...] = jnp.zeros(acc_scr.shape, F32)

    def update(scores, values):
        chunks, sizes = [], []
        for s in scores:
            n = s.shape[1] // LANES
            sizes.append(n)
            chunks += [s[:, c * LANES:(c + 1) * LANES] for c in range(n)]
        ps, m_next, l_next, alpha = _softmax_update(chunks, m_scr[...], l_scr[...])
        pv, at = None, 0
        for n, v in zip(sizes, values):
            p = jnp.concatenate([x.astype(BF16) for x in ps[at:at + n]], axis=1) if n > 1 \
                else ps[at].astype(BF16)
            at += n
            pv = _dot(p, v) if pv is None else pv + _dot(p, v)
        acc_scr[...] = alpha * acc_scr[...] + pv
        m_scr[...] = m_next
        l_scr[...] = l_next

    q = q_ref[...]

    def group_body(g, carry):
        t = b * groups + g
        slot = t % PAGE_SLOTS
        for cp in group_copies(t, slot):
            cp.wait()
        start_group(t + PAGE_SLOTS - 1)
        kg = kbuf.at[slot]
        vg = vbuf.at[slot]
        last = g == groups - 1
        scores = [_dot_nt(q, kg[i * pk:(i + 1) * pk, :].astype(BF16)) + hm_scr[...]
                  for i in range(pages - 1)]
        scores.append(_dot_nt(q, kg[(pages - 1) * pk:pages * pk, :].astype(BF16))
                      + jnp.where(last, bl_scr[...], hm_scr[...]))
        update(scores, [vg[i * pk:(i + 1) * pk, :].astype(BF16) for i in range(pages)])
        return carry

    lax.fori_loop(0, groups, group_body, 0)

    update([_dot_nt(q, kn_ref[...]) + bn_scr[...]], [vn_ref[...]])
    lam = _lam(lq1, lk1, lq2, lk2, lam_init)
    acc = acc_scr[...]
    l = l_scr[...]
    o_ref[...] = _head_out(acc[0:rm], l[0:rm], acc[rm:rows], l[rm:rows], lam, sw_ref[...], lam_init)


def _attn_sample(page_table, rel_bias, q_all, knew, vnew, lq1, lk1, lq2, lk2, sw, cache_k, cache_v,
                 *, page, page_offset, ts, lam_init):
    bs, n_pages = page_table.shape
    rows = q_all.shape[1]
    nk = knew.shape[1]
    pk = page * N_HEADS
    pages = math.gcd(PAGES_PER_GROUP, n_pages)
    vec = lambda b, pt: (0, 0)
    per_b = lambda b, pt: (b, 0, 0)
    grid_spec = pltpu.PrefetchScalarGridSpec(
        num_scalar_prefetch=1,
        grid=(bs,),
        in_specs=[pl.BlockSpec(memory_space=pltpu.SMEM),
                  pl.BlockSpec((None, rows, HEAD_DV), per_b),
                  pl.BlockSpec((None, nk, HEAD_DV), per_b),
                  pl.BlockSpec((None, nk, HEAD_DV), per_b),
                  pl.BlockSpec((1, HEAD_DK), vec), pl.BlockSpec((1, HEAD_DK), vec),
                  pl.BlockSpec((1, HEAD_DK), vec), pl.BlockSpec((1, HEAD_DK), vec),
                  pl.BlockSpec((1, HEAD_DV), vec),
                  pl.BlockSpec(memory_space=pl.ANY), pl.BlockSpec(memory_space=pl.ANY)],
        out_specs=pl.BlockSpec((None, rows // 2, HEAD_DV), per_b),
        scratch_shapes=[pltpu.VMEM((rows, LANES), F32),
                        pltpu.VMEM((rows, LANES), F32),
                        pltpu.VMEM((rows, HEAD_DV), F32),
                        pltpu.VMEM((rows, pk), F32),
                        pltpu.VMEM((rows, pk), F32),
                        pltpu.VMEM((rows, nk), F32),
                        pltpu.VMEM((PAGE_SLOTS, pages * pk, HEAD_DV), F32),
                        pltpu.VMEM((PAGE_SLOTS, pages * pk, HEAD_DV), F32),
                        pltpu.SemaphoreType.DMA((2, PAGE_SLOTS))],
    )
    return pl.pallas_call(
        functools.partial(_attn_sample_kernel, pages=pages, page_offset=page_offset, ts=ts,
                          lam_init=lam_init),
        grid_spec=grid_spec,
        out_shape=jax.ShapeDtypeStruct((bs, rows // 2, HEAD_DV), F32),
        compiler_params=_params("arbitrary"),
    )(page_table, rel_bias, q_all, knew, vnew, lq1, lk1, lq2, lk2, sw, cache_k, cache_v)


def _cross_heads(qc, mk_ref, mv_ref, o_scr, dh):
    for hd in range(MEM_HEADS):
        cols = slice(hd * dh, (hd + 1) * dh)
        s = _dot_nt(qc[:, cols], mk_ref[:, cols].astype(BF16))
        p = jnp.exp(s - jnp.max(s, axis=1, keepdims=True))
        o = _dot(p.astype(BF16), mv_ref[:, cols].astype(BF16)) / jnp.sum(p, axis=1, keepdims=True)
        o_scr[:, cols] = o.astype(BF16)


def _cross_prompt_kernel(x_ref, yc_ref, ya_ref, wo_ref, nw_ref, wq_ref, mk_ref, mv_ref, wc_ref,
                         o_ref, o_scr, *, dc, dh):
    x1 = x_ref[...] + _dot(yc_ref[...], wo_ref[0:dc, :]) + _dot(ya_ref[...], wo_ref[dc:, :])
    h = _rms(x1, nw_ref[...]).astype(BF16)
    qc = (_dot(h, wq_ref[...]) * (dh ** -0.5)).astype(BF16)
    _cross_heads(qc, mk_ref, mv_ref, o_scr, dh)
    o_ref[...] = x1 + _dot(o_scr[...], wc_ref[...])


def _cross_prompt(x2, yconv, yattn, w_out, nw, w_cq, mk, mv, w_co, *, batch, seq, n_mem):
    d = x2.shape[1]
    dc = yconv.shape[1]
    da = yattn.shape[1]
    tm = min(ROW_TILE, seq)
    nj = seq // tm
    rows = lambda b, j: (b * nj + j, 0)
    const = lambda b, j: (0, 0)
    memb = lambda b, j: (b, 0)
    return pl.pallas_call(
        functools.partial(_cross_prompt_kernel, dc=dc, dh=d // MEM_HEADS),
        grid=(batch, nj),
        in_specs=[pl.BlockSpec((tm, d), rows), pl.BlockSpec((tm, dc), rows), pl.BlockSpec((tm, da), rows),
                  pl.BlockSpec(w_out.shape, const), pl.BlockSpec((1, d), const),
                  pl.BlockSpec(w_cq.shape, const),
                  pl.BlockSpec((n_mem, d), memb), pl.BlockSpec((n_mem, d), memb),
                  pl.BlockSpec(w_co.shape, const)],
        out_specs=pl.BlockSpec((tm, d), rows),
        out_shape=jax.ShapeDtypeStruct(x2.shape, F32),
        scratch_shapes=[pltpu.VMEM((tm, d), BF16)],
        compiler_params=_params("arbitrary", "arbitrary"),
    )(x2, yconv, yattn, w_out, nw, w_cq, mk, mv, w_co)


def _outproj_q_sample_kernel(x_ref, yc_ref, ya_ref, wo_ref, nw_ref, wq_ref, x1_ref, qc_ref, *, dc, dh):
    x1 = x_ref[...] + _dot(yc_ref[...], wo_ref[0:dc, :]) + _dot(ya_ref[...], wo_ref[dc:, :])
    x1_ref[...] = x1
    h = _rms(x1, nw_ref[...]).astype(BF16)
    qc_ref[...] = (_dot(h, wq_ref[...]) * (dh ** -0.5)).astype(BF16)


def _outproj_q_sample(xt, yconv, yattn, w_out, nw, w_cq):
    d = xt.shape[1]
    return pl.pallas_call(
        functools.partial(_outproj_q_sample_kernel, dc=yconv.shape[1], dh=d // MEM_HEADS),
        out_shape=[jax.ShapeDtypeStruct(xt.shape, F32), jax.ShapeDtypeStruct(xt.shape, BF16)],
        compiler_params=pltpu.CompilerParams(vmem_limit_bytes=VMEM_LIMIT),
    )(xt, yconv, yattn, w_out, nw, w_cq)


def _cross_sample_kernel(q_ref, mk_ref, mv_ref, o_ref, *, ts):
    rq = q_ref.shape[0]
    ncols = mk_ref.shape[0]
    per_key = (rq // (MEM_HEADS * ts)) * MEM_HEADS
    nh = per_key // MEM_HEADS
    rows = MEM_HEADS * ts
    s2 = _dot_nt(q_ref[...], mk_ref[...].astype(BF16))
    s = s2[0:rows]
    for half in range(1, nh):
        s = s + pltpu.roll(s2[half * rows:(half + 1) * rows], ncols - half * MEM_HEADS, axis=1)
    col = lax.broadcasted_iota(jnp.int32, (rows, ncols), 1) & (per_key - 1)
    head = lax.broadcasted_iota(jnp.int32, (rows, ncols), 0) >> (ts.bit_length() - 1)
    s = jnp.where(col == head, s, NEG)
    p = jnp.exp(s - jnp.max(s, axis=1, keepdims=True))
    l = jnp.sum(p, axis=1, keepdims=True)
    lhs = jnp.concatenate([p] + [pltpu.roll(p, half * MEM_HEADS, axis=1) for half in range(1, nh)], axis=0)
    o2 = _dot(lhs.astype(BF16), mv_ref[...].astype(BF16))
    o = jnp.concatenate([o2[half * rows:(half + 1) * rows] for half in range(nh)], axis=1) / l
    o_ref[...] = o.astype(o_ref.dtype)


def _cross_sample(q2, mk, mv, *, row_offset, ts):
    bs, rq, _ = q2.shape
    ncols = mk.shape[1]
    rows = MEM_HEADS * ts
    nh = rq // rows
    per_b = lambda b: (b, 0, 0)
    mem_b = lambda b: (row_offset + b, 0, 0)
    return pl.pallas_call(
        functools.partial(_cross_sample_kernel, ts=ts),
        grid=(bs,),
        in_specs=[pl.BlockSpec((None, rq, LANES), per_b), pl.BlockSpec((None, ncols, LANES), mem_b),
                  pl.BlockSpec((None, ncols, LANES), mem_b)],
        out_specs=pl.BlockSpec((None, rows, nh * LANES), per_b),
        out_shape=jax.ShapeDtypeStruct((bs, rows, nh * LANES), BF16),
        compiler_params=_params("arbitrary"),
    )(q2, mk, mv)


def _silu(g):
    return g * (1.0 / (1.0 + jnp.exp(-g)))


def _ffn_prompt_kernel(x_ref, nw_ref, wu_ref, cw_ref, cb_ref, wd_ref, fw_ref, y_ref, st_ref,
                       up_scr, carry_scr, *, dff, final):
    j = pl.program_id(1)
    tm = x_ref.shape[0]
    x = x_ref[...]
    h = _rms(x, nw_ref[...]).astype(BF16)

    @pl.when(j == 0)
    def _zero_prefix():
        carry_scr[...] = jnp.zeros(carry_scr.shape, F32)

    def conv_half(lo):
        cols = slice(lo, lo + dff)
        up = _dot(h, wu_ref[:, cols])
        up_scr[0:SUBLANES, :] = carry_scr[:, cols]
        up_scr[SUBLANES:SUBLANES + tm, :] = up
        cw = cw_ref[:, cols]
        conv = (cb_ref[:, cols] + cw[0:1, :] * up_scr[SUBLANES - 2:SUBLANES - 2 + tm, :]
                + cw[1:2, :] * up_scr[SUBLANES - 1:SUBLANES - 1 + tm, :] + cw[2:3, :] * up)
        carry_scr[:, cols] = up[tm - SUBLANES:tm, :]
        st_ref[:, cols] = up[tm - 2:tm, :]
        return conv

    g = conv_half(0)
    u = conv_half(dff)
    x3 = x + _dot((_silu(g) * u).astype(BF16), wd_ref[...])
    y_ref[...] = _rms(x3, fw_ref[...]) if final else x3


def _ffn_prompt(x2, nw, w_up, cw, cb, w_down, fw, *, batch, seq, final):
    d = x2.shape[1]
    dff = w_down.shape[0]
    tm = min(FFN_ROW_TILE, seq)
    nj = seq // tm
    rows = lambda b, j: (b * nj + j, 0)
    const = lambda b, j: (0, 0)
    resident = lambda shape: pl.BlockSpec(shape, const, pipeline_mode=pl.Buffered(1))
    return pl.pallas_call(
        functools.partial(_ffn_prompt_kernel, dff=dff, final=final),
        grid=(batch, nj),
        in_specs=[pl.BlockSpec((tm, d), rows), pl.BlockSpec((1, d), const),
                  resident(w_up.shape), pl.BlockSpec(cw.shape, const),
                  pl.BlockSpec((1, 2 * dff), const), resident(w_down.shape),
                  pl.BlockSpec((1, d), const)],
        out_specs=[pl.BlockSpec((tm, d), rows),
                   pl.BlockSpec((None, CONV_W - 1, 2 * dff), lambda b, j: (b, 0, 0))],
        out_shape=[jax.ShapeDtypeStruct(x2.shape, F32),
                   jax.ShapeDtypeStruct((batch, CONV_W - 1, 2 * dff), F32)],
        scratch_shapes=[pltpu.VMEM((tm + SUBLANES, dff), F32),
                        pltpu.VMEM((SUBLANES, 2 * dff), F32)],
        compiler_params=_params("arbitrary", "arbitrary"),
    )(x2, nw, w_up, cw, cb, w_down, fw)


def _ffn_sample_kernel(x1_ref, o_ref, wc_ref, nw_ref, wu_ref, cw_ref, cb_ref, wd_ref, fw_ref,
                       s0_ref, s1_ref, y_ref, st_ref, hid_scr, *, dff, ts, bs, final):
    x2 = x1_ref[...] + _dot(o_ref[...], wc_ref[...])
    h = _rms(x2, nw_ref[...]).astype(BF16)

    def conv_half(lo):
        cols = slice(lo, lo + dff)
        up = _dot(h, wu_ref[:, cols])
        slabs = [s0_ref[:, cols], s1_ref[:, cols]] + [up[t * bs:(t + 1) * bs, :] for t in range(ts)]
        st_ref[0, :, cols] = slabs[-2]
        st_ref[1, :, cols] = slabs[-1]
        return _conv_time_major(slabs, cw_ref[:, cols], cb_ref[:, cols])

    g = conv_half(0)
    u = conv_half(dff)
    for t in range(ts):
        hid_scr[t * bs:(t + 1) * bs, :] = (_silu(g[t]) * u[t]).astype(BF16)
    x3 = x2 + _dot(hid_scr[...], wd_ref[...])
    y_ref[...] = _rms(x3, fw_ref[...]) if final else x3


def _ffn_sample(x1, o, w_co, nw, w_up, cw, cb, w_down, fw, s0, s1, *, ts, bs, final):
    dff = w_down.shape[0]
    return pl.pallas_call(
        functools.partial(_ffn_sample_kernel, dff=dff, ts=ts, bs=bs, final=final),
        out_shape=[jax.ShapeDtypeStruct(x1.shape, F32),
                   jax.ShapeDtypeStruct((CONV_W - 1, bs, 2 * dff), F32)],
        scratch_shapes=[pltpu.VMEM((ts * bs, dff), BF16)],
        compiler_params=pltpu.CompilerParams(vmem_limit_bytes=VMEM_LIMIT),
    )(x1, o, w_co, nw, w_up, cw, cb, w_down, fw, s0, s1)


def kernel(x_prompt, x_sample, mem_prompt, cache_k, cache_v, page_table, state_conv_mix, state_conv_ffn, cache_mem_k, cache_mem_v, rel_bias, norm_mix_w, w_in, conv_mix_w, conv_mix_b, lambda_q1, lambda_k1, lambda_q2, lambda_k2, subln_w, w_out, norm_cross_w, norm_mem_w, w_cq, w_ckv, w_co, norm_ffn_w, w_up, conv_ffn_w, conv_ffn_b, w_down, norm_final_w):
    depth = w_in.shape[0]
    bp, sp, d = x_prompt.shape
    bs, ts, _ = x_sample.shape
    n_mem = mem_prompt.shape[1]
    n_phys, page = cache_k.shape[1], cache_k.shape[2]
    n_pages = page_table.shape[1]
    past_len = n_pages * page
    dc = conv_mix_w.shape[2]
    da = N_HEADS * HEAD_DV
    dff = w_down.shape[1]
    dh = d // MEM_HEADS
    assert page == LANES and CONV_W - 1 <= ts <= SUBLANES and ts & (ts - 1) == 0
    assert (N_HEADS * ts) % SUBLANES == 0 and cache_k.shape[3:] == (N_HEADS, HEAD_DV)
    rows = 2 * SUBLANES

    assert _bucket_np(np.arange(REL_MAX_DIST, max(sp, past_len + ts) + 1)).min() == REL_BUCKETS - 1
    row = lambda a: a.reshape(1, -1).astype(F32)
    rel_bias = rel_bias.astype(F32)
    cache_k2 = cache_k.reshape(depth * n_phys * page * N_HEADS, HEAD_DV)
    cache_v2 = cache_v.reshape(depth * n_phys * page * N_HEADS, HEAD_DV)
    mem_rows = lambda a: a.reshape(depth * bs, n_mem, MEM_HEADS, dh // LANES, LANES).transpose(
        0, 1, 3, 2, 4).reshape(depth * bs, n_mem * (dh // LANES) * MEM_HEADS, LANES)
    mem_k3 = mem_rows(cache_mem_k)
    mem_v3 = mem_rows(cache_mem_v)

    xp = x_prompt.reshape(bp * sp, d)
    xs = x_sample.transpose(1, 0, 2).reshape(ts * bs, d)
    mem2 = mem_prompt.reshape(bp * n_mem, d)
    outs = [[] for _ in range(10)]
    for l in range(depth):
        lam_init = 0.8 - 0.6 * math.exp(-0.3 * l)
        final = l == depth - 1
        w_in_b, w_out_b = w_in[l].astype(BF16), w_out[l].astype(BF16)
        w_cq_b, w_ckv_b, w_co_b = w_cq[l].astype(BF16), w_ckv[l].astype(BF16), w_co[l].astype(BF16)
        w_up_b, w_down_b = w_up[l].astype(BF16), w_down[l].astype(BF16)
        lam_args = (row(lambda_q1[l]), row(lambda_k1[l]), row(lambda_q2[l]), row(lambda_k2[l]),
                    row(subln_w[l]))

        yconv_p, q_p, kf_p, vf_p, kb_p, vb_p, cmix_p = _mix_in_prompt(
            xp, row(norm_mix_w[l]), w_in_b, conv_mix_w[l], row(conv_mix_b[l]),
            batch=bp, seq=sp, dc=dc, da=da)
        yattn_p = _attn_prompt(rel_bias, q_p, kb_p, vb_p, *lam_args[:4], subln_w[l].reshape(-1, 1).astype(F32),
                               batch=bp, seq=sp, lam_init=lam_init)
        mkf, mvf, mkb, mvb = _mem_kv(mem2, row(norm_mem_w[l]), w_ckv_b, batch=bp, n_mem=n_mem)
        x2_p = _cross_prompt(xp, yconv_p, yattn_p, w_out_b, row(norm_cross_w[l]), w_cq_b, mkb, mvb,
                             w_co_b, batch=bp, seq=sp, n_mem=n_mem)
        xp, cffn_p = _ffn_prompt(x2_p, row(norm_ffn_w[l]), w_up_b, conv_ffn_w[l], row(conv_ffn_b[l]),
                                 w_down_b, row(norm_final_w), batch=bp, seq=sp, final=final)

        yconv_s, qm_s, kf_s, vf_s, cmix_s = _mix_in_sample(
            xs, row(norm_mix_w[l]), w_in_b, conv_mix_w[l], row(conv_mix_b[l]),
            state_conv_mix[l][:, 0], state_conv_mix[l][:, 1], dc=dc, da=da, ts=ts, bs=bs)
        q_all = qm_s.reshape(2, ts, bs, N_HEADS, HEAD_DV).transpose(2, 0, 3, 1, 4)
        q_all = q_all.reshape(bs, 2 * N_HEADS * ts, HEAD_DV)
        new_rows = lambda a: jnp.pad(
            a.reshape(ts, bs, N_HEADS * HEAD_DV).transpose(1, 0, 2).reshape(bs, ts * N_HEADS, HEAD_DV),
            ((0, 0), (0, LANES - ts * N_HEADS), (0, 0))).astype(BF16)
        yattn_s = _attn_sample(page_table, rel_bias, q_all, new_rows(kf_s), new_rows(vf_s), *lam_args,
                               cache_k2, cache_v2, page=page, page_offset=l * n_phys, ts=ts,
                               lam_init=lam_init)
        yattn_s = yattn_s.reshape(bs, N_HEADS, ts, HEAD_DV).transpose(2, 0, 1, 3)
        yattn_s = yattn_s.reshape(ts * bs, da).astype(BF16)
        x1_s, qc_s = _outproj_q_sample(xs, yconv_s, yattn_s, w_out_b, row(norm_cross_w[l]), w_cq_b)
        q2 = qc_s.reshape(ts, bs, MEM_HEADS, dh // LANES, LANES).transpose(1, 3, 2, 0, 4)
        q2 = q2.reshape(bs, (dh // LANES) * MEM_HEADS * ts, LANES)
        o_b = _cross_sample(q2, mem_k3, mem_v3, row_offset=l * bs, ts=ts)
        o_s = o_b.reshape(bs, MEM_HEADS, ts, dh).transpose(2, 0, 1, 3).reshape(ts * bs, d)
        xs, cffn_s = _ffn_sample(x1_s, o_s, w_co_b, row(norm_ffn_w[l]), w_up_b, conv_ffn_w[l],
                                 row(conv_ffn_b[l]), w_down_b, row(norm_final_w),
                                 state_conv_ffn[l][:, 0], state_conv_ffn[l][:, 1],
                                 ts=ts, bs=bs, final=final)

        t2b = lambda a: a.reshape(ts, bs, N_HEADS, HEAD_DV).transpose(1, 0, 2, 3)
        for lst, val in zip(outs, (
                kf_p.reshape(bp, sp, N_HEADS, 2 * HEAD_DK), vf_p.reshape(bp, sp, N_HEADS, HEAD_DV),
                t2b(kf_s), t2b(vf_s), cmix_p, cmix_s.transpose(1, 0, 2), cffn_p,
                cffn_s.transpose(1, 0, 2), mkf.reshape(bp, n_mem, MEM_HEADS, dh),
                mvf.reshape(bp, n_mem, MEM_HEADS, dh))):
            lst.append(val)

    y_prompt = xp.reshape(bp, sp, d)
    y_sample = xs.reshape(ts, bs, d).transpose(1, 0, 2)
    return (y_prompt, y_sample) + tuple(jnp.stack(o) for o in outs)
```

```python
import functools
import math

import numpy as np
import jax
import jax.numpy as jnp
from jax import lax
from jax.experimental import pallas as pl
from jax.experimental.pallas import tpu as pltpu

F32 = jnp.float32
BF16 = jnp.bfloat16

EPS = 1e-6
NEG = -1e30
LANES = 128
SUBLANES = 8
N_HEADS = 4
HEAD_DK = 64
HEAD_DV = 2 * HEAD_DK
MEM_HEADS = 4
CONV_W = 3
REL_BUCKETS = 32
REL_MAX_EXACT = 16
REL_MAX_DIST = 128
VMEM_LIMIT = 56 * 1024 * 1024

ATTN_TILE = 512
QUERY_CHUNK = 256
ONES_ROWS = 16
LOG2E = 1.4426950408889634
ROW_TILE = 512
FFN_ROW_TILE = 512
PAGES_PER_GROUP = 8
PAGE_SLOTS = 3


def _params(*sem):
    return pltpu.CompilerParams(dimension_semantics=sem, vmem_limit_bytes=VMEM_LIMIT)


def _rms(x, w):
    return x * lax.rsqrt(jnp.mean(x * x, axis=-1, keepdims=True) + EPS) * w


def _dot(a, b):
    return jnp.dot(a, b, preferred_element_type=F32)


def _dot_nt(a, b):
    return lax.dot_general(a, b, (((1,), (1,)), ((), ())), preferred_element_type=F32)


def _lam(lq1, lk1, lq2, lk2, lam_init):
    return (jnp.exp(jnp.sum(lq1[...] * lk1[...], axis=-1, keepdims=True))
            - jnp.exp(jnp.sum(lq2[...] * lk2[...], axis=-1, keepdims=True)) + lam_init)


def _bucket_np(rel):
    n = np.maximum(rel, 0)
    nf = np.maximum(n, 1).astype(np.float32)
    large = REL_MAX_EXACT + (np.log(nf / np.float32(REL_MAX_EXACT))
                             / np.float32(math.log(REL_MAX_DIST / REL_MAX_EXACT))
                             * np.float32(REL_BUCKETS - REL_MAX_EXACT)).astype(np.int32)
    large = np.minimum(large, REL_BUCKETS - 1)
    return np.where(n < REL_MAX_EXACT, n, large).astype(np.int32)


def _bucket_starts():
    buckets = _bucket_np(np.arange(REL_MAX_DIST + 1))
    assert (np.diff(buckets) >= 0).all() and buckets[-1] == REL_BUCKETS - 1
    return [int(np.argmax(buckets >= k)) for k in range(REL_BUCKETS)]


def _rel_bias_tile(rel, rb_ref, h):
    far = rb_ref[REL_BUCKETS - 1, h]
    val = jnp.full(rel.shape, rb_ref[0, h] - far, F32)
    for k, start in enumerate(_bucket_starts()):
        if k > 0:
            val = jnp.where(rel >= start, rb_ref[k, h] - far, val)
    return val


def _softmax_update(chunks, m_prev, l_prev):
    mx = chunks[0]
    for c in chunks[1:]:
        mx = jnp.maximum(mx, c)
    m_next = jnp.maximum(m_prev, jnp.max(mx, axis=1, keepdims=True))
    ps = [jnp.exp(c - m_next) for c in chunks]
    sm = ps[0]
    for p in ps[1:]:
        sm = sm + p
    alpha = jnp.exp(m_prev - m_next)
    l_next = alpha * l_prev + jnp.sum(sm, axis=1, keepdims=True)
    return ps, m_next, l_next, alpha


def _head_out(acc1, l1, acc2, l2, lam, sw, lam_init):
    o = acc1 / l1 - lam * (acc2 / l2)
    return _rms(o, sw) * (1.0 - lam_init)


def _mix_in_prompt_kernel(x_ref, nw_ref, w_ref, cw_ref, cb_ref,
                          yconv_ref, q_ref, kf_ref, vf_ref, kb_ref, vb_ref, st_ref,
                          pre_scr, *, dc, da):
    j = pl.program_id(1)
    tm = x_ref.shape[0]
    h = _rms(x_ref[...], nw_ref[...]).astype(BF16)

    def proj(lo, width):
        return _dot(h, w_ref[:, lo:lo + width])

    @pl.when(j == 0)
    def _zero_prefix():
        pre_scr[0:SUBLANES, :] = jnp.zeros((SUBLANES, dc), F32)

    pre = proj(dc, dc) * proj(2 * dc, dc)
    pre_scr[SUBLANES:SUBLANES + tm, :] = pre
    cw = cw_ref[...]
    conv = (cb_ref[...] + cw[0:1, :] * pre_scr[SUBLANES - 2:SUBLANES - 2 + tm, :]
            + cw[1:2, :] * pre_scr[SUBLANES - 1:SUBLANES - 1 + tm, :] + cw[2:3, :] * pre)
    yconv_ref[...] = (proj(0, dc) * conv).astype(BF16)
    st_ref[...] = pre[tm - 2:tm, :]
    pre_scr[0:SUBLANES, :] = pre[tm - SUBLANES:tm, :]

    q = proj(3 * dc, da) * (HEAD_DK ** -0.5 * LOG2E)
    k = proj(3 * dc + da, da)
    v = proj(3 * dc + 2 * da, da)
    ones_rows = (lax.broadcasted_iota(jnp.int32, (ONES_ROWS, tm), 0) == 0).astype(BF16)
    for hd in range(N_HEADS):
        cols = slice(hd * HEAD_DV, (hd + 1) * HEAD_DV)
        kb_ref[hd] = k[:, cols].astype(BF16)
        kf_ref[pl.ds(hd, tm, stride=N_HEADS), :] = k[:, cols]
        vf_ref[pl.ds(hd, tm, stride=N_HEADS), :] = v[:, cols]
        q_ref[hd] = q[:, cols].T.astype(BF16)
        vb_ref[hd, 0:HEAD_DV, :] = v[:, cols].T.astype(BF16)
        vb_ref[hd, HEAD_DV:, :] = ones_rows


def _mix_in_prompt(x2, nw, w_in, cw, cb, *, batch, seq, dc, da):
    d = x2.shape[1]
    tm = min(ATTN_TILE, seq)
    nj = seq // tm
    rows = lambda b, j: (b * nj + j, 0)
    const = lambda b, j: (0, 0)
    tposed = lambda r: pl.BlockSpec((None, N_HEADS, None, r, tm), lambda b, j: (b, 0, j, 0, 0))
    tposed_shape = lambda r: jax.ShapeDtypeStruct((batch, N_HEADS, nj, r, tm), BF16)
    n = batch * seq
    return pl.pallas_call(
        functools.partial(_mix_in_prompt_kernel, dc=dc, da=da),
        grid=(batch, nj),
        in_specs=[pl.BlockSpec((tm, d), rows),
                  pl.BlockSpec((1, d), const),
                  pl.BlockSpec(w_in.shape, const),
                  pl.BlockSpec(cw.shape, const),
                  pl.BlockSpec((1, dc), const)],
        out_specs=[pl.BlockSpec((tm, dc), rows),
                   tposed(HEAD_DV),
                   pl.BlockSpec((tm * N_HEADS, HEAD_DV), rows),
                   pl.BlockSpec((tm * N_HEADS, HEAD_DV), rows),
                   pl.BlockSpec((None, N_HEADS, tm, HEAD_DV), lambda b, j: (b, 0, j, 0)),
                   tposed(HEAD_DV + ONES_ROWS),
                   pl.BlockSpec((None, CONV_W - 1, dc), lambda b, j: (b, 0, 0))],
        out_shape=[jax.ShapeDtypeStruct((n, dc), BF16),
                   tposed_shape(HEAD_DV),
                   jax.ShapeDtypeStruct((n * N_HEADS, HEAD_DV), F32),
                   jax.ShapeDtypeStruct((n * N_HEADS, HEAD_DV), F32),
                   jax.ShapeDtypeStruct((batch, N_HEADS, seq, HEAD_DV), BF16),
                   tposed_shape(HEAD_DV + ONES_ROWS),
                   jax.ShapeDtypeStruct((batch, CONV_W - 1, dc), F32)],
        scratch_shapes=[pltpu.VMEM((tm + SUBLANES, dc), F32)],
        compiler_params=_params("arbitrary", "arbitrary"),
    )(x2, nw, w_in, cw, cb)


def _conv_time_major(slabs, cw, cb):
    return [cb + cw[0:1, :] * slabs[t] + cw[1:2, :] * slabs[t + 1] + cw[2:3, :] * slabs[t + 2]
            for t in range(len(slabs) - 2)]


def _mix_in_sample_kernel(x_ref, nw_ref, w_ref, cw_ref, cb_ref, s0_ref, s1_ref,
                          yconv_ref, qm_ref, kf_ref, vf_ref, st_ref, *, dc, da, ts, bs):
    h = _rms(x_ref[...], nw_ref[...]).astype(BF16)

    def proj(lo, width):
        return _dot(h, w_ref[:, lo:lo + width])

    pre = proj(dc, dc) * proj(2 * dc, dc)
    gate = proj(0, dc)
    slabs = [s0_ref[...], s1_ref[...]] + [pre[t * bs:(t + 1) * bs, :] for t in range(ts)]
    conv = _conv_time_major(slabs, cw_ref[...], cb_ref[...])
    for t in range(ts):
        yconv_ref[t * bs:(t + 1) * bs, :] = (gate[t * bs:(t + 1) * bs, :] * conv[t]).astype(BF16)
    st_ref[0] = slabs[-2]
    st_ref[1] = slabs[-1]

    q = proj(3 * dc, da) * (HEAD_DK ** -0.5)
    lane = lax.broadcasted_iota(jnp.int32, q.shape, 1) % HEAD_DV
    qm_ref[0] = jnp.where(lane < HEAD_DK, q, 0.0).astype(BF16)
    qm_ref[1] = jnp.where(lane >= HEAD_DK, q, 0.0).astype(BF16)
    kf_ref[...] = proj(3 * dc + da, da)
    vf_ref[...] = proj(3 * dc + 2 * da, da)


def _mix_in_sample(xt, nw, w_in, cw, cb, s0, s1, *, dc, da, ts, bs):
    n = ts * bs
    return pl.pallas_call(
        functools.partial(_mix_in_sample_kernel, dc=dc, da=da, ts=ts, bs=bs),
        out_shape=[jax.ShapeDtypeStruct((n, dc), BF16),
                   jax.ShapeDtypeStruct((2, n, da), BF16),
                   jax.ShapeDtypeStruct((n, da), F32),
                   jax.ShapeDtypeStruct((n, da), F32),
                   jax.ShapeDtypeStruct((CONV_W - 1, bs, dc), F32)],
        compiler_params=pltpu.CompilerParams(vmem_limit_bytes=VMEM_LIMIT),
    )(xt, nw, w_in, cw, cb, s0, s1)


def _mem_kv_kernel(m_ref, nw_ref, w_ref, kf_ref, vf_ref, kb_ref, vb_ref, *, d):
    h = _rms(m_ref[...], nw_ref[...]).astype(BF16)
    k = _dot(h, w_ref[:, 0:d])
    kf_ref[...] = k
    kb_ref[...] = k.astype(BF16)
    v = _dot(h, w_ref[:, d:2 * d])
    vf_ref[...] = v
    vb_ref[...] = v.astype(BF16)


def _mem_kv(mem2, nw, w_ckv, *, batch, n_mem):
    d = mem2.shape[1]
    rows = lambda b: (b, 0)
    const = lambda b: (0, 0)
    n = batch * n_mem
    return pl.pallas_call(
        functools.partial(_mem_kv_kernel, d=d),
        grid=(batch,),
        in_specs=[pl.BlockSpec((n_mem, d), rows), pl.BlockSpec((1, d), const),
                  pl.BlockSpec(w_ckv.shape, const)],
        out_specs=[pl.BlockSpec((n_mem, d), rows)] * 4,
        out_shape=[jax.ShapeDtypeStruct((n, d), F32), jax.ShapeDtypeStruct((n, d), F32),
                   jax.ShapeDtypeStruct((n, d), BF16), jax.ShapeDtypeStruct((n, d), BF16)],
        compiler_params=_params("arbitrary"),
    )(mem2, nw, w_ckv)


def _attn_prompt_kernel(rb_ref, q_ref, k_ref, vt_ref, lq1, lk1, lq2, lk2, sw_ref, o_ref,
                        qa_scr, ma_scr, acca_scr, qb_scr, mb_scr, accb_scr, bd_scr, bs_scr, sa_scr, sb_scr,
                        *, T, lam_init):
    nb = T // LANES
    n_tiles = q_ref.shape[0]

    def _assemble_bias_tiles():
        h = pl.program_id(1)
        rel = (lax.broadcasted_iota(jnp.int32, (LANES, LANES), 1)
               - lax.broadcasted_iota(jnp.int32, (LANES, LANES), 0))
        d0 = jnp.where(rel >= 0, _rel_bias_tile(jnp.maximum(rel, 0), rb_ref, h) * LOG2E, NEG)
        d1 = _rel_bias_tile(rel + LANES, rb_ref, h) * LOG2E
        zero = jnp.zeros((LANES, LANES), F32)
        neg = jnp.full((LANES, LANES), NEG, F32)
        for bi in range(nb):
            for bj in range(nb):
                rs = slice(bi * LANES, (bi + 1) * LANES)
                cs = slice(bj * LANES, (bj + 1) * LANES)
                bd_scr[rs, cs] = d0 if bi == bj else d1 if bj == bi + 1 else zero if bj > bi else neg
                bs_scr[rs, cs] = d1 if (bj == 0 and bi == nb - 1) else zero

    _assemble_bias_tiles()
    n_chunks = 2 * T // QUERY_CHUNK

    def prep(st, qi):
        qp, m, acc = st
        qt = q_ref[qi].astype(F32)
        sub = lax.broadcasted_iota(jnp.int32, qt.shape, 0)
        qp[:, 0:T] = jnp.where(sub < HEAD_DK, qt, 0.0).astype(BF16)
        qp[:, T:2 * T] = jnp.where(sub >= HEAD_DK, qt, 0.0).astype(BF16)
        m[...] = jnp.full(m.shape, -jnp.inf, F32)
        acc[...] = jnp.zeros(acc.shape, F32)

    def chunk_scores(kh, qp, dst, c):
        dst[c] = _dot(kh, qp[:, c * QUERY_CHUNK:(c + 1) * QUERY_CHUNK])

    def key_block(j):
        return k_ref[pl.ds(pl.multiple_of(j * T, T), T), :]

    def stage(st, j, src, bias_ref, nxt=None):
        _, m_scr, acc_scr = st
        if nxt is not None:
            j_next, qp_next, dst = nxt
            kh_next = key_block(j_next)
        vt = vt_ref[j]
        for c in range(n_chunks):
            if nxt is not None and dst is not src:
                chunk_scores(kh_next, qp_next, dst, c)
            cs = slice(c * QUERY_CHUNK, (c + 1) * QUERY_CHUNK)
            s = src[c]
            if bias_ref is not None:
                lo = (c * QUERY_CHUNK) % T
                s = s + bias_ref[:, lo:lo + QUERY_CHUNK]
            m_prev = m_scr[:, cs]
            m_next = jnp.maximum(m_prev, jnp.max(s, axis=0, keepdims=True))
            p = jnp.exp2(s - m_next).astype(BF16)
            if nxt is not None and dst is src:
                chunk_scores(kh_next, qp_next, dst, c)
            acc_scr[:, cs] = jnp.exp2(m_prev - m_next) * acc_scr[:, cs] + _dot(vt, p)
            m_scr[:, cs] = m_next

    def finalize(st, qi):
        _, _, acc_scr = st
        lam = _lam(lq1, lk1, lq2, lk2, lam_init)
        acc = acc_scr[0:HEAD_DV, :]
        l = acc_scr[HEAD_DV:HEAD_DV + 1, :]
        o = acc[:, 0:T] / l[:, 0:T] - lam * (acc[:, T:2 * T] / l[:, T:2 * T])
        y = o * lax.rsqrt(jnp.mean(o * o, axis=0, keepdims=True) + EPS) * sw_ref[...] * (1.0 - lam_init)
        o_ref[pl.ds(pl.multiple_of(qi * T, T), T), :] = y.T.astype(o_ref.dtype)

    even = (qa_scr, ma_scr, acca_scr)
    odd = (qb_scr, mb_scr, accb_scr)
    prep(even, 0)
    kh0 = key_block(0)
    for c in range(n_chunks):
        chunk_scores(kh0, qa_scr, sb_scr, c)

    def tile_pair(t, carry):
        a = 2 * t
        b = a + 1

        @pl.when(t >= 1)
        def _even_first_block():
            stage(even, 0, sb_scr, None, (1, qa_scr, sa_scr))

        def even_far_pair(i, c):
            j = 1 + 2 * i
            stage(even, j, sa_scr, None, (j + 1, qa_scr, sb_scr))
            stage(even, j + 1, sb_scr, None, (j + 2, qa_scr, sa_scr))
            return c

        lax.fori_loop(0, jnp.maximum(t - 1, 0), even_far_pair, 0)
        prep(odd, b)

        @pl.when(t >= 1)
        def _even_last_blocks():
            stage(even, a - 1, sa_scr, bs_scr, (a, qa_scr, sb_scr))
            stage(even, a, sb_scr, bd_scr, (0, qb_scr, sa_scr))

        @pl.when(t == 0)
        def _even_only_block():
            stage(even, 0, sb_scr, bd_scr, (0, qb_scr, sa_scr))

        finalize(even, a)

        def odd_far_pair(i, c):
            j = 2 * i
            stage(odd, j, sa_scr, None, (j + 1, qb_scr, sb_scr))
            stage(odd, j + 1, sb_scr, None, (j + 2, qb_scr, sa_scr))
            return c

        lax.fori_loop(0, t, odd_far_pair, 0)
        prep(even, jnp.minimum(a + 2, n_tiles - 1))
        stage(odd, b - 1, sa_scr, bs_scr, (b, qb_scr, sb_scr))
        stage(odd, b, sb_scr, bd_scr, (0, qa_scr, sb_scr))
        finalize(odd, b)
        return carry

    lax.fori_loop(0, n_tiles // 2, tile_pair, 0)


def _attn_prompt(rel_bias, qt, kb, vt, lq1, lk1, lq2, lk2, sw_col, *, batch, seq, lam_init):
    T = qt.shape[-1]
    nq = seq // T
    assert nq % 2 == 0, "query tiles run in (even, odd) pairs"
    da = N_HEADS * HEAD_DV
    vec = lambda b, h: (0, 0)
    return pl.pallas_call(
        functools.partial(_attn_prompt_kernel, T=T, lam_init=lam_init),
        grid=(batch, N_HEADS),
        in_specs=[pl.BlockSpec(memory_space=pltpu.SMEM),
                  pl.BlockSpec((None, None, nq, HEAD_DV, T), lambda b, h: (b, h, 0, 0, 0)),
                  pl.BlockSpec((None, None, seq, HEAD_DV), lambda b, h: (b, h, 0, 0)),
                  pl.BlockSpec((None, None, nq, vt.shape[3], T), lambda b, h: (b, h, 0, 0, 0)),
                  pl.BlockSpec((1, HEAD_DK), vec), pl.BlockSpec((1, HEAD_DK), vec),
                  pl.BlockSpec((1, HEAD_DK), vec), pl.BlockSpec((1, HEAD_DK), vec),
                  pl.BlockSpec((HEAD_DV, 1), vec)],
        out_specs=pl.BlockSpec((seq, HEAD_DV), lambda b, h: (b, h)),
        out_shape=jax.ShapeDtypeStruct((batch * seq, da), BF16),
        scratch_shapes=[pltpu.VMEM((HEAD_DV, 2 * T), BF16),
                        pltpu.VMEM((1, 2 * T), F32),
                        pltpu.VMEM((vt.shape[3], 2 * T), F32),
                        pltpu.VMEM((HEAD_DV, 2 * T), BF16),
                        pltpu.VMEM((1, 2 * T), F32),
                        pltpu.VMEM((vt.shape[3], 2 * T), F32),
                        pltpu.VMEM((T, T), F32),
                        pltpu.VMEM((T, T), F32),
                        pltpu.VMEM((2 * T // QUERY_CHUNK, T, QUERY_CHUNK), F32),
                        pltpu.VMEM((2 * T // QUERY_CHUNK, T, QUERY_CHUNK), F32)],
        compiler_params=_params("arbitrary", "arbitrary"),
    )(rel_bias, qt, kb, vt, lq1, lk1, lq2, lk2, sw_col)


def _attn_sample_kernel(pt_ref, rb_ref, q_ref, kn_ref, vn_ref, lq1, lk1, lq2, lk2, sw_ref, k_hbm, v_hbm,
                        o_ref, m_scr, l_scr, acc_scr, hm_scr, bl_scr, bn_scr, kbuf, vbuf, sems,
                        *, pages, page_offset, ts, lam_init):
    b = pl.program_id(0)
    nb = pl.num_programs(0)
    rows, pk = hm_scr.shape
    nk = bn_scr.shape[1]
    rm = rows // 2
    page = pk // N_HEADS
    groups = pt_ref.shape[1] // pages
    hbits = N_HEADS.bit_length() - 1
    tbits = ts.bit_length() - 1

    def group_copies(t, slot):
        bt = t // groups
        g0 = (t % groups) * pages
        out = []
        for i in range(pages):
            row0 = pl.multiple_of((page_offset + pt_ref[bt, g0 + i]) * pk, pk)
            dst = pl.ds(i * pk, pk)
            out.append(pltpu.make_async_copy(k_hbm.at[pl.ds(row0, pk)], kbuf.at[slot, dst], sems.at[0, slot]))
            out.append(pltpu.make_async_copy(v_hbm.at[pl.ds(row0, pk)], vbuf.at[slot, dst], sems.at[1, slot]))
        return out

    def start_group(t):
        @pl.when(t < nb * groups)
        def _():
            for cp in group_copies(t, t % PAGE_SLOTS):
                cp.start()

    @pl.when(b == 0)
    def _first_step():
        for t in range(PAGE_SLOTS - 1):
            start_group(jnp.int32(t))

        def tables(ncols, rel_of):
            r = lax.broadcasted_iota(jnp.int32, (rows, ncols), 0)
            c = lax.broadcasted_iota(jnp.int32, (rows, ncols), 1)
            rhead = (r >> tbits) & (N_HEADS - 1)
            same = rhead == (c & (N_HEADS - 1))
            rel = rel_of(r & (ts - 1), c >> hbits)
            bias = jnp.zeros((rows, ncols), F32)
            for h in range(N_HEADS):
                bias = jnp.where(rhead == h, _rel_bias_tile(jnp.maximum(rel, 0), rb_ref, h), bias)
            return same, rel, c >> hbits, bias

        same, _, _, bias = tables(pk, lambda tok, key: tok + page - key)
        hm_scr[...] = jnp.where(same, 0.0, NEG)
        bl_scr[...] = jnp.where(same, bias, NEG)
        same, rel, key, bias = tables(nk, lambda tok, key: tok - key)
        bn_scr[...] = jnp.where(same & (rel >= 0) & (key < ts), bias, NEG)

    m_scr[...] = jnp.full(m_scr.shape, -jnp.inf, F32)
    l_scr[...] = jnp.zeros(l_scr.shape, F32)
    acc_scr[...] = jnp.zeros(acc_scr.shape, F32)

    def update(scores, values):
        chunks, sizes = [], []
        for s in scores:
            n = s.shape[1] // LANES
            sizes.append(n)
            chunks += [s[:, c * LANES:(c + 1) * LANES] for c in range(n)]
        ps, m_next, l_next, alpha = _softmax_update(chunks, m_scr[...], l_scr[...])
        pv, at = None, 0
        for n, v in zip(sizes, values):
            p = jnp.concatenate([x.astype(BF16) for x in ps[at:at + n]], axis=1) if n > 1 \
                else ps[at].astype(BF16)
            at += n
            pv = _dot(p, v) if pv is None else pv + _dot(p, v)
        acc_scr[...] = alpha * acc_scr[...] + pv
        m_scr[...] = m_next
        l_scr[...] = l_next

    q = q_ref[...]

    def group_body(g, carry):
        t = b * groups + g
        slot = t % PAGE_SLOTS
        for cp in group_copies(t, slot):
            cp.wait()
        start_group(t + PAGE_SLOTS - 1)
        kg = kbuf.at[slot]
        vg = vbuf.at[slot]
        last = g == groups - 1
        scores = [_dot_nt(q, kg[i * pk:(i + 1) * pk, :].astype(BF16)) + hm_scr[...]
                  for i in range(pages - 1)]
        scores.append(_dot_nt(q, kg[(pages - 1) * pk:pages * pk, :].astype(BF16))
                      + jnp.where(last, bl_scr[...], hm_scr[...]))
        update(scores, [vg[i * pk:(i + 1) * pk, :].astype(BF16) for i in range(pages)])
        return carry

    lax.fori_loop(0, groups, group_body, 0)

    update([_dot_nt(q, kn_ref[...]) + bn_scr[...]], [vn_ref[...]])
    lam = _lam(lq1, lk1, lq2, lk2, lam_init)
    acc = acc_scr[...]
    l = l_scr[...]
    o_ref[...] = _head_out(acc[0:rm], l[0:rm], acc[rm:rows], l[rm:rows], lam, sw_ref[...], lam_init)


def _attn_sample(page_table, rel_bias, q_all, knew, vnew, lq1, lk1, lq2, lk2, sw, cache_k, cache_v,
                 *, page, page_offset, ts, lam_init):
    bs, n_pages = page_table.shape
    rows = q_all.shape[1]
    nk = knew.shape[1]
    pk = page * N_HEADS
    pages = math.gcd(PAGES_PER_GROUP, n_pages)
    vec = lambda b, pt: (0, 0)
    per_b = lambda b, pt: (b, 0, 0)
    grid_spec = pltpu.PrefetchScalarGridSpec(
        num_scalar_prefetch=1,
        grid=(bs,),
        in_specs=[pl.BlockSpec(memory_space=pltpu.SMEM),
                  pl.BlockSpec((None, rows, HEAD_DV), per_b),
                  pl.BlockSpec((None, nk, HEAD_DV), per_b),
                  pl.BlockSpec((None, nk, HEAD_DV), per_b),
                  pl.BlockSpec((1, HEAD_DK), vec), pl.BlockSpec((1, HEAD_DK), vec),
                  pl.BlockSpec((1, HEAD_DK), vec), pl.BlockSpec((1, HEAD_DK), vec),
                  pl.BlockSpec((1, HEAD_DV), vec),
                  pl.BlockSpec(memory_space=pl.ANY), pl.BlockSpec(memory_space=pl.ANY)],
        out_specs=pl.BlockSpec((None, rows // 2, HEAD_DV), per_b),
        scratch_shapes=[pltpu.VMEM((rows, LANES), F32),
                        pltpu.VMEM((rows, LANES), F32),
                        pltpu.VMEM((rows, HEAD_DV), F32),
                        pltpu.VMEM((rows, pk), F32),
                        pltpu.VMEM((rows, pk), F32),
                        pltpu.VMEM((rows, nk), F32),
                        pltpu.VMEM((PAGE_SLOTS, pages * pk, HEAD_DV), F32),
                        pltpu.VMEM((PAGE_SLOTS, pages * pk, HEAD_DV), F32),
                        pltpu.SemaphoreType.DMA((2, PAGE_SLOTS))],
    )
    return pl.pallas_call(
        functools.partial(_attn_sample_kernel, pages=pages, page_offset=page_offset, ts=ts,
                          lam_init=lam_init),
        grid_spec=grid_spec,
        out_shape=jax.ShapeDtypeStruct((bs, rows // 2, HEAD_DV), F32),
        compiler_params=_params("arbitrary"),
    )(page_table, rel_bias, q_all, knew, vnew, lq1, lk1, lq2, lk2, sw, cache_k, cache_v)


def _cross_heads(qc, mk_ref, mv_ref, o_scr, dh):
    for hd in range(MEM_HEADS):
        cols = slice(hd * dh, (hd + 1) * dh)
        s = _dot_nt(qc[:, cols], mk_ref[:, cols].astype(BF16))
        p = jnp.exp(s - jnp.max(s, axis=1, keepdims=True))
        o = _dot(p.astype(BF16), mv_ref[:, cols].astype(BF16)) / jnp.sum(p, axis=1, keepdims=True)
        o_scr[:, cols] = o.astype(BF16)


def _cross_prompt_kernel(x_ref, yc_ref, ya_ref, wo_ref, nw_ref, wq_ref, mk_ref, mv_ref, wc_ref,
                         o_ref, o_scr, *, dc, dh):
    x1 = x_ref[...] + _dot(yc_ref[...], wo_ref[0:dc, :]) + _dot(ya_ref[...], wo_ref[dc:, :])
    h = _rms(x1, nw_ref[...]).astype(BF16)
    qc = (_dot(h, wq_ref[...]) * (dh ** -0.5)).astype(BF16)
    _cross_heads(qc, mk_ref, mv_ref, o_scr, dh)
    o_ref[...] = x1 + _dot(o_scr[...], wc_ref[...])


def _cross_prompt(x2, yconv, yattn, w_out, nw, w_cq, mk, mv, w_co, *, batch, seq, n_mem):
    d = x2.shape[1]
    dc = yconv.shape[1]
    da = yattn.shape[1]
    tm = min(ROW_TILE, seq)
    nj = seq // tm
    rows = lambda b, j: (b * nj + j, 0)
    const = lambda b, j: (0, 0)
    memb = lambda b, j: (b, 0)
    return pl.pallas_call(
        functools.partial(_cross_prompt_kernel, dc=dc, dh=d // MEM_HEADS),
        grid=(batch, nj),
        in_specs=[pl.BlockSpec((tm, d), rows), pl.BlockSpec((tm, dc), rows), pl.BlockSpec((tm, da), rows),
                  pl.BlockSpec(w_out.shape, const), pl.BlockSpec((1, d), const),
                  pl.BlockSpec(w_cq.shape, const),
                  pl.BlockSpec((n_mem, d), memb), pl.BlockSpec((n_mem, d), memb),
                  pl.BlockSpec(w_co.shape, const)],
        out_specs=pl.BlockSpec((tm, d), rows),
        out_shape=jax.ShapeDtypeStruct(x2.shape, F32),
        scratch_shapes=[pltpu.VMEM((tm, d), BF16)],
        compiler_params=_params("arbitrary", "arbitrary"),
    )(x2, yconv, yattn, w_out, nw, w_cq, mk, mv, w_co)


def _outproj_q_sample_kernel(x_ref, yc_ref, ya_ref, wo_ref, nw_ref, wq_ref, x1_ref, qc_ref, *, dc, dh):
    x1 = x_ref[...] + _dot(yc_ref[...], wo_ref[0:dc, :]) + _dot(ya_ref[...], wo_ref[dc:, :])
    x1_ref[...] = x1
    h = _rms(x1, nw_ref[...]).astype(BF16)
    qc_ref[...] = (_dot(h, wq_ref[...]) * (dh ** -0.5)).astype(BF16)


def _outproj_q_sample(xt, yconv, yattn, w_out, nw, w_cq):
    d = xt.shape[1]
    return pl.pallas_call(
        functools.partial(_outproj_q_sample_kernel, dc=yconv.shape[1], dh=d // MEM_HEADS),
        out_shape=[jax.ShapeDtypeStruct(xt.shape, F32), jax.ShapeDtypeStruct(xt.shape, BF16)],
        compiler_params=pltpu.CompilerParams(vmem_limit_bytes=VMEM_LIMIT),
    )(xt, yconv, yattn, w_out, nw, w_cq)


def _cross_sample_kernel(q_ref, mk_ref, mv_ref, o_ref, *, ts):
    rq = q_ref.shape[0]
    ncols = mk_ref.shape[0]
    per_key = (rq // (MEM_HEADS * ts)) * MEM_HEADS
    nh = per_key // MEM_HEADS
    rows = MEM_HEADS * ts
    s2 = _dot_nt(q_ref[...], mk_ref[...].astype(BF16))
    s = s2[0:rows]
    for half in range(1, nh):
        s = s + pltpu.roll(s2[half * rows:(half + 1) * rows], ncols - half * MEM_HEADS, axis=1)
    col = lax.broadcasted_iota(jnp.int32, (rows, ncols), 1) & (per_key - 1)
    head = lax.broadcasted_iota(jnp.int32, (rows, ncols), 0) >> (ts.bit_length() - 1)
    s = jnp.where(col == head, s, NEG)
    p = jnp.exp(s - jnp.max(s, axis=1, keepdims=True))
    l = jnp.sum(p, axis=1, keepdims=True)
    lhs = jnp.concatenate([p] + [pltpu.roll(p, half * MEM_HEADS, axis=1) for half in range(1, nh)], axis=0)
    o2 = _dot(lhs.astype(BF16), mv_ref[...].astype(BF16))
    o = jnp.concatenate([o2[half * rows:(half + 1) * rows] for half in range(nh)], axis=1) / l
    o_ref[...] = o.astype(o_ref.dtype)


def _cross_sample(q2, mk, mv, *, row_offset, ts):
    bs, rq, _ = q2.shape
    ncols = mk.shape[1]
    rows = MEM_HEADS * ts
    nh = rq // rows
    per_b = lambda b: (b, 0, 0)
    mem_b = lambda b: (row_offset + b, 0, 0)
    return pl.pallas_call(
        functools.partial(_cross_sample_kernel, ts=ts),
        grid=(bs,),
        in_specs=[pl.BlockSpec((None, rq, LANES), per_b), pl.BlockSpec((None, ncols, LANES), mem_b),
                  pl.BlockSpec((None, ncols, LANES), mem_b)],
        out_specs=pl.BlockSpec((None, rows, nh * LANES), per_b),
        out_shape=jax.ShapeDtypeStruct((bs, rows, nh * LANES), BF16),
        compiler_params=_params("arbitrary"),
    )(q2, mk, mv)


def _silu(g):
    return g * (1.0 / (1.0 + jnp.exp(-g)))


def _ffn_prompt_kernel(x_ref, nw_ref, wu_ref, cw_ref, cb_ref, wd_ref, fw_ref, y_ref, st_ref,
                       up_scr, carry_scr, *, dff, final):
    j = pl.program_id(1)
    tm = x_ref.shape[0]
    x = x_ref[...]
    h = _rms(x, nw_ref[...]).astype(BF16)

    @pl.when(j == 0)
    def _zero_prefix():
        carry_scr[...] = jnp.zeros(carry_scr.shape, F32)

    def conv_half(lo):
        cols = slice(lo, lo + dff)
        up = _dot(h, wu_ref[:, cols])
        up_scr[0:SUBLANES, :] = carry_scr[:, cols]
        up_scr[SUBLANES:SUBLANES + tm, :] = up
        cw = cw_ref[:, cols]
        conv = (cb_ref[:, cols] + cw[0:1, :] * up_scr[SUBLANES - 2:SUBLANES - 2 + tm, :]
                + cw[1:2, :] * up_scr[SUBLANES - 1:SUBLANES - 1 + tm, :] + cw[2:3, :] * up)
        carry_scr[:, cols] = up[tm - SUBLANES:tm, :]
        st_ref[:, cols] = up[tm - 2:tm, :]
        return conv

    g = conv_half(0)
    u = conv_half(dff)
    x3 = x + _dot((_silu(g) * u).astype(BF16), wd_ref[...])
    y_ref[...] = _rms(x3, fw_ref[...]) if final else x3


def _ffn_prompt(x2, nw, w_up, cw, cb, w_down, fw, *, batch, seq, final):
    d = x2.shape[1]
    dff = w_down.shape[0]
    tm = min(FFN_ROW_TILE, seq)
    nj = seq // tm
    rows = lambda b, j: (b * nj + j, 0)
    const = lambda b, j: (0, 0)
    resident = lambda shape: pl.BlockSpec(shape, const, pipeline_mode=pl.Buffered(1))
    return pl.pallas_call(
        functools.partial(_ffn_prompt_kernel, dff=dff, final=final),
        grid=(batch, nj),
        in_specs=[pl.BlockSpec((tm, d), rows), pl.BlockSpec((1, d), const),
                  resident(w_up.shape), pl.BlockSpec(cw.shape, const),
                  pl.BlockSpec((1, 2 * dff), const), resident(w_down.shape),
                  pl.BlockSpec((1, d), const)],
        out_specs=[pl.BlockSpec((tm, d), rows),
                   pl.BlockSpec((None, CONV_W - 1, 2 * dff), lambda b, j: (b, 0, 0))],
        out_shape=[jax.ShapeDtypeStruct(x2.shape, F32),
                   jax.ShapeDtypeStruct((batch, CONV_W - 1, 2 * dff), F32)],
        scratch_shapes=[pltpu.VMEM((tm + SUBLANES, dff), F32),
                        pltpu.VMEM((SUBLANES, 2 * dff), F32)],
        compiler_params=_params("arbitrary", "arbitrary"),
    )(x2, nw, w_up, cw, cb, w_down, fw)


def _ffn_sample_kernel(x1_ref, o_ref, wc_ref, nw_ref, wu_ref, cw_ref, cb_ref, wd_ref, fw_ref,
                       s0_ref, s1_ref, y_ref, st_ref, hid_scr, *, dff, ts, bs, final):
    x2 = x1_ref[...] + _dot(o_ref[...], wc_ref[...])
    h = _rms(x2, nw_ref[...]).astype(BF16)

    def conv_half(lo):
        cols = slice(lo, lo + dff)
        up = _dot(h, wu_ref[:, cols])
        slabs = [s0_ref[:, cols], s1_ref[:, cols]] + [up[t * bs:(t + 1) * bs, :] for t in range(ts)]
        st_ref[0, :, cols] = slabs[-2]
        st_ref[1, :, cols] = slabs[-1]
        return _conv_time_major(slabs, cw_ref[:, cols], cb_ref[:, cols])

    g = conv_half(0)
    u = conv_half(dff)
    for t in range(ts):
        hid_scr[t * bs:(t + 1) * bs, :] = (_silu(g[t]) * u[t]).astype(BF16)
    x3 = x2 + _dot(hid_scr[...], wd_ref[...])
    y_ref[...] = _rms(x3, fw_ref[...]) if final else x3


def _ffn_sample(x1, o, w_co, nw, w_up, cw, cb, w_down, fw, s0, s1, *, ts, bs, final):
    dff = w_down.shape[0]
    return pl.pallas_call(
        functools.partial(_ffn_sample_kernel, dff=dff, ts=ts, bs=bs, final=final),
        out_shape=[jax.ShapeDtypeStruct(x1.shape, F32),
                   jax.ShapeDtypeStruct((CONV_W - 1, bs, 2 * dff), F32)],
        scratch_shapes=[pltpu.VMEM((ts * bs, dff), BF16)],
        compiler_params=pltpu.CompilerParams(vmem_limit_bytes=VMEM_LIMIT),
    )(x1, o, w_co, nw, w_up, cw, cb, w_down, fw, s0, s1)


def kernel(x_prompt, x_sample, mem_prompt, cache_k, cache_v, page_table, state_conv_mix, state_conv_ffn, cache_mem_k, cache_mem_v, rel_bias, norm_mix_w, w_in, conv_mix_w, conv_mix_b, lambda_q1, lambda_k1, lambda_q2, lambda_k2, subln_w, w_out, norm_cross_w, norm_mem_w, w_cq, w_ckv, w_co, norm_ffn_w, w_up, conv_ffn_w, conv_ffn_b, w_down, norm_final_w):
    depth = w_in.shape[0]
    bp, sp, d = x_prompt.shape
    bs, ts, _ = x_sample.shape
    n_mem = mem_prompt.shape[1]
    n_phys, page = cache_k.shape[1], cache_k.shape[2]
    n_pages = page_table.shape[1]
    past_len = n_pages * page
    dc = conv_mix_w.shape[2]
    da = N_HEADS * HEAD_DV
    dff = w_down.shape[1]
    dh = d // MEM_HEADS
    assert page == LANES and CONV_W - 1 <= ts <= SUBLANES and ts & (ts - 1) == 0
    assert (N_HEADS * ts) % SUBLANES == 0 and cache_k.shape[3:] == (N_HEADS, HEAD_DV)
    rows = 2 * SUBLANES

    assert _bucket_np(np.arange(REL_MAX_DIST, max(sp, past_len + ts) + 1)).min() == REL_BUCKETS - 1
    row = lambda a: a.reshape(1, -1).astype(F32)
    rel_bias = rel_bias.astype(F32)
    cache_k2 = cache_k.reshape(depth * n_phys * page * N_HEADS, HEAD_DV)
    cache_v2 = cache_v.reshape(depth * n_phys * page * N_HEADS, HEAD_DV)
    mem_rows = lambda a: a.reshape(depth * bs, n_mem, MEM_HEADS, dh // LANES, LANES).transpose(
        0, 1, 3, 2, 4).reshape(depth * bs, n_mem * (dh // LANES) * MEM_HEADS, LANES)
    mem_k3 = mem_rows(cache_mem_k)
    mem_v3 = mem_rows(cache_mem_v)

    xp = x_prompt.reshape(bp * sp, d)
    xs = x_sample.transpose(1, 0, 2).reshape(ts * bs, d)
    mem2 = mem_prompt.reshape(bp * n_mem, d)
    outs = [[] for _ in range(10)]
    for l in range(depth):
        lam_init = 0.8 - 0.6 * math.exp(-0.3 * l)
        final = l == depth - 1
        w_in_b, w_out_b = w_in[l].astype(BF16), w_out[l].astype(BF16)
        w_cq_b, w_ckv_b, w_co_b = w_cq[l].astype(BF16), w_ckv[l].astype(BF16), w_co[l].astype(BF16)
        w_up_b, w_down_b = w_up[l].astype(BF16), w_down[l].astype(BF16)
        lam_args = (row(lambda_q1[l]), row(lambda_k1[l]), row(lambda_q2[l]), row(lambda_k2[l]),
                    row(subln_w[l]))

        yconv_p, q_p, kf_p, vf_p, kb_p, vb_p, cmix_p = _mix_in_prompt(
            xp, row(norm_mix_w[l]), w_in_b, conv_mix_w[l], row(conv_mix_b[l]),
            batch=bp, seq=sp, dc=dc, da=da)
        yattn_p = _attn_prompt(rel_bias, q_p, kb_p, vb_p, *lam_args[:4], subln_w[l].reshape(-1, 1).astype(F32),
                               batch=bp, seq=sp, lam_init=lam_init)
        mkf, mvf, mkb, mvb = _mem_kv(mem2, row(norm_mem_w[l]), w_ckv_b, batch=bp, n_mem=n_mem)
        x2_p = _cross_prompt(xp, yconv_p, yattn_p, w_out_b, row(norm_cross_w[l]), w_cq_b, mkb, mvb,
                             w_co_b, batch=bp, seq=sp, n_mem=n_mem)
        xp, cffn_p = _ffn_prompt(x2_p, row(norm_ffn_w[l]), w_up_b, conv_ffn_w[l], row(conv_ffn_b[l]),
                                 w_down_b, row(norm_final_w), batch=bp, seq=sp, final=final)

        yconv_s, qm_s, kf_s, vf_s, cmix_s = _mix_in_sample(
            xs, row(norm_mix_w[l]), w_in_b, conv_mix_w[l], row(conv_mix_b[l]),
            state_conv_mix[l][:, 0], state_conv_mix[l][:, 1], dc=dc, da=da, ts=ts, bs=bs)
        q_all = qm_s.reshape(2, ts, bs, N_HEADS, HEAD_DV).transpose(2, 0, 3, 1, 4)
        q_all = q_all.reshape(bs, 2 * N_HEADS * ts, HEAD_DV)
        new_rows = lambda a: jnp.pad(
            a.reshape(ts, bs, N_HEADS * HEAD_DV).transpose(1, 0, 2).reshape(bs, ts * N_HEADS, HEAD_DV),
            ((0, 0), (0, LANES - ts * N_HEADS), (0, 0))).astype(BF16)
        yattn_s = _attn_sample(page_table, rel_bias, q_all, new_rows(kf_s), new_rows(vf_s), *lam_args,
                               cache_k2, cache_v2, page=page, page_offset=l * n_phys, ts=ts,
                               lam_init=lam_init)
        yattn_s = yattn_s.reshape(bs, N_HEADS, ts, HEAD_DV).transpose(2, 0, 1, 3)
        yattn_s = yattn_s.reshape(ts * bs, da).astype(BF16)
        x1_s, qc_s = _outproj_q_sample(xs, yconv_s, yattn_s, w_out_b, row(norm_cross_w[l]), w_cq_b)
        q2 = qc_s.reshape(ts, bs, MEM_HEADS, dh // LANES, LANES).transpose(1, 3, 2, 0, 4)
        q2 = q2.reshape(bs, (dh // LANES) * MEM_HEADS * ts, LANES)
        o_b = _cross_sample(q2, mem_k3, mem_v3, row_offset=l * bs, ts=ts)
        o_s = o_b.reshape(bs, MEM_HEADS, ts, dh).transpose(2, 0, 1, 3).reshape(ts * bs, d)
        xs, cffn_s = _ffn_sample(x1_s, o_s, w_co_b, row(norm_ffn_w[l]), w_up_b, conv_ffn_w[l],
                                 row(conv_ffn_b[l]), w_down_b, row(norm_final_w),
                                 state_conv_ffn[l][:, 0], state_conv_ffn[l][:, 1],
                                 ts=ts, bs=bs, final=final)

        t2b = lambda a: a.reshape(ts, bs, N_HEADS, HEAD_DV).transpose(1, 0, 2, 3)
        for lst, val in zip(outs, (
                kf_p.reshape(bp, sp, N_HEADS, 2 * HEAD_DK), vf_p.reshape(bp, sp, N_HEADS, HEAD_DV),
                t2b(kf_s), t2b(vf_s), cmix_p, cmix_s.transpose(1, 0, 2), cffn_p,
                cffn_s.transpose(1, 0, 2), mkf.reshape(bp, n_mem, MEM_HEADS, dh),
                mvf.reshape(bp, n_mem, MEM_HEADS, dh))):
            lst.append(val)

    y_prompt = xp.reshape(bp, sp, d)
    y_sample = xs.reshape(ts, bs, d).transpose(1, 0, 2)
    return (y_prompt, y_sample) + tuple(jnp.stack(o) for o in outs)
```

```python
import functools
import math

import numpy as np
import jax
import jax.numpy as jnp
from jax import lax
from jax.experimental import pallas as pl
from jax.experimental.pallas import tpu as pltpu

F32 = jnp.float32
BF16 = jnp.bfloat16

EPS = 1e-6
NEG = -1e30
LANES = 128
SUBLANES = 8
N_HEADS = 4
HEAD_DK = 64
HEAD_DV = 2 * HEAD_DK
MEM_HEADS = 4
CONV_W = 3
REL_BUCKETS = 32
REL_MAX_EXACT = 16
REL_MAX_DIST = 128
VMEM_LIMIT = 56 * 1024 * 1024

ATTN_TILE = 512
QUERY_CHUNK = 256
ONES_ROWS = 16
LOG2E = 1.4426950408889634
ROW_TILE = 512
FFN_ROW_TILE = 512
PAGES_PER_GROUP = 8
PAGE_SLOTS = 3


def _params(*sem):
    return pltpu.CompilerParams(dimension_semantics=sem, vmem_limit_bytes=VMEM_LIMIT)


def _rms(x, w):
    return x * lax.rsqrt(jnp.mean(x * x, axis=-1, keepdims=True) + EPS) * w


def _dot(a, b):
    return jnp.dot(a, b, preferred_element_type=F32)


def _dot_nt(a, b):
    return lax.dot_general(a, b, (((1,), (1,)), ((), ())), preferred_element_type=F32)


def _lam(lq1, lk1, lq2, lk2, lam_init):
    return (jnp.exp(jnp.sum(lq1[...] * lk1[...], axis=-1, keepdims=True))
            - jnp.exp(jnp.sum(lq2[...] * lk2[...], axis=-1, keepdims=True)) + lam_init)


def _bucket_np(rel):
    n = np.maximum(rel, 0)
    nf = np.maximum(n, 1).astype(np.float32)
    large = REL_MAX_EXACT + (np.log(nf / np.float32(REL_MAX_EXACT))
                             / np.float32(math.log(REL_MAX_DIST / REL_MAX_EXACT))
                             * np.float32(REL_BUCKETS - REL_MAX_EXACT)).astype(np.int32)
    large = np.minimum(large, REL_BUCKETS - 1)
    return np.where(n < REL_MAX_EXACT, n, large).astype(np.int32)


def _bucket_starts():
    buckets = _bucket_np(np.arange(REL_MAX_DIST + 1))
    assert (np.diff(buckets) >= 0).all() and buckets[-1] == REL_BUCKETS - 1
    return [int(np.argmax(buckets >= k)) for k in range(REL_BUCKETS)]


def _rel_bias_tile(rel, rb_ref, h):
    far = rb_ref[REL_BUCKETS - 1, h]
    val = jnp.full(rel.shape, rb_ref[0, h] - far, F32)
    for k, start in enumerate(_bucket_starts()):
        if k > 0:
            val = jnp.where(rel >= start, rb_ref[k, h] - far, val)
    return val


def _softmax_update(chunks, m_prev, l_prev):
    mx = chunks[0]
    for c in chunks[1:]:
        mx = jnp.maximum(mx, c)
    m_next = jnp.maximum(m_prev, jnp.max(mx, axis=1, keepdims=True))
    ps = [jnp.exp(c - m_next) for c in chunks]
    sm = ps[0]
    for p in ps[1:]:
        sm = sm + p
    alpha = jnp.exp(m_prev - m_next)
    l_next = alpha * l_prev + jnp.sum(sm, axis=1, keepdims=True)
    return ps, m_next, l_next, alpha


def _head_out(acc1, l1, acc2, l2, lam, sw, lam_init):
    o = acc1 / l1 - lam * (acc2 / l2)
    return _rms(o, sw) * (1.0 - lam_init)


def _mix_in_prompt_kernel(x_ref, nw_ref, w_ref, cw_ref, cb_ref,
                          yconv_ref, q_ref, kf_ref, vf_ref, kb_ref, vb_ref, st_ref,
                          pre_scr, *, dc, da):
    j = pl.program_id(1)
    tm = x_ref.shape[0]
    h = _rms(x_ref[...], nw_ref[...]).astype(BF16)

    def proj(lo, width):
        return _dot(h, w_ref[:, lo:lo + width])

    @pl.when(j == 0)
    def _zero_prefix():
        pre_scr[0:SUBLANES, :] = jnp.zeros((SUBLANES, dc), F32)

    pre = proj(dc, dc) * proj(2 * dc, dc)
    pre_scr[SUBLANES:SUBLANES + tm, :] = pre
    cw = cw_ref[...]
    conv = (cb_ref[...] + cw[0:1, :] * pre_scr[SUBLANES - 2:SUBLANES - 2 + tm, :]
            + cw[1:2, :] * pre_scr[SUBLANES - 1:SUBLANES - 1 + tm, :] + cw[2:3, :] * pre)
    yconv_ref[...] = (proj(0, dc) * conv).astype(BF16)
    st_ref[...] = pre[tm - 2:tm, :]
    pre_scr[0:SUBLANES, :] = pre[tm - SUBLANES:tm, :]

    q = proj(3 * dc, da) * (HEAD_DK ** -0.5 * LOG2E)
    k = proj(3 * dc + da, da)
    v = proj(3 * dc + 2 * da, da)
    ones_rows = (lax.broadcasted_iota(jnp.int32, (ONES_ROWS, tm), 0) == 0).astype(BF16)
    for hd in range(N_HEADS):
        cols = slice(hd * HEAD_DV, (hd + 1) * HEAD_DV)
        kb_ref[hd] = k[:, cols].astype(BF16)
        kf_ref[pl.ds(hd, tm, stride=N_HEADS), :] = k[:, cols]
        vf_ref[pl.ds(hd, tm, stride=N_HEADS), :] = v[:, cols]
        q_ref[hd] = q[:, cols].T.astype(BF16)
        vb_ref[hd, 0:HEAD_DV, :] = v[:, cols].T.astype(BF16)
        vb_ref[hd, HEAD_DV:, :] = ones_rows


def _mix_in_prompt(x2, nw, w_in, cw, cb, *, batch, seq, dc, da):
    d = x2.shape[1]
    tm = min(ATTN_TILE, seq)
    nj = seq // tm
    rows = lambda b, j: (b * nj + j, 0)
    const = lambda b, j: (0, 0)
    tposed = lambda r: pl.BlockSpec((None, N_HEADS, None, r, tm), lambda b, j: (b, 0, j, 0, 0))
    tposed_shape = lambda r: jax.ShapeDtypeStruct((batch, N_HEADS, nj, r, tm), BF16)
    n = batch * seq
    return pl.pallas_call(
        functools.partial(_mix_in_prompt_kernel, dc=dc, da=da),
        grid=(batch, nj),
        in_specs=[pl.BlockSpec((tm, d), rows),
                  pl.BlockSpec((1, d), const),
                  pl.BlockSpec(w_in.shape, const),
                  pl.BlockSpec(cw.shape, const),
                  pl.BlockSpec((1, dc), const)],
        out_specs=[pl.BlockSpec((tm, dc), rows),
                   tposed(HEAD_DV),
                   pl.BlockSpec((tm * N_HEADS, HEAD_DV), rows),
                   pl.BlockSpec((tm * N_HEADS, HEAD_DV), rows),
                   pl.BlockSpec((None, N_HEADS, tm, HEAD_DV), lambda b, j: (b, 0, j, 0)),
                   tposed(HEAD_DV + ONES_ROWS),
                   pl.BlockSpec((None, CONV_W - 1, dc), lambda b, j: (b, 0, 0))],
        out_shape=[jax.ShapeDtypeStruct((n, dc), BF16),
                   tposed_shape(HEAD_DV),
                   jax.ShapeDtypeStruct((n * N_HEADS, HEAD_DV), F32),
                   jax.ShapeDtypeStruct((n * N_HEADS, HEAD_DV), F32),
                   jax.ShapeDtypeStruct((batch, N_HEADS, seq, HEAD_DV), BF16),
                   tposed_shape(HEAD_DV + ONES_ROWS),
                   jax.ShapeDtypeStruct((batch, CONV_W - 1, dc), F32)],
        scratch_shapes=[pltpu.VMEM((tm + SUBLANES, dc), F32)],
        compiler_params=_params("arbitrary", "arbitrary"),
    )(x2, nw, w_in, cw, cb)


def _conv_time_major(slabs, cw, cb):
    return [cb + cw[0:1, :] * slabs[t] + cw[1:2, :] * slabs[t + 1] + cw[2:3, :] * slabs[t + 2]
            for t in range(len(slabs) - 2)]


def _mix_in_sample_kernel(x_ref, nw_ref, w_ref, cw_ref, cb_ref, s0_ref, s1_ref,
                          yconv_ref, qm_ref, kf_ref, vf_ref, st_ref, *, dc, da, ts, bs):
    h = _rms(x_ref[...], nw_ref[...]).astype(BF16)

    def proj(lo, width):
        return _dot(h, w_ref[:, lo:lo + width])

    pre = proj(dc, dc) * proj(2 * dc, dc)
    gate = proj(0, dc)
    slabs = [s0_ref[...], s1_ref[...]] + [pre[t * bs:(t + 1) * bs, :] for t in range(ts)]
    conv = _conv_time_major(slabs, cw_ref[...], cb_ref[...])
    for t in range(ts):
        yconv_ref[t * bs:(t + 1) * bs, :] = (gate[t * bs:(t + 1) * bs, :] * conv[t]).astype(BF16)
    st_ref[0] = slabs[-2]
    st_ref[1] = slabs[-1]

    q = proj(3 * dc, da) * (HEAD_DK ** -0.5)
    lane = lax.broadcasted_iota(jnp.int32, q.shape, 1) % HEAD_DV
    qm_ref[0] = jnp.where(lane < HEAD_DK, q, 0.0).astype(BF16)
    qm_ref[1] = jnp.where(lane >= HEAD_DK, q, 0.0).astype(BF16)
    kf_ref[...] = proj(3 * dc + da, da)
    vf_ref[...] = proj(3 * dc + 2 * da, da)


def _mix_in_sample(xt, nw, w_in, cw, cb, s0, s1, *, dc, da, ts, bs):
    n = ts * bs
    return pl.pallas_call(
        functools.partial(_mix_in_sample_kernel, dc=dc, da=da, ts=ts, bs=bs),
        out_shape=[jax.ShapeDtypeStruct((n, dc), BF16),
                   jax.ShapeDtypeStruct((2, n, da), BF16),
                   jax.ShapeDtypeStruct((n, da), F32),
                   jax.ShapeDtypeStruct((n, da), F32),
                   jax.ShapeDtypeStruct((CONV_W - 1, bs, dc), F32)],
        compiler_params=pltpu.CompilerParams(vmem_limit_bytes=VMEM_LIMIT),
    )(xt, nw, w_in, cw, cb, s0, s1)


def _mem_kv_kernel(m_ref, nw_ref, w_ref, kf_ref, vf_ref, kb_ref, vb_ref, *, d):
    n_mem = m_ref.shape[0]
    dh = d // MEM_HEADS
    nh = dh // LANES
    h = _rms(m_ref[...], nw_ref[...]).astype(BF16)
    k = _dot(h, w_ref[:, 0:d])
    kb_ref[...] = k.astype(BF16)
    v = _dot(h, w_ref[:, d:2 * d])
    vb_ref[...] = v.astype(BF16)
    for hd in range(MEM_HEADS):
        for half in range(nh):
            rows = pl.ds(half * MEM_HEADS + hd, n_mem, stride=nh * MEM_HEADS)
            cols = slice(hd * dh + half * LANES, hd * dh + (half + 1) * LANES)
            kf_ref[rows, :] = k[:, cols]
            vf_ref[rows, :] = v[:, cols]


def _mem_kv(mem2, nw, w_ckv, *, batch, n_mem):
    d = mem2.shape[1]
    rows = lambda b: (b, 0)
    const = lambda b: (0, 0)
    n = batch * n_mem
    return pl.pallas_call(
        functools.partial(_mem_kv_kernel, d=d),
        grid=(batch,),
        in_specs=[pl.BlockSpec((n_mem, d), rows), pl.BlockSpec((1, d), const),
                  pl.BlockSpec(w_ckv.shape, const)],
        out_specs=[pl.BlockSpec((n_mem * d // LANES, LANES), rows)] * 2 + [pl.BlockSpec((n_mem, d), rows)] * 2,
        out_shape=[jax.ShapeDtypeStruct((n * d // LANES, LANES), F32)] * 2
                  + [jax.ShapeDtypeStruct((n, d), BF16)] * 2,
        compiler_params=_params("arbitrary"),
    )(mem2, nw, w_ckv)


def _attn_prompt_kernel(rb_ref, q_ref, k_ref, vt_ref, lq1, lk1, lq2, lk2, sw_ref, o_ref,
                        qa_scr, ma_scr, acca_scr, qb_scr, mb_scr, accb_scr, bd_scr, bs_scr, sa_scr, sb_scr,
                        *, T, lam_init):
    nb = T // LANES
    n_tiles = q_ref.shape[0]

    def _assemble_bias_tiles():
        h = pl.program_id(1)
        rel = (lax.broadcasted_iota(jnp.int32, (LANES, LANES), 1)
               - lax.broadcasted_iota(jnp.int32, (LANES, LANES), 0))
        d0 = jnp.where(rel >= 0, _rel_bias_tile(jnp.maximum(rel, 0), rb_ref, h) * LOG2E, NEG)
        d1 = _rel_bias_tile(rel + LANES, rb_ref, h) * LOG2E
        zero = jnp.zeros((LANES, LANES), F32)
        neg = jnp.full((LANES, LANES), NEG, F32)
        for bi in range(nb):
            for bj in range(nb):
                rs = slice(bi * LANES, (bi + 1) * LANES)
                cs = slice(bj * LANES, (bj + 1) * LANES)
                bd_scr[rs, cs] = d0 if bi == bj else d1 if bj == bi + 1 else zero if bj > bi else neg
                bs_scr[rs, cs] = d1 if (bj == 0 and bi == nb - 1) else zero

    _assemble_bias_tiles()
    n_chunks = 2 * T // QUERY_CHUNK

    def prep(st, qi):
        qp, m, acc = st
        qt = q_ref[qi].astype(F32)
        sub = lax.broadcasted_iota(jnp.int32, qt.shape, 0)
        qp[:, 0:T] = jnp.where(sub < HEAD_DK, qt, 0.0).astype(BF16)
        qp[:, T:2 * T] = jnp.where(sub >= HEAD_DK, qt, 0.0).astype(BF16)
        m[...] = jnp.full(m.shape, -jnp.inf, F32)
        acc[...] = jnp.zeros(acc.shape, F32)

    def chunk_scores(kh, qp, dst, c):
        dst[c] = _dot(kh, qp[:, c * QUERY_CHUNK:(c + 1) * QUERY_CHUNK])

    def key_block(j):
        return k_ref[pl.ds(pl.multiple_of(j * T, T), T), :]

    def stage(st, j, src, bias_ref, nxt=None):
        _, m_scr, acc_scr = st
        if nxt is not None:
            j_next, qp_next, dst = nxt
            kh_next = key_block(j_next)
        for c in range(n_chunks):
            if nxt is not None and dst is not src:
                chunk_scores(kh_next, qp_next, dst, c)
            cs = slice(c * QUERY_CHUNK, (c + 1) * QUERY_CHUNK)
            lo = (c * QUERY_CHUNK) % T
            nk = min(T, lo + QUERY_CHUNK) if bias_ref is bd_scr else T
            s = src[c, 0:nk, :]
            if bias_ref is not None:
                s = s + bias_ref[0:nk, lo:lo + QUERY_CHUNK]
            m_prev = m_scr[:, cs]
            m_next = jnp.maximum(m_prev, jnp.max(s, axis=0, keepdims=True))
            p = jnp.exp2(s - m_next).astype(BF16)
            if nxt is not None and dst is src:
                chunk_scores(kh_next, qp_next, dst, c)
            acc_scr[:, cs] = (jnp.exp2(m_prev - m_next) * acc_scr[:, cs]
                              + _dot(vt_ref[j, :, 0:nk], p))
            m_scr[:, cs] = m_next

    def finalize(st, qi):
        _, _, acc_scr = st
        lam = _lam(lq1, lk1, lq2, lk2, lam_init)
        acc = acc_scr[0:HEAD_DV, :]
        l = acc_scr[HEAD_DV:HEAD_DV + 1, :]
        o = acc[:, 0:T] / l[:, 0:T] - lam * (acc[:, T:2 * T] / l[:, T:2 * T])
        y = o * lax.rsqrt(jnp.mean(o * o, axis=0, keepdims=True) + EPS) * sw_ref[...] * (1.0 - lam_init)
        o_ref[pl.ds(pl.multiple_of(qi * T, T), T), :] = y.T.astype(o_ref.dtype)

    even = (qa_scr, ma_scr, acca_scr)
    odd = (qb_scr, mb_scr, accb_scr)
    prep(even, 0)
    kh0 = key_block(0)
    for c in range(n_chunks):
        chunk_scores(kh0, qa_scr, sb_scr, c)

    def tile_pair(t, carry):
        a = 2 * t
        b = a + 1

        @pl.when(t >= 1)
        def _even_first_block():
            stage(even, 0, sb_scr, None, (1, qa_scr, sa_scr))

        def even_far_pair(i, c):
            j = 1 + 2 * i
            stage(even, j, sa_scr, None, (j + 1, qa_scr, sb_scr))
            stage(even, j + 1, sb_scr, None, (j + 2, qa_scr, sa_scr))
            return c

        lax.fori_loop(0, jnp.maximum(t - 1, 0), even_far_pair, 0)
        prep(odd, b)

        @pl.when(t >= 1)
        def _even_last_blocks():
            stage(even, a - 1, sa_scr, bs_scr, (a, qa_scr, sb_scr))
            stage(even, a, sb_scr, bd_scr, (0, qb_scr, sa_scr))

        @pl.when(t == 0)
        def _even_only_block():
            stage(even, 0, sb_scr, bd_scr, (0, qb_scr, sa_scr))

        def odd_far_pair(i, c):
            j = 2 * i
            stage(odd, j, sa_scr, None, (j + 1, qb_scr, sb_scr))
            stage(odd, j + 1, sb_scr, None, (j + 2, qb_scr, sa_scr))
            return c

        lax.fori_loop(0, t, odd_far_pair, 0)
        finalize(even, a)
        prep(even, jnp.minimum(a + 2, n_tiles - 1))
        stage(odd, b - 1, sa_scr, bs_scr, (b, qb_scr, sb_scr))
        stage(odd, b, sb_scr, bd_scr, (0, qa_scr, sb_scr))
        finalize(odd, b)
        return carry

    lax.fori_loop(0, n_tiles // 2, tile_pair, 0)


def _attn_prompt(rel_bias, qt, kb, vt, lq1, lk1, lq2, lk2, sw_col, *, batch, seq, lam_init):
    T = qt.shape[-1]
    nq = seq // T
    assert nq % 2 == 0, "query tiles run in (even, odd) pairs"
    da = N_HEADS * HEAD_DV
    vec = lambda b, h: (0, 0)
    return pl.pallas_call(
        functools.partial(_attn_prompt_kernel, T=T, lam_init=lam_init),
        grid=(batch, N_HEADS),
        in_specs=[pl.BlockSpec(memory_space=pltpu.SMEM),
                  pl.BlockSpec((None, None, nq, HEAD_DV, T), lambda b, h: (b, h, 0, 0, 0)),
                  pl.BlockSpec((None, None, seq, HEAD_DV), lambda b, h: (b, h, 0, 0)),
                  pl.BlockSpec((None, None, nq, vt.shape[3], T), lambda b, h: (b, h, 0, 0, 0)),
                  pl.BlockSpec((1, HEAD_DK), vec), pl.BlockSpec((1, HEAD_DK), vec),
                  pl.BlockSpec((1, HEAD_DK), vec), pl.BlockSpec((1, HEAD_DK), vec),
                  pl.BlockSpec((HEAD_DV, 1), vec)],
        out_specs=pl.BlockSpec((seq, HEAD_DV), lambda b, h: (b, h)),
        out_shape=jax.ShapeDtypeStruct((batch * seq, da), BF16),
        scratch_shapes=[pltpu.VMEM((HEAD_DV, 2 * T), BF16),
                        pltpu.VMEM((1, 2 * T), F32),
                        pltpu.VMEM((vt.shape[3], 2 * T), F32),
                        pltpu.VMEM((HEAD_DV, 2 * T), BF16),
                        pltpu.VMEM((1, 2 * T), F32),
                        pltpu.VMEM((vt.shape[3], 2 * T), F32),
                        pltpu.VMEM((T, T), F32),
                        pltpu.VMEM((T, T), F32),
                        pltpu.VMEM((2 * T // QUERY_CHUNK, T, QUERY_CHUNK), F32),
                        pltpu.VMEM((2 * T // QUERY_CHUNK, T, QUERY_CHUNK), F32)],
        compiler_params=_params("arbitrary", "arbitrary"),
    )(rel_bias, qt, kb, vt, lq1, lk1, lq2, lk2, sw_col)


def _attn_sample_kernel(pt_ref, rb_ref, q_ref, kn_ref, vn_ref, lq1, lk1, lq2, lk2, sw_ref, k_hbm, v_hbm,
                        o_ref, m_scr, l_scr, acc_scr, hm_scr, bl_scr, bn_scr, kbuf, vbuf, sems,
                        *, pages, page_offset, ts, lam_init):
    b = pl.program_id(0)
    nb = pl.num_programs(0)
    rows, pk = hm_scr.shape
    nk = bn_scr.shape[1]
    rm = rows // 2
    page = pk // N_HEADS
    groups = pt_ref.shape[1] // pages
    hbits = N_HEADS.bit_length() - 1
    tbits = ts.bit_length() - 1

    def group_copies(t, slot):
        bt = t // groups
        g0 = (t % groups) * pages
        out = []
        for i in range(pages):
            row0 = pl.multiple_of((page_offset + pt_ref[bt, g0 + i]) * pk, pk)
            dst = pl.ds(i * pk, pk)
            out.append(pltpu.make_async_copy(k_hbm.at[pl.ds(row0, pk)], kbuf.at[slot, dst], sems.at[0, slot]))
            out.append(pltpu.make_async_copy(v_hbm.at[pl.ds(row0, pk)], vbuf.at[slot, dst], sems.at[1, slot]))
        return out

    def start_group(t):
        @pl.when(t < nb * groups)
        def _():
            for cp in group_copies(t, t % PAGE_SLOTS):
                cp.start()

    @pl.when(b == 0)
    def _first_step():
        for t in range(PAGE_SLOTS - 1):
            start_group(jnp.int32(t))

        def tables(ncols, rel_of):
            r = lax.broadcasted_iota(jnp.int32, (rows, ncols), 0)
            c = lax.broadcasted_iota(jnp.int32, (rows, ncols), 1)
            rhead = (r >> tbits) & (N_HEADS - 1)
            same = rhead == (c & (N_HEADS - 1))
            rel = rel_of(r & (ts - 1), c >> hbits)
            bias = jnp.zeros((rows, ncols), F32)
            for h in range(N_HEADS):
                bias = jnp.where(rhead == h, _rel_bias_tile(jnp.maximum(rel, 0), rb_ref, h), bias)
            return same, rel, c >> hbits, bias

        same, _, _, bias = tables(pk, lambda tok, key: tok + page - key)
        hm_scr[...] = jnp.where(same, 0.0, NEG)
        bl_scr[...] = jnp.where(same, bias, NEG)
        same, rel, key, bias = tables(nk, lambda tok, key: tok - key)
        bn_scr[...] = jnp.where(same & (rel >= 0) & (key < ts), bias, NEG)

    m_scr[...] = jnp.full(m_scr.shape, -jnp.inf, F32)
    l_scr[...] = jnp.zeros(l_scr.shape, F32)
    acc_scr[...] = jnp.zeros(acc_scr.shape, F32)

    def update(scores, values):
        chunks, sizes = [], []
        for s in scores:
            n = s.shape[1] // LANES
            sizes.append(n)
            chunks += [s[:, c * LANES:(c + 1) * LANES] for c in range(n)]
        ps, m_next, l_next, alpha = _softmax_update(chunks, m_scr[...], l_scr[...])
        pv, at = None, 0
        for n, v in zip(sizes, values):
            p = jnp.concatenate([x.astype(BF16) for x in ps[at:at + n]], axis=1) if n > 1 \
                else ps[at].astype(BF16)
            at += n
            pv = _dot(p, v) if pv is None else pv + _dot(p, v)
        acc_scr[...] = alpha * acc_scr[...] + pv
        m_scr[...] = m_next
        l_scr[...] = l_next

    q = q_ref[...]

    def group_body(g, carry):
        t = b * groups + g
        slot = t % PAGE_SLOTS
        for cp in group_copies(t, slot):
            cp.wait()
        start_group(t + PAGE_SLOTS - 1)
        kg = kbuf.at[slot]
        vg = vbuf.at[slot]
        last = g == groups - 1
        scores = [_dot_nt(q, kg[i * pk:(i + 1) * pk, :].astype(BF16)) + hm_scr[...]
                  for i in range(pages - 1)]
        scores.append(_dot_nt(q, kg[(pages - 1) * pk:pages * pk, :].astype(BF16))
                      + jnp.where(last, bl_scr[...], hm_scr[...]))
        update(scores, [vg[i * pk:(i + 1) * pk, :].astype(BF16) for i in range(pages)])
        return carry

    lax.fori_loop(0, groups, group_body, 0)

    update([_dot_nt(q, kn_ref[...]) + bn_scr[...]], [vn_ref[...]])
    lam = _lam(lq1, lk1, lq2, lk2, lam_init)
    acc = acc_scr[...]
    l = l_scr[...]
    o_ref[...] = _head_out(acc[0:rm], l[0:rm], acc[rm:rows], l[rm:rows], lam, sw_ref[...], lam_init)


def _attn_sample(page_table, rel_bias, q_all, knew, vnew, lq1, lk1, lq2, lk2, sw, cache_k, cache_v,
                 *, page, page_offset, ts, lam_init):
    bs, n_pages = page_table.shape
    rows = q_all.shape[1]
    nk = knew.shape[1]
    pk = page * N_HEADS
    pages = math.gcd(PAGES_PER_GROUP, n_pages)
    vec = lambda b, pt: (0, 0)
    per_b = lambda b, pt: (b, 0, 0)
    grid_spec = pltpu.PrefetchScalarGridSpec(
        num_scalar_prefetch=1,
        grid=(bs,),
        in_specs=[pl.BlockSpec(memory_space=pltpu.SMEM),
                  pl.BlockSpec((None, rows, HEAD_DV), per_b),
                  pl.BlockSpec((None, nk, HEAD_DV), per_b),
                  pl.BlockSpec((None, nk, HEAD_DV), per_b),
                  pl.BlockSpec((1, HEAD_DK), vec), pl.BlockSpec((1, HEAD_DK), vec),
                  pl.BlockSpec((1, HEAD_DK), vec), pl.BlockSpec((1, HEAD_DK), vec),
                  pl.BlockSpec((1, HEAD_DV), vec),
                  pl.BlockSpec(memory_space=pl.ANY), pl.BlockSpec(memory_space=pl.ANY)],
        out_specs=pl.BlockSpec((None, rows // 2, HEAD_DV), per_b),
        scratch_shapes=[pltpu.VMEM((rows, LANES), F32),
                        pltpu.VMEM((rows, LANES), F32),
                        pltpu.VMEM((rows, HEAD_DV), F32),
                        pltpu.VMEM((rows, pk), F32),
                        pltpu.VMEM((rows, pk), F32),
                        pltpu.VMEM((rows, nk), F32),
                        pltpu.VMEM((PAGE_SLOTS, pages * pk, HEAD_DV), F32),
                        pltpu.VMEM((PAGE_SLOTS, pages * pk, HEAD_DV), F32),
                        pltpu.SemaphoreType.DMA((2, PAGE_SLOTS))],
    )
    return pl.pallas_call(
        functools.partial(_attn_sample_kernel, pages=pages, page_offset=page_offset, ts=ts,
                          lam_init=lam_init),
        grid_spec=grid_spec,
        out_shape=jax.ShapeDtypeStruct((bs, rows // 2, HEAD_DV), F32),
        compiler_params=_params("arbitrary"),
    )(page_table, rel_bias, q_all, knew, vnew, lq1, lk1, lq2, lk2, sw, cache_k, cache_v)


def _cross_heads(qc, mk_ref, mv_ref, o_scr, dh):
    for hd in range(MEM_HEADS):
        cols = slice(hd * dh, (hd + 1) * dh)
        s = _dot_nt(qc[:, cols], mk_ref[:, cols].astype(BF16))
        p = jnp.exp(s - jnp.max(s, axis=1, keepdims=True))
        o = _dot(p.astype(BF16), mv_ref[:, cols].astype(BF16)) / jnp.sum(p, axis=1, keepdims=True)
        o_scr[:, cols] = o.astype(BF16)


def _cross_prompt_kernel(x_ref, yc_ref, ya_ref, wo_ref, nw_ref, wq_ref, mk_ref, mv_ref, wc_ref,
                         o_ref, o_scr, *, dc, dh):
    x1 = x_ref[...] + _dot(yc_ref[...], wo_ref[0:dc, :]) + _dot(ya_ref[...], wo_ref[dc:, :])
    h = _rms(x1, nw_ref[...]).astype(BF16)
    qc = (_dot(h, wq_ref[...]) * (dh ** -0.5)).astype(BF16)
    _cross_heads(qc, mk_ref, mv_ref, o_scr, dh)
    o_ref[...] = x1 + _dot(o_scr[...], wc_ref[...])


def _cross_prompt(x2, yconv, yattn, w_out, nw, w_cq, mk, mv, w_co, *, batch, seq, n_mem):
    d = x2.shape[1]
    dc = yconv.shape[1]
    da = yattn.shape[1]
    tm = min(ROW_TILE, seq)
    nj = seq // tm
    rows = lambda b, j: (b * nj + j, 0)
    const = lambda b, j: (0, 0)
    memb = lambda b, j: (b, 0)
    return pl.pallas_call(
        functools.partial(_cross_prompt_kernel, dc=dc, dh=d // MEM_HEADS),
        grid=(batch, nj),
        in_specs=[pl.BlockSpec((tm, d), rows), pl.BlockSpec((tm, dc), rows), pl.BlockSpec((tm, da), rows),
                  pl.BlockSpec(w_out.shape, const), pl.BlockSpec((1, d), const),
                  pl.BlockSpec(w_cq.shape, const),
                  pl.BlockSpec((n_mem, d), memb), pl.BlockSpec((n_mem, d), memb),
                  pl.BlockSpec(w_co.shape, const)],
        out_specs=pl.BlockSpec((tm, d), rows),
        out_shape=jax.ShapeDtypeStruct(x2.shape, F32),
        scratch_shapes=[pltpu.VMEM((tm, d), BF16)],
        compiler_params=_params("arbitrary", "arbitrary"),
    )(x2, yconv, yattn, w_out, nw, w_cq, mk, mv, w_co)


def _outproj_q_sample_kernel(x_ref, yc_ref, ya_ref, wo_ref, nw_ref, wq_ref, x1_ref, qc_ref, *, dc, dh):
    x1 = x_ref[...] + _dot(yc_ref[...], wo_ref[0:dc, :]) + _dot(ya_ref[...], wo_ref[dc:, :])
    x1_ref[...] = x1
    h = _rms(x1, nw_ref[...]).astype(BF16)
    qc_ref[...] = (_dot(h, wq_ref[...]) * (dh ** -0.5)).astype(BF16)


def _outproj_q_sample(xt, yconv, yattn, w_out, nw, w_cq):
    d = xt.shape[1]
    return pl.pallas_call(
        functools.partial(_outproj_q_sample_kernel, dc=yconv.shape[1], dh=d // MEM_HEADS),
        out_shape=[jax.ShapeDtypeStruct(xt.shape, F32), jax.ShapeDtypeStruct(xt.shape, BF16)],
        compiler_params=pltpu.CompilerParams(vmem_limit_bytes=VMEM_LIMIT),
    )(xt, yconv, yattn, w_out, nw, w_cq)


def _cross_sample_kernel(q_ref, mk_ref, mv_ref, o_ref, *, ts):
    rq = q_ref.shape[0]
    ncols = mk_ref.shape[0]
    per_key = (rq // (MEM_HEADS * ts)) * MEM_HEADS
    nh = per_key // MEM_HEADS
    rows = MEM_HEADS * ts
    s2 = _dot_nt(q_ref[...], mk_ref[...].astype(BF16))
    s = s2[0:rows]
    for half in range(1, nh):
        s = s + pltpu.roll(s2[half * rows:(half + 1) * rows], ncols - half * MEM_HEADS, axis=1)
    col = lax.broadcasted_iota(jnp.int32, (rows, ncols), 1) & (per_key - 1)
    head = lax.broadcasted_iota(jnp.int32, (rows, ncols), 0) >> (ts.bit_length() - 1)
    s = jnp.where(col == head, s, NEG)
    p = jnp.exp(s - jnp.max(s, axis=1, keepdims=True))
    l = jnp.sum(p, axis=1, keepdims=True)
    lhs = jnp.concatenate([p] + [pltpu.roll(p, half * MEM_HEADS, axis=1) for half in range(1, nh)], axis=0)
    o2 = _dot(lhs.astype(BF16), mv_ref[...].astype(BF16))
    o = jnp.concatenate([o2[half * rows:(half + 1) * rows] for half in range(nh)], axis=1) / l
    o_ref[...] = o.astype(o_ref.dtype)


def _cross_sample(q2, mk, mv, *, row_offset, ts):
    bs, rq, _ = q2.shape
    ncols = mk.shape[1]
    rows = MEM_HEADS * ts
    nh = rq // rows
    per_b = lambda b: (b, 0, 0)
    mem_b = lambda b: (row_offset + b, 0, 0)
    return pl.pallas_call(
        functools.partial(_cross_sample_kernel, ts=ts),
        grid=(bs,),
        in_specs=[pl.BlockSpec((None, rq, LANES), per_b), pl.BlockSpec((None, ncols, LANES), mem_b),
                  pl.BlockSpec((None, ncols, LANES), mem_b)],
        out_specs=pl.BlockSpec((None, rows, nh * LANES), per_b),
        out_shape=jax.ShapeDtypeStruct((bs, rows, nh * LANES), BF16),
        compiler_params=_params("arbitrary"),
    )(q2, mk, mv)


def _silu(g):
    return g * (1.0 / (1.0 + jnp.exp(-g)))


def _ffn_prompt_kernel(x_ref, nw_ref, wu_ref, cw_ref, cb_ref, wd_ref, fw_ref, y_ref, st_ref,
                       up_scr, carry_scr, *, dff, final):
    j = pl.program_id(1)
    tm = x_ref.shape[0]
    x = x_ref[...]
    h = _rms(x, nw_ref[...]).astype(BF16)

    @pl.when(j == 0)
    def _zero_prefix():
        carry_scr[...] = jnp.zeros(carry_scr.shape, F32)

    def conv_half(lo):
        cols = slice(lo, lo + dff)
        up = _dot(h, wu_ref[:, cols])
        up_scr[0:SUBLANES, :] = carry_scr[:, cols]
        up_scr[SUBLANES:SUBLANES + tm, :] = up
        cw = cw_ref[:, cols]
        conv = (cb_ref[:, cols] + cw[0:1, :] * up_scr[SUBLANES - 2:SUBLANES - 2 + tm, :]
                + cw[1:2, :] * up_scr[SUBLANES - 1:SUBLANES - 1 + tm, :] + cw[2:3, :] * up)
        carry_scr[:, cols] = up[tm - SUBLANES:tm, :]
        st_ref[:, cols] = up[tm - 2:tm, :]
        return conv

    g = conv_half(0)
    u = conv_half(dff)
    hid = (_silu(g) * u).astype(BF16)
    rh = tm // 2
    for r in range(2):
        rows = slice(r * rh, (r + 1) * rh)
        x3 = x[rows] + _dot(hid[rows], wd_ref[...])
        y_ref[rows, :] = _rms(x3, fw_ref[...]) if final else x3


def _ffn_prompt(x2, nw, w_up, cw, cb, w_down, fw, *, batch, seq, final):
    d = x2.shape[1]
    dff = w_down.shape[0]
    tm = min(FFN_ROW_TILE, seq)
    nj = seq // tm
    rows = lambda b, j: (b * nj + j, 0)
    const = lambda b, j: (0, 0)
    resident = lambda shape: pl.BlockSpec(shape, const, pipeline_mode=pl.Buffered(1))
    return pl.pallas_call(
        functools.partial(_ffn_prompt_kernel, dff=dff, final=final),
        grid=(batch, nj),
        in_specs=[pl.BlockSpec((tm, d), rows), pl.BlockSpec((1, d), const),
                  resident(w_up.shape), pl.BlockSpec(cw.shape, const),
                  pl.BlockSpec((1, 2 * dff), const), resident(w_down.shape),
                  pl.BlockSpec((1, d), const)],
        out_specs=[pl.BlockSpec((tm, d), rows),
                   pl.BlockSpec((None, CONV_W - 1, 2 * dff), lambda b, j: (b, 0, 0))],
        out_shape=[jax.ShapeDtypeStruct(x2.shape, F32),
                   jax.ShapeDtypeStruct((batch, CONV_W - 1, 2 * dff), F32)],
        scratch_shapes=[pltpu.VMEM((tm + SUBLANES, dff), F32),
                        pltpu.VMEM((SUBLANES, 2 * dff), F32)],
        compiler_params=_params("arbitrary", "arbitrary"),
    )(x2, nw, w_up, cw, cb, w_down, fw)


def _ffn_sample_kernel(x1_ref, o_ref, wc_ref, nw_ref, wu_ref, cw_ref, cb_ref, wd_ref, fw_ref,
                       s0_ref, s1_ref, y_ref, st_ref, hid_scr, *, dff, ts, bs, final):
    x2 = x1_ref[...] + _dot(o_ref[...], wc_ref[...])
    h = _rms(x2, nw_ref[...]).astype(BF16)

    def conv_half(lo):
        cols = slice(lo, lo + dff)
        up = _dot(h, wu_ref[:, cols])
        slabs = [s0_ref[:, cols], s1_ref[:, cols]] + [up[t * bs:(t + 1) * bs, :] for t in range(ts)]
        st_ref[0, :, cols] = slabs[-2]
        st_ref[1, :, cols] = slabs[-1]
        return _conv_time_major(slabs, cw_ref[:, cols], cb_ref[:, cols])

    g = conv_half(0)
    u = conv_half(dff)
    for t in range(ts):
        hid_scr[t * bs:(t + 1) * bs, :] = (_silu(g[t]) * u[t]).astype(BF16)
    x3 = x2 + _dot(hid_scr[...], wd_ref[...])
    y_ref[...] = _rms(x3, fw_ref[...]) if final else x3


def _ffn_sample(x1, o, w_co, nw, w_up, cw, cb, w_down, fw, s0, s1, *, ts, bs, final):
    dff = w_down.shape[0]
    return pl.pallas_call(
        functools.partial(_ffn_sample_kernel, dff=dff, ts=ts, bs=bs, final=final),
        out_shape=[jax.ShapeDtypeStruct(x1.shape, F32),
                   jax.ShapeDtypeStruct((CONV_W - 1, bs, 2 * dff), F32)],
        scratch_shapes=[pltpu.VMEM((ts * bs, dff), BF16)],
        compiler_params=pltpu.CompilerParams(vmem_limit_bytes=VMEM_LIMIT),
    )(x1, o, w_co, nw, w_up, cw, cb, w_down, fw, s0, s1)


def kernel(x_prompt, x_sample, mem_prompt, cache_k, cache_v, page_table, state_conv_mix, state_conv_ffn, cache_mem_k, cache_mem_v, rel_bias, norm_mix_w, w_in, conv_mix_w, conv_mix_b, lambda_q1, lambda_k1, lambda_q2, lambda_k2, subln_w, w_out, norm_cross_w, norm_mem_w, w_cq, w_ckv, w_co, norm_ffn_w, w_up, conv_ffn_w, conv_ffn_b, w_down, norm_final_w):
    depth = w_in.shape[0]
    bp, sp, d = x_prompt.shape
    bs, ts, _ = x_sample.shape
    n_mem = mem_prompt.shape[1]
    n_phys, page = cache_k.shape[1], cache_k.shape[2]
    n_pages = page_table.shape[1]
    past_len = n_pages * page
    dc = conv_mix_w.shape[2]
    da = N_HEADS * HEAD_DV
    dff = w_down.shape[1]
    dh = d // MEM_HEADS
    assert page == LANES and CONV_W - 1 <= ts <= SUBLANES and ts & (ts - 1) == 0
    assert (N_HEADS * ts) % SUBLANES == 0 and cache_k.shape[3:] == (N_HEADS, HEAD_DV)
    rows = 2 * SUBLANES

    assert _bucket_np(np.arange(REL_MAX_DIST, max(sp, past_len + ts) + 1)).min() == REL_BUCKETS - 1
    row = lambda a: a.reshape(1, -1).astype(F32)
    rel_bias = rel_bias.astype(F32)
    cache_k2 = cache_k.reshape(depth * n_phys * page * N_HEADS, HEAD_DV)
    cache_v2 = cache_v.reshape(depth * n_phys * page * N_HEADS, HEAD_DV)
    mem_rows = lambda a: a.reshape(depth * bs, n_mem, MEM_HEADS, dh // LANES, LANES).transpose(
        0, 1, 3, 2, 4).reshape(depth * bs, n_mem * (dh // LANES) * MEM_HEADS, LANES)
    mem_k3 = mem_rows(cache_mem_k)
    mem_v3 = mem_rows(cache_mem_v)

    xp = x_prompt.reshape(bp * sp, d)
    xs = x_sample.transpose(1, 0, 2).reshape(ts * bs, d)
    mem2 = mem_prompt.reshape(bp * n_mem, d)
    outs = [[] for _ in range(10)]
    for l in range(depth):
        lam_init = 0.8 - 0.6 * math.exp(-0.3 * l)
        final = l == depth - 1
        w_in_b, w_out_b = w_in[l].astype(BF16), w_out[l].astype(BF16)
        w_cq_b, w_ckv_b, w_co_b = w_cq[l].astype(BF16), w_ckv[l].astype(BF16), w_co[l].astype(BF16)
        w_up_b, w_down_b = w_up[l].astype(BF16), w_down[l].astype(BF16)
        lam_args = (row(lambda_q1[l]), row(lambda_k1[l]), row(lambda_q2[l]), row(lambda_k2[l]),
                    row(subln_w[l]))

        yconv_p, q_p, kf_p, vf_p, kb_p, vb_p, cmix_p = _mix_in_prompt(
            xp, row(norm_mix_w[l]), w_in_b, conv_mix_w[l], row(conv_mix_b[l]),
            batch=bp, seq=sp, dc=dc, da=da)
        yattn_p = _attn_prompt(rel_bias, q_p, kb_p, vb_p, *lam_args[:4], subln_w[l].reshape(-1, 1).astype(F32),
                               batch=bp, seq=sp, lam_init=lam_init)
        mkf, mvf, mkb, mvb = _mem_kv(mem2, row(norm_mem_w[l]), w_ckv_b, batch=bp, n_mem=n_mem)
        x2_p = _cross_prompt(xp, yconv_p, yattn_p, w_out_b, row(norm_cross_w[l]), w_cq_b, mkb, mvb,
                             w_co_b, batch=bp, seq=sp, n_mem=n_mem)
        xp, cffn_p = _ffn_prompt(x2_p, row(norm_ffn_w[l]), w_up_b, conv_ffn_w[l], row(conv_ffn_b[l]),
                                 w_down_b, row(norm_final_w), batch=bp, seq=sp, final=final)

        yconv_s, qm_s, kf_s, vf_s, cmix_s = _mix_in_sample(
            xs, row(norm_mix_w[l]), w_in_b, conv_mix_w[l], row(conv_mix_b[l]),
            state_conv_mix[l][:, 0], state_conv_mix[l][:, 1], dc=dc, da=da, ts=ts, bs=bs)
        q_all = qm_s.reshape(2, ts, bs, N_HEADS, HEAD_DV).transpose(2, 0, 3, 1, 4)
        q_all = q_all.reshape(bs, 2 * N_HEADS * ts, HEAD_DV)
        new_rows = lambda a: jnp.pad(
            a.reshape(ts, bs, N_HEADS * HEAD_DV).transpose(1, 0, 2).reshape(bs, ts * N_HEADS, HEAD_DV),
            ((0, 0), (0, LANES - ts * N_HEADS), (0, 0))).astype(BF16)
        yattn_s = _attn_sample(page_table, rel_bias, q_all, new_rows(kf_s), new_rows(vf_s), *lam_args,
                               cache_k2, cache_v2, page=page, page_offset=l * n_phys, ts=ts,
                               lam_init=lam_init)
        yattn_s = yattn_s.reshape(bs, N_HEADS, ts, HEAD_DV).transpose(2, 0, 1, 3)
        yattn_s = yattn_s.reshape(ts * bs, da).astype(BF16)
        x1_s, qc_s = _outproj_q_sample(xs, yconv_s, yattn_s, w_out_b, row(norm_cross_w[l]), w_cq_b)
        q2 = qc_s.reshape(ts, bs, MEM_HEADS, dh // LANES, LANES).transpose(1, 3, 2, 0, 4)
        q2 = q2.reshape(bs, (dh // LANES) * MEM_HEADS * ts, LANES)
        o_b = _cross_sample(q2, mem_k3, mem_v3, row_offset=l * bs, ts=ts)
        o_s = o_b.reshape(bs, MEM_HEADS, ts, dh).transpose(2, 0, 1, 3).reshape(ts * bs, d)
        xs, cffn_s = _ffn_sample(x1_s, o_s, w_co_b, row(norm_ffn_w[l]), w_up_b, conv_ffn_w[l],
                                 row(conv_ffn_b[l]), w_down_b, row(norm_final_w),
                                 state_conv_ffn[l][:, 0], state_conv_ffn[l][:, 1],
                                 ts=ts, bs=bs, final=final)

        t2b = lambda a: a.reshape(ts, bs, N_HEADS, HEAD_DV).transpose(1, 0, 2, 3)
        mem_out = lambda a: a.reshape(bp, n_mem, dh // LANES, MEM_HEADS, LANES).transpose(
            0, 1, 3, 2, 4).reshape(bp, n_mem, MEM_HEADS, dh)
        for lst, val in zip(outs, (
                kf_p.reshape(bp, sp, N_HEADS, 2 * HEAD_DK), vf_p.reshape(bp, sp, N_HEADS, HEAD_DV),
                t2b(kf_s), t2b(vf_s), cmix_p, cmix_s.transpose(1, 0, 2), cffn_p,
                cffn_s.transpose(1, 0, 2), mem_out(mkf), mem_out(mvf))):
            lst.append(val)

    y_prompt = xp.reshape(bp, sp, d)
    y_sample = xs.reshape(ts, bs, d).transpose(1, 0, 2)
    return (y_prompt, y_sample) + tuple(jnp.stack(o) for o in outs)
```

```python
import functools
import math

import numpy as np
import jax
import jax.numpy as jnp
from jax import lax
from jax.experimental import pallas as pl
from jax.experimental.pallas import tpu as pltpu

F32 = jnp.float32
BF16 = jnp.bfloat16

EPS = 1e-6
NEG = -1e30
LANES = 128
SUBLANES = 8
N_HEADS = 4
HEAD_DK = 64
HEAD_DV = 2 * HEAD_DK
MEM_HEADS = 4
CONV_W = 3
REL_BUCKETS = 32
REL_MAX_EXACT = 16
REL_MAX_DIST = 128
VMEM_LIMIT = 56 * 1024 * 1024

ATTN_TILE = 512
QUERY_CHUNK = 256
ONES_ROWS = 16
LOG2E = 1.4426950408889634
ROW_TILE = 512
FFN_ROW_TILE = 512
PAGES_PER_GROUP = 8
PAGE_SLOTS = 3


def _params(*sem):
    return pltpu.CompilerParams(dimension_semantics=sem, vmem_limit_bytes=VMEM_LIMIT)


def _rms(x, w):
    return x * lax.rsqrt(jnp.mean(x * x, axis=-1, keepdims=True) + EPS) * w


def _dot(a, b):
    return jnp.dot(a, b, preferred_element_type=F32)


def _dot_nt(a, b):
    return lax.dot_general(a, b, (((1,), (1,)), ((), ())), preferred_element_type=F32)


def _lam(lq1, lk1, lq2, lk2, lam_init):
    return (jnp.exp(jnp.sum(lq1[...] * lk1[...], axis=-1, keepdims=True))
            - jnp.exp(jnp.sum(lq2[...] * lk2[...], axis=-1, keepdims=True)) + lam_init)


def _bucket_np(rel):
    n = np.maximum(rel, 0)
    nf = np.maximum(n, 1).astype(np.float32)
    large = REL_MAX_EXACT + (np.log(nf / np.float32(REL_MAX_EXACT))
                             / np.float32(math.log(REL_MAX_DIST / REL_MAX_EXACT))
                             * np.float32(REL_BUCKETS - REL_MAX_EXACT)).astype(np.int32)
    large = np.minimum(large, REL_BUCKETS - 1)
    return np.where(n < REL_MAX_EXACT, n, large).astype(np.int32)


def _bucket_starts():
    buckets = _bucket_np(np.arange(REL_MAX_DIST + 1))
    assert (np.diff(buckets) >= 0).all() and buckets[-1] == REL_BUCKETS - 1
    return [int(np.argmax(buckets >= k)) for k in range(REL_BUCKETS)]


def _rel_bias_tile(rel, rb_ref, h):
    far = rb_ref[REL_BUCKETS - 1, h]
    val = jnp.full(rel.shape, rb_ref[0, h] - far, F32)
    for k, start in enumerate(_bucket_starts()):
        if k > 0:
            val = jnp.where(rel >= start, rb_ref[k, h] - far, val)
    return val


def _softmax_update(chunks, m_prev, l_prev):
    mx = chunks[0]
    for c in chunks[1:]:
        mx = jnp.maximum(mx, c)
    m_next = jnp.maximum(m_prev, jnp.max(mx, axis=1, keepdims=True))
    ps = [jnp.exp(c - m_next) for c in chunks]
    sm = ps[0]
    for p in ps[1:]:
        sm = sm + p
    alpha = jnp.exp(m_prev - m_next)
    l_next = alpha * l_prev + jnp.sum(sm, axis=1, keepdims=True)
    return ps, m_next, l_next, alpha


def _head_out(acc1, l1, acc2, l2, lam, sw, lam_init):
    o = acc1 / l1 - lam * (acc2 / l2)
    return _rms(o, sw) * (1.0 - lam_init)


def _mix_in_prompt_kernel(x_ref, nw_ref, w_ref, cw_ref, cb_ref,
                          yconv_ref, q_ref, kf_ref, vf_ref, kb_ref, vb_ref, st_ref,
                          pre_scr, *, dc, da):
    j = pl.program_id(1)
    tm = x_ref.shape[0]
    h = _rms(x_ref[...], nw_ref[...]).astype(BF16)

    def proj(lo, width):
        return _dot(h, w_ref[:, lo:lo + width])

    @pl.when(j == 0)
    def _zero_prefix():
        pre_scr[0:SUBLANES, :] = jnp.zeros((SUBLANES, dc), F32)

    pre = proj(dc, dc) * proj(2 * dc, dc)
    pre_scr[SUBLANES:SUBLANES + tm, :] = pre
    cw = cw_ref[...]
    conv = (cb_ref[...] + cw[0:1, :] * pre_scr[SUBLANES - 2:SUBLANES - 2 + tm, :]
            + cw[1:2, :] * pre_scr[SUBLANES - 1:SUBLANES - 1 + tm, :] + cw[2:3, :] * pre)
    yconv_ref[...] = (proj(0, dc) * conv).astype(BF16)
    st_ref[...] = pre[tm - 2:tm, :]
    pre_scr[0:SUBLANES, :] = pre[tm - SUBLANES:tm, :]

    q = proj(3 * dc, da) * (HEAD_DK ** -0.5 * LOG2E)
    k = proj(3 * dc + da, da)
    v = proj(3 * dc + 2 * da, da)
    ones_rows = (lax.broadcasted_iota(jnp.int32, (ONES_ROWS, tm), 0) == 0).astype(BF16)
    for hd in range(N_HEADS):
        cols = slice(hd * HEAD_DV, (hd + 1) * HEAD_DV)
        kb_ref[hd] = k[:, cols].astype(BF16)
        kf_ref[pl.ds(hd, tm, stride=N_HEADS), :] = k[:, cols]
        vf_ref[pl.ds(hd, tm, stride=N_HEADS), :] = v[:, cols]
        q_ref[hd] = q[:, cols].T.astype(BF16)
        vb_ref[hd, 0:HEAD_DV, :] = v[:, cols].T.astype(BF16)
        vb_ref[hd, HEAD_DV:, :] = ones_rows


def _mix_in_prompt(x2, nw, w_in, cw, cb, *, batch, seq, dc, da):
    d = x2.shape[1]
    tm = min(ATTN_TILE, seq)
    nj = seq // tm
    rows = lambda b, j: (b * nj + j, 0)
    const = lambda b, j: (0, 0)
    tposed = lambda r: pl.BlockSpec((None, N_HEADS, None, r, tm), lambda b, j: (b, 0, j, 0, 0))
    tposed_shape = lambda r: jax.ShapeDtypeStruct((batch, N_HEADS, nj, r, tm), BF16)
    n = batch * seq
    return pl.pallas_call(
        functools.partial(_mix_in_prompt_kernel, dc=dc, da=da),
        grid=(batch, nj),
        in_specs=[pl.BlockSpec((tm, d), rows),
                  pl.BlockSpec((1, d), const),
                  pl.BlockSpec(w_in.shape, const),
                  pl.BlockSpec(cw.shape, const),
                  pl.BlockSpec((1, dc), const)],
        out_specs=[pl.BlockSpec((tm, dc), rows),
                   tposed(HEAD_DV),
                   pl.BlockSpec((tm * N_HEADS, HEAD_DV), rows),
                   pl.BlockSpec((tm * N_HEADS, HEAD_DV), rows),
                   pl.BlockSpec((None, N_HEADS, tm, HEAD_DV), lambda b, j: (b, 0, j, 0)),
                   tposed(HEAD_DV + ONES_ROWS),
                   pl.BlockSpec((None, CONV_W - 1, dc), lambda b, j: (b, 0, 0))],
        out_shape=[jax.ShapeDtypeStruct((n, dc), BF16),
                   tposed_shape(HEAD_DV),
                   jax.ShapeDtypeStruct((n * N_HEADS, HEAD_DV), F32),
                   jax.ShapeDtypeStruct((n * N_HEADS, HEAD_DV), F32),
                   jax.ShapeDtypeStruct((batch, N_HEADS, seq, HEAD_DV), BF16),
                   tposed_shape(HEAD_DV + ONES_ROWS),
                   jax.ShapeDtypeStruct((batch, CONV_W - 1, dc), F32)],
        scratch_shapes=[pltpu.VMEM((tm + SUBLANES, dc), F32)],
        compiler_params=_params("arbitrary", "arbitrary"),
    )(x2, nw, w_in, cw, cb)


def _conv_time_major(slabs, cw, cb):
    return [cb + cw[0:1, :] * slabs[t] + cw[1:2, :] * slabs[t + 1] + cw[2:3, :] * slabs[t + 2]
            for t in range(len(slabs) - 2)]


def _mix_in_sample_kernel(x_ref, nw_ref, w_ref, cw_ref, cb_ref, s0_ref, s1_ref,
                          yconv_ref, qm_ref, kf_ref, vf_ref, st_ref, wb_ref, *, dc, da, ts, bs):
    h = _rms(x_ref[...], nw_ref[...]).astype(BF16)

    def proj(lo, width):
        wb = w_ref[:, lo:lo + width].astype(BF16)
        wb_ref[:, lo:lo + width] = wb
        return _dot(h, wb)

    pre = proj(dc, dc) * proj(2 * dc, dc)
    gate = proj(0, dc)
    slabs = [s0_ref[...], s1_ref[...]] + [pre[t * bs:(t + 1) * bs, :] for t in range(ts)]
    conv = _conv_time_major(slabs, cw_ref[...], cb_ref[...])
    for t in range(ts):
        yconv_ref[t * bs:(t + 1) * bs, :] = (gate[t * bs:(t + 1) * bs, :] * conv[t]).astype(BF16)
    st_ref[0] = slabs[-2]
    st_ref[1] = slabs[-1]

    q = proj(3 * dc, da) * (HEAD_DK ** -0.5)
    lane = lax.broadcasted_iota(jnp.int32, q.shape, 1) % HEAD_DV
    qm_ref[0] = jnp.where(lane < HEAD_DK, q, 0.0).astype(BF16)
    qm_ref[1] = jnp.where(lane >= HEAD_DK, q, 0.0).astype(BF16)
    kf_ref[...] = proj(3 * dc + da, da)
    vf_ref[...] = proj(3 * dc + 2 * da, da)


def _mix_in_sample(xt, nw, w_in, cw, cb, s0, s1, *, dc, da, ts, bs):
    n = ts * bs
    return pl.pallas_call(
        functools.partial(_mix_in_sample_kernel, dc=dc, da=da, ts=ts, bs=bs),
        out_shape=[jax.ShapeDtypeStruct((n, dc), BF16),
                   jax.ShapeDtypeStruct((2, n, da), BF16),
                   jax.ShapeDtypeStruct((n, da), F32),
                   jax.ShapeDtypeStruct((n, da), F32),
                   jax.ShapeDtypeStruct((CONV_W - 1, bs, dc), F32),
                   jax.ShapeDtypeStruct(w_in.shape, BF16)],
        compiler_params=pltpu.CompilerParams(vmem_limit_bytes=VMEM_LIMIT),
    )(xt, nw, w_in, cw, cb, s0, s1)


def _mem_kv_kernel(m_ref, nw_ref, w_ref, kf_ref, vf_ref, kb_ref, vb_ref, *, d):
    n_mem = m_ref.shape[0]
    dh = d // MEM_HEADS
    nh = dh // LANES
    h = _rms(m_ref[...], nw_ref[...]).astype(BF16)
    k = _dot(h, w_ref[:, 0:d].astype(BF16))
    kb_ref[...] = k.astype(BF16)
    v = _dot(h, w_ref[:, d:2 * d].astype(BF16))
    vb_ref[...] = v.astype(BF16)
    for hd in range(MEM_HEADS):
        for half in range(nh):
            rows = pl.ds(half * MEM_HEADS + hd, n_mem, stride=nh * MEM_HEADS)
            cols = slice(hd * dh + half * LANES, hd * dh + (half + 1) * LANES)
            kf_ref[rows, :] = k[:, cols]
            vf_ref[rows, :] = v[:, cols]


def _mem_kv(mem2, nw, w_ckv, *, batch, n_mem):
    d = mem2.shape[1]
    rows = lambda b: (b, 0)
    const = lambda b: (0, 0)
    n = batch * n_mem
    return pl.pallas_call(
        functools.partial(_mem_kv_kernel, d=d),
        grid=(batch,),
        in_specs=[pl.BlockSpec((n_mem, d), rows), pl.BlockSpec((1, d), const),
                  pl.BlockSpec(w_ckv.shape, const, pipeline_mode=pl.Buffered(1))],
        out_specs=[pl.BlockSpec((n_mem * d // LANES, LANES), rows)] * 2 + [pl.BlockSpec((n_mem, d), rows)] * 2,
        out_shape=[jax.ShapeDtypeStruct((n * d // LANES, LANES), F32)] * 2
                  + [jax.ShapeDtypeStruct((n, d), BF16)] * 2,
        compiler_params=_params("arbitrary"),
    )(mem2, nw, w_ckv)


def _attn_prompt_kernel(rb_ref, q_ref, k_ref, vt_ref, lq1, lk1, lq2, lk2, sw_ref, o_ref,
                        qa_scr, ma_scr, acca_scr, qb_scr, mb_scr, accb_scr, bd_scr, bs_scr, sa_scr, sb_scr,
                        *, T, lam_init):
    nb = T // LANES
    n_tiles = q_ref.shape[0]

    def _assemble_bias_tiles():
        h = pl.program_id(1)
        rel = (lax.broadcasted_iota(jnp.int32, (LANES, LANES), 1)
               - lax.broadcasted_iota(jnp.int32, (LANES, LANES), 0))
        d0 = jnp.where(rel >= 0, _rel_bias_tile(jnp.maximum(rel, 0), rb_ref, h) * LOG2E, NEG)
        d1 = _rel_bias_tile(rel + LANES, rb_ref, h) * LOG2E
        zero = jnp.zeros((LANES, LANES), F32)
        neg = jnp.full((LANES, LANES), NEG, F32)
        for bi in range(nb):
            for bj in range(nb):
                rs = slice(bi * LANES, (bi + 1) * LANES)
                cs = slice(bj * LANES, (bj + 1) * LANES)
                bd_scr[rs, cs] = d0 if bi == bj else d1 if bj == bi + 1 else zero if bj > bi else neg
                bs_scr[rs, cs] = d1 if (bj == 0 and bi == nb - 1) else zero

    _assemble_bias_tiles()
    n_chunks = 2 * T // QUERY_CHUNK

    def prep(st, qi):
        qp, m, acc = st
        qt = q_ref[qi].astype(F32)
        sub = lax.broadcasted_iota(jnp.int32, qt.shape, 0)
        qp[:, 0:T] = jnp.where(sub < HEAD_DK, qt, 0.0).astype(BF16)
        qp[:, T:2 * T] = jnp.where(sub >= HEAD_DK, qt, 0.0).astype(BF16)
        m[...] = jnp.full(m.shape, -jnp.inf, F32)
        acc[...] = jnp.zeros(acc.shape, F32)

    def chunk_scores(kh, qp, dst, c):
        dst[c] = _dot(kh, qp[:, c * QUERY_CHUNK:(c + 1) * QUERY_CHUNK])

    def key_block(j):
        return k_ref[pl.ds(pl.multiple_of(j * T, T), T), :]

    def stage(st, j, src, bias_ref, nxt=None):
        _, m_scr, acc_scr = st
        if nxt is not None:
            j_next, qp_next, dst = nxt
            kh_next = key_block(j_next)
        for c in range(n_chunks):
            if nxt is not None and dst is not src:
                chunk_scores(kh_next, qp_next, dst, c)
            cs = slice(c * QUERY_CHUNK, (c + 1) * QUERY_CHUNK)
            lo = (c * QUERY_CHUNK) % T
            nk = min(T, lo + QUERY_CHUNK) if bias_ref is bd_scr else T
            s = src[c, 0:nk, :]
            if bias_ref is not None:
                s = s + bias_ref[0:nk, lo:lo + QUERY_CHUNK]
            m_prev = m_scr[:, cs]
            m_next = jnp.maximum(m_prev, jnp.max(s, axis=0, keepdims=True))
            p = jnp.exp2(s - m_next).astype(BF16)
            if nxt is not None and dst is src:
                chunk_scores(kh_next, qp_next, dst, c)
            acc_scr[:, cs] = (jnp.exp2(m_prev - m_next) * acc_scr[:, cs]
                              + _dot(vt_ref[j, :, 0:nk], p))
            m_scr[:, cs] = m_next

    def finalize(st, qi):
        _, _, acc_scr = st
        lam = _lam(lq1, lk1, lq2, lk2, lam_init)
        acc = acc_scr[0:HEAD_DV, :]
        l = acc_scr[HEAD_DV:HEAD_DV + 1, :]
        o = acc[:, 0:T] / l[:, 0:T] - lam * (acc[:, T:2 * T] / l[:, T:2 * T])
        y = o * lax.rsqrt(jnp.mean(o * o, axis=0, keepdims=True) + EPS) * sw_ref[...] * (1.0 - lam_init)
        o_ref[pl.ds(pl.multiple_of(qi * T, T), T), :] = y.T.astype(o_ref.dtype)

    even = (qa_scr, ma_scr, acca_scr)
    odd = (qb_scr, mb_scr, accb_scr)
    prep(even, 0)
    kh0 = key_block(0)
    for c in range(n_chunks):
        chunk_scores(kh0, qa_scr, sb_scr, c)

    def tile_pair(t, carry):
        a = 2 * t
        b = a + 1

        @pl.when(t >= 1)
        def _even_first_block():
            stage(even, 0, sb_scr, None, (1, qa_scr, sa_scr))

        def even_far_pair(i, c):
            j = 1 + 2 * i
            stage(even, j, sa_scr, None, (j + 1, qa_scr, sb_scr))
            stage(even, j + 1, sb_scr, None, (j + 2, qa_scr, sa_scr))
            return c

        lax.fori_loop(0, jnp.maximum(t - 1, 0), even_far_pair, 0)
        prep(odd, b)

        @pl.when(t >= 1)
        def _even_last_blocks():
            stage(even, a - 1, sa_scr, bs_scr, (a, qa_scr, sb_scr))
            stage(even, a, sb_scr, bd_scr, (0, qb_scr, sa_scr))

        @pl.when(t == 0)
        def _even_only_block():
            stage(even, 0, sb_scr, bd_scr, (0, qb_scr, sa_scr))

        def odd_far_pair(i, c):
            j = 2 * i
            stage(odd, j, sa_scr, None, (j + 1, qb_scr, sb_scr))
            stage(odd, j + 1, sb_scr, None, (j + 2, qb_scr, sa_scr))
            return c

        lax.fori_loop(0, t, odd_far_pair, 0)
        finalize(even, a)
        prep(even, jnp.minimum(a + 2, n_tiles - 1))
        stage(odd, b - 1, sa_scr, bs_scr, (b, qb_scr, sb_scr))
        stage(odd, b, sb_scr, bd_scr, (0, qa_scr, sb_scr))
        finalize(odd, b)
        return carry

    lax.fori_loop(0, n_tiles // 2, tile_pair, 0)


def _attn_prompt(rel_bias, qt, kb, vt, lq1, lk1, lq2, lk2, sw_col, *, batch, seq, lam_init):
    T = qt.shape[-1]
    nq = seq // T
    assert nq % 2 == 0, "query tiles run in (even, odd) pairs"
    da = N_HEADS * HEAD_DV
    vec = lambda b, h: (0, 0)
    return pl.pallas_call(
        functools.partial(_attn_prompt_kernel, T=T, lam_init=lam_init),
        grid=(batch, N_HEADS),
        in_specs=[pl.BlockSpec(memory_space=pltpu.SMEM),
                  pl.BlockSpec((None, None, nq, HEAD_DV, T), lambda b, h: (b, h, 0, 0, 0)),
                  pl.BlockSpec((None, None, seq, HEAD_DV), lambda b, h: (b, h, 0, 0)),
                  pl.BlockSpec((None, None, nq, vt.shape[3], T), lambda b, h: (b, h, 0, 0, 0)),
                  pl.BlockSpec((1, HEAD_DK), vec), pl.BlockSpec((1, HEAD_DK), vec),
                  pl.BlockSpec((1, HEAD_DK), vec), pl.BlockSpec((1, HEAD_DK), vec),
                  pl.BlockSpec((HEAD_DV, 1), vec)],
        out_specs=pl.BlockSpec((seq, HEAD_DV), lambda b, h: (b, h)),
        out_shape=jax.ShapeDtypeStruct((batch * seq, da), BF16),
        scratch_shapes=[pltpu.VMEM((HEAD_DV, 2 * T), BF16),
                        pltpu.VMEM((1, 2 * T), F32),
                        pltpu.VMEM((vt.shape[3], 2 * T), F32),
                        pltpu.VMEM((HEAD_DV, 2 * T), BF16),
                        pltpu.VMEM((1, 2 * T), F32),
                        pltpu.VMEM((vt.shape[3], 2 * T), F32),
                        pltpu.VMEM((T, T), F32),
                        pltpu.VMEM((T, T), F32),
                        pltpu.VMEM((2 * T // QUERY_CHUNK, T, QUERY_CHUNK), F32),
                        pltpu.VMEM((2 * T // QUERY_CHUNK, T, QUERY_CHUNK), F32)],
        compiler_params=_params("arbitrary", "arbitrary"),
    )(rel_bias, qt, kb, vt, lq1, lk1, lq2, lk2, sw_col)


def _attn_sample_kernel(pt_ref, rb_ref, q_ref, kn_ref, vn_ref, lq1, lk1, lq2, lk2, sw_ref, k_hbm, v_hbm,
                        o_ref, m_scr, l_scr, acc_scr, hm_scr, bl_scr, bn_scr, kbuf, vbuf, sems,
                        *, pages, page_offset, ts, lam_init):
    b = pl.program_id(0)
    nb = pl.num_programs(0)
    rows, pk = hm_scr.shape
    nk = bn_scr.shape[1]
    rm = rows // 2
    page = pk // N_HEADS
    groups = pt_ref.shape[1] // pages
    hbits = N_HEADS.bit_length() - 1
    tbits = ts.bit_length() - 1

    def group_copies(t, slot):
        bt = t // groups
        g0 = (t % groups) * pages
        out = []
        for i in range(pages):
            row0 = pl.multiple_of((page_offset + pt_ref[bt, g0 + i]) * pk, pk)
            dst = pl.ds(i * pk, pk)
            out.append(pltpu.make_async_copy(k_hbm.at[pl.ds(row0, pk)], kbuf.at[slot, dst], sems.at[0, slot]))
            out.append(pltpu.make_async_copy(v_hbm.at[pl.ds(row0, pk)], vbuf.at[slot, dst], sems.at[1, slot]))
        return out

    def start_group(t):
        @pl.when(t < nb * groups)
        def _():
            for cp in group_copies(t, t % PAGE_SLOTS):
                cp.start()

    @pl.when(b == 0)
    def _first_step():
        for t in range(PAGE_SLOTS - 1):
            start_group(jnp.int32(t))

        def tables(ncols, rel_of):
            r = lax.broadcasted_iota(jnp.int32, (rows, ncols), 0)
            c = lax.broadcasted_iota(jnp.int32, (rows, ncols), 1)
            rhead = (r >> tbits) & (N_HEADS - 1)
            same = rhead == (c & (N_HEADS - 1))
            rel = rel_of(r & (ts - 1), c >> hbits)
            bias = jnp.zeros((rows, ncols), F32)
            for h in range(N_HEADS):
                bias = jnp.where(rhead == h, _rel_bias_tile(jnp.maximum(rel, 0), rb_ref, h), bias)
            return same, rel, c >> hbits, bias

        same, _, _, bias = tables(pk, lambda tok, key: tok + page - key)
        hm_scr[...] = jnp.where(same, 0.0, NEG)
        bl_scr[...] = jnp.where(same, bias, NEG)
        same, rel, key, bias = tables(nk, lambda tok, key: tok - key)
        bn_scr[...] = jnp.where(same & (rel >= 0) & (key < ts), bias, NEG)

    m_scr[...] = jnp.full(m_scr.shape, -jnp.inf, F32)
    l_scr[...] = jnp.zeros(l_scr.shape, F32)
    acc_scr[...] = jnp.zeros(acc_scr.shape, F32)

    def update(scores, values):
        chunks, sizes = [], []
        for s in scores:
            n = s.shape[1] // LANES
            sizes.append(n)
            chunks += [s[:, c * LANES:(c + 1) * LANES] for c in range(n)]
        ps, m_next, l_next, alpha = _softmax_update(chunks, m_scr[...], l_scr[...])
        pv, at = None, 0
        for n, v in zip(sizes, values):
            p = jnp.concatenate([x.astype(BF16) for x in ps[at:at + n]], axis=1) if n > 1 \
                else ps[at].astype(BF16)
            at += n
            pv = _dot(p, v) if pv is None else pv + _dot(p, v)
        acc_scr[...] = alpha * acc_scr[...] + pv
        m_scr[...] = m_next
        l_scr[...] = l_next

    q = q_ref[...]

    def group_body(g, carry):
        t = b * groups + g
        slot = t % PAGE_SLOTS
        for cp in group_copies(t, slot):
            cp.wait()
        start_group(t + PAGE_SLOTS - 1)
        kg = kbuf.at[slot]
        vg = vbuf.at[slot]
        last = g == groups - 1
        scores = [_dot_nt(q, kg[i * pk:(i + 1) * pk, :].astype(BF16)) + hm_scr[...]
                  for i in range(pages - 1)]
        scores.append(_dot_nt(q, kg[(pages - 1) * pk:pages * pk, :].astype(BF16))
                      + jnp.where(last, bl_scr[...], hm_scr[...]))
        update(scores, [vg[i * pk:(i + 1) * pk, :].astype(BF16) for i in range(pages)])
        return carry

    lax.fori_loop(0, groups, group_body, 0)

    update([_dot_nt(q, kn_ref[...]) + bn_scr[...]], [vn_ref[...]])
    lam = _lam(lq1, lk1, lq2, lk2, lam_init)
    acc = acc_scr[...]
    l = l_scr[...]
    o_ref[...] = _head_out(acc[0:rm], l[0:rm], acc[rm:rows], l[rm:rows], lam, sw_ref[...], lam_init)


def _attn_sample(page_table, rel_bias, q_all, knew, vnew, lq1, lk1, lq2, lk2, sw, cache_k, cache_v,
                 *, page, page_offset, ts, lam_init):
    bs, n_pages = page_table.shape
    rows = q_all.shape[1]
    nk = knew.shape[1]
    pk = page * N_HEADS
    pages = math.gcd(PAGES_PER_GROUP, n_pages)
    vec = lambda b, pt: (0, 0)
    per_b = lambda b, pt: (b, 0, 0)
    grid_spec = pltpu.PrefetchScalarGridSpec(
        num_scalar_prefetch=1,
        grid=(bs,),
        in_specs=[pl.BlockSpec(memory_space=pltpu.SMEM),
                  pl.BlockSpec((None, rows, HEAD_DV), per_b),
                  pl.BlockSpec((None, nk, HEAD_DV), per_b),
                  pl.BlockSpec((None, nk, HEAD_DV), per_b),
                  pl.BlockSpec((1, HEAD_DK), vec), pl.BlockSpec((1, HEAD_DK), vec),
                  pl.BlockSpec((1, HEAD_DK), vec), pl.BlockSpec((1, HEAD_DK), vec),
                  pl.BlockSpec((1, HEAD_DV), vec),
                  pl.BlockSpec(memory_space=pl.ANY), pl.BlockSpec(memory_space=pl.ANY)],
        out_specs=pl.BlockSpec((None, rows // 2, HEAD_DV), per_b),
        scratch_shapes=[pltpu.VMEM((rows, LANES), F32),
                        pltpu.VMEM((rows, LANES), F32),
                        pltpu.VMEM((rows, HEAD_DV), F32),
                        pltpu.VMEM((rows, pk), F32),
                        pltpu.VMEM((rows, pk), F32),
                        pltpu.VMEM((rows, nk), F32),
                        pltpu.VMEM((PAGE_SLOTS, pages * pk, HEAD_DV), F32),
                        pltpu.VMEM((PAGE_SLOTS, pages * pk, HEAD_DV), F32),
                        pltpu.SemaphoreType.DMA((2, PAGE_SLOTS))],
    )
    return pl.pallas_call(
        functools.partial(_attn_sample_kernel, pages=pages, page_offset=page_offset, ts=ts,
                          lam_init=lam_init),
        grid_spec=grid_spec,
        out_shape=jax.ShapeDtypeStruct((bs, rows // 2, HEAD_DV), F32),
        compiler_params=_params("arbitrary"),
    )(page_table, rel_bias, q_all, knew, vnew, lq1, lk1, lq2, lk2, sw, cache_k, cache_v)


def _cross_heads(qc, mk_ref, mv_ref, o_scr, dh):
    for hd in range(MEM_HEADS):
        cols = slice(hd * dh, (hd + 1) * dh)
        s = _dot_nt(qc[:, cols], mk_ref[:, cols].astype(BF16))
        p = jnp.exp(s - jnp.max(s, axis=1, keepdims=True))
        o = _dot(p.astype(BF16), mv_ref[:, cols].astype(BF16)) / jnp.sum(p, axis=1, keepdims=True)
        o_scr[:, cols] = o.astype(BF16)


def _cross_prompt_kernel(x_ref, yc_ref, ya_ref, wo_ref, nw_ref, wq_ref, mk_ref, mv_ref, wc_ref,
                         o_ref, o_scr, *, dc, dh):
    x1 = x_ref[...] + _dot(yc_ref[...], wo_ref[0:dc, :]) + _dot(ya_ref[...], wo_ref[dc:, :])
    h = _rms(x1, nw_ref[...]).astype(BF16)
    qc = (_dot(h, wq_ref[...]) * (dh ** -0.5)).astype(BF16)
    _cross_heads(qc, mk_ref, mv_ref, o_scr, dh)
    o_ref[...] = x1 + _dot(o_scr[...], wc_ref[...])


def _cross_prompt(x2, yconv, yattn, w_out, nw, w_cq, mk, mv, w_co, *, batch, seq, n_mem):
    d = x2.shape[1]
    dc = yconv.shape[1]
    da = yattn.shape[1]
    tm = min(ROW_TILE, seq)
    nj = seq // tm
    rows = lambda b, j: (b * nj + j, 0)
    const = lambda b, j: (0, 0)
    memb = lambda b, j: (b, 0)
    return pl.pallas_call(
        functools.partial(_cross_prompt_kernel, dc=dc, dh=d // MEM_HEADS),
        grid=(batch, nj),
        in_specs=[pl.BlockSpec((tm, d), rows), pl.BlockSpec((tm, dc), rows), pl.BlockSpec((tm, da), rows),
                  pl.BlockSpec(w_out.shape, const), pl.BlockSpec((1, d), const),
                  pl.BlockSpec(w_cq.shape, const),
                  pl.BlockSpec((n_mem, d), memb), pl.BlockSpec((n_mem, d), memb),
                  pl.BlockSpec(w_co.shape, const)],
        out_specs=pl.BlockSpec((tm, d), rows),
        out_shape=jax.ShapeDtypeStruct(x2.shape, F32),
        scratch_shapes=[pltpu.VMEM((tm, d), BF16)],
        compiler_params=_params("arbitrary", "arbitrary"),
    )(x2, yconv, yattn, w_out, nw, w_cq, mk, mv, w_co)


def _outproj_q_sample_kernel(x_ref, yc_ref, ya_ref, wo_ref, nw_ref, wq_ref, x1_ref, qc_ref,
                             wob_ref, wqb_ref, *, dc, dh):
    wob_ref[...] = wo_ref[...].astype(BF16)
    wqb_ref[...] = wq_ref[...].astype(BF16)
    x1 = x_ref[...] + _dot(yc_ref[...], wob_ref[0:dc, :]) + _dot(ya_ref[...], wob_ref[dc:, :])
    x1_ref[...] = x1
    h = _rms(x1, nw_ref[...]).astype(BF16)
    qc_ref[...] = (_dot(h, wqb_ref[...]) * (dh ** -0.5)).astype(BF16)


def _outproj_q_sample(xt, yconv, yattn, w_out, nw, w_cq):
    d = xt.shape[1]
    return pl.pallas_call(
        functools.partial(_outproj_q_sample_kernel, dc=yconv.shape[1], dh=d // MEM_HEADS),
        out_shape=[jax.ShapeDtypeStruct(xt.shape, F32), jax.ShapeDtypeStruct(xt.shape, BF16),
                   jax.ShapeDtypeStruct(w_out.shape, BF16), jax.ShapeDtypeStruct(w_cq.shape, BF16)],
        compiler_params=pltpu.CompilerParams(vmem_limit_bytes=VMEM_LIMIT),
    )(xt, yconv, yattn, w_out, nw, w_cq)


def _cross_sample_kernel(q_ref, mk_ref, mv_ref, o_ref, *, ts):
    rq = q_ref.shape[0]
    ncols = mk_ref.shape[0]
    per_key = (rq // (MEM_HEADS * ts)) * MEM_HEADS
    nh = per_key // MEM_HEADS
    rows = MEM_HEADS * ts
    s2 = _dot_nt(q_ref[...], mk_ref[...].astype(BF16))
    s = s2[0:rows]
    for half in range(1, nh):
        s = s + pltpu.roll(s2[half * rows:(half + 1) * rows], ncols - half * MEM_HEADS, axis=1)
    col = lax.broadcasted_iota(jnp.int32, (rows, ncols), 1) & (per_key - 1)
    head = lax.broadcasted_iota(jnp.int32, (rows, ncols), 0) >> (ts.bit_length() - 1)
    s = jnp.where(col == head, s, NEG)
    p = jnp.exp(s - jnp.max(s, axis=1, keepdims=True))
    l = jnp.sum(p, axis=1, keepdims=True)
    lhs = jnp.concatenate([p] + [pltpu.roll(p, half * MEM_HEADS, axis=1) for half in range(1, nh)], axis=0)
    o2 = _dot(lhs.astype(BF16), mv_ref[...].astype(BF16))
    o = jnp.concatenate([o2[half * rows:(half + 1) * rows] for half in range(nh)], axis=1) / l
    o_ref[...] = o.astype(o_ref.dtype)


def _cross_sample(q2, mk, mv, *, row_offset, ts):
    bs, rq, _ = q2.shape
    ncols = mk.shape[1]
    rows = MEM_HEADS * ts
    nh = rq // rows
    per_b = lambda b: (b, 0, 0)
    mem_b = lambda b: (row_offset + b, 0, 0)
    return pl.pallas_call(
        functools.partial(_cross_sample_kernel, ts=ts),
        grid=(bs,),
        in_specs=[pl.BlockSpec((None, rq, LANES), per_b), pl.BlockSpec((None, ncols, LANES), mem_b),
                  pl.BlockSpec((None, ncols, LANES), mem_b)],
        out_specs=pl.BlockSpec((None, rows, nh * LANES), per_b),
        out_shape=jax.ShapeDtypeStruct((bs, rows, nh * LANES), BF16),
        compiler_params=_params("arbitrary"),
    )(q2, mk, mv)


def _silu(g):
    return g * (1.0 / (1.0 + jnp.exp(-g)))


def _ffn_prompt_kernel(x_ref, nw_ref, wu_ref, cw_ref, cb_ref, wd_ref, fw_ref, y_ref, st_ref,
                       up_scr, carry_scr, *, dff, final):
    j = pl.program_id(1)
    tm = x_ref.shape[0]
    x = x_ref[...]
    h = _rms(x, nw_ref[...]).astype(BF16)

    @pl.when(j == 0)
    def _zero_prefix():
        carry_scr[...] = jnp.zeros(carry_scr.shape, F32)

    def conv_half(lo):
        cols = slice(lo, lo + dff)
        up = _dot(h, wu_ref[:, cols])
        up_scr[0:SUBLANES, :] = carry_scr[:, cols]
        up_scr[SUBLANES:SUBLANES + tm, :] = up
        cw = cw_ref[:, cols]
        conv = (cb_ref[:, cols] + cw[0:1, :] * up_scr[SUBLANES - 2:SUBLANES - 2 + tm, :]
                + cw[1:2, :] * up_scr[SUBLANES - 1:SUBLANES - 1 + tm, :] + cw[2:3, :] * up)
        carry_scr[:, cols] = up[tm - SUBLANES:tm, :]
        st_ref[:, cols] = up[tm - 2:tm, :]
        return conv

    g = conv_half(0)
    u = conv_half(dff)
    hid = (_silu(g) * u).astype(BF16)
    rh = tm // 2
    for r in range(2):
        rows = slice(r * rh, (r + 1) * rh)
        x3 = x[rows] + _dot(hid[rows], wd_ref[...])
        y_ref[rows, :] = _rms(x3, fw_ref[...]) if final else x3


def _ffn_prompt(x2, nw, w_up, cw, cb, w_down, fw, *, batch, seq, final):
    d = x2.shape[1]
    dff = w_down.shape[0]
    tm = min(FFN_ROW_TILE, seq)
    nj = seq // tm
    rows = lambda b, j: (b * nj + j, 0)
    const = lambda b, j: (0, 0)
    resident = lambda shape: pl.BlockSpec(shape, const, pipeline_mode=pl.Buffered(1))
    return pl.pallas_call(
        functools.partial(_ffn_prompt_kernel, dff=dff, final=final),
        grid=(batch, nj),
        in_specs=[pl.BlockSpec((tm, d), rows), pl.BlockSpec((1, d), const),
                  resident(w_up.shape), pl.BlockSpec(cw.shape, const),
                  pl.BlockSpec((1, 2 * dff), const), resident(w_down.shape),
                  pl.BlockSpec((1, d), const)],
        out_specs=[pl.BlockSpec((tm, d), rows),
                   pl.BlockSpec((None, CONV_W - 1, 2 * dff), lambda b, j: (b, 0, 0))],
        out_shape=[jax.ShapeDtypeStruct(x2.shape, F32),
                   jax.ShapeDtypeStruct((batch, CONV_W - 1, 2 * dff), F32)],
        scratch_shapes=[pltpu.VMEM((tm + SUBLANES, dff), F32),
                        pltpu.VMEM((SUBLANES, 2 * dff), F32)],
        compiler_params=_params("arbitrary", "arbitrary"),
    )(x2, nw, w_up, cw, cb, w_down, fw)


def _ffn_sample_kernel(x1_ref, o_ref, wc_ref, nw_ref, wu_ref, cw_ref, cb_ref, wd_ref, fw_ref,
                       s0_ref, s1_ref, y_ref, st_ref, wcb_ref, hid_scr, *, dff, ts, bs, final):
    wcb_ref[...] = wc_ref[...].astype(BF16)
    x2 = x1_ref[...] + _dot(o_ref[...], wcb_ref[...])
    h = _rms(x2, nw_ref[...]).astype(BF16)

    def conv_half(lo):
        cols = slice(lo, lo + dff)
        up = _dot(h, wu_ref[:, cols])
        slabs = [s0_ref[:, cols], s1_ref[:, cols]] + [up[t * bs:(t + 1) * bs, :] for t in range(ts)]
        st_ref[0, :, cols] = slabs[-2]
        st_ref[1, :, cols] = slabs[-1]
        return _conv_time_major(slabs, cw_ref[:, cols], cb_ref[:, cols])

    g = conv_half(0)
    u = conv_half(dff)
    for t in range(ts):
        hid_scr[t * bs:(t + 1) * bs, :] = (_silu(g[t]) * u[t]).astype(BF16)
    x3 = x2 + _dot(hid_scr[...], wd_ref[...])
    y_ref[...] = _rms(x3, fw_ref[...]) if final else x3


def _ffn_sample(x1, o, w_co, nw, w_up, cw, cb, w_down, fw, s0, s1, *, ts, bs, final):
    dff = w_down.shape[0]
    return pl.pallas_call(
        functools.partial(_ffn_sample_kernel, dff=dff, ts=ts, bs=bs, final=final),
        out_shape=[jax.ShapeDtypeStruct(x1.shape, F32),
                   jax.ShapeDtypeStruct((CONV_W - 1, bs, 2 * dff), F32),
                   jax.ShapeDtypeStruct(w_co.shape, BF16)],
        scratch_shapes=[pltpu.VMEM((ts * bs, dff), BF16)],
        compiler_params=pltpu.CompilerParams(vmem_limit_bytes=VMEM_LIMIT),
    )(x1, o, w_co, nw, w_up, cw, cb, w_down, fw, s0, s1)


def kernel(x_prompt, x_sample, mem_prompt, cache_k, cache_v, page_table, state_conv_mix, state_conv_ffn, cache_mem_k, cache_mem_v, rel_bias, norm_mix_w, w_in, conv_mix_w, conv_mix_b, lambda_q1, lambda_k1, lambda_q2, lambda_k2, subln_w, w_out, norm_cross_w, norm_mem_w, w_cq, w_ckv, w_co, norm_ffn_w, w_up, conv_ffn_w, conv_ffn_b, w_down, norm_final_w):
    depth = w_in.shape[0]
    bp, sp, d = x_prompt.shape
    bs, ts, _ = x_sample.shape
    n_mem = mem_prompt.shape[1]
    n_phys, page = cache_k.shape[1], cache_k.shape[2]
    n_pages = page_table.shape[1]
    past_len = n_pages * page
    dc = conv_mix_w.shape[2]
    da = N_HEADS * HEAD_DV
    dff = w_down.shape[1]
    dh = d // MEM_HEADS
    assert page == LANES and CONV_W - 1 <= ts <= SUBLANES and ts & (ts - 1) == 0
    assert (N_HEADS * ts) % SUBLANES == 0 and cache_k.shape[3:] == (N_HEADS, HEAD_DV)
    rows = 2 * SUBLANES

    assert _bucket_np(np.arange(REL_MAX_DIST, max(sp, past_len + ts) + 1)).min() == REL_BUCKETS - 1
    row = lambda a: a.reshape(1, -1).astype(F32)
    rel_bias = rel_bias.astype(F32)
    cache_k2 = cache_k.reshape(depth * n_phys * page * N_HEADS, HEAD_DV)
    cache_v2 = cache_v.reshape(depth * n_phys * page * N_HEADS, HEAD_DV)
    mem_rows = lambda a: a.reshape(depth * bs, n_mem, MEM_HEADS, dh // LANES, LANES).transpose(
        0, 1, 3, 2, 4).reshape(depth * bs, n_mem * (dh // LANES) * MEM_HEADS, LANES)
    mem_k3 = mem_rows(cache_mem_k)
    mem_v3 = mem_rows(cache_mem_v)

    xp = x_prompt.reshape(bp * sp, d)
    xs = x_sample.transpose(1, 0, 2).reshape(ts * bs, d)
    mem2 = mem_prompt.reshape(bp * n_mem, d)
    outs = [[] for _ in range(10)]
    for l in range(depth):
        lam_init = 0.8 - 0.6 * math.exp(-0.3 * l)
        final = l == depth - 1
        w_up_b, w_down_b = w_up[l].astype(BF16), w_down[l].astype(BF16)
        lam_args = (row(lambda_q1[l]), row(lambda_k1[l]), row(lambda_q2[l]), row(lambda_k2[l]),
                    row(subln_w[l]))

        yconv_s, qm_s, kf_s, vf_s, cmix_s, w_in_b = _mix_in_sample(
            xs, row(norm_mix_w[l]), w_in[l], conv_mix_w[l], row(conv_mix_b[l]),
            state_conv_mix[l][:, 0], state_conv_mix[l][:, 1], dc=dc, da=da, ts=ts, bs=bs)
        q_all = qm_s.reshape(2, ts, bs, N_HEADS, HEAD_DV).transpose(2, 0, 3, 1, 4)
        q_all = q_all.reshape(bs, 2 * N_HEADS * ts, HEAD_DV)
        new_rows = lambda a: jnp.pad(
            a.reshape(ts, bs, N_HEADS * HEAD_DV).transpose(1, 0, 2).reshape(bs, ts * N_HEADS, HEAD_DV),
            ((0, 0), (0, LANES - ts * N_HEADS), (0, 0))).astype(BF16)
        yattn_s = _attn_sample(page_table, rel_bias, q_all, new_rows(kf_s), new_rows(vf_s), *lam_args,
                               cache_k2, cache_v2, page=page, page_offset=l * n_phys, ts=ts,
                               lam_init=lam_init)
        yattn_s = yattn_s.reshape(bs, N_HEADS, ts, HEAD_DV).transpose(2, 0, 1, 3)
        yattn_s = yattn_s.reshape(ts * bs, da).astype(BF16)
        x1_s, qc_s, w_out_b, w_cq_b = _outproj_q_sample(xs, yconv_s, yattn_s, w_out[l],
                                                        row(norm_cross_w[l]), w_cq[l])
        q2 = qc_s.reshape(ts, bs, MEM_HEADS, dh // LANES, LANES).transpose(1, 3, 2, 0, 4)
        q2 = q2.reshape(bs, (dh // LANES) * MEM_HEADS * ts, LANES)
        o_b = _cross_sample(q2, mem_k3, mem_v3, row_offset=l * bs, ts=ts)
        o_s = o_b.reshape(bs, MEM_HEADS, ts, dh).transpose(2, 0, 1, 3).reshape(ts * bs, d)
        xs, cffn_s, w_co_b = _ffn_sample(x1_s, o_s, w_co[l], row(norm_ffn_w[l]), w_up_b, conv_ffn_w[l],
                                         row(conv_ffn_b[l]), w_down_b, row(norm_final_w),
                                         state_conv_ffn[l][:, 0], state_conv_ffn[l][:, 1],
                                         ts=ts, bs=bs, final=final)

        yconv_p, q_p, kf_p, vf_p, kb_p, vb_p, cmix_p = _mix_in_prompt(
            xp, row(norm_mix_w[l]), w_in_b, conv_mix_w[l], row(conv_mix_b[l]),
            batch=bp, seq=sp, dc=dc, da=da)
        yattn_p = _attn_prompt(rel_bias, q_p, kb_p, vb_p, *lam_args[:4], subln_w[l].reshape(-1, 1).astype(F32),
                               batch=bp, seq=sp, lam_init=lam_init)
        mkf, mvf, mkb, mvb = _mem_kv(mem2, row(norm_mem_w[l]), w_ckv[l], batch=bp, n_mem=n_mem)
        x2_p = _cross_prompt(xp, yconv_p, yattn_p, w_out_b, row(norm_cross_w[l]), w_cq_b, mkb, mvb,
                             w_co_b, batch=bp, seq=sp, n_mem=n_mem)
        xp, cffn_p = _ffn_prompt(x2_p, row(norm_ffn_w[l]), w_up_b, conv_ffn_w[l], row(conv_ffn_b[l]),
                                 w_down_b, row(norm_final_w), batch=bp, seq=sp, final=final)

        t2b = lambda a: a.reshape(ts, bs, N_HEADS, HEAD_DV).transpose(1, 0, 2, 3)
        mem_out = lambda a: a.reshape(bp, n_mem, dh // LANES, MEM_HEADS, LANES).transpose(
            0, 1, 3, 2, 4).reshape(bp, n_mem, MEM_HEADS, dh)
        for lst, val in zip(outs, (
                kf_p.reshape(bp, sp, N_HEADS, 2 * HEAD_DK), vf_p.reshape(bp, sp, N_HEADS, HEAD_DV),
                t2b(kf_s), t2b(vf_s), cmix_p, cmix_s.transpose(1, 0, 2), cffn_p,
                cffn_s.transpose(1, 0, 2), mem_out(mkf), mem_out(mvf))):
            lst.append(val)

    y_prompt = xp.reshape(bp, sp, d)
    y_sample = xs.reshape(ts, bs, d).transpose(1, 0, 2)
    return (y_prompt, y_sample) + tuple(jnp.stack(o) for o in outs)
```

```python
import functools
import math

import numpy as np
import jax
import jax.numpy as jnp
from jax import lax
from jax.experimental import pallas as pl
from jax.experimental.pallas import tpu as pltpu

F32 = jnp.float32
BF16 = jnp.bfloat16

EPS = 1e-6
NEG = -1e30
LANES = 128
SUBLANES = 8
N_HEADS = 4
HEAD_DK = 64
HEAD_DV = 2 * HEAD_DK
MEM_HEADS = 4
CONV_W = 3
REL_BUCKETS = 32
REL_MAX_EXACT = 16
REL_MAX_DIST = 128
VMEM_LIMIT = 56 * 1024 * 1024

ATTN_TILE = 512
QUERY_CHUNK = 256
ONES_ROWS = 16
LOG2E = 1.4426950408889634
ROW_TILE = 512
FFN_ROW_TILE = 512
PAGES_PER_GROUP = 8
PAGE_SLOTS = 4


def _params(*sem):
    return pltpu.CompilerParams(dimension_semantics=sem, vmem_limit_bytes=VMEM_LIMIT)


def _rms(x, w):
    return x * lax.rsqrt(jnp.mean(x * x, axis=-1, keepdims=True) + EPS) * w


def _dot(a, b):
    return jnp.dot(a, b, preferred_element_type=F32)


def _dot_nt(a, b):
    return lax.dot_general(a, b, (((1,), (1,)), ((), ())), preferred_element_type=F32)


def _lam(lq1, lk1, lq2, lk2, lam_init):
    return (jnp.exp(jnp.sum(lq1[...] * lk1[...], axis=-1, keepdims=True))
            - jnp.exp(jnp.sum(lq2[...] * lk2[...], axis=-1, keepdims=True)) + lam_init)


def _bucket_np(rel):
    n = np.maximum(rel, 0)
    nf = np.maximum(n, 1).astype(np.float32)
    large = REL_MAX_EXACT + (np.log(nf / np.float32(REL_MAX_EXACT))
                             / np.float32(math.log(REL_MAX_DIST / REL_MAX_EXACT))
                             * np.float32(REL_BUCKETS - REL_MAX_EXACT)).astype(np.int32)
    large = np.minimum(large, REL_BUCKETS - 1)
    return np.where(n < REL_MAX_EXACT, n, large).astype(np.int32)


def _bucket_starts():
    buckets = _bucket_np(np.arange(REL_MAX_DIST + 1))
    assert (np.diff(buckets) >= 0).all() and buckets[-1] == REL_BUCKETS - 1
    return [int(np.argmax(buckets >= k)) for k in range(REL_BUCKETS)]


def _rel_bias_tile(rel, rb_ref, h):
    far = rb_ref[REL_BUCKETS - 1, h]
    val = jnp.full(rel.shape, rb_ref[0, h] - far, F32)
    for k, start in enumerate(_bucket_starts()):
        if k > 0:
            val = jnp.where(rel >= start, rb_ref[k, h] - far, val)
    return val


def _softmax_update(chunks, m_prev, l_prev):
    mx = chunks[0]
    for c in chunks[1:]:
        mx = jnp.maximum(mx, c)
    m_next = jnp.maximum(m_prev, jnp.max(mx, axis=1, keepdims=True))
    ps = [jnp.exp(c - m_next) for c in chunks]
    sm = ps[0]
    for p in ps[1:]:
        sm = sm + p
    alpha = jnp.exp(m_prev - m_next)
    l_next = alpha * l_prev + jnp.sum(sm, axis=1, keepdims=True)
    return ps, m_next, l_next, alpha


def _head_out(acc1, l1, acc2, l2, lam, sw, lam_init):
    o = acc1 / l1 - lam * (acc2 / l2)
    return _rms(o, sw) * (1.0 - lam_init)


def _mix_in_prompt_kernel(x_ref, nw_ref, w_ref, cw_ref, cb_ref,
                          yconv_ref, q_ref, kf_ref, vf_ref, kb_ref, vb_ref, st_ref,
                          pre_scr, *, dc, da):
    j = pl.program_id(1)
    tm = x_ref.shape[0]
    h = _rms(x_ref[...], nw_ref[...]).astype(BF16)

    def proj(lo, width):
        return _dot(h, w_ref[:, lo:lo + width])

    @pl.when(j == 0)
    def _zero_prefix():
        pre_scr[0:SUBLANES, :] = jnp.zeros((SUBLANES, dc), F32)

    pre = proj(dc, dc) * proj(2 * dc, dc)
    pre_scr[SUBLANES:SUBLANES + tm, :] = pre
    cw = cw_ref[...]
    conv = (cb_ref[...] + cw[0:1, :] * pre_scr[SUBLANES - 2:SUBLANES - 2 + tm, :]
            + cw[1:2, :] * pre_scr[SUBLANES - 1:SUBLANES - 1 + tm, :] + cw[2:3, :] * pre)
    yconv_ref[...] = (proj(0, dc) * conv).astype(BF16)
    st_ref[...] = pre[tm - 2:tm, :]
    pre_scr[0:SUBLANES, :] = pre[tm - SUBLANES:tm, :]

    q = proj(3 * dc, da) * (HEAD_DK ** -0.5 * LOG2E)
    k = proj(3 * dc + da, da)
    v = proj(3 * dc + 2 * da, da)
    ones_rows = (lax.broadcasted_iota(jnp.int32, (ONES_ROWS, tm), 0) == 0).astype(BF16)
    for hd in range(N_HEADS):
        cols = slice(hd * HEAD_DV, (hd + 1) * HEAD_DV)
        kb_ref[hd] = k[:, cols].astype(BF16)
        kf_ref[pl.ds(hd, tm, stride=N_HEADS), :] = k[:, cols]
        vf_ref[pl.ds(hd, tm, stride=N_HEADS), :] = v[:, cols]
        q_ref[hd] = q[:, cols].T.astype(BF16)
        vb_ref[hd, 0:HEAD_DV, :] = v[:, cols].T.astype(BF16)
        vb_ref[hd, HEAD_DV:, :] = ones_rows


def _mix_in_prompt(x2, nw, w_in, cw, cb, *, batch, seq, dc, da):
    d = x2.shape[1]
    tm = min(ATTN_TILE, seq)
    nj = seq // tm
    rows = lambda b, j: (b * nj + j, 0)
    const = lambda b, j: (0, 0)
    tposed = lambda r: pl.BlockSpec((None, N_HEADS, None, r, tm), lambda b, j: (b, 0, j, 0, 0))
    tposed_shape = lambda r: jax.ShapeDtypeStruct((batch, N_HEADS, nj, r, tm), BF16)
    n = batch * seq
    return pl.pallas_call(
        functools.partial(_mix_in_prompt_kernel, dc=dc, da=da),
        grid=(batch, nj),
        in_specs=[pl.BlockSpec((tm, d), rows),
                  pl.BlockSpec((1, d), const),
                  pl.BlockSpec(w_in.shape, const),
                  pl.BlockSpec(cw.shape, const),
                  pl.BlockSpec((1, dc), const)],
        out_specs=[pl.BlockSpec((tm, dc), rows),
                   tposed(HEAD_DV),
                   pl.BlockSpec((tm * N_HEADS, HEAD_DV), rows),
                   pl.BlockSpec((tm * N_HEADS, HEAD_DV), rows),
                   pl.BlockSpec((None, N_HEADS, tm, HEAD_DV), lambda b, j: (b, 0, j, 0)),
                   tposed(HEAD_DV + ONES_ROWS),
                   pl.BlockSpec((None, CONV_W - 1, dc), lambda b, j: (b, 0, 0))],
        out_shape=[jax.ShapeDtypeStruct((n, dc), BF16),
                   tposed_shape(HEAD_DV),
                   jax.ShapeDtypeStruct((n * N_HEADS, HEAD_DV), F32),
                   jax.ShapeDtypeStruct((n * N_HEADS, HEAD_DV), F32),
                   jax.ShapeDtypeStruct((batch, N_HEADS, seq, HEAD_DV), BF16),
                   tposed_shape(HEAD_DV + ONES_ROWS),
                   jax.ShapeDtypeStruct((batch, CONV_W - 1, dc), F32)],
        scratch_shapes=[pltpu.VMEM((tm + SUBLANES, dc), F32)],
        compiler_params=_params("arbitrary", "arbitrary"),
    )(x2, nw, w_in, cw, cb)


def _conv_time_major(slabs, cw, cb):
    return [cb + cw[0:1, :] * slabs[t] + cw[1:2, :] * slabs[t + 1] + cw[2:3, :] * slabs[t + 2]
            for t in range(len(slabs) - 2)]


def _mix_in_sample_kernel(x_ref, nw_ref, w_ref, cw_ref, cb_ref, s0_ref, s1_ref,
                          yconv_ref, qm_ref, kf_ref, vf_ref, st_ref, wb_ref, *, dc, da, ts, bs):
    h = _rms(x_ref[...], nw_ref[...]).astype(BF16)

    def proj(lo, width):
        wb = w_ref[:, lo:lo + width].astype(BF16)
        wb_ref[:, lo:lo + width] = wb
        return _dot(h, wb)

    pre = proj(dc, dc) * proj(2 * dc, dc)
    gate = proj(0, dc)
    slabs = [s0_ref[...], s1_ref[...]] + [pre[t * bs:(t + 1) * bs, :] for t in range(ts)]
    conv = _conv_time_major(slabs, cw_ref[...], cb_ref[...])
    for t in range(ts):
        yconv_ref[t * bs:(t + 1) * bs, :] = (gate[t * bs:(t + 1) * bs, :] * conv[t]).astype(BF16)
    st_ref[0] = slabs[-2]
    st_ref[1] = slabs[-1]

    q = proj(3 * dc, da) * (HEAD_DK ** -0.5)
    lane = lax.broadcasted_iota(jnp.int32, q.shape, 1) % HEAD_DV
    qm_ref[0] = jnp.where(lane < HEAD_DK, q, 0.0).astype(BF16)
    qm_ref[1] = jnp.where(lane >= HEAD_DK, q, 0.0).astype(BF16)
    kf_ref[...] = proj(3 * dc + da, da)
    vf_ref[...] = proj(3 * dc + 2 * da, da)


def _mix_in_sample(xt, nw, w_in, cw, cb, s0, s1, *, dc, da, ts, bs):
    n = ts * bs
    return pl.pallas_call(
        functools.partial(_mix_in_sample_kernel, dc=dc, da=da, ts=ts, bs=bs),
        out_shape=[jax.ShapeDtypeStruct((n, dc), BF16),
                   jax.ShapeDtypeStruct((2, n, da), BF16),
                   jax.ShapeDtypeStruct((n, da), F32),
                   jax.ShapeDtypeStruct((n, da), F32),
                   jax.ShapeDtypeStruct((CONV_W - 1, bs, dc), F32),
                   jax.ShapeDtypeStruct(w_in.shape, BF16)],
        compiler_params=pltpu.CompilerParams(vmem_limit_bytes=VMEM_LIMIT),
    )(xt, nw, w_in, cw, cb, s0, s1)


def _mem_kv_kernel(m_ref, nw_ref, w_ref, kf_ref, vf_ref, kb_ref, vb_ref, *, d):
    n_mem = m_ref.shape[0]
    dh = d // MEM_HEADS
    nh = dh // LANES
    h = _rms(m_ref[...], nw_ref[...]).astype(BF16)
    k = _dot(h, w_ref[:, 0:d].astype(BF16))
    kb_ref[...] = k.astype(BF16)
    v = _dot(h, w_ref[:, d:2 * d].astype(BF16))
    vb_ref[...] = v.astype(BF16)
    for hd in range(MEM_HEADS):
        for half in range(nh):
            rows = pl.ds(half * MEM_HEADS + hd, n_mem, stride=nh * MEM_HEADS)
            cols = slice(hd * dh + half * LANES, hd * dh + (half + 1) * LANES)
            kf_ref[rows, :] = k[:, cols]
            vf_ref[rows, :] = v[:, cols]


def _mem_kv(mem2, nw, w_ckv, *, batch, n_mem):
    d = mem2.shape[1]
    rows = lambda b: (b, 0)
    const = lambda b: (0, 0)
    n = batch * n_mem
    return pl.pallas_call(
        functools.partial(_mem_kv_kernel, d=d),
        grid=(batch,),
        in_specs=[pl.BlockSpec((n_mem, d), rows), pl.BlockSpec((1, d), const),
                  pl.BlockSpec(w_ckv.shape, const, pipeline_mode=pl.Buffered(1))],
        out_specs=[pl.BlockSpec((n_mem * d // LANES, LANES), rows)] * 2 + [pl.BlockSpec((n_mem, d), rows)] * 2,
        out_shape=[jax.ShapeDtypeStruct((n * d // LANES, LANES), F32)] * 2
                  + [jax.ShapeDtypeStruct((n, d), BF16)] * 2,
        compiler_params=_params("arbitrary"),
    )(mem2, nw, w_ckv)


def _attn_prompt_kernel(rb_ref, q_ref, k_ref, vt_ref, lq1, lk1, lq2, lk2, sw_ref, o_ref,
                        qa_scr, ma_scr, acca_scr, qb_scr, mb_scr, accb_scr, bd_scr, bs_scr, sa_scr, sb_scr,
                        *, T, lam_init):
    nb = T // LANES
    n_tiles = q_ref.shape[0]

    def _assemble_bias_tiles():
        h = pl.program_id(1)
        rel = (lax.broadcasted_iota(jnp.int32, (LANES, LANES), 1)
               - lax.broadcasted_iota(jnp.int32, (LANES, LANES), 0))
        d0 = jnp.where(rel >= 0, _rel_bias_tile(jnp.maximum(rel, 0), rb_ref, h) * LOG2E, NEG)
        d1 = _rel_bias_tile(rel + LANES, rb_ref, h) * LOG2E
        zero = jnp.zeros((LANES, LANES), F32)
        neg = jnp.full((LANES, LANES), NEG, F32)
        for bi in range(nb):
            for bj in range(nb):
                rs = slice(bi * LANES, (bi + 1) * LANES)
                cs = slice(bj * LANES, (bj + 1) * LANES)
                bd_scr[rs, cs] = d0 if bi == bj else d1 if bj == bi + 1 else zero if bj > bi else neg
                bs_scr[rs, cs] = d1 if (bj == 0 and bi == nb - 1) else zero

    _assemble_bias_tiles()
    n_chunks = 2 * T // QUERY_CHUNK

    def prep(st, qi):
        qp, m, acc = st
        qt = q_ref[qi].astype(F32)
        sub = lax.broadcasted_iota(jnp.int32, qt.shape, 0)
        qp[:, 0:T] = jnp.where(sub < HEAD_DK, qt, 0.0).astype(BF16)
        qp[:, T:2 * T] = jnp.where(sub >= HEAD_DK, qt, 0.0).astype(BF16)
        m[...] = jnp.full(m.shape, -jnp.inf, F32)
        acc[...] = jnp.zeros(acc.shape, F32)

    def chunk_scores(kh, qp, dst, c):
        dst[c] = _dot(kh, qp[:, c * QUERY_CHUNK:(c + 1) * QUERY_CHUNK])

    def key_block(j):
        return k_ref[pl.ds(pl.multiple_of(j * T, T), T), :]

    def stage(st, j, src, bias_ref, nxt=None):
        _, m_scr, acc_scr = st
        if nxt is not None:
            j_next, qp_next, dst = nxt
            kh_next = key_block(j_next)
        for c in range(n_chunks):
            if nxt is not None and dst is not src:
                chunk_scores(kh_next, qp_next, dst, c)
            cs = slice(c * QUERY_CHUNK, (c + 1) * QUERY_CHUNK)
            lo = (c * QUERY_CHUNK) % T
            nk = min(T, lo + QUERY_CHUNK) if bias_ref is bd_scr else T
            s = src[c, 0:nk, :]
            if bias_ref is not None:
                s = s + bias_ref[0:nk, lo:lo + QUERY_CHUNK]
            m_prev = m_scr[:, cs]
            m_next = jnp.maximum(m_prev, jnp.max(s, axis=0, keepdims=True))
            p = jnp.exp2(s - m_next).astype(BF16)
            if nxt is not None and dst is src:
                chunk_scores(kh_next, qp_next, dst, c)
            acc_scr[:, cs] = (jnp.exp2(m_prev - m_next) * acc_scr[:, cs]
                              + _dot(vt_ref[j, :, 0:nk], p))
            m_scr[:, cs] = m_next

    def finalize(st, qi):
        _, _, acc_scr = st
        lam = _lam(lq1, lk1, lq2, lk2, lam_init)
        acc = acc_scr[0:HEAD_DV, :]
        l = acc_scr[HEAD_DV:HEAD_DV + 1, :]
        o = acc[:, 0:T] / l[:, 0:T] - lam * (acc[:, T:2 * T] / l[:, T:2 * T])
        y = o * lax.rsqrt(jnp.mean(o * o, axis=0, keepdims=True) + EPS) * sw_ref[...] * (1.0 - lam_init)
        o_ref[pl.ds(pl.multiple_of(qi * T, T), T), :] = y.T.astype(o_ref.dtype)

    even = (qa_scr, ma_scr, acca_scr)
    odd = (qb_scr, mb_scr, accb_scr)
    prep(even, 0)
    kh0 = key_block(0)
    for c in range(n_chunks):
        chunk_scores(kh0, qa_scr, sb_scr, c)

    def tile_pair(t, carry):
        a = 2 * t
        b = a + 1

        @pl.when(t >= 1)
        def _even_first_block():
            stage(even, 0, sb_scr, None, (1, qa_scr, sa_scr))

        def even_far_pair(i, c):
            j = 1 + 2 * i
            stage(even, j, sa_scr, None, (j + 1, qa_scr, sb_scr))
            stage(even, j + 1, sb_scr, None, (j + 2, qa_scr, sa_scr))
            return c

        lax.fori_loop(0, jnp.maximum(t - 1, 0), even_far_pair, 0)
        prep(odd, b)

        @pl.when(t >= 1)
        def _even_last_blocks():
            stage(even, a - 1, sa_scr, bs_scr, (a, qa_scr, sb_scr))
            stage(even, a, sb_scr, bd_scr, (0, qb_scr, sa_scr))

        @pl.when(t == 0)
        def _even_only_block():
            stage(even, 0, sb_scr, bd_scr, (0, qb_scr, sa_scr))

        def odd_far_pair(i, c):
            j = 2 * i
            stage(odd, j, sa_scr, None, (j + 1, qb_scr, sb_scr))
            stage(odd, j + 1, sb_scr, None, (j + 2, qb_scr, sa_scr))
            return c

        lax.fori_loop(0, t, odd_far_pair, 0)
        finalize(even, a)
        prep(even, jnp.minimum(a + 2, n_tiles - 1))
        stage(odd, b - 1, sa_scr, bs_scr, (b, qb_scr, sb_scr))
        stage(odd, b, sb_scr, bd_scr, (0, qa_scr, sb_scr))
        finalize(odd, b)
        return carry

    lax.fori_loop(0, n_tiles // 2, tile_pair, 0)


def _attn_prompt(rel_bias, qt, kb, vt, lq1, lk1, lq2, lk2, sw_col, *, batch, seq, lam_init):
    T = qt.shape[-1]
    nq = seq // T
    assert nq % 2 == 0, "query tiles run in (even, odd) pairs"
    da = N_HEADS * HEAD_DV
    vec = lambda b, h: (0, 0)
    return pl.pallas_call(
        functools.partial(_attn_prompt_kernel, T=T, lam_init=lam_init),
        grid=(batch, N_HEADS),
        in_specs=[pl.BlockSpec(memory_space=pltpu.SMEM),
                  pl.BlockSpec((None, None, nq, HEAD_DV, T), lambda b, h: (b, h, 0, 0, 0)),
                  pl.BlockSpec((None, None, seq, HEAD_DV), lambda b, h: (b, h, 0, 0)),
                  pl.BlockSpec((None, None, nq, vt.shape[3], T), lambda b, h: (b, h, 0, 0, 0)),
                  pl.BlockSpec((1, HEAD_DK), vec), pl.BlockSpec((1, HEAD_DK), vec),
                  pl.BlockSpec((1, HEAD_DK), vec), pl.BlockSpec((1, HEAD_DK), vec),
                  pl.BlockSpec((HEAD_DV, 1), vec)],
        out_specs=pl.BlockSpec((seq, HEAD_DV), lambda b, h: (b, h)),
        out_shape=jax.ShapeDtypeStruct((batch * seq, da), BF16),
        scratch_shapes=[pltpu.VMEM((HEAD_DV, 2 * T), BF16),
                        pltpu.VMEM((1, 2 * T), F32),
                        pltpu.VMEM((vt.shape[3], 2 * T), F32),
                        pltpu.VMEM((HEAD_DV, 2 * T), BF16),
                        pltpu.VMEM((1, 2 * T), F32),
                        pltpu.VMEM((vt.shape[3], 2 * T), F32),
                        pltpu.VMEM((T, T), F32),
                        pltpu.VMEM((T, T), F32),
                        pltpu.VMEM((2 * T // QUERY_CHUNK, T, QUERY_CHUNK), F32),
                        pltpu.VMEM((2 * T // QUERY_CHUNK, T, QUERY_CHUNK), F32)],
        compiler_params=_params("arbitrary", "arbitrary"),
    )(rel_bias, qt, kb, vt, lq1, lk1, lq2, lk2, sw_col)


def _attn_sample_kernel(pt_ref, rb_ref, q_ref, kn_ref, vn_ref, lq1, lk1, lq2, lk2, sw_ref, k_hbm, v_hbm,
                        o_ref, m_scr, l_scr, acc_scr, hm_scr, bl_scr, bn_scr, kbuf, vbuf, sems,
                        *, pages, page_offset, ts, lam_init):
    b = pl.program_id(0)
    nb = pl.num_programs(0)
    rows, pk = hm_scr.shape
    nk = bn_scr.shape[1]
    rm = rows // 2
    page = pk // N_HEADS
    groups = pt_ref.shape[1] // pages
    hbits = N_HEADS.bit_length() - 1
    tbits = ts.bit_length() - 1

    def group_copies(t, slot):
        bt = t // groups
        g0 = (t % groups) * pages
        out = []
        for i in range(pages):
            row0 = pl.multiple_of((page_offset + pt_ref[bt, g0 + i]) * pk, pk)
            dst = pl.ds(i * pk, pk)
            out.append(pltpu.make_async_copy(k_hbm.at[pl.ds(row0, pk)], kbuf.at[slot, dst], sems.at[0, slot]))
            out.append(pltpu.make_async_copy(v_hbm.at[pl.ds(row0, pk)], vbuf.at[slot, dst], sems.at[1, slot]))
        return out

    def start_group(t):
        @pl.when(t < nb * groups)
        def _():
            for cp in group_copies(t, t % PAGE_SLOTS):
                cp.start()

    @pl.when(b == 0)
    def _first_step():
        for t in range(PAGE_SLOTS - 1):
            start_group(jnp.int32(t))

        def tables(ncols, rel_of):
            r = lax.broadcasted_iota(jnp.int32, (rows, ncols), 0)
            c = lax.broadcasted_iota(jnp.int32, (rows, ncols), 1)
            rhead = (r >> tbits) & (N_HEADS - 1)
            same = rhead == (c & (N_HEADS - 1))
            rel = rel_of(r & (ts - 1), c >> hbits)
            bias = jnp.zeros((rows, ncols), F32)
            for h in range(N_HEADS):
                bias = jnp.where(rhead == h, _rel_bias_tile(jnp.maximum(rel, 0), rb_ref, h), bias)
            return same, rel, c >> hbits, bias

        same, _, _, bias = tables(pk, lambda tok, key: tok + page - key)
        hm_scr[...] = jnp.where(same, 0.0, NEG)
        bl_scr[...] = jnp.where(same, bias, NEG)
        same, rel, key, bias = tables(nk, lambda tok, key: tok - key)
        bn_scr[...] = jnp.where(same & (rel >= 0) & (key < ts), bias, NEG)

    m_scr[...] = jnp.full(m_scr.shape, -jnp.inf, F32)
    l_scr[...] = jnp.zeros(l_scr.shape, F32)
    acc_scr[...] = jnp.zeros(acc_scr.shape, F32)

    def update(scores, values):
        chunks, sizes = [], []
        for s in scores:
            n = s.shape[1] // LANES
            sizes.append(n)
            chunks += [s[:, c * LANES:(c + 1) * LANES] for c in range(n)]
        ps, m_next, l_next, alpha = _softmax_update(chunks, m_scr[...], l_scr[...])
        pv, at = None, 0
        for n, v in zip(sizes, values):
            p = jnp.concatenate([x.astype(BF16) for x in ps[at:at + n]], axis=1) if n > 1 \
                else ps[at].astype(BF16)
            at += n
            pv = _dot(p, v) if pv is None else pv + _dot(p, v)
        acc_scr[...] = alpha * acc_scr[...] + pv
        m_scr[...] = m_next
        l_scr[...] = l_next

    q = q_ref[...]

    def group_body(g, carry):
        t = b * groups + g
        slot = t % PAGE_SLOTS
        for cp in group_copies(t, slot):
            cp.wait()
        start_group(t + PAGE_SLOTS - 1)
        kg = kbuf.at[slot]
        vg = vbuf.at[slot]
        last = g == groups - 1
        scores = [_dot_nt(q, kg[i * pk:(i + 1) * pk, :].astype(BF16)) + hm_scr[...]
                  for i in range(pages - 1)]
        scores.append(_dot_nt(q, kg[(pages - 1) * pk:pages * pk, :].astype(BF16))
                      + jnp.where(last, bl_scr[...], hm_scr[...]))
        update(scores, [vg[i * pk:(i + 1) * pk, :].astype(BF16) for i in range(pages)])
        return carry

    lax.fori_loop(0, groups, group_body, 0)

    update([_dot_nt(q, kn_ref[...]) + bn_scr[...]], [vn_ref[...]])
    lam = _lam(lq1, lk1, lq2, lk2, lam_init)
    acc = acc_scr[...]
    l = l_scr[...]
    o_ref[...] = _head_out(acc[0:rm], l[0:rm], acc[rm:rows], l[rm:rows], lam, sw_ref[...], lam_init)


def _attn_sample(page_table, rel_bias, q_all, knew, vnew, lq1, lk1, lq2, lk2, sw, cache_k, cache_v,
                 *, page, page_offset, ts, lam_init):
    bs, n_pages = page_table.shape
    rows = q_all.shape[1]
    nk = knew.shape[1]
    pk = page * N_HEADS
    pages = math.gcd(PAGES_PER_GROUP, n_pages)
    vec = lambda b, pt: (0, 0)
    per_b = lambda b, pt: (b, 0, 0)
    grid_spec = pltpu.PrefetchScalarGridSpec(
        num_scalar_prefetch=1,
        grid=(bs,),
        in_specs=[pl.BlockSpec(memory_space=pltpu.SMEM),
                  pl.BlockSpec((None, rows, HEAD_DV), per_b),
                  pl.BlockSpec((None, nk, HEAD_DV), per_b),
                  pl.BlockSpec((None, nk, HEAD_DV), per_b),
                  pl.BlockSpec((1, HEAD_DK), vec), pl.BlockSpec((1, HEAD_DK), vec),
                  pl.BlockSpec((1, HEAD_DK), vec), pl.BlockSpec((1, HEAD_DK), vec),
                  pl.BlockSpec((1, HEAD_DV), vec),
                  pl.BlockSpec(memory_space=pl.ANY), pl.BlockSpec(memory_space=pl.ANY)],
        out_specs=pl.BlockSpec((None, rows // 2, HEAD_DV), per_b),
        scratch_shapes=[pltpu.VMEM((rows, LANES), F32),
                        pltpu.VMEM((rows, LANES), F32),
                        pltpu.VMEM((rows, HEAD_DV), F32),
                        pltpu.VMEM((rows, pk), F32),
                        pltpu.VMEM((rows, pk), F32),
                        pltpu.VMEM((rows, nk), F32),
                        pltpu.VMEM((PAGE_SLOTS, pages * pk, HEAD_DV), F32),
                        pltpu.VMEM((PAGE_SLOTS, pages * pk, HEAD_DV), F32),
                        pltpu.SemaphoreType.DMA((2, PAGE_SLOTS))],
    )
    return pl.pallas_call(
        functools.partial(_attn_sample_kernel, pages=pages, page_offset=page_offset, ts=ts,
                          lam_init=lam_init),
        grid_spec=grid_spec,
        out_shape=jax.ShapeDtypeStruct((bs, rows // 2, HEAD_DV), F32),
        compiler_params=_params("arbitrary"),
    )(page_table, rel_bias, q_all, knew, vnew, lq1, lk1, lq2, lk2, sw, cache_k, cache_v)


def _cross_heads(qc, mk_ref, mv_ref, o_scr, dh):
    for hd in range(MEM_HEADS):
        cols = slice(hd * dh, (hd + 1) * dh)
        s = _dot_nt(qc[:, cols], mk_ref[:, cols].astype(BF16))
        p = jnp.exp(s - jnp.max(s, axis=1, keepdims=True))
        o = _dot(p.astype(BF16), mv_ref[:, cols].astype(BF16)) / jnp.sum(p, axis=1, keepdims=True)
        o_scr[:, cols] = o.astype(BF16)


def _cross_prompt_kernel(x_ref, yc_ref, ya_ref, wo_ref, nw_ref, wq_ref, mk_ref, mv_ref, wc_ref,
                         o_ref, o_scr, *, dc, dh):
    x1 = x_ref[...] + _dot(yc_ref[...], wo_ref[0:dc, :]) + _dot(ya_ref[...], wo_ref[dc:, :])
    h = _rms(x1, nw_ref[...]).astype(BF16)
    qc = (_dot(h, wq_ref[...]) * (dh ** -0.5)).astype(BF16)
    _cross_heads(qc, mk_ref, mv_ref, o_scr, dh)
    o_ref[...] = x1 + _dot(o_scr[...], wc_ref[...])


def _cross_prompt(x2, yconv, yattn, w_out, nw, w_cq, mk, mv, w_co, *, batch, seq, n_mem):
    d = x2.shape[1]
    dc = yconv.shape[1]
    da = yattn.shape[1]
    tm = min(ROW_TILE, seq)
    nj = seq // tm
    rows = lambda b, j: (b * nj + j, 0)
    const = lambda b, j: (0, 0)
    memb = lambda b, j: (b, 0)
    return pl.pallas_call(
        functools.partial(_cross_prompt_kernel, dc=dc, dh=d // MEM_HEADS),
        grid=(batch, nj),
        in_specs=[pl.BlockSpec((tm, d), rows), pl.BlockSpec((tm, dc), rows), pl.BlockSpec((tm, da), rows),
                  pl.BlockSpec(w_out.shape, const), pl.BlockSpec((1, d), const),
                  pl.BlockSpec(w_cq.shape, const),
                  pl.BlockSpec((n_mem, d), memb), pl.BlockSpec((n_mem, d), memb),
                  pl.BlockSpec(w_co.shape, const)],
        out_specs=pl.BlockSpec((tm, d), rows),
        out_shape=jax.ShapeDtypeStruct(x2.shape, F32),
        scratch_shapes=[pltpu.VMEM((tm, d), BF16)],
        compiler_params=_params("arbitrary", "arbitrary"),
    )(x2, yconv, yattn, w_out, nw, w_cq, mk, mv, w_co)


def _outproj_q_sample_kernel(x_ref, yc_ref, ya_ref, wo_ref, nw_ref, wq_ref, x1_ref, qc_ref,
                             wob_ref, wqb_ref, *, dc, dh):
    wob_ref[...] = wo_ref[...].astype(BF16)
    wqb_ref[...] = wq_ref[...].astype(BF16)
    x1 = x_ref[...] + _dot(yc_ref[...], wob_ref[0:dc, :]) + _dot(ya_ref[...], wob_ref[dc:, :])
    x1_ref[...] = x1
    h = _rms(x1, nw_ref[...]).astype(BF16)
    qc_ref[...] = (_dot(h, wqb_ref[...]) * (dh ** -0.5)).astype(BF16)


def _outproj_q_sample(xt, yconv, yattn, w_out, nw, w_cq):
    d = xt.shape[1]
    return pl.pallas_call(
        functools.partial(_outproj_q_sample_kernel, dc=yconv.shape[1], dh=d // MEM_HEADS),
        out_shape=[jax.ShapeDtypeStruct(xt.shape, F32), jax.ShapeDtypeStruct(xt.shape, BF16),
                   jax.ShapeDtypeStruct(w_out.shape, BF16), jax.ShapeDtypeStruct(w_cq.shape, BF16)],
        compiler_params=pltpu.CompilerParams(vmem_limit_bytes=VMEM_LIMIT),
    )(xt, yconv, yattn, w_out, nw, w_cq)


def _cross_sample_kernel(q_ref, mk_ref, mv_ref, o_ref, *, ts):
    rq = q_ref.shape[0]
    ncols = mk_ref.shape[0]
    per_key = (rq // (MEM_HEADS * ts)) * MEM_HEADS
    nh = per_key // MEM_HEADS
    rows = MEM_HEADS * ts
    s2 = _dot_nt(q_ref[...], mk_ref[...].astype(BF16))
    s = s2[0:rows]
    for half in range(1, nh):
        s = s + pltpu.roll(s2[half * rows:(half + 1) * rows], ncols - half * MEM_HEADS, axis=1)
    col = lax.broadcasted_iota(jnp.int32, (rows, ncols), 1) & (per_key - 1)
    head = lax.broadcasted_iota(jnp.int32, (rows, ncols), 0) >> (ts.bit_length() - 1)
    s = jnp.where(col == head, s, NEG)
    p = jnp.exp(s - jnp.max(s, axis=1, keepdims=True))
    l = jnp.sum(p, axis=1, keepdims=True)
    lhs = jnp.concatenate([p] + [pltpu.roll(p, half * MEM_HEADS, axis=1) for half in range(1, nh)], axis=0)
    o2 = _dot(lhs.astype(BF16), mv_ref[...].astype(BF16))
    o = jnp.concatenate([o2[half * rows:(half + 1) * rows] for half in range(nh)], axis=1) / l
    o_ref[...] = o.astype(o_ref.dtype)


def _cross_sample(q2, mk, mv, *, row_offset, ts):
    bs, rq, _ = q2.shape
    ncols = mk.shape[1]
    rows = MEM_HEADS * ts
    nh = rq // rows
    per_b = lambda b: (b, 0, 0)
    mem_b = lambda b: (row_offset + b, 0, 0)
    return pl.pallas_call(
        functools.partial(_cross_sample_kernel, ts=ts),
        grid=(bs,),
        in_specs=[pl.BlockSpec((None, rq, LANES), per_b), pl.BlockSpec((None, ncols, LANES), mem_b),
                  pl.BlockSpec((None, ncols, LANES), mem_b)],
        out_specs=pl.BlockSpec((None, rows, nh * LANES), per_b),
        out_shape=jax.ShapeDtypeStruct((bs, rows, nh * LANES), BF16),
        compiler_params=_params("arbitrary"),
    )(q2, mk, mv)


def _silu(g):
    return g * (1.0 / (1.0 + jnp.exp(-g)))


def _ffn_prompt_kernel(x_ref, nw_ref, wu_ref, cw_ref, cb_ref, wd_ref, fw_ref, y_ref, st_ref,
                       up_scr, carry_scr, *, dff, final):
    j = pl.program_id(1)
    tm = x_ref.shape[0]
    x = x_ref[...]
    h = _rms(x, nw_ref[...]).astype(BF16)

    @pl.when(j == 0)
    def _zero_prefix():
        carry_scr[...] = jnp.zeros(carry_scr.shape, F32)

    def conv_half(lo):
        cols = slice(lo, lo + dff)
        up = _dot(h, wu_ref[:, cols])
        up_scr[0:SUBLANES, :] = carry_scr[:, cols]
        up_scr[SUBLANES:SUBLANES + tm, :] = up
        cw = cw_ref[:, cols]
        conv = (cb_ref[:, cols] + cw[0:1, :] * up_scr[SUBLANES - 2:SUBLANES - 2 + tm, :]
                + cw[1:2, :] * up_scr[SUBLANES - 1:SUBLANES - 1 + tm, :] + cw[2:3, :] * up)
        carry_scr[:, cols] = up[tm - SUBLANES:tm, :]
        st_ref[:, cols] = up[tm - 2:tm, :]
        return conv

    g = conv_half(0)
    u = conv_half(dff)
    hid = (_silu(g) * u).astype(BF16)
    rh = tm // 2
    for r in range(2):
        rows = slice(r * rh, (r + 1) * rh)
        x3 = x[rows] + _dot(hid[rows], wd_ref[...])
        y_ref[rows, :] = _rms(x3, fw_ref[...]) if final else x3


def _ffn_prompt(x2, nw, w_up, cw, cb, w_down, fw, *, batch, seq, final):
    d = x2.shape[1]
    dff = w_down.shape[0]
    tm = min(FFN_ROW_TILE, seq)
    nj = seq // tm
    rows = lambda b, j: (b * nj + j, 0)
    const = lambda b, j: (0, 0)
    resident = lambda shape: pl.BlockSpec(shape, const, pipeline_mode=pl.Buffered(1))
    return pl.pallas_call(
        functools.partial(_ffn_prompt_kernel, dff=dff, final=final),
        grid=(batch, nj),
        in_specs=[pl.BlockSpec((tm, d), rows), pl.BlockSpec((1, d), const),
                  resident(w_up.shape), pl.BlockSpec(cw.shape, const),
                  pl.BlockSpec((1, 2 * dff), const), resident(w_down.shape),
                  pl.BlockSpec((1, d), const)],
        out_specs=[pl.BlockSpec((tm, d), rows),
                   pl.BlockSpec((None, CONV_W - 1, 2 * dff), lambda b, j: (b, 0, 0))],
        out_shape=[jax.ShapeDtypeStruct(x2.shape, F32),
                   jax.ShapeDtypeStruct((batch, CONV_W - 1, 2 * dff), F32)],
        scratch_shapes=[pltpu.VMEM((tm + SUBLANES, dff), F32),
                        pltpu.VMEM((SUBLANES, 2 * dff), F32)],
        compiler_params=_params("arbitrary", "arbitrary"),
    )(x2, nw, w_up, cw, cb, w_down, fw)


def _ffn_sample_kernel(x1_ref, o_ref, wc_ref, nw_ref, wu_ref, cw_ref, cb_ref, wd_ref, fw_ref,
                       s0_ref, s1_ref, y_ref, st_ref, wcb_ref, hid_scr, *, dff, ts, bs, final):
    wcb_ref[...] = wc_ref[...].astype(BF16)
    x2 = x1_ref[...] + _dot(o_ref[...], wcb_ref[...])
    h = _rms(x2, nw_ref[...]).astype(BF16)

    def conv_half(lo):
        cols = slice(lo, lo + dff)
        up = _dot(h, wu_ref[:, cols])
        slabs = [s0_ref[:, cols], s1_ref[:, cols]] + [up[t * bs:(t + 1) * bs, :] for t in range(ts)]
        st_ref[0, :, cols] = slabs[-2]
        st_ref[1, :, cols] = slabs[-1]
        return _conv_time_major(slabs, cw_ref[:, cols], cb_ref[:, cols])

    g = conv_half(0)
    u = conv_half(dff)
    for t in range(ts):
        hid_scr[t * bs:(t + 1) * bs, :] = (_silu(g[t]) * u[t]).astype(BF16)
    x3 = x2 + _dot(hid_scr[...], wd_ref[...])
    y_ref[...] = _rms(x3, fw_ref[...]) if final else x3


def _ffn_sample(x1, o, w_co, nw, w_up, cw, cb, w_down, fw, s0, s1, *, ts, bs, final):
    dff = w_down.shape[0]
    return pl.pallas_call(
        functools.partial(_ffn_sample_kernel, dff=dff, ts=ts, bs=bs, final=final),
        out_shape=[jax.ShapeDtypeStruct(x1.shape, F32),
                   jax.ShapeDtypeStruct((CONV_W - 1, bs, 2 * dff), F32),
                   jax.ShapeDtypeStruct(w_co.shape, BF16)],
        scratch_shapes=[pltpu.VMEM((ts * bs, dff), BF16)],
        compiler_params=pltpu.CompilerParams(vmem_limit_bytes=VMEM_LIMIT),
    )(x1, o, w_co, nw, w_up, cw, cb, w_down, fw, s0, s1)


def kernel(x_prompt, x_sample, mem_prompt, cache_k, cache_v, page_table, state_conv_mix, state_conv_ffn, cache_mem_k, cache_mem_v, rel_bias, norm_mix_w, w_in, conv_mix_w, conv_mix_b, lambda_q1, lambda_k1, lambda_q2, lambda_k2, subln_w, w_out, norm_cross_w, norm_mem_w, w_cq, w_ckv, w_co, norm_ffn_w, w_up, conv_ffn_w, conv_ffn_b, w_down, norm_final_w):
    depth = w_in.shape[0]
    bp, sp, d = x_prompt.shape
    bs, ts, _ = x_sample.shape
    n_mem = mem_prompt.shape[1]
    n_phys, page = cache_k.shape[1], cache_k.shape[2]
    n_pages = page_table.shape[1]
    past_len = n_pages * page
    dc = conv_mix_w.shape[2]
    da = N_HEADS * HEAD_DV
    dff = w_down.shape[1]
    dh = d // MEM_HEADS
    assert page == LANES and CONV_W - 1 <= ts <= SUBLANES and ts & (ts - 1) == 0
    assert (N_HEADS * ts) % SUBLANES == 0 and cache_k.shape[3:] == (N_HEADS, HEAD_DV)
    rows = 2 * SUBLANES

    assert _bucket_np(np.arange(REL_MAX_DIST, max(sp, past_len + ts) + 1)).min() == REL_BUCKETS - 1
    row = lambda a: a.reshape(1, -1).astype(F32)
    rel_bias = rel_bias.astype(F32)
    cache_k2 = cache_k.reshape(depth * n_phys * page * N_HEADS, HEAD_DV)
    cache_v2 = cache_v.reshape(depth * n_phys * page * N_HEADS, HEAD_DV)
    mem_rows = lambda a: a.reshape(depth * bs, n_mem, MEM_HEADS, dh // LANES, LANES).transpose(
        0, 1, 3, 2, 4).reshape(depth * bs, n_mem * (dh // LANES) * MEM_HEADS, LANES)
    mem_k3 = mem_rows(cache_mem_k)
    mem_v3 = mem_rows(cache_mem_v)

    xp = x_prompt.reshape(bp * sp, d)
    xs = x_sample.transpose(1, 0, 2).reshape(ts * bs, d)
    mem2 = mem_prompt.reshape(bp * n_mem, d)
    outs = [[] for _ in range(10)]
    for l in range(depth):
        lam_init = 0.8 - 0.6 * math.exp(-0.3 * l)
        final = l == depth - 1
        w_up_b, w_down_b = w_up[l].astype(BF16), w_down[l].astype(BF16)
        lam_args = (row(lambda_q1[l]), row(lambda_k1[l]), row(lambda_q2[l]), row(lambda_k2[l]),
                    row(subln_w[l]))

        yconv_s, qm_s, kf_s, vf_s, cmix_s, w_in_b = _mix_in_sample(
            xs, row(norm_mix_w[l]), w_in[l], conv_mix_w[l], row(conv_mix_b[l]),
            state_conv_mix[l][:, 0], state_conv_mix[l][:, 1], dc=dc, da=da, ts=ts, bs=bs)
        q_all = qm_s.reshape(2, ts, bs, N_HEADS, HEAD_DV).transpose(2, 0, 3, 1, 4)
        q_all = q_all.reshape(bs, 2 * N_HEADS * ts, HEAD_DV)
        new_rows = lambda a: jnp.pad(
            a.reshape(ts, bs, N_HEADS * HEAD_DV).transpose(1, 0, 2).reshape(bs, ts * N_HEADS, HEAD_DV),
            ((0, 0), (0, LANES - ts * N_HEADS), (0, 0))).astype(BF16)
        yattn_s = _attn_sample(page_table, rel_bias, q_all, new_rows(kf_s), new_rows(vf_s), *lam_args,
                               cache_k2, cache_v2, page=page, page_offset=l * n_phys, ts=ts,
                               lam_init=lam_init)
        yattn_s = yattn_s.reshape(bs, N_HEADS, ts, HEAD_DV).transpose(2, 0, 1, 3)
        yattn_s = yattn_s.reshape(ts * bs, da).astype(BF16)
        x1_s, qc_s, w_out_b, w_cq_b = _outproj_q_sample(xs, yconv_s, yattn_s, w_out[l],
                                                        row(norm_cross_w[l]), w_cq[l])
        q2 = qc_s.reshape(ts, bs, MEM_HEADS, dh // LANES, LANES).transpose(1, 3, 2, 0, 4)
        q2 = q2.reshape(bs, (dh // LANES) * MEM_HEADS * ts, LANES)
        o_b = _cross_sample(q2, mem_k3, mem_v3, row_offset=l * bs, ts=ts)
        o_s = o_b.reshape(bs, MEM_HEADS, ts, dh).transpose(2, 0, 1, 3).reshape(ts * bs, d)
        xs, cffn_s, w_co_b = _ffn_sample(x1_s, o_s, w_co[l], row(norm_ffn_w[l]), w_up_b, conv_ffn_w[l],
                                         row(conv_ffn_b[l]), w_down_b, row(norm_final_w),
                                         state_conv_ffn[l][:, 0], state_conv_ffn[l][:, 1],
                                         ts=ts, bs=bs, final=final)

        yconv_p, q_p, kf_p, vf_p, kb_p, vb_p, cmix_p = _mix_in_prompt(
            xp, row(norm_mix_w[l]), w_in_b, conv_mix_w[l], row(conv_mix_b[l]),
            batch=bp, seq=sp, dc=dc, da=da)
        yattn_p = _attn_prompt(rel_bias, q_p, kb_p, vb_p, *lam_args[:4], subln_w[l].reshape(-1, 1).astype(F32),
                               batch=bp, seq=sp, lam_init=lam_init)
        mkf, mvf, mkb, mvb = _mem_kv(mem2, row(norm_mem_w[l]), w_ckv[l], batch=bp, n_mem=n_mem)
        x2_p = _cross_prompt(xp, yconv_p, yattn_p, w_out_b, row(norm_cross_w[l]), w_cq_b, mkb, mvb,
                             w_co_b, batch=bp, seq=sp, n_mem=n_mem)
        xp, cffn_p = _ffn_prompt(x2_p, row(norm_ffn_w[l]), w_up_b, conv_ffn_w[l], row(conv_ffn_b[l]),
                                 w_down_b, row(norm_final_w), batch=bp, seq=sp, final=final)

        t2b = lambda a: a.reshape(ts, bs, N_HEADS, HEAD_DV).transpose(1, 0, 2, 3)
        mem_out = lambda a: a.reshape(bp, n_mem, dh // LANES, MEM_HEADS, LANES).transpose(
            0, 1, 3, 2, 4).reshape(bp, n_mem, MEM_HEADS, dh)
        for lst, val in zip(outs, (
                kf_p.reshape(bp, sp, N_HEADS, 2 * HEAD_DK), vf_p.reshape(bp, sp, N_HEADS, HEAD_DV),
                t2b(kf_s), t2b(vf_s), cmix_p, cmix_s.transpose(1, 0, 2), cffn_p,
                cffn_s.transpose(1, 0, 2), mem_out(mkf), mem_out(mvf))):
            lst.append(val)

    y_prompt = xp.reshape(bp, sp, d)
    y_sample = xs.reshape(ts, bs, d).transpose(1, 0, 2)
    return (y_prompt, y_sample) + tuple(jnp.stack(o) for o in outs)
```

```python
import functools
import math

import numpy as np
import jax
import jax.numpy as jnp
from jax import lax
from jax.experimental import pallas as pl
from jax.experimental.pallas import tpu as pltpu

F32 = jnp.float32
BF16 = jnp.bfloat16

EPS = 1e-6
NEG = -1e30
LANES = 128
SUBLANES = 8
N_HEADS = 4
HEAD_DK = 64
HEAD_DV = 2 * HEAD_DK
MEM_HEADS = 4
CONV_W = 3
REL_BUCKETS = 32
REL_MAX_EXACT = 16
REL_MAX_DIST = 128
VMEM_LIMIT = 56 * 1024 * 1024

ATTN_TILE = 512
QUERY_CHUNK = 256
ONES_ROWS = 16
LOG2E = 1.4426950408889634
ROW_TILE = 1024
FFN_ROW_TILE = 512
PAGES_PER_GROUP = 8
PAGE_SLOTS = 4


def _params(*sem):
    return pltpu.CompilerParams(dimension_semantics=sem, vmem_limit_bytes=VMEM_LIMIT)


def _rms(x, w):
    return x * lax.rsqrt(jnp.mean(x * x, axis=-1, keepdims=True) + EPS) * w


def _dot(a, b):
    return jnp.dot(a, b, preferred_element_type=F32)


def _dot_nt(a, b):
    return lax.dot_general(a, b, (((1,), (1,)), ((), ())), preferred_element_type=F32)


def _lam(lq1, lk1, lq2, lk2, lam_init):
    return (jnp.exp(jnp.sum(lq1[...] * lk1[...], axis=-1, keepdims=True))
            - jnp.exp(jnp.sum(lq2[...] * lk2[...], axis=-1, keepdims=True)) + lam_init)


def _bucket_np(rel):
    n = np.maximum(rel, 0)
    nf = np.maximum(n, 1).astype(np.float32)
    large = REL_MAX_EXACT + (np.log(nf / np.float32(REL_MAX_EXACT))
                             / np.float32(math.log(REL_MAX_DIST / REL_MAX_EXACT))
                             * np.float32(REL_BUCKETS - REL_MAX_EXACT)).astype(np.int32)
    large = np.minimum(large, REL_BUCKETS - 1)
    return np.where(n < REL_MAX_EXACT, n, large).astype(np.int32)


def _bucket_starts():
    buckets = _bucket_np(np.arange(REL_MAX_DIST + 1))
    assert (np.diff(buckets) >= 0).all() and buckets[-1] == REL_BUCKETS - 1
    return [int(np.argmax(buckets >= k)) for k in range(REL_BUCKETS)]


def _rel_bias_tile(rel, rb_ref, h):
    far = rb_ref[REL_BUCKETS - 1, h]
    val = jnp.full(rel.shape, rb_ref[0, h] - far, F32)
    for k, start in enumerate(_bucket_starts()):
        if k > 0:
            val = jnp.where(rel >= start, rb_ref[k, h] - far, val)
    return val


def _softmax_update(chunks, m_prev, l_prev):
    mx = chunks[0]
    for c in chunks[1:]:
        mx = jnp.maximum(mx, c)
    m_next = jnp.maximum(m_prev, jnp.max(mx, axis=1, keepdims=True))
    ps = [jnp.exp(c - m_next) for c in chunks]
    sm = ps[0]
    for p in ps[1:]:
        sm = sm + p
    alpha = jnp.exp(m_prev - m_next)
    l_next = alpha * l_prev + jnp.sum(sm, axis=1, keepdims=True)
    return ps, m_next, l_next, alpha


def _head_out(acc1, l1, acc2, l2, lam, sw, lam_init):
    o = acc1 / l1 - lam * (acc2 / l2)
    return _rms(o, sw) * (1.0 - lam_init)


def _mix_in_prompt_kernel(x_ref, nw_ref, w_ref, cw_ref, cb_ref,
                          yconv_ref, q_ref, kf_ref, vf_ref, kb_ref, vb_ref, st_ref,
                          pre_scr, *, dc, da):
    j = pl.program_id(1)
    tm = x_ref.shape[0]
    h = _rms(x_ref[...], nw_ref[...]).astype(BF16)

    def proj(lo, width):
        return _dot(h, w_ref[:, lo:lo + width])

    @pl.when(j == 0)
    def _zero_prefix():
        pre_scr[0:SUBLANES, :] = jnp.zeros((SUBLANES, dc), F32)

    pre = proj(dc, dc) * proj(2 * dc, dc)
    pre_scr[SUBLANES:SUBLANES + tm, :] = pre
    cw = cw_ref[...]
    conv = (cb_ref[...] + cw[0:1, :] * pre_scr[SUBLANES - 2:SUBLANES - 2 + tm, :]
            + cw[1:2, :] * pre_scr[SUBLANES - 1:SUBLANES - 1 + tm, :] + cw[2:3, :] * pre)
    yconv_ref[...] = (proj(0, dc) * conv).astype(BF16)
    st_ref[...] = pre[tm - 2:tm, :]
    pre_scr[0:SUBLANES, :] = pre[tm - SUBLANES:tm, :]

    q = proj(3 * dc, da) * (HEAD_DK ** -0.5 * LOG2E)
    k = proj(3 * dc + da, da)
    v = proj(3 * dc + 2 * da, da)
    ones_rows = (lax.broadcasted_iota(jnp.int32, (ONES_ROWS, tm), 0) == 0).astype(BF16)
    for hd in range(N_HEADS):
        cols = slice(hd * HEAD_DV, (hd + 1) * HEAD_DV)
        kb_ref[hd] = k[:, cols].astype(BF16)
        kf_ref[pl.ds(hd, tm, stride=N_HEADS), :] = k[:, cols]
        vf_ref[pl.ds(hd, tm, stride=N_HEADS), :] = v[:, cols]
        q_ref[hd] = q[:, cols].T.astype(BF16)
        vb_ref[hd, 0:HEAD_DV, :] = v[:, cols].T.astype(BF16)
        vb_ref[hd, HEAD_DV:, :] = ones_rows


def _mix_in_prompt(x2, nw, w_in, cw, cb, *, batch, seq, dc, da):
    d = x2.shape[1]
    tm = min(ATTN_TILE, seq)
    nj = seq // tm
    rows = lambda b, j: (b * nj + j, 0)
    const = lambda b, j: (0, 0)
    tposed = lambda r: pl.BlockSpec((None, N_HEADS, None, r, tm), lambda b, j: (b, 0, j, 0, 0))
    tposed_shape = lambda r: jax.ShapeDtypeStruct((batch, N_HEADS, nj, r, tm), BF16)
    n = batch * seq
    return pl.pallas_call(
        functools.partial(_mix_in_prompt_kernel, dc=dc, da=da),
        grid=(batch, nj),
        in_specs=[pl.BlockSpec((tm, d), rows),
                  pl.BlockSpec((1, d), const),
                  pl.BlockSpec(w_in.shape, const),
                  pl.BlockSpec(cw.shape, const),
                  pl.BlockSpec((1, dc), const)],
        out_specs=[pl.BlockSpec((tm, dc), rows),
                   tposed(HEAD_DV),
                   pl.BlockSpec((tm * N_HEADS, HEAD_DV), rows),
                   pl.BlockSpec((tm * N_HEADS, HEAD_DV), rows),
                   pl.BlockSpec((None, N_HEADS, tm, HEAD_DV), lambda b, j: (b, 0, j, 0)),
                   tposed(HEAD_DV + ONES_ROWS),
                   pl.BlockSpec((None, CONV_W - 1, dc), lambda b, j: (b, 0, 0))],
        out_shape=[jax.ShapeDtypeStruct((n, dc), BF16),
                   tposed_shape(HEAD_DV),
                   jax.ShapeDtypeStruct((n * N_HEADS, HEAD_DV), F32),
                   jax.ShapeDtypeStruct((n * N_HEADS, HEAD_DV), F32),
                   jax.ShapeDtypeStruct((batch, N_HEADS, seq, HEAD_DV), BF16),
                   tposed_shape(HEAD_DV + ONES_ROWS),
                   jax.ShapeDtypeStruct((batch, CONV_W - 1, dc), F32)],
        scratch_shapes=[pltpu.VMEM((tm + SUBLANES, dc), F32)],
        compiler_params=_params("arbitrary", "arbitrary"),
    )(x2, nw, w_in, cw, cb)


def _conv_time_major(slabs, cw, cb):
    return [cb + cw[0:1, :] * slabs[t] + cw[1:2, :] * slabs[t + 1] + cw[2:3, :] * slabs[t + 2]
            for t in range(len(slabs) - 2)]


def _mix_in_sample_kernel(x_ref, nw_ref, w_ref, cw_ref, cb_ref, s0_ref, s1_ref,
                          yconv_ref, qm_ref, kf_ref, vf_ref, st_ref, wb_ref, *, dc, da, ts, bs):
    h = _rms(x_ref[...], nw_ref[...]).astype(BF16)

    def proj(lo, width):
        wb = w_ref[:, lo:lo + width].astype(BF16)
        wb_ref[:, lo:lo + width] = wb
        return _dot(h, wb)

    pre = proj(dc, dc) * proj(2 * dc, dc)
    gate = proj(0, dc)
    slabs = [s0_ref[...], s1_ref[...]] + [pre[t * bs:(t + 1) * bs, :] for t in range(ts)]
    conv = _conv_time_major(slabs, cw_ref[...], cb_ref[...])
    for t in range(ts):
        yconv_ref[t * bs:(t + 1) * bs, :] = (gate[t * bs:(t + 1) * bs, :] * conv[t]).astype(BF16)
    st_ref[0] = slabs[-2]
    st_ref[1] = slabs[-1]

    q = proj(3 * dc, da) * (HEAD_DK ** -0.5)
    lane = lax.broadcasted_iota(jnp.int32, q.shape, 1) % HEAD_DV
    qm_ref[0] = jnp.where(lane < HEAD_DK, q, 0.0).astype(BF16)
    qm_ref[1] = jnp.where(lane >= HEAD_DK, q, 0.0).astype(BF16)
    kf_ref[...] = proj(3 * dc + da, da)
    vf_ref[...] = proj(3 * dc + 2 * da, da)


def _mix_in_sample(xt, nw, w_in, cw, cb, s0, s1, *, dc, da, ts, bs):
    n = ts * bs
    return pl.pallas_call(
        functools.partial(_mix_in_sample_kernel, dc=dc, da=da, ts=ts, bs=bs),
        out_shape=[jax.ShapeDtypeStruct((n, dc), BF16),
                   jax.ShapeDtypeStruct((2, n, da), BF16),
                   jax.ShapeDtypeStruct((n, da), F32),
                   jax.ShapeDtypeStruct((n, da), F32),
                   jax.ShapeDtypeStruct((CONV_W - 1, bs, dc), F32),
                   jax.ShapeDtypeStruct(w_in.shape, BF16)],
        compiler_params=pltpu.CompilerParams(vmem_limit_bytes=VMEM_LIMIT),
    )(xt, nw, w_in, cw, cb, s0, s1)


def _mem_kv_kernel(m_ref, nw_ref, w_ref, kf_ref, vf_ref, kb_ref, vb_ref, *, d):
    n_mem = m_ref.shape[0]
    dh = d // MEM_HEADS
    nh = dh // LANES
    h = _rms(m_ref[...], nw_ref[...]).astype(BF16)
    k = _dot(h, w_ref[:, 0:d].astype(BF16))
    kb_ref[...] = k.astype(BF16)
    v = _dot(h, w_ref[:, d:2 * d].astype(BF16))
    vb_ref[...] = v.astype(BF16)
    for hd in range(MEM_HEADS):
        for half in range(nh):
            rows = pl.ds(half * MEM_HEADS + hd, n_mem, stride=nh * MEM_HEADS)
            cols = slice(hd * dh + half * LANES, hd * dh + (half + 1) * LANES)
            kf_ref[rows, :] = k[:, cols]
            vf_ref[rows, :] = v[:, cols]


def _mem_kv(mem2, nw, w_ckv, *, batch, n_mem):
    d = mem2.shape[1]
    rows = lambda b: (b, 0)
    const = lambda b: (0, 0)
    n = batch * n_mem
    return pl.pallas_call(
        functools.partial(_mem_kv_kernel, d=d),
        grid=(batch,),
        in_specs=[pl.BlockSpec((n_mem, d), rows), pl.BlockSpec((1, d), const),
                  pl.BlockSpec(w_ckv.shape, const, pipeline_mode=pl.Buffered(1))],
        out_specs=[pl.BlockSpec((n_mem * d // LANES, LANES), rows)] * 2 + [pl.BlockSpec((n_mem, d), rows)] * 2,
        out_shape=[jax.ShapeDtypeStruct((n * d // LANES, LANES), F32)] * 2
                  + [jax.ShapeDtypeStruct((n, d), BF16)] * 2,
        compiler_params=_params("arbitrary"),
    )(mem2, nw, w_ckv)


def _attn_prompt_kernel(rb_ref, q_ref, k_ref, vt_ref, lq1, lk1, lq2, lk2, sw_ref, o_ref,
                        qa_scr, ma_scr, acca_scr, qb_scr, mb_scr, accb_scr, bd_scr, bs_scr, sa_scr, sb_scr,
                        *, T, lam_init):
    nb = T // LANES
    n_tiles = q_ref.shape[0]

    def _assemble_bias_tiles():
        h = pl.program_id(1)
        rel = (lax.broadcasted_iota(jnp.int32, (LANES, LANES), 1)
               - lax.broadcasted_iota(jnp.int32, (LANES, LANES), 0))
        d0 = jnp.where(rel >= 0, _rel_bias_tile(jnp.maximum(rel, 0), rb_ref, h) * LOG2E, NEG)
        d1 = _rel_bias_tile(rel + LANES, rb_ref, h) * LOG2E
        zero = jnp.zeros((LANES, LANES), F32)
        neg = jnp.full((LANES, LANES), NEG, F32)
        for bi in range(nb):
            for bj in range(nb):
                rs = slice(bi * LANES, (bi + 1) * LANES)
                cs = slice(bj * LANES, (bj + 1) * LANES)
                bd_scr[rs, cs] = d0 if bi == bj else d1 if bj == bi + 1 else zero if bj > bi else neg
                bs_scr[rs, cs] = d1 if (bj == 0 and bi == nb - 1) else zero

    _assemble_bias_tiles()
    n_chunks = 2 * T // QUERY_CHUNK

    def prep(st, qi):
        qp, m, acc = st
        qt = q_ref[qi].astype(F32)
        sub = lax.broadcasted_iota(jnp.int32, qt.shape, 0)
        qp[:, 0:T] = jnp.where(sub < HEAD_DK, qt, 0.0).astype(BF16)
        qp[:, T:2 * T] = jnp.where(sub >= HEAD_DK, qt, 0.0).astype(BF16)
        m[...] = jnp.full(m.shape, -jnp.inf, F32)
        acc[...] = jnp.zeros(acc.shape, F32)

    def chunk_scores(kh, qp, dst, c):
        dst[c] = _dot(kh, qp[:, c * QUERY_CHUNK:(c + 1) * QUERY_CHUNK])

    def key_block(j):
        return k_ref[pl.ds(pl.multiple_of(j * T, T), T), :]

    def stage(st, j, src, bias_ref, nxt=None):
        _, m_scr, acc_scr = st
        if nxt is not None:
            j_next, qp_next, dst = nxt
            kh_next = key_block(j_next)
        for c in range(n_chunks):
            if nxt is not None and dst is not src:
                chunk_scores(kh_next, qp_next, dst, c)
            cs = slice(c * QUERY_CHUNK, (c + 1) * QUERY_CHUNK)
            lo = (c * QUERY_CHUNK) % T
            nk = min(T, lo + QUERY_CHUNK) if bias_ref is bd_scr else T
            s = src[c, 0:nk, :]
            if bias_ref is not None:
                s = s + bias_ref[0:nk, lo:lo + QUERY_CHUNK]
            m_prev = m_scr[:, cs]
            m_next = jnp.maximum(m_prev, jnp.max(s, axis=0, keepdims=True))
            p = jnp.exp2(s - m_next).astype(BF16)
            if nxt is not None and dst is src:
                chunk_scores(kh_next, qp_next, dst, c)
            acc_scr[:, cs] = (jnp.exp2(m_prev - m_next) * acc_scr[:, cs]
                              + _dot(vt_ref[j, :, 0:nk], p))
            m_scr[:, cs] = m_next

    def finalize(st, qi):
        _, _, acc_scr = st
        lam = _lam(lq1, lk1, lq2, lk2, lam_init)
        acc = acc_scr[0:HEAD_DV, :]
        l = acc_scr[HEAD_DV:HEAD_DV + 1, :]
        o = acc[:, 0:T] / l[:, 0:T] - lam * (acc[:, T:2 * T] / l[:, T:2 * T])
        y = o * lax.rsqrt(jnp.mean(o * o, axis=0, keepdims=True) + EPS) * sw_ref[...] * (1.0 - lam_init)
        o_ref[pl.ds(pl.multiple_of(qi * T, T), T), :] = y.T.astype(o_ref.dtype)

    even = (qa_scr, ma_scr, acca_scr)
    odd = (qb_scr, mb_scr, accb_scr)
    prep(even, 0)
    kh0 = key_block(0)
    for c in range(n_chunks):
        chunk_scores(kh0, qa_scr, sb_scr, c)

    def tile_pair(t, carry):
        a = 2 * t
        b = a + 1

        @pl.when(t >= 1)
        def _even_first_block():
            stage(even, 0, sb_scr, None, (1, qa_scr, sa_scr))

        def even_far_pair(i, c):
            j = 1 + 2 * i
            stage(even, j, sa_scr, None, (j + 1, qa_scr, sb_scr))
            stage(even, j + 1, sb_scr, None, (j + 2, qa_scr, sa_scr))
            return c

        lax.fori_loop(0, jnp.maximum(t - 1, 0), even_far_pair, 0)
        prep(odd, b)

        @pl.when(t >= 1)
        def _even_last_blocks():
            stage(even, a - 1, sa_scr, bs_scr, (a, qa_scr, sb_scr))
            stage(even, a, sb_scr, bd_scr, (0, qb_scr, sa_scr))

        @pl.when(t == 0)
        def _even_only_block():
            stage(even, 0, sb_scr, bd_scr, (0, qb_scr, sa_scr))

        def odd_far_pair(i, c):
            j = 2 * i
            stage(odd, j, sa_scr, None, (j + 1, qb_scr, sb_scr))
            stage(odd, j + 1, sb_scr, None, (j + 2, qb_scr, sa_scr))
            return c

        lax.fori_loop(0, t, odd_far_pair, 0)
        finalize(even, a)
        prep(even, jnp.minimum(a + 2, n_tiles - 1))
        stage(odd, b - 1, sa_scr, bs_scr, (b, qb_scr, sb_scr))
        stage(odd, b, sb_scr, bd_scr, (0, qa_scr, sb_scr))
        finalize(odd, b)
        return carry

    lax.fori_loop(0, n_tiles // 2, tile_pair, 0)


def _attn_prompt(rel_bias, qt, kb, vt, lq1, lk1, lq2, lk2, sw_col, *, batch, seq, lam_init):
    T = qt.shape[-1]
    nq = seq // T
    assert nq % 2 == 0, "query tiles run in (even, odd) pairs"
    da = N_HEADS * HEAD_DV
    vec = lambda b, h: (0, 0)
    return pl.pallas_call(
        functools.partial(_attn_prompt_kernel, T=T, lam_init=lam_init),
        grid=(batch, N_HEADS),
        in_specs=[pl.BlockSpec(memory_space=pltpu.SMEM),
                  pl.BlockSpec((None, None, nq, HEAD_DV, T), lambda b, h: (b, h, 0, 0, 0)),
                  pl.BlockSpec((None, None, seq, HEAD_DV), lambda b, h: (b, h, 0, 0)),
                  pl.BlockSpec((None, None, nq, vt.shape[3], T), lambda b, h: (b, h, 0, 0, 0)),
                  pl.BlockSpec((1, HEAD_DK), vec), pl.BlockSpec((1, HEAD_DK), vec),
                  pl.BlockSpec((1, HEAD_DK), vec), pl.BlockSpec((1, HEAD_DK), vec),
                  pl.BlockSpec((HEAD_DV, 1), vec)],
        out_specs=pl.BlockSpec((seq, HEAD_DV), lambda b, h: (b, h)),
        out_shape=jax.ShapeDtypeStruct((batch * seq, da), BF16),
        scratch_shapes=[pltpu.VMEM((HEAD_DV, 2 * T), BF16),
                        pltpu.VMEM((1, 2 * T), F32),
                        pltpu.VMEM((vt.shape[3], 2 * T), F32),
                        pltpu.VMEM((HEAD_DV, 2 * T), BF16),
                        pltpu.VMEM((1, 2 * T), F32),
                        pltpu.VMEM((vt.shape[3], 2 * T), F32),
                        pltpu.VMEM((T, T), F32),
                        pltpu.VMEM((T, T), F32),
                        pltpu.VMEM((2 * T // QUERY_CHUNK, T, QUERY_CHUNK), F32),
                        pltpu.VMEM((2 * T // QUERY_CHUNK, T, QUERY_CHUNK), F32)],
        compiler_params=_params("arbitrary", "arbitrary"),
    )(rel_bias, qt, kb, vt, lq1, lk1, lq2, lk2, sw_col)


def _attn_sample_kernel(pt_ref, rb_ref, q_ref, kn_ref, vn_ref, lq1, lk1, lq2, lk2, sw_ref, k_hbm, v_hbm,
                        o_ref, m_scr, l_scr, acc_scr, hm_scr, bl_scr, bn_scr, kbuf, vbuf, sems,
                        *, pages, page_offset, ts, lam_init):
    b = pl.program_id(0)
    nb = pl.num_programs(0)
    rows, pk = hm_scr.shape
    nk = bn_scr.shape[1]
    rm = rows // 2
    page = pk // N_HEADS
    groups = pt_ref.shape[1] // pages
    hbits = N_HEADS.bit_length() - 1
    tbits = ts.bit_length() - 1

    def group_copies(t, slot):
        bt = t // groups
        g0 = (t % groups) * pages
        out = []
        for i in range(pages):
            row0 = pl.multiple_of((page_offset + pt_ref[bt, g0 + i]) * pk, pk)
            dst = pl.ds(i * pk, pk)
            out.append(pltpu.make_async_copy(k_hbm.at[pl.ds(row0, pk)], kbuf.at[slot, dst], sems.at[0, slot]))
            out.append(pltpu.make_async_copy(v_hbm.at[pl.ds(row0, pk)], vbuf.at[slot, dst], sems.at[1, slot]))
        return out

    def start_group(t):
        @pl.when(t < nb * groups)
        def _():
            for cp in group_copies(t, t % PAGE_SLOTS):
                cp.start()

    @pl.when(b == 0)
    def _first_step():
        for t in range(PAGE_SLOTS - 1):
            start_group(jnp.int32(t))

        def tables(ncols, rel_of):
            r = lax.broadcasted_iota(jnp.int32, (rows, ncols), 0)
            c = lax.broadcasted_iota(jnp.int32, (rows, ncols), 1)
            rhead = (r >> tbits) & (N_HEADS - 1)
            same = rhead == (c & (N_HEADS - 1))
            rel = rel_of(r & (ts - 1), c >> hbits)
            bias = jnp.zeros((rows, ncols), F32)
            for h in range(N_HEADS):
                bias = jnp.where(rhead == h, _rel_bias_tile(jnp.maximum(rel, 0), rb_ref, h), bias)
            return same, rel, c >> hbits, bias

        same, _, _, bias = tables(pk, lambda tok, key: tok + page - key)
        hm_scr[...] = jnp.where(same, 0.0, NEG)
        bl_scr[...] = jnp.where(same, bias, NEG)
        same, rel, key, bias = tables(nk, lambda tok, key: tok - key)
        bn_scr[...] = jnp.where(same & (rel >= 0) & (key < ts), bias, NEG)

    m_scr[...] = jnp.full(m_scr.shape, -jnp.inf, F32)
    l_scr[...] = jnp.zeros(l_scr.shape, F32)
    acc_scr[...] = jnp.zeros(acc_scr.shape, F32)

    def update(scores, values):
        chunks, sizes = [], []
        for s in scores:
            n = s.shape[1] // LANES
            sizes.append(n)
            chunks += [s[:, c * LANES:(c + 1) * LANES] for c in range(n)]
        ps, m_next, l_next, alpha = _softmax_update(chunks, m_scr[...], l_scr[...])
        pv, at = None, 0
        for n, v in zip(sizes, values):
            p = jnp.concatenate([x.astype(BF16) for x in ps[at:at + n]], axis=1) if n > 1 \
                else ps[at].astype(BF16)
            at += n
            pv = _dot(p, v) if pv is None else pv + _dot(p, v)
        acc_scr[...] = alpha * acc_scr[...] + pv
        m_scr[...] = m_next
        l_scr[...] = l_next

    q = q_ref[...]

    def group_body(g, carry):
        t = b * groups + g
        slot = t % PAGE_SLOTS
        for cp in group_copies(t, slot):
            cp.wait()
        start_group(t + PAGE_SLOTS - 1)
        kg = kbuf.at[slot]
        vg = vbuf.at[slot]
        last = g == groups - 1
        scores = [_dot_nt(q, kg[i * pk:(i + 1) * pk, :].astype(BF16)) + hm_scr[...]
                  for i in range(pages - 1)]
        scores.append(_dot_nt(q, kg[(pages - 1) * pk:pages * pk, :].astype(BF16))
                      + jnp.where(last, bl_scr[...], hm_scr[...]))
        update(scores, [vg[i * pk:(i + 1) * pk, :].astype(BF16) for i in range(pages)])
        return carry

    lax.fori_loop(0, groups, group_body, 0)

    update([_dot_nt(q, kn_ref[...]) + bn_scr[...]], [vn_ref[...]])
    lam = _lam(lq1, lk1, lq2, lk2, lam_init)
    acc = acc_scr[...]
    l = l_scr[...]
    o_ref[...] = _head_out(acc[0:rm], l[0:rm], acc[rm:rows], l[rm:rows], lam, sw_ref[...], lam_init)


def _attn_sample(page_table, rel_bias, q_all, knew, vnew, lq1, lk1, lq2, lk2, sw, cache_k, cache_v,
                 *, page, page_offset, ts, lam_init):
    bs, n_pages = page_table.shape
    rows = q_all.shape[1]
    nk = knew.shape[1]
    pk = page * N_HEADS
    pages = math.gcd(PAGES_PER_GROUP, n_pages)
    vec = lambda b, pt: (0, 0)
    per_b = lambda b, pt: (b, 0, 0)
    grid_spec = pltpu.PrefetchScalarGridSpec(
        num_scalar_prefetch=1,
        grid=(bs,),
        in_specs=[pl.BlockSpec(memory_space=pltpu.SMEM),
                  pl.BlockSpec((None, rows, HEAD_DV), per_b),
                  pl.BlockSpec((None, nk, HEAD_DV), per_b),
                  pl.BlockSpec((None, nk, HEAD_DV), per_b),
                  pl.BlockSpec((1, HEAD_DK), vec), pl.BlockSpec((1, HEAD_DK), vec),
                  pl.BlockSpec((1, HEAD_DK), vec), pl.BlockSpec((1, HEAD_DK), vec),
                  pl.BlockSpec((1, HEAD_DV), vec),
                  pl.BlockSpec(memory_space=pl.ANY), pl.BlockSpec(memory_space=pl.ANY)],
        out_specs=pl.BlockSpec((None, rows // 2, HEAD_DV), per_b),
        scratch_shapes=[pltpu.VMEM((rows, LANES), F32),
                        pltpu.VMEM((rows, LANES), F32),
                        pltpu.VMEM((rows, HEAD_DV), F32),
                        pltpu.VMEM((rows, pk), F32),
                        pltpu.VMEM((rows, pk), F32),
                        pltpu.VMEM((rows, nk), F32),
                        pltpu.VMEM((PAGE_SLOTS, pages * pk, HEAD_DV), F32),
                        pltpu.VMEM((PAGE_SLOTS, pages * pk, HEAD_DV), F32),
                        pltpu.SemaphoreType.DMA((2, PAGE_SLOTS))],
    )
    return pl.pallas_call(
        functools.partial(_attn_sample_kernel, pages=pages, page_offset=page_offset, ts=ts,
                          lam_init=lam_init),
        grid_spec=grid_spec,
        out_shape=jax.ShapeDtypeStruct((bs, rows // 2, HEAD_DV), F32),
        compiler_params=_params("arbitrary"),
    )(page_table, rel_bias, q_all, knew, vnew, lq1, lk1, lq2, lk2, sw, cache_k, cache_v)


def _cross_heads(qc, mk_ref, mv_ref, o_scr, dh):
    for hd in range(MEM_HEADS):
        cols = slice(hd * dh, (hd + 1) * dh)
        s = _dot_nt(qc[:, cols], mk_ref[:, cols].astype(BF16))
        p = jnp.exp(s - jnp.max(s, axis=1, keepdims=True))
        o = _dot(p.astype(BF16), mv_ref[:, cols].astype(BF16)) / jnp.sum(p, axis=1, keepdims=True)
        o_scr[:, cols] = o.astype(BF16)


def _cross_prompt_kernel(x_ref, yc_ref, ya_ref, wo_ref, nw_ref, wq_ref, mk_ref, mv_ref, wc_ref,
                         o_ref, o_scr, *, dc, dh):
    x1 = x_ref[...] + _dot(yc_ref[...], wo_ref[0:dc, :]) + _dot(ya_ref[...], wo_ref[dc:, :])
    h = _rms(x1, nw_ref[...]).astype(BF16)
    qc = (_dot(h, wq_ref[...]) * (dh ** -0.5)).astype(BF16)
    _cross_heads(qc, mk_ref, mv_ref, o_scr, dh)
    o_ref[...] = x1 + _dot(o_scr[...], wc_ref[...])


def _cross_prompt(x2, yconv, yattn, w_out, nw, w_cq, mk, mv, w_co, *, batch, seq, n_mem):
    d = x2.shape[1]
    dc = yconv.shape[1]
    da = yattn.shape[1]
    tm = min(ROW_TILE, seq)
    nj = seq // tm
    rows = lambda b, j: (b * nj + j, 0)
    const = lambda b, j: (0, 0)
    memb = lambda b, j: (b, 0)
    return pl.pallas_call(
        functools.partial(_cross_prompt_kernel, dc=dc, dh=d // MEM_HEADS),
        grid=(batch, nj),
        in_specs=[pl.BlockSpec((tm, d), rows), pl.BlockSpec((tm, dc), rows), pl.BlockSpec((tm, da), rows),
                  pl.BlockSpec(w_out.shape, const), pl.BlockSpec((1, d), const),
                  pl.BlockSpec(w_cq.shape, const),
                  pl.BlockSpec((n_mem, d), memb), pl.BlockSpec((n_mem, d), memb),
                  pl.BlockSpec(w_co.shape, const)],
        out_specs=pl.BlockSpec((tm, d), rows),
        out_shape=jax.ShapeDtypeStruct(x2.shape, F32),
        scratch_shapes=[pltpu.VMEM((tm, d), BF16)],
        compiler_params=_params("arbitrary", "arbitrary"),
    )(x2, yconv, yattn, w_out, nw, w_cq, mk, mv, w_co)


def _outproj_q_sample_kernel(x_ref, yc_ref, ya_ref, wo_ref, nw_ref, wq_ref, x1_ref, qc_ref,
                             wob_ref, wqb_ref, *, dc, dh):
    wob_ref[...] = wo_ref[...].astype(BF16)
    wqb_ref[...] = wq_ref[...].astype(BF16)
    x1 = x_ref[...] + _dot(yc_ref[...], wob_ref[0:dc, :]) + _dot(ya_ref[...], wob_ref[dc:, :])
    x1_ref[...] = x1
    h = _rms(x1, nw_ref[...]).astype(BF16)
    qc_ref[...] = (_dot(h, wqb_ref[...]) * (dh ** -0.5)).astype(BF16)


def _outproj_q_sample(xt, yconv, yattn, w_out, nw, w_cq):
    d = xt.shape[1]
    return pl.pallas_call(
        functools.partial(_outproj_q_sample_kernel, dc=yconv.shape[1], dh=d // MEM_HEADS),
        out_shape=[jax.ShapeDtypeStruct(xt.shape, F32), jax.ShapeDtypeStruct(xt.shape, BF16),
                   jax.ShapeDtypeStruct(w_out.shape, BF16), jax.ShapeDtypeStruct(w_cq.shape, BF16)],
        compiler_params=pltpu.CompilerParams(vmem_limit_bytes=VMEM_LIMIT),
    )(xt, yconv, yattn, w_out, nw, w_cq)


def _cross_sample_kernel(q_ref, mk_ref, mv_ref, o_ref, *, ts):
    rq = q_ref.shape[0]
    ncols = mk_ref.shape[0]
    per_key = (rq // (MEM_HEADS * ts)) * MEM_HEADS
    nh = per_key // MEM_HEADS
    rows = MEM_HEADS * ts
    s2 = _dot_nt(q_ref[...], mk_ref[...].astype(BF16))
    s = s2[0:rows]
    for half in range(1, nh):
        s = s + pltpu.roll(s2[half * rows:(half + 1) * rows], ncols - half * MEM_HEADS, axis=1)
    col = lax.broadcasted_iota(jnp.int32, (rows, ncols), 1) & (per_key - 1)
    head = lax.broadcasted_iota(jnp.int32, (rows, ncols), 0) >> (ts.bit_length() - 1)
    s = jnp.where(col == head, s, NEG)
    p = jnp.exp(s - jnp.max(s, axis=1, keepdims=True))
    l = jnp.sum(p, axis=1, keepdims=True)
    lhs = jnp.concatenate([p] + [pltpu.roll(p, half * MEM_HEADS, axis=1) for half in range(1, nh)], axis=0)
    o2 = _dot(lhs.astype(BF16), mv_ref[...].astype(BF16))
    o = jnp.concatenate([o2[half * rows:(half + 1) * rows] for half in range(nh)], axis=1) / l
    o_ref[...] = o.astype(o_ref.dtype)


def _cross_sample(q2, mk, mv, *, row_offset, ts):
    bs, rq, _ = q2.shape
    ncols = mk.shape[1]
    rows = MEM_HEADS * ts
    nh = rq // rows
    per_b = lambda b: (b, 0, 0)
    mem_b = lambda b: (row_offset + b, 0, 0)
    return pl.pallas_call(
        functools.partial(_cross_sample_kernel, ts=ts),
        grid=(bs,),
        in_specs=[pl.BlockSpec((None, rq, LANES), per_b), pl.BlockSpec((None, ncols, LANES), mem_b),
                  pl.BlockSpec((None, ncols, LANES), mem_b)],
        out_specs=pl.BlockSpec((None, rows, nh * LANES), per_b),
        out_shape=jax.ShapeDtypeStruct((bs, rows, nh * LANES), BF16),
        compiler_params=_params("arbitrary"),
    )(q2, mk, mv)


def _silu(g):
    return g * (1.0 / (1.0 + jnp.exp(-g)))


def _ffn_prompt_kernel(x_ref, nw_ref, wu_ref, cw_ref, cb_ref, wd_ref, fw_ref, y_ref, st_ref,
                       up_scr, carry_scr, *, dff, final):
    j = pl.program_id(1)
    tm = x_ref.shape[0]
    x = x_ref[...]
    h = _rms(x, nw_ref[...]).astype(BF16)

    @pl.when(j == 0)
    def _zero_prefix():
        carry_scr[...] = jnp.zeros(carry_scr.shape, F32)

    def conv_half(lo):
        cols = slice(lo, lo + dff)
        up = _dot(h, wu_ref[:, cols])
        up_scr[0:SUBLANES, :] = carry_scr[:, cols]
        up_scr[SUBLANES:SUBLANES + tm, :] = up
        cw = cw_ref[:, cols]
        conv = (cb_ref[:, cols] + cw[0:1, :] * up_scr[SUBLANES - 2:SUBLANES - 2 + tm, :]
                + cw[1:2, :] * up_scr[SUBLANES - 1:SUBLANES - 1 + tm, :] + cw[2:3, :] * up)
        carry_scr[:, cols] = up[tm - SUBLANES:tm, :]
        st_ref[:, cols] = up[tm - 2:tm, :]
        return conv

    g = conv_half(0)
    u = conv_half(dff)
    hid = (_silu(g) * u).astype(BF16)
    rh = tm // 2
    for r in range(2):
        rows = slice(r * rh, (r + 1) * rh)
        x3 = x[rows] + _dot(hid[rows], wd_ref[...])
        y_ref[rows, :] = _rms(x3, fw_ref[...]) if final else x3


def _ffn_prompt(x2, nw, w_up, cw, cb, w_down, fw, *, batch, seq, final):
    d = x2.shape[1]
    dff = w_down.shape[0]
    tm = min(FFN_ROW_TILE, seq)
    nj = seq // tm
    rows = lambda b, j: (b * nj + j, 0)
    const = lambda b, j: (0, 0)
    resident = lambda shape: pl.BlockSpec(shape, const, pipeline_mode=pl.Buffered(1))
    return pl.pallas_call(
        functools.partial(_ffn_prompt_kernel, dff=dff, final=final),
        grid=(batch, nj),
        in_specs=[pl.BlockSpec((tm, d), rows), pl.BlockSpec((1, d), const),
                  resident(w_up.shape), pl.BlockSpec(cw.shape, const),
                  pl.BlockSpec((1, 2 * dff), const), resident(w_down.shape),
                  pl.BlockSpec((1, d), const)],
        out_specs=[pl.BlockSpec((tm, d), rows),
                   pl.BlockSpec((None, CONV_W - 1, 2 * dff), lambda b, j: (b, 0, 0))],
        out_shape=[jax.ShapeDtypeStruct(x2.shape, F32),
                   jax.ShapeDtypeStruct((batch, CONV_W - 1, 2 * dff), F32)],
        scratch_shapes=[pltpu.VMEM((tm + SUBLANES, dff), F32),
                        pltpu.VMEM((SUBLANES, 2 * dff), F32)],
        compiler_params=_params("arbitrary", "arbitrary"),
    )(x2, nw, w_up, cw, cb, w_down, fw)


def _ffn_sample_kernel(x1_ref, o_ref, wc_ref, nw_ref, wu_ref, cw_ref, cb_ref, wd_ref, fw_ref,
                       s0_ref, s1_ref, y_ref, st_ref, wcb_ref, hid_scr, *, dff, ts, bs, final):
    wcb_ref[...] = wc_ref[...].astype(BF16)
    x2 = x1_ref[...] + _dot(o_ref[...], wcb_ref[...])
    h = _rms(x2, nw_ref[...]).astype(BF16)

    def conv_half(lo):
        cols = slice(lo, lo + dff)
        up = _dot(h, wu_ref[:, cols])
        slabs = [s0_ref[:, cols], s1_ref[:, cols]] + [up[t * bs:(t + 1) * bs, :] for t in range(ts)]
        st_ref[0, :, cols] = slabs[-2]
        st_ref[1, :, cols] = slabs[-1]
        return _conv_time_major(slabs, cw_ref[:, cols], cb_ref[:, cols])

    g = conv_half(0)
    u = conv_half(dff)
    for t in range(ts):
        hid_scr[t * bs:(t + 1) * bs, :] = (_silu(g[t]) * u[t]).astype(BF16)
    x3 = x2 + _dot(hid_scr[...], wd_ref[...])
    y_ref[...] = _rms(x3, fw_ref[...]) if final else x3


def _ffn_sample(x1, o, w_co, nw, w_up, cw, cb, w_down, fw, s0, s1, *, ts, bs, final):
    dff = w_down.shape[0]
    return pl.pallas_call(
        functools.partial(_ffn_sample_kernel, dff=dff, ts=ts, bs=bs, final=final),
        out_shape=[jax.ShapeDtypeStruct(x1.shape, F32),
                   jax.ShapeDtypeStruct((CONV_W - 1, bs, 2 * dff), F32),
                   jax.ShapeDtypeStruct(w_co.shape, BF16)],
        scratch_shapes=[pltpu.VMEM((ts * bs, dff), BF16)],
        compiler_params=pltpu.CompilerParams(vmem_limit_bytes=VMEM_LIMIT),
    )(x1, o, w_co, nw, w_up, cw, cb, w_down, fw, s0, s1)


def kernel(x_prompt, x_sample, mem_prompt, cache_k, cache_v, page_table, state_conv_mix, state_conv_ffn, cache_mem_k, cache_mem_v, rel_bias, norm_mix_w, w_in, conv_mix_w, conv_mix_b, lambda_q1, lambda_k1, lambda_q2, lambda_k2, subln_w, w_out, norm_cross_w, norm_mem_w, w_cq, w_ckv, w_co, norm_ffn_w, w_up, conv_ffn_w, conv_ffn_b, w_down, norm_final_w):
    depth = w_in.shape[0]
    bp, sp, d = x_prompt.shape
    bs, ts, _ = x_sample.shape
    n_mem = mem_prompt.shape[1]
    n_phys, page = cache_k.shape[1], cache_k.shape[2]
    n_pages = page_table.shape[1]
    past_len = n_pages * page
    dc = conv_mix_w.shape[2]
    da = N_HEADS * HEAD_DV
    dff = w_down.shape[1]
    dh = d // MEM_HEADS
    assert page == LANES and CONV_W - 1 <= ts <= SUBLANES and ts & (ts - 1) == 0
    assert (N_HEADS * ts) % SUBLANES == 0 and cache_k.shape[3:] == (N_HEADS, HEAD_DV)
    rows = 2 * SUBLANES

    assert _bucket_np(np.arange(REL_MAX_DIST, max(sp, past_len + ts) + 1)).min() == REL_BUCKETS - 1
    row = lambda a: a.reshape(1, -1).astype(F32)
    rel_bias = rel_bias.astype(F32)
    cache_k2 = cache_k.reshape(depth * n_phys * page * N_HEADS, HEAD_DV)
    cache_v2 = cache_v.reshape(depth * n_phys * page * N_HEADS, HEAD_DV)
    mem_rows = lambda a: a.reshape(depth * bs, n_mem, MEM_HEADS, dh // LANES, LANES).transpose(
        0, 1, 3, 2, 4).reshape(depth * bs, n_mem * (dh // LANES) * MEM_HEADS, LANES)
    mem_k3 = mem_rows(cache_mem_k)
    mem_v3 = mem_rows(cache_mem_v)

    xp = x_prompt.reshape(bp * sp, d)
    xs = x_sample.transpose(1, 0, 2).reshape(ts * bs, d)
    mem2 = mem_prompt.reshape(bp * n_mem, d)
    outs = [[] for _ in range(10)]
    for l in range(depth):
        lam_init = 0.8 - 0.6 * math.exp(-0.3 * l)
        final = l == depth - 1
        w_up_b, w_down_b = w_up[l].astype(BF16), w_down[l].astype(BF16)
        lam_args = (row(lambda_q1[l]), row(lambda_k1[l]), row(lambda_q2[l]), row(lambda_k2[l]),
                    row(subln_w[l]))

        yconv_s, qm_s, kf_s, vf_s, cmix_s, w_in_b = _mix_in_sample(
            xs, row(norm_mix_w[l]), w_in[l], conv_mix_w[l], row(conv_mix_b[l]),
            state_conv_mix[l][:, 0], state_conv_mix[l][:, 1], dc=dc, da=da, ts=ts, bs=bs)
        q_all = qm_s.reshape(2, ts, bs, N_HEADS, HEAD_DV).transpose(2, 0, 3, 1, 4)
        q_all = q_all.reshape(bs, 2 * N_HEADS * ts, HEAD_DV)
        new_rows = lambda a: jnp.pad(
            a.reshape(ts, bs, N_HEADS * HEAD_DV).transpose(1, 0, 2).reshape(bs, ts * N_HEADS, HEAD_DV),
            ((0, 0), (0, LANES - ts * N_HEADS), (0, 0))).astype(BF16)
        yattn_s = _attn_sample(page_table, rel_bias, q_all, new_rows(kf_s), new_rows(vf_s), *lam_args,
                               cache_k2, cache_v2, page=page, page_offset=l * n_phys, ts=ts,
                               lam_init=lam_init)
        yattn_s = yattn_s.reshape(bs, N_HEADS, ts, HEAD_DV).transpose(2, 0, 1, 3)
        yattn_s = yattn_s.reshape(ts * bs, da).astype(BF16)
        x1_s, qc_s, w_out_b, w_cq_b = _outproj_q_sample(xs, yconv_s, yattn_s, w_out[l],
                                                        row(norm_cross_w[l]), w_cq[l])
        q2 = qc_s.reshape(ts, bs, MEM_HEADS, dh // LANES, LANES).transpose(1, 3, 2, 0, 4)
        q2 = q2.reshape(bs, (dh // LANES) * MEM_HEADS * ts, LANES)
        o_b = _cross_sample(q2, mem_k3, mem_v3, row_offset=l * bs, ts=ts)
        o_s = o_b.reshape(bs, MEM_HEADS, ts, dh).transpose(2, 0, 1, 3).reshape(ts * bs, d)
        xs, cffn_s, w_co_b = _ffn_sample(x1_s, o_s, w_co[l], row(norm_ffn_w[l]), w_up_b, conv_ffn_w[l],
                                         row(conv_ffn_b[l]), w_down_b, row(norm_final_w),
                                         state_conv_ffn[l][:, 0], state_conv_ffn[l][:, 1],
                                         ts=ts, bs=bs, final=final)

        yconv_p, q_p, kf_p, vf_p, kb_p, vb_p, cmix_p = _mix_in_prompt(
            xp, row(norm_mix_w[l]), w_in_b, conv_mix_w[l], row(conv_mix_b[l]),
            batch=bp, seq=sp, dc=dc, da=da)
        yattn_p = _attn_prompt(rel_bias, q_p, kb_p, vb_p, *lam_args[:4], subln_w[l].reshape(-1, 1).astype(F32),
                               batch=bp, seq=sp, lam_init=lam_init)
        mkf, mvf, mkb, mvb = _mem_kv(mem2, row(norm_mem_w[l]), w_ckv[l], batch=bp, n_mem=n_mem)
        x2_p = _cross_prompt(xp, yconv_p, yattn_p, w_out_b, row(norm_cross_w[l]), w_cq_b, mkb, mvb,
                             w_co_b, batch=bp, seq=sp, n_mem=n_mem)
        xp, cffn_p = _ffn_prompt(x2_p, row(norm_ffn_w[l]), w_up_b, conv_ffn_w[l], row(conv_ffn_b[l]),
                                 w_down_b, row(norm_final_w), batch=bp, seq=sp, final=final)

        t2b = lambda a: a.reshape(ts, bs, N_HEADS, HEAD_DV).transpose(1, 0, 2, 3)
        mem_out = lambda a: a.reshape(bp, n_mem, dh // LANES, MEM_HEADS, LANES).transpose(
            0, 1, 3, 2, 4).reshape(bp, n_mem, MEM_HEADS, dh)
        for lst, val in zip(outs, (
                kf_p.reshape(bp, sp, N_HEADS, 2 * HEAD_DK), vf_p.reshape(bp, sp, N_HEADS, HEAD_DV),
                t2b(kf_s), t2b(vf_s), cmix_p, cmix_s.transpose(1, 0, 2), cffn_p,
                cffn_s.transpose(1, 0, 2), mem_out(mkf), mem_out(mvf))):
            lst.append(val)

    y_prompt = xp.reshape(bp, sp, d)
    y_sample = xs.reshape(ts, bs, d).transpose(1, 0, 2)
    return (y_prompt, y_sample) + tuple(jnp.stack(o) for o in outs)
```

```python
import functools
import math

import numpy as np
import jax
import jax.numpy as jnp
from jax import lax
from jax.experimental import pallas as pl
from jax.experimental.pallas import tpu as pltpu

F32 = jnp.float32
BF16 = jnp.bfloat16

EPS = 1e-6
NEG = -1e30
LANES = 128
SUBLANES = 8
N_HEADS = 4
HEAD_DK = 64
HEAD_DV = 2 * HEAD_DK
MEM_HEADS = 4
CONV_W = 3
REL_BUCKETS = 32
REL_MAX_EXACT = 16
REL_MAX_DIST = 128
VMEM_LIMIT = 56 * 1024 * 1024

ATTN_TILE = 512
QUERY_CHUNK = 256
ONES_ROWS = 16
LOG2E = 1.4426950408889634
ROW_TILE = 1024
FFN_ROW_TILE = 512
PAGES_PER_GROUP = 8
PAGE_SLOTS = 4


def _params(*sem):
    return pltpu.CompilerParams(dimension_semantics=sem, vmem_limit_bytes=VMEM_LIMIT)


def _rms(x, w):
    return x * lax.rsqrt(jnp.mean(x * x, axis=-1, keepdims=True) + EPS) * w


def _dot(a, b):
    return jnp.dot(a, b, preferred_element_type=F32)


def _dot_nt(a, b):
    return lax.dot_general(a, b, (((1,), (1,)), ((), ())), preferred_element_type=F32)


def _lam(lq1, lk1, lq2, lk2, lam_init):
    return (jnp.exp(jnp.sum(lq1[...] * lk1[...], axis=-1, keepdims=True))
            - jnp.exp(jnp.sum(lq2[...] * lk2[...], axis=-1, keepdims=True)) + lam_init)


def _bucket_np(rel):
    n = np.maximum(rel, 0)
    nf = np.maximum(n, 1).astype(np.float32)
    large = REL_MAX_EXACT + (np.log(nf / np.float32(REL_MAX_EXACT))
                             / np.float32(math.log(REL_MAX_DIST / REL_MAX_EXACT))
                             * np.float32(REL_BUCKETS - REL_MAX_EXACT)).astype(np.int32)
    large = np.minimum(large, REL_BUCKETS - 1)
    return np.where(n < REL_MAX_EXACT, n, large).astype(np.int32)


def _bucket_starts():
    buckets = _bucket_np(np.arange(REL_MAX_DIST + 1))
    assert (np.diff(buckets) >= 0).all() and buckets[-1] == REL_BUCKETS - 1
    return [int(np.argmax(buckets >= k)) for k in range(REL_BUCKETS)]


def _rel_bias_tile(rel, rb_ref, h):
    far = rb_ref[REL_BUCKETS - 1, h]
    val = jnp.full(rel.shape, rb_ref[0, h] - far, F32)
    for k, start in enumerate(_bucket_starts()):
        if k > 0:
            val = jnp.where(rel >= start, rb_ref[k, h] - far, val)
    return val


def _softmax_update(chunks, m_prev, l_prev):
    mx = chunks[0]
    for c in chunks[1:]:
        mx = jnp.maximum(mx, c)
    m_next = jnp.maximum(m_prev, jnp.max(mx, axis=1, keepdims=True))
    ps = [jnp.exp(c - m_next) for c in chunks]
    sm = ps[0]
    for p in ps[1:]:
        sm = sm + p
    alpha = jnp.exp(m_prev - m_next)
    l_next = alpha * l_prev + jnp.sum(sm, axis=1, keepdims=True)
    return ps, m_next, l_next, alpha


def _head_out(acc1, l1, acc2, l2, lam, sw, lam_init):
    o = acc1 / l1 - lam * (acc2 / l2)
    return _rms(o, sw) * (1.0 - lam_init)


def _mix_in_prompt_kernel(x_ref, nw_ref, w_ref, cw_ref, cb_ref,
                          yconv_ref, q_ref, kf_ref, vf_ref, kb_ref, vb_ref, st_ref,
                          pre_scr, *, dc, da):
    j = pl.program_id(1)
    tm = x_ref.shape[0]
    h = _rms(x_ref[...], nw_ref[...]).astype(BF16)

    def proj(lo, width):
        return _dot(h, w_ref[:, lo:lo + width])

    @pl.when(j == 0)
    def _zero_prefix():
        pre_scr[0:SUBLANES, :] = jnp.zeros((SUBLANES, dc), F32)

    pre = proj(dc, dc) * proj(2 * dc, dc)
    pre_scr[SUBLANES:SUBLANES + tm, :] = pre
    cw = cw_ref[...]
    conv = (cb_ref[...] + cw[0:1, :] * pre_scr[SUBLANES - 2:SUBLANES - 2 + tm, :]
            + cw[1:2, :] * pre_scr[SUBLANES - 1:SUBLANES - 1 + tm, :] + cw[2:3, :] * pre)
    yconv_ref[...] = (proj(0, dc) * conv).astype(BF16)
    st_ref[...] = pre[tm - 2:tm, :]
    pre_scr[0:SUBLANES, :] = pre[tm - SUBLANES:tm, :]

    q = proj(3 * dc, da) * (HEAD_DK ** -0.5 * LOG2E)
    k = proj(3 * dc + da, da)
    v = proj(3 * dc + 2 * da, da)
    ones_rows = (lax.broadcasted_iota(jnp.int32, (ONES_ROWS, tm), 0) == 0).astype(BF16)
    for hd in range(N_HEADS):
        cols = slice(hd * HEAD_DV, (hd + 1) * HEAD_DV)
        kb_ref[hd] = k[:, cols].astype(BF16)
        kf_ref[pl.ds(hd, tm, stride=N_HEADS), :] = k[:, cols]
        vf_ref[pl.ds(hd, tm, stride=N_HEADS), :] = v[:, cols]
        q_ref[hd] = q[:, cols].T.astype(BF16)
        vb_ref[hd, 0:HEAD_DV, :] = v[:, cols].T.astype(BF16)
        vb_ref[hd, HEAD_DV:, :] = ones_rows


def _mix_in_prompt(x2, nw, w_in, cw, cb, *, batch, seq, dc, da):
    d = x2.shape[1]
    tm = min(ATTN_TILE, seq)
    nj = seq // tm
    rows = lambda b, j: (b * nj + j, 0)
    const = lambda b, j: (0, 0)
    tposed = lambda r: pl.BlockSpec((None, N_HEADS, None, r, tm), lambda b, j: (b, 0, j, 0, 0))
    tposed_shape = lambda r: jax.ShapeDtypeStruct((batch, N_HEADS, nj, r, tm), BF16)
    n = batch * seq
    return pl.pallas_call(
        functools.partial(_mix_in_prompt_kernel, dc=dc, da=da),
        grid=(batch, nj),
        in_specs=[pl.BlockSpec((tm, d), rows),
                  pl.BlockSpec((1, d), const),
                  pl.BlockSpec(w_in.shape, const),
                  pl.BlockSpec(cw.shape, const),
                  pl.BlockSpec((1, dc), const)],
        out_specs=[pl.BlockSpec((tm, dc), rows),
                   tposed(HEAD_DV),
                   pl.BlockSpec((tm * N_HEADS, HEAD_DV), rows),
                   pl.BlockSpec((tm * N_HEADS, HEAD_DV), rows),
                   pl.BlockSpec((None, N_HEADS, tm, HEAD_DV), lambda b, j: (b, 0, j, 0)),
                   tposed(HEAD_DV + ONES_ROWS),
                   pl.BlockSpec((None, CONV_W - 1, dc), lambda b, j: (b, 0, 0))],
        out_shape=[jax.ShapeDtypeStruct((n, dc), BF16),
                   tposed_shape(HEAD_DV),
                   jax.ShapeDtypeStruct((n * N_HEADS, HEAD_DV), F32),
                   jax.ShapeDtypeStruct((n * N_HEADS, HEAD_DV), F32),
                   jax.ShapeDtypeStruct((batch, N_HEADS, seq, HEAD_DV), BF16),
                   tposed_shape(HEAD_DV + ONES_ROWS),
                   jax.ShapeDtypeStruct((batch, CONV_W - 1, dc), F32)],
        scratch_shapes=[pltpu.VMEM((tm + SUBLANES, dc), F32)],
        compiler_params=_params("arbitrary", "arbitrary"),
    )(x2, nw, w_in, cw, cb)


def _conv_time_major(slabs, cw, cb):
    return [cb + cw[0:1, :] * slabs[t] + cw[1:2, :] * slabs[t + 1] + cw[2:3, :] * slabs[t + 2]
            for t in range(len(slabs) - 2)]


def _mix_in_sample_kernel(x_ref, nw_ref, w_ref, cw_ref, cb_ref, s0_ref, s1_ref,
                          yconv_ref, qm_ref, kf_ref, vf_ref, st_ref, wb_ref, *, dc, da, ts, bs):
    h = _rms(x_ref[...], nw_ref[...]).astype(BF16)

    def proj(lo, width):
        wb = w_ref[:, lo:lo + width].astype(BF16)
        wb_ref[:, lo:lo + width] = wb
        return _dot(h, wb)

    pre = proj(dc, dc) * proj(2 * dc, dc)
    gate = proj(0, dc)
    slabs = [s0_ref[...], s1_ref[...]] + [pre[t * bs:(t + 1) * bs, :] for t in range(ts)]
    conv = _conv_time_major(slabs, cw_ref[...], cb_ref[...])
    for t in range(ts):
        yconv_ref[t * bs:(t + 1) * bs, :] = (gate[t * bs:(t + 1) * bs, :] * conv[t]).astype(BF16)
    st_ref[0] = slabs[-2]
    st_ref[1] = slabs[-1]

    q = proj(3 * dc, da) * (HEAD_DK ** -0.5)
    lane = lax.broadcasted_iota(jnp.int32, q.shape, 1) % HEAD_DV
    qm_ref[0] = jnp.where(lane < HEAD_DK, q, 0.0).astype(BF16)
    qm_ref[1] = jnp.where(lane >= HEAD_DK, q, 0.0).astype(BF16)
    kf_ref[...] = proj(3 * dc + da, da)
    vf_ref[...] = proj(3 * dc + 2 * da, da)


def _mix_in_sample(xt, nw, w_in, cw, cb, s0, s1, *, dc, da, ts, bs):
    n = ts * bs
    return pl.pallas_call(
        functools.partial(_mix_in_sample_kernel, dc=dc, da=da, ts=ts, bs=bs),
        out_shape=[jax.ShapeDtypeStruct((n, dc), BF16),
                   jax.ShapeDtypeStruct((2, n, da), BF16),
                   jax.ShapeDtypeStruct((n, da), F32),
                   jax.ShapeDtypeStruct((n, da), F32),
                   jax.ShapeDtypeStruct((CONV_W - 1, bs, dc), F32),
                   jax.ShapeDtypeStruct(w_in.shape, BF16)],
        compiler_params=pltpu.CompilerParams(vmem_limit_bytes=VMEM_LIMIT),
    )(xt, nw, w_in, cw, cb, s0, s1)


def _mem_kv_kernel(m_ref, nw_ref, w_ref, kf_ref, vf_ref, kb_ref, vb_ref, *, d):
    n_mem = m_ref.shape[0]
    dh = d // MEM_HEADS
    nh = dh // LANES
    h = _rms(m_ref[...], nw_ref[...]).astype(BF16)
    k = _dot(h, w_ref[:, 0:d].astype(BF16))
    kb_ref[...] = k.astype(BF16)
    v = _dot(h, w_ref[:, d:2 * d].astype(BF16))
    vb_ref[...] = v.astype(BF16)
    for hd in range(MEM_HEADS):
        for half in range(nh):
            rows = pl.ds(half * MEM_HEADS + hd, n_mem, stride=nh * MEM_HEADS)
            cols = slice(hd * dh + half * LANES, hd * dh + (half + 1) * LANES)
            kf_ref[rows, :] = k[:, cols]
            vf_ref[rows, :] = v[:, cols]


def _mem_kv(mem2, nw, w_ckv, *, batch, n_mem):
    d = mem2.shape[1]
    rows = lambda b: (b, 0)
    const = lambda b: (0, 0)
    n = batch * n_mem
    return pl.pallas_call(
        functools.partial(_mem_kv_kernel, d=d),
        grid=(batch,),
        in_specs=[pl.BlockSpec((n_mem, d), rows), pl.BlockSpec((1, d), const),
                  pl.BlockSpec(w_ckv.shape, const, pipeline_mode=pl.Buffered(1))],
        out_specs=[pl.BlockSpec((n_mem * d // LANES, LANES), rows)] * 2 + [pl.BlockSpec((n_mem, d), rows)] * 2,
        out_shape=[jax.ShapeDtypeStruct((n * d // LANES, LANES), F32)] * 2
                  + [jax.ShapeDtypeStruct((n, d), BF16)] * 2,
        compiler_params=_params("arbitrary"),
    )(mem2, nw, w_ckv)


def _attn_prompt_kernel(rb_ref, q_ref, k_ref, vt_ref, lq1, lk1, lq2, lk2, sw_ref, o_ref,
                        qa_scr, ma_scr, acca_scr, qb_scr, mb_scr, accb_scr, bd_scr, bs_scr, sa_scr, sb_scr,
                        *, T, lam_init):
    nb = T // LANES
    n_tiles = q_ref.shape[0]

    def _assemble_bias_tiles():
        h = pl.program_id(1)
        rel = (lax.broadcasted_iota(jnp.int32, (LANES, LANES), 1)
               - lax.broadcasted_iota(jnp.int32, (LANES, LANES), 0))
        d0 = jnp.where(rel >= 0, _rel_bias_tile(jnp.maximum(rel, 0), rb_ref, h) * LOG2E, NEG)
        d1 = _rel_bias_tile(rel + LANES, rb_ref, h) * LOG2E
        zero = jnp.zeros((LANES, LANES), F32)
        neg = jnp.full((LANES, LANES), NEG, F32)
        for bi in range(nb):
            for bj in range(nb):
                rs = slice(bi * LANES, (bi + 1) * LANES)
                cs = slice(bj * LANES, (bj + 1) * LANES)
                bd_scr[rs, cs] = d0 if bi == bj else d1 if bj == bi + 1 else zero if bj > bi else neg
                bs_scr[rs, cs] = d1 if (bj == 0 and bi == nb - 1) else zero

    _assemble_bias_tiles()
    n_chunks = 2 * T // QUERY_CHUNK

    def prep(st, qi):
        qp, m, acc = st
        qt = q_ref[qi].astype(F32)
        sub = lax.broadcasted_iota(jnp.int32, qt.shape, 0)
        qp[:, 0:T] = jnp.where(sub < HEAD_DK, qt, 0.0).astype(BF16)
        qp[:, T:2 * T] = jnp.where(sub >= HEAD_DK, qt, 0.0).astype(BF16)
        m[...] = jnp.full(m.shape, -jnp.inf, F32)
        acc[...] = jnp.zeros(acc.shape, F32)

    def chunk_scores(kh, qp, dst, c):
        dst[c] = _dot(kh, qp[:, c * QUERY_CHUNK:(c + 1) * QUERY_CHUNK])

    def key_block(j):
        return k_ref[pl.ds(pl.multiple_of(j * T, T), T), :]

    def stage(st, j, src, bias_ref, nxt=None):
        _, m_scr, acc_scr = st
        if nxt is not None:
            j_next, qp_next, dst = nxt
            kh_next = key_block(j_next)
        for c in range(n_chunks):
            if nxt is not None and dst is not src:
                chunk_scores(kh_next, qp_next, dst, c)
            cs = slice(c * QUERY_CHUNK, (c + 1) * QUERY_CHUNK)
            lo = (c * QUERY_CHUNK) % T
            nk = min(T, lo + QUERY_CHUNK) if bias_ref is bd_scr else T
            s = src[c, 0:nk, :]
            if bias_ref is not None:
                s = s + bias_ref[0:nk, lo:lo + QUERY_CHUNK]
            m_prev = m_scr[:, cs]
            m_next = jnp.maximum(m_prev, jnp.max(s, axis=0, keepdims=True))
            p = jnp.exp2(s - m_next).astype(BF16)
            if nxt is not None and dst is src:
                chunk_scores(kh_next, qp_next, dst, c)
            acc_scr[:, cs] = (jnp.exp2(m_prev - m_next) * acc_scr[:, cs]
                              + _dot(vt_ref[j, :, 0:nk], p))
            m_scr[:, cs] = m_next

    def finalize(st, qi):
        _, _, acc_scr = st
        lam = _lam(lq1, lk1, lq2, lk2, lam_init)
        acc = acc_scr[0:HEAD_DV, :]
        l = acc_scr[HEAD_DV:HEAD_DV + 1, :]
        o = acc[:, 0:T] / l[:, 0:T] - lam * (acc[:, T:2 * T] / l[:, T:2 * T])
        y = o * lax.rsqrt(jnp.mean(o * o, axis=0, keepdims=True) + EPS) * sw_ref[...] * (1.0 - lam_init)
        o_ref[pl.ds(pl.multiple_of(qi * T, T), T), :] = y.T.astype(o_ref.dtype)

    even = (qa_scr, ma_scr, acca_scr)
    odd = (qb_scr, mb_scr, accb_scr)
    prep(even, 0)
    kh0 = key_block(0)
    for c in range(n_chunks):
        chunk_scores(kh0, qa_scr, sb_scr, c)

    def tile_pair(t, carry):
        a = 2 * t
        b = a + 1

        @pl.when(t >= 1)
        def _even_first_block():
            stage(even, 0, sb_scr, None, (1, qa_scr, sa_scr))

        def even_far_pair(i, c):
            j = 1 + 2 * i
            stage(even, j, sa_scr, None, (j + 1, qa_scr, sb_scr))
            stage(even, j + 1, sb_scr, None, (j + 2, qa_scr, sa_scr))
            return c

        lax.fori_loop(0, jnp.maximum(t - 1, 0), even_far_pair, 0)
        prep(odd, b)

        @pl.when(t >= 1)
        def _even_last_blocks():
            stage(even, a - 1, sa_scr, bs_scr, (a, qa_scr, sb_scr))
            stage(even, a, sb_scr, bd_scr, (0, qb_scr, sa_scr))

        @pl.when(t == 0)
        def _even_only_block():
            stage(even, 0, sb_scr, bd_scr, (0, qb_scr, sa_scr))

        def odd_far_pair(i, c):
            j = 2 * i
            stage(odd, j, sa_scr, None, (j + 1, qb_scr, sb_scr))
            stage(odd, j + 1, sb_scr, None, (j + 2, qb_scr, sa_scr))
            return c

        lax.fori_loop(0, t, odd_far_pair, 0)
        finalize(even, a)
        prep(even, jnp.minimum(a + 2, n_tiles - 1))
        stage(odd, b - 1, sa_scr, bs_scr, (b, qb_scr, sb_scr))
        stage(odd, b, sb_scr, bd_scr, (0, qa_scr, sb_scr))
        finalize(odd, b)
        return carry

    lax.fori_loop(0, n_tiles // 2, tile_pair, 0)


def _attn_prompt(rel_bias, qt, kb, vt, lq1, lk1, lq2, lk2, sw_col, *, batch, seq, lam_init):
    T = qt.shape[-1]
    nq = seq // T
    assert nq % 2 == 0, "query tiles run in (even, odd) pairs"
    da = N_HEADS * HEAD_DV
    vec = lambda b, h: (0, 0)
    return pl.pallas_call(
        functools.partial(_attn_prompt_kernel, T=T, lam_init=lam_init),
        grid=(batch, N_HEADS),
        in_specs=[pl.BlockSpec(memory_space=pltpu.SMEM),
                  pl.BlockSpec((None, None, nq, HEAD_DV, T), lambda b, h: (b, h, 0, 0, 0)),
                  pl.BlockSpec((None, None, seq, HEAD_DV), lambda b, h: (b, h, 0, 0)),
                  pl.BlockSpec((None, None, nq, vt.shape[3], T), lambda b, h: (b, h, 0, 0, 0)),
                  pl.BlockSpec((1, HEAD_DK), vec), pl.BlockSpec((1, HEAD_DK), vec),
                  pl.BlockSpec((1, HEAD_DK), vec), pl.BlockSpec((1, HEAD_DK), vec),
                  pl.BlockSpec((HEAD_DV, 1), vec)],
        out_specs=pl.BlockSpec((seq, HEAD_DV), lambda b, h: (b, h)),
        out_shape=jax.ShapeDtypeStruct((batch * seq, da), BF16),
        scratch_shapes=[pltpu.VMEM((HEAD_DV, 2 * T), BF16),
                        pltpu.VMEM((1, 2 * T), F32),
                        pltpu.VMEM((vt.shape[3], 2 * T), F32),
                        pltpu.VMEM((HEAD_DV, 2 * T), BF16),
                        pltpu.VMEM((1, 2 * T), F32),
                        pltpu.VMEM((vt.shape[3], 2 * T), F32),
                        pltpu.VMEM((T, T), F32),
                        pltpu.VMEM((T, T), F32),
                        pltpu.VMEM((2 * T // QUERY_CHUNK, T, QUERY_CHUNK), F32),
                        pltpu.VMEM((2 * T // QUERY_CHUNK, T, QUERY_CHUNK), F32)],
        compiler_params=_params("arbitrary", "arbitrary"),
    )(rel_bias, qt, kb, vt, lq1, lk1, lq2, lk2, sw_col)


def _attn_sample_kernel(pt_ref, rb_ref, q_ref, kn_ref, vn_ref, lq1, lk1, lq2, lk2, sw_ref, k_hbm, v_hbm,
                        o_ref, m_scr, l_scr, acc_scr, hm_scr, bl_scr, bn_scr, kbuf, vbuf, sems,
                        *, pages, page_offset, ts, lam_init):
    b = pl.program_id(0)
    nb = pl.num_programs(0)
    rows, pk = hm_scr.shape
    nk = bn_scr.shape[1]
    rm = rows // 2
    page = pk // N_HEADS
    groups = pt_ref.shape[1] // pages
    hbits = N_HEADS.bit_length() - 1
    tbits = ts.bit_length() - 1

    def group_copies(t, slot):
        bt = t // groups
        g0 = (t % groups) * pages
        out = []
        for i in range(pages):
            row0 = pl.multiple_of((page_offset + pt_ref[bt, g0 + i]) * pk, pk)
            dst = pl.ds(i * pk, pk)
            out.append(pltpu.make_async_copy(k_hbm.at[pl.ds(row0, pk)], kbuf.at[slot, dst], sems.at[0, slot]))
            out.append(pltpu.make_async_copy(v_hbm.at[pl.ds(row0, pk)], vbuf.at[slot, dst], sems.at[1, slot]))
        return out

    def start_group(t):
        @pl.when(t < nb * groups)
        def _():
            for i, cp in enumerate(group_copies(t, t % PAGE_SLOTS)):
                cp.start(priority=(i // 2) % 2)

    @pl.when(b == 0)
    def _first_step():
        for t in range(PAGE_SLOTS - 1):
            start_group(jnp.int32(t))

        def tables(ncols, rel_of):
            r = lax.broadcasted_iota(jnp.int32, (rows, ncols), 0)
            c = lax.broadcasted_iota(jnp.int32, (rows, ncols), 1)
            rhead = (r >> tbits) & (N_HEADS - 1)
            same = rhead == (c & (N_HEADS - 1))
            rel = rel_of(r & (ts - 1), c >> hbits)
            bias = jnp.zeros((rows, ncols), F32)
            for h in range(N_HEADS):
                bias = jnp.where(rhead == h, _rel_bias_tile(jnp.maximum(rel, 0), rb_ref, h), bias)
            return same, rel, c >> hbits, bias

        same, _, _, bias = tables(pk, lambda tok, key: tok + page - key)
        hm_scr[...] = jnp.where(same, 0.0, NEG)
        bl_scr[...] = jnp.where(same, bias, NEG)
        same, rel, key, bias = tables(nk, lambda tok, key: tok - key)
        bn_scr[...] = jnp.where(same & (rel >= 0) & (key < ts), bias, NEG)

    m_scr[...] = jnp.full(m_scr.shape, -jnp.inf, F32)
    l_scr[...] = jnp.zeros(l_scr.shape, F32)
    acc_scr[...] = jnp.zeros(acc_scr.shape, F32)

    def update(scores, values):
        chunks, sizes = [], []
        for s in scores:
            n = s.shape[1] // LANES
            sizes.append(n)
            chunks += [s[:, c * LANES:(c + 1) * LANES] for c in range(n)]
        ps, m_next, l_next, alpha = _softmax_update(chunks, m_scr[...], l_scr[...])
        pv, at = None, 0
        for n, v in zip(sizes, values):
            p = jnp.concatenate([x.astype(BF16) for x in ps[at:at + n]], axis=1) if n > 1 \
                else ps[at].astype(BF16)
            at += n
            pv = _dot(p, v) if pv is None else pv + _dot(p, v)
        acc_scr[...] = alpha * acc_scr[...] + pv
        m_scr[...] = m_next
        l_scr[...] = l_next

    q = q_ref[...]

    def group_body(g, carry):
        t = b * groups + g
        slot = t % PAGE_SLOTS
        for cp in group_copies(t, slot):
            cp.wait()
        start_group(t + PAGE_SLOTS - 1)
        kg = kbuf.at[slot]
        vg = vbuf.at[slot]
        last = g == groups - 1
        scores = [_dot_nt(q, kg[i * pk:(i + 1) * pk, :].astype(BF16)) + hm_scr[...]
                  for i in range(pages - 1)]
        scores.append(_dot_nt(q, kg[(pages - 1) * pk:pages * pk, :].astype(BF16))
                      + jnp.where(last, bl_scr[...], hm_scr[...]))
        update(scores, [vg[i * pk:(i + 1) * pk, :].astype(BF16) for i in range(pages)])
        return carry

    lax.fori_loop(0, groups, group_body, 0)

    update([_dot_nt(q, kn_ref[...]) + bn_scr[...]], [vn_ref[...]])
    lam = _lam(lq1, lk1, lq2, lk2, lam_init)
    acc = acc_scr[...]
    l = l_scr[...]
    o_ref[...] = _head_out(acc[0:rm], l[0:rm], acc[rm:rows], l[rm:rows], lam, sw_ref[...], lam_init)


def _attn_sample(page_table, rel_bias, q_all, knew, vnew, lq1, lk1, lq2, lk2, sw, cache_k, cache_v,
                 *, page, page_offset, ts, lam_init):
    bs, n_pages = page_table.shape
    rows = q_all.shape[1]
    nk = knew.shape[1]
    pk = page * N_HEADS
    pages = math.gcd(PAGES_PER_GROUP, n_pages)
    vec = lambda b, pt: (0, 0)
    per_b = lambda b, pt: (b, 0, 0)
    grid_spec = pltpu.PrefetchScalarGridSpec(
        num_scalar_prefetch=1,
        grid=(bs,),
        in_specs=[pl.BlockSpec(memory_space=pltpu.SMEM),
                  pl.BlockSpec((None, rows, HEAD_DV), per_b),
                  pl.BlockSpec((None, nk, HEAD_DV), per_b),
                  pl.BlockSpec((None, nk, HEAD_DV), per_b),
                  pl.BlockSpec((1, HEAD_DK), vec), pl.BlockSpec((1, HEAD_DK), vec),
                  pl.BlockSpec((1, HEAD_DK), vec), pl.BlockSpec((1, HEAD_DK), vec),
                  pl.BlockSpec((1, HEAD_DV), vec),
                  pl.BlockSpec(memory_space=pl.ANY), pl.BlockSpec(memory_space=pl.ANY)],
        out_specs=pl.BlockSpec((None, rows // 2, HEAD_DV), per_b),
        scratch_shapes=[pltpu.VMEM((rows, LANES), F32),
                        pltpu.VMEM((rows, LANES), F32),
                        pltpu.VMEM((rows, HEAD_DV), F32),
                        pltpu.VMEM((rows, pk), F32),
                        pltpu.VMEM((rows, pk), F32),
                        pltpu.VMEM((rows, nk), F32),
                        pltpu.VMEM((PAGE_SLOTS, pages * pk, HEAD_DV), F32),
                        pltpu.VMEM((PAGE_SLOTS, pages * pk, HEAD_DV), F32),
                        pltpu.SemaphoreType.DMA((2, PAGE_SLOTS))],
    )
    return pl.pallas_call(
        functools.partial(_attn_sample_kernel, pages=pages, page_offset=page_offset, ts=ts,
                          lam_init=lam_init),
        grid_spec=grid_spec,
        out_shape=jax.ShapeDtypeStruct((bs, rows // 2, HEAD_DV), F32),
        compiler_params=_params("arbitrary"),
    )(page_table, rel_bias, q_all, knew, vnew, lq1, lk1, lq2, lk2, sw, cache_k, cache_v)


def _cross_heads(qc, mk_ref, mv_ref, o_scr, dh):
    for hd in range(MEM_HEADS):
        cols = slice(hd * dh, (hd + 1) * dh)
        s = _dot_nt(qc[:, cols], mk_ref[:, cols].astype(BF16))
        p = jnp.exp(s - jnp.max(s, axis=1, keepdims=True))
        o = _dot(p.astype(BF16), mv_ref[:, cols].astype(BF16)) / jnp.sum(p, axis=1, keepdims=True)
        o_scr[:, cols] = o.astype(BF16)


def _cross_prompt_kernel(x_ref, yc_ref, ya_ref, wo_ref, nw_ref, wq_ref, mk_ref, mv_ref, wc_ref,
                         o_ref, o_scr, *, dc, dh):
    x1 = x_ref[...] + _dot(yc_ref[...], wo_ref[0:dc, :]) + _dot(ya_ref[...], wo_ref[dc:, :])
    h = _rms(x1, nw_ref[...]).astype(BF16)
    qc = (_dot(h, wq_ref[...]) * (dh ** -0.5)).astype(BF16)
    _cross_heads(qc, mk_ref, mv_ref, o_scr, dh)
    o_ref[...] = x1 + _dot(o_scr[...], wc_ref[...])


def _cross_prompt(x2, yconv, yattn, w_out, nw, w_cq, mk, mv, w_co, *, batch, seq, n_mem):
    d = x2.shape[1]
    dc = yconv.shape[1]
    da = yattn.shape[1]
    tm = min(ROW_TILE, seq)
    nj = seq // tm
    rows = lambda b, j: (b * nj + j, 0)
    const = lambda b, j: (0, 0)
    memb = lambda b, j: (b, 0)
    return pl.pallas_call(
        functools.partial(_cross_prompt_kernel, dc=dc, dh=d // MEM_HEADS),
        grid=(batch, nj),
        in_specs=[pl.BlockSpec((tm, d), rows), pl.BlockSpec((tm, dc), rows), pl.BlockSpec((tm, da), rows),
                  pl.BlockSpec(w_out.shape, const), pl.BlockSpec((1, d), const),
                  pl.BlockSpec(w_cq.shape, const),
                  pl.BlockSpec((n_mem, d), memb), pl.BlockSpec((n_mem, d), memb),
                  pl.BlockSpec(w_co.shape, const)],
        out_specs=pl.BlockSpec((tm, d), rows),
        out_shape=jax.ShapeDtypeStruct(x2.shape, F32),
        scratch_shapes=[pltpu.VMEM((tm, d), BF16)],
        compiler_params=_params("arbitrary", "arbitrary"),
    )(x2, yconv, yattn, w_out, nw, w_cq, mk, mv, w_co)


def _outproj_q_sample_kernel(x_ref, yc_ref, ya_ref, wo_ref, nw_ref, wq_ref, x1_ref, qc_ref,
                             wob_ref, wqb_ref, *, dc, dh):
    wob_ref[...] = wo_ref[...].astype(BF16)
    wqb_ref[...] = wq_ref[...].astype(BF16)
    x1 = x_ref[...] + _dot(yc_ref[...], wob_ref[0:dc, :]) + _dot(ya_ref[...], wob_ref[dc:, :])
    x1_ref[...] = x1
    h = _rms(x1, nw_ref[...]).astype(BF16)
    qc_ref[...] = (_dot(h, wqb_ref[...]) * (dh ** -0.5)).astype(BF16)


def _outproj_q_sample(xt, yconv, yattn, w_out, nw, w_cq):
    d = xt.shape[1]
    return pl.pallas_call(
        functools.partial(_outproj_q_sample_kernel, dc=yconv.shape[1], dh=d // MEM_HEADS),
        out_shape=[jax.ShapeDtypeStruct(xt.shape, F32), jax.ShapeDtypeStruct(xt.shape, BF16),
                   jax.ShapeDtypeStruct(w_out.shape, BF16), jax.ShapeDtypeStruct(w_cq.shape, BF16)],
        compiler_params=pltpu.CompilerParams(vmem_limit_bytes=VMEM_LIMIT),
    )(xt, yconv, yattn, w_out, nw, w_cq)


def _cross_sample_kernel(q_ref, mk_ref, mv_ref, o_ref, *, ts):
    rq = q_ref.shape[0]
    ncols = mk_ref.shape[0]
    per_key = (rq // (MEM_HEADS * ts)) * MEM_HEADS
    nh = per_key // MEM_HEADS
    rows = MEM_HEADS * ts
    s2 = _dot_nt(q_ref[...], mk_ref[...].astype(BF16))
    s = s2[0:rows]
    for half in range(1, nh):
        s = s + pltpu.roll(s2[half * rows:(half + 1) * rows], ncols - half * MEM_HEADS, axis=1)
    col = lax.broadcasted_iota(jnp.int32, (rows, ncols), 1) & (per_key - 1)
    head = lax.broadcasted_iota(jnp.int32, (rows, ncols), 0) >> (ts.bit_length() - 1)
    s = jnp.where(col == head, s, NEG)
    p = jnp.exp(s - jnp.max(s, axis=1, keepdims=True))
    l = jnp.sum(p, axis=1, keepdims=True)
    lhs = jnp.concatenate([p] + [pltpu.roll(p, half * MEM_HEADS, axis=1) for half in range(1, nh)], axis=0)
    o2 = _dot(lhs.astype(BF16), mv_ref[...].astype(BF16))
    o = jnp.concatenate([o2[half * rows:(half + 1) * rows] for half in range(nh)], axis=1) / l
    o_ref[...] = o.astype(o_ref.dtype)


def _cross_sample(q2, mk, mv, *, row_offset, ts):
    bs, rq, _ = q2.shape
    ncols = mk.shape[1]
    rows = MEM_HEADS * ts
    nh = rq // rows
    per_b = lambda b: (b, 0, 0)
    mem_b = lambda b: (row_offset + b, 0, 0)
    return pl.pallas_call(
        functools.partial(_cross_sample_kernel, ts=ts),
        grid=(bs,),
        in_specs=[pl.BlockSpec((None, rq, LANES), per_b), pl.BlockSpec((None, ncols, LANES), mem_b),
                  pl.BlockSpec((None, ncols, LANES), mem_b)],
        out_specs=pl.BlockSpec((None, rows, nh * LANES), per_b),
        out_shape=jax.ShapeDtypeStruct((bs, rows, nh * LANES), BF16),
        compiler_params=_params("arbitrary"),
    )(q2, mk, mv)


def _silu(g):
    return g * (1.0 / (1.0 + jnp.exp(-g)))


def _ffn_prompt_kernel(x_ref, nw_ref, wu_ref, cw_ref, cb_ref, wd_ref, fw_ref, y_ref, st_ref,
                       up_scr, carry_scr, *, dff, final):
    j = pl.program_id(1)
    tm = x_ref.shape[0]
    x = x_ref[...]
    h = _rms(x, nw_ref[...]).astype(BF16)

    @pl.when(j == 0)
    def _zero_prefix():
        carry_scr[...] = jnp.zeros(carry_scr.shape, F32)

    def conv_half(lo):
        cols = slice(lo, lo + dff)
        up = _dot(h, wu_ref[:, cols])
        up_scr[0:SUBLANES, :] = carry_scr[:, cols]
        up_scr[SUBLANES:SUBLANES + tm, :] = up
        cw = cw_ref[:, cols]
        conv = (cb_ref[:, cols] + cw[0:1, :] * up_scr[SUBLANES - 2:SUBLANES - 2 + tm, :]
                + cw[1:2, :] * up_scr[SUBLANES - 1:SUBLANES - 1 + tm, :] + cw[2:3, :] * up)
        carry_scr[:, cols] = up[tm - SUBLANES:tm, :]
        st_ref[:, cols] = up[tm - 2:tm, :]
        return conv

    g = conv_half(0)
    u = conv_half(dff)
    hid = (_silu(g) * u).astype(BF16)
    rh = tm // 2
    for r in range(2):
        rows = slice(r * rh, (r + 1) * rh)
        x3 = x[rows] + _dot(hid[rows], wd_ref[...])
        y_ref[rows, :] = _rms(x3, fw_ref[...]) if final else x3


def _ffn_prompt(x2, nw, w_up, cw, cb, w_down, fw, *, batch, seq, final):
    d = x2.shape[1]
    dff = w_down.shape[0]
    tm = min(FFN_ROW_TILE, seq)
    nj = seq // tm
    rows = lambda b, j: (b * nj + j, 0)
    const = lambda b, j: (0, 0)
    resident = lambda shape: pl.BlockSpec(shape, const, pipeline_mode=pl.Buffered(1))
    return pl.pallas_call(
        functools.partial(_ffn_prompt_kernel, dff=dff, final=final),
        grid=(batch, nj),
        in_specs=[pl.BlockSpec((tm, d), rows), pl.BlockSpec((1, d), const),
                  resident(w_up.shape), pl.BlockSpec(cw.shape, const),
                  pl.BlockSpec((1, 2 * dff), const), resident(w_down.shape),
                  pl.BlockSpec((1, d), const)],
        out_specs=[pl.BlockSpec((tm, d), rows),
                   pl.BlockSpec((None, CONV_W - 1, 2 * dff), lambda b, j: (b, 0, 0))],
        out_shape=[jax.ShapeDtypeStruct(x2.shape, F32),
                   jax.ShapeDtypeStruct((batch, CONV_W - 1, 2 * dff), F32)],
        scratch_shapes=[pltpu.VMEM((tm + SUBLANES, dff), F32),
                        pltpu.VMEM((SUBLANES, 2 * dff), F32)],
        compiler_params=_params("arbitrary", "arbitrary"),
    )(x2, nw, w_up, cw, cb, w_down, fw)


def _ffn_sample_kernel(x1_ref, o_ref, wc_ref, nw_ref, wu_ref, cw_ref, cb_ref, wd_ref, fw_ref,
                       s0_ref, s1_ref, y_ref, st_ref, wcb_ref, hid_scr, *, dff, ts, bs, final):
    wcb_ref[...] = wc_ref[...].astype(BF16)
    x2 = x1_ref[...] + _dot(o_ref[...], wcb_ref[...])
    h = _rms(x2, nw_ref[...]).astype(BF16)

    def conv_half(lo):
        cols = slice(lo, lo + dff)
        up = _dot(h, wu_ref[:, cols])
        slabs = [s0_ref[:, cols], s1_ref[:, cols]] + [up[t * bs:(t + 1) * bs, :] for t in range(ts)]
        st_ref[0, :, cols] = slabs[-2]
        st_ref[1, :, cols] = slabs[-1]
        return _conv_time_major(slabs, cw_ref[:, cols], cb_ref[:, cols])

    g = conv_half(0)
    u = conv_half(dff)
    for t in range(ts):
        hid_scr[t * bs:(t + 1) * bs, :] = (_silu(g[t]) * u[t]).astype(BF16)
    x3 = x2 + _dot(hid_scr[...], wd_ref[...])
    y_ref[...] = _rms(x3, fw_ref[...]) if final else x3


def _ffn_sample(x1, o, w_co, nw, w_up, cw, cb, w_down, fw, s0, s1, *, ts, bs, final):
    dff = w_down.shape[0]
    return pl.pallas_call(
        functools.partial(_ffn_sample_kernel, dff=dff, ts=ts, bs=bs, final=final),
        out_shape=[jax.ShapeDtypeStruct(x1.shape, F32),
                   jax.ShapeDtypeStruct((CONV_W - 1, bs, 2 * dff), F32),
                   jax.ShapeDtypeStruct(w_co.shape, BF16)],
        scratch_shapes=[pltpu.VMEM((ts * bs, dff), BF16)],
        compiler_params=pltpu.CompilerParams(vmem_limit_bytes=VMEM_LIMIT),
    )(x1, o, w_co, nw, w_up, cw, cb, w_down, fw, s0, s1)


def kernel(x_prompt, x_sample, mem_prompt, cache_k, cache_v, page_table, state_conv_mix, state_conv_ffn, cache_mem_k, cache_mem_v, rel_bias, norm_mix_w, w_in, conv_mix_w, conv_mix_b, lambda_q1, lambda_k1, lambda_q2, lambda_k2, subln_w, w_out, norm_cross_w, norm_mem_w, w_cq, w_ckv, w_co, norm_ffn_w, w_up, conv_ffn_w, conv_ffn_b, w_down, norm_final_w):
    depth = w_in.shape[0]
    bp, sp, d = x_prompt.shape
    bs, ts, _ = x_sample.shape
    n_mem = mem_prompt.shape[1]
    n_phys, page = cache_k.shape[1], cache_k.shape[2]
    n_pages = page_table.shape[1]
    past_len = n_pages * page
    dc = conv_mix_w.shape[2]
    da = N_HEADS * HEAD_DV
    dff = w_down.shape[1]
    dh = d // MEM_HEADS
    assert page == LANES and CONV_W - 1 <= ts <= SUBLANES and ts & (ts - 1) == 0
    assert (N_HEADS * ts) % SUBLANES == 0 and cache_k.shape[3:] == (N_HEADS, HEAD_DV)
    rows = 2 * SUBLANES

    assert _bucket_np(np.arange(REL_MAX_DIST, max(sp, past_len + ts) + 1)).min() == REL_BUCKETS - 1
    row = lambda a: a.reshape(1, -1).astype(F32)
    rel_bias = rel_bias.astype(F32)
    cache_k2 = cache_k.reshape(depth * n_phys * page * N_HEADS, HEAD_DV)
    cache_v2 = cache_v.reshape(depth * n_phys * page * N_HEADS, HEAD_DV)
    mem_rows = lambda a: a.reshape(depth * bs, n_mem, MEM_HEADS, dh // LANES, LANES).transpose(
        0, 1, 3, 2, 4).reshape(depth * bs, n_mem * (dh // LANES) * MEM_HEADS, LANES)
    mem_k3 = mem_rows(cache_mem_k)
    mem_v3 = mem_rows(cache_mem_v)

    xp = x_prompt.reshape(bp * sp, d)
    xs = x_sample.transpose(1, 0, 2).reshape(ts * bs, d)
    mem2 = mem_prompt.reshape(bp * n_mem, d)
    outs = [[] for _ in range(10)]
    for l in range(depth):
        lam_init = 0.8 - 0.6 * math.exp(-0.3 * l)
        final = l == depth - 1
        w_up_b, w_down_b = w_up[l].astype(BF16), w_down[l].astype(BF16)
        lam_args = (row(lambda_q1[l]), row(lambda_k1[l]), row(lambda_q2[l]), row(lambda_k2[l]),
                    row(subln_w[l]))

        yconv_s, qm_s, kf_s, vf_s, cmix_s, w_in_b = _mix_in_sample(
            xs, row(norm_mix_w[l]), w_in[l], conv_mix_w[l], row(conv_mix_b[l]),
            state_conv_mix[l][:, 0], state_conv_mix[l][:, 1], dc=dc, da=da, ts=ts, bs=bs)
        q_all = qm_s.reshape(2, ts, bs, N_HEADS, HEAD_DV).transpose(2, 0, 3, 1, 4)
        q_all = q_all.reshape(bs, 2 * N_HEADS * ts, HEAD_DV)
        new_rows = lambda a: jnp.pad(
            a.reshape(ts, bs, N_HEADS * HEAD_DV).transpose(1, 0, 2).reshape(bs, ts * N_HEADS, HEAD_DV),
            ((0, 0), (0, LANES - ts * N_HEADS), (0, 0))).astype(BF16)
        yattn_s = _attn_sample(page_table, rel_bias, q_all, new_rows(kf_s), new_rows(vf_s), *lam_args,
                               cache_k2, cache_v2, page=page, page_offset=l * n_phys, ts=ts,
                               lam_init=lam_init)
        yattn_s = yattn_s.reshape(bs, N_HEADS, ts, HEAD_DV).transpose(2, 0, 1, 3)
        yattn_s = yattn_s.reshape(ts * bs, da).astype(BF16)
        x1_s, qc_s, w_out_b, w_cq_b = _outproj_q_sample(xs, yconv_s, yattn_s, w_out[l],
                                                        row(norm_cross_w[l]), w_cq[l])
        q2 = qc_s.reshape(ts, bs, MEM_HEADS, dh // LANES, LANES).transpose(1, 3, 2, 0, 4)
        q2 = q2.reshape(bs, (dh // LANES) * MEM_HEADS * ts, LANES)
        o_b = _cross_sample(q2, mem_k3, mem_v3, row_offset=l * bs, ts=ts)
        o_s = o_b.reshape(bs, MEM_HEADS, ts, dh).transpose(2, 0, 1, 3).reshape(ts * bs, d)
        xs, cffn_s, w_co_b = _ffn_sample(x1_s, o_s, w_co[l], row(norm_ffn_w[l]), w_up_b, conv_ffn_w[l],
                                         row(conv_ffn_b[l]), w_down_b, row(norm_final_w),
                                         state_conv_ffn[l][:, 0], state_conv_ffn[l][:, 1],
                                         ts=ts, bs=bs, final=final)

        yconv_p, q_p, kf_p, vf_p, kb_p, vb_p, cmix_p = _mix_in_prompt(
            xp, row(norm_mix_w[l]), w_in_b, conv_mix_w[l], row(conv_mix_b[l]),
            batch=bp, seq=sp, dc=dc, da=da)
        yattn_p = _attn_prompt(rel_bias, q_p, kb_p, vb_p, *lam_args[:4], subln_w[l].reshape(-1, 1).astype(F32),
                               batch=bp, seq=sp, lam_init=lam_init)
        mkf, mvf, mkb, mvb = _mem_kv(mem2, row(norm_mem_w[l]), w_ckv[l], batch=bp, n_mem=n_mem)
        x2_p = _cross_prompt(xp, yconv_p, yattn_p, w_out_b, row(norm_cross_w[l]), w_cq_b, mkb, mvb,
                             w_co_b, batch=bp, seq=sp, n_mem=n_mem)
        xp, cffn_p = _ffn_prompt(x2_p, row(norm_ffn_w[l]), w_up_b, conv_ffn_w[l], row(conv_ffn_b[l]),
                                 w_down_b, row(norm_final_w), batch=bp, seq=sp, final=final)

        t2b = lambda a: a.reshape(ts, bs, N_HEADS, HEAD_DV).transpose(1, 0, 2, 3)
        mem_out = lambda a: a.reshape(bp, n_mem, dh // LANES, MEM_HEADS, LANES).transpose(
            0, 1, 3, 2, 4).reshape(bp, n_mem, MEM_HEADS, dh)
        for lst, val in zip(outs, (
                kf_p.reshape(bp, sp, N_HEADS, 2 * HEAD_DK), vf_p.reshape(bp, sp, N_HEADS, HEAD_DV),
                t2b(kf_s), t2b(vf_s), cmix_p, cmix_s.transpose(1, 0, 2), cffn_p,
                cffn_s.transpose(1, 0, 2), mem_out(mkf), mem_out(mvf))):
            lst.append(val)

    y_prompt = xp.reshape(bp, sp, d)
    y_sample = xs.reshape(ts, bs, d).transpose(1, 0, 2)
    return (y_prompt, y_sample) + tuple(jnp.stack(o) for o in outs)
```
